```python
import jax, jax.numpy as jnp
from jax import lax
import numpy as np

D_MODEL = 2048
BATCH = 8
SEQ = 4096
DEPTH = 2

N_MIXERS = 2
N_A = (DEPTH + 1) // 2
N_B = DEPTH // 2
CHUNK = 128
A_WIDTH = D_MODEL
A_GROUPS = 16
A_GROUP_DIM = A_WIDTH // A_GROUPS
B_HEADS = 16
B_HEAD_DIM = D_MODEL // B_HEADS
Q_BLOCK = 128
FFN_HIDDEN = ((-(-8 * D_MODEL // 3) + 255) // 256) * 256
N_MOD = 6
EPS = 1e-6

kernel_name = "hybrid_sgu_fox_adaln_trunk"


def rms_norm(x, g):
    xf = x.astype(jnp.float32)
    y = xf * lax.rsqrt(jnp.mean(xf * xf, axis=-1, keepdims=True) + EPS)
    return (y * g.astype(jnp.float32)).astype(x.dtype)


def layer_norm(x, g, b):
    xf = x.astype(jnp.float32)
    mu = jnp.mean(xf, axis=-1, keepdims=True)
    var = jnp.mean(jnp.square(xf - mu), axis=-1, keepdims=True)
    y = (xf - mu) * lax.rsqrt(var + EPS)
    return (y * g.astype(jnp.float32) + b.astype(jnp.float32)).astype(x.dtype)


def modulate(h, shift, scale):
    return h * (1 + scale[:, None, :]) + shift[:, None, :]


def chunk_sgu_mixer(h, w_in, b_in, ln_g, ln_b, w_s, b_s, w_out):
    B, S, _ = h.shape
    z = jax.nn.gelu(h @ w_in + b_in, approximate=False)
    u, v = jnp.split(z, 2, axis=-1)
    v = layer_norm(v, ln_g, ln_b)
    causal = jnp.tril(jnp.ones((CHUNK, CHUNK), dtype=bool))
    w = jnp.where(causal[None], w_s, 0).astype(v.dtype)
    vc = v.reshape(B, S // CHUNK, CHUNK, A_GROUPS, A_GROUP_DIM)
    sv = jnp.einsum('gts,bnsgc->bntgc', w, vc) + b_s.T[None, None, :, :, None].astype(v.dtype)
    y = u * sv.reshape(B, S, A_WIDTH)
    return y @ w_out


def forgetting_attention(h, w_in, b_f, w_out):
    B, S, D = h.shape
    proj = h @ w_in
    q, k, v, f_logit = jnp.split(proj, [D, 2 * D, 3 * D], axis=-1)
    q = q.reshape(B, S, B_HEADS, B_HEAD_DIM).transpose(0, 2, 1, 3)
    k = k.reshape(B, S, B_HEADS, B_HEAD_DIM).transpose(0, 2, 1, 3)
    v = v.reshape(B, S, B_HEADS, B_HEAD_DIM).transpose(0, 2, 1, 3)
    log_f = jax.nn.log_sigmoid(f_logit.astype(jnp.float32) + b_f.astype(jnp.float32))
    F = jnp.cumsum(log_f, axis=1).transpose(0, 2, 1)
    n_blk = S // Q_BLOCK
    q_blocks = q.reshape(B, B_HEADS, n_blk, Q_BLOCK, B_HEAD_DIM).transpose(2, 0, 1, 3, 4)
    F_blocks = F.reshape(B, B_HEADS, n_blk, Q_BLOCK).transpose(2, 0, 1, 3)
    k_pos = jnp.arange(S)
    scale = 1.0 / float(np.sqrt(B_HEAD_DIM))

    def attend_block(args):
        q_blk, F_blk, i = args
        q_pos = i * Q_BLOCK + jnp.arange(Q_BLOCK)
        s = jnp.einsum('bhqd,bhkd->bhqk', q_blk, k).astype(jnp.float32) * scale
        s = s + F_blk[..., None] - F[:, :, None, :]
        s = jnp.where(k_pos[None, :] <= q_pos[:, None], s, -jnp.inf)
        p = jax.nn.softmax(s, axis=-1).astype(v.dtype)
        return jnp.einsum('bhqk,bhkd->bhqd', p, v)

    o = lax.map(attend_block, (q_blocks, F_blocks, jnp.arange(n_blk)))
    o = o.transpose(1, 0, 3, 2, 4).reshape(B, S, D)
    return o @ w_out


def swiglu_ffn(h, w_gate, w_up, w_down):
    return (jax.nn.silu(h @ w_gate) * (h @ w_up)) @ w_down


def _fwd_setup_inputs(seed: int = 0) -> dict:
    key = jax.random.key(seed)
    ks = jax.random.split(key, 24)
    D, F_H, AW = D_MODEL, FFN_HIDDEN, A_WIDTH
    nrm = lambda k, shape, s: jax.random.normal(k, shape, jnp.float32) * s
    x = nrm(ks[0], (BATCH, SEQ, D), 1.0)
    c = nrm(ks[1], (BATCH, D), 1.0)
    ada_w = nrm(ks[2], (DEPTH, D, N_MOD * D), 0.5 * D ** -0.5)
    ada_b = nrm(ks[3], (DEPTH, N_MOD * D), 0.02)
    norm_mix_g = 1.0 + nrm(ks[4], (DEPTH, D), 0.02)
    norm_ffn_g = 1.0 + nrm(ks[5], (DEPTH, D), 0.02)
    a_w_in = nrm(ks[6], (N_A, D, 2 * AW), D ** -0.5)
    a_b_in = nrm(ks[7], (N_A, 2 * AW), 0.02)
    a_ln_g = 1.0 + nrm(ks[8], (N_A, AW), 0.02)
    a_ln_b = nrm(ks[9], (N_A, AW), 0.02)
    a_w_s = nrm(ks[10], (N_A, A_GROUPS, CHUNK, CHUNK), CHUNK ** -0.5)
    a_b_s = 1.0 + nrm(ks[11], (N_A, A_GROUPS, CHUNK), 0.02)
    a_w_out = nrm(ks[12], (N_A, AW, D), AW ** -0.5)
    b_w_qkv = nrm(ks[13], (N_B, D, 3 * D), D ** -0.5)
    b_w_f = nrm(ks[14], (N_B, D, B_HEADS), 0.5 * D ** -0.5)
    b_w_in = jnp.concatenate([b_w_qkv, b_w_f], axis=-1)
    b_b_f = jax.random.uniform(ks[15], (N_B, B_HEADS), jnp.float32, 1.0, 6.0)
    b_w_out = nrm(ks[16], (N_B, D, D), D ** -0.5)
    ffn_w_gate = nrm(ks[17], (DEPTH, D, F_H), D ** -0.5)
    ffn_w_up = nrm(ks[18], (DEPTH, D, F_H), D ** -0.5)
    ffn_w_down = nrm(ks[19], (DEPTH, F_H, D), F_H ** -0.5)
    final_g = 1.0 + nrm(ks[20], (D,), 0.02)
    return {"x": x, "c": c, "ada_w": ada_w, "ada_b": ada_b,
            "norm_mix_g": norm_mix_g, "norm_ffn_g": norm_ffn_g,
            "a_w_in": a_w_in, "a_b_in": a_b_in, "a_ln_g": a_ln_g, "a_ln_b": a_ln_b,
            "a_w_s": a_w_s, "a_b_s": a_b_s, "a_w_out": a_w_out,
            "b_w_in": b_w_in, "b_b_f": b_b_f, "b_w_out": b_w_out,
            "ffn_w_gate": ffn_w_gate, "ffn_w_up": ffn_w_up, "ffn_w_down": ffn_w_down,
            "final_g": final_g}


def _fwd_reference(x, c, ada_w, ada_b, norm_mix_g, norm_ffn_g,
              a_w_in, a_b_in, a_ln_g, a_ln_b, a_w_s, a_b_s, a_w_out,
              b_w_in, b_b_f, b_w_out,
              ffn_w_gate, ffn_w_up, ffn_w_down, final_g):
    c_act = jax.nn.silu(c)
    for i in range(DEPTH):
        mod = c_act @ ada_w[i] + ada_b[i]
        sh1, sc1, g1, sh2, sc2, g2 = jnp.split(mod, N_MOD, axis=-1)
        h = modulate(rms_norm(x, norm_mix_g[i]), sh1, sc1)
        j = i // N_MIXERS
        if i % N_MIXERS == 0:
            y = chunk_sgu_mixer(h, a_w_in[j], a_b_in[j], a_ln_g[j], a_ln_b[j],
                                a_w_s[j], a_b_s[j], a_w_out[j])
        else:
            y = forgetting_attention(h, b_w_in[j], b_b_f[j], b_w_out[j])
        x = x + g1[:, None, :] * y
        h = modulate(rms_norm(x, norm_ffn_g[i]), sh2, sc2)
        x = x + g2[:, None, :] * swiglu_ffn(h, ffn_w_gate[i], ffn_w_up[i], ffn_w_down[i])
    return rms_norm(x, final_g)


import jax as _jax
import jax.numpy as _jnp

TWIN_FORMAT = 'train_step'
FWD_PARAMS = ['x', 'c', 'ada_w', 'ada_b', 'norm_mix_g', 'norm_ffn_g', 'a_w_in', 'a_b_in', 'a_ln_g', 'a_ln_b', 'a_w_s', 'a_b_s', 'a_w_out', 'b_w_in', 'b_b_f', 'b_w_out', 'ffn_w_gate', 'ffn_w_up', 'ffn_w_down', 'final_g']
TWIN_WEIGHTS = ['ada_w', 'ada_b', 'norm_mix_g', 'norm_ffn_g', 'a_w_in', 'a_b_in', 'a_ln_g', 'a_ln_b', 'a_w_s', 'a_b_s', 'a_w_out', 'b_w_in', 'b_b_f', 'b_w_out', 'ffn_w_gate', 'ffn_w_up', 'ffn_w_down', 'final_g']
TWIN_DIFF_INPUT = 'x'
TWIN_INPUTS = ['x', 'c', 'ada_w', 'ada_b', 'norm_mix_g', 'norm_ffn_g', 'a_w_in', 'a_b_in', 'a_ln_g', 'a_ln_b', 'a_w_s', 'a_b_s', 'a_w_out', 'b_w_in', 'b_b_f', 'b_w_out', 'ffn_w_gate', 'ffn_w_up', 'ffn_w_down', 'final_g', 'loss_target', 'm_ada_w', 'm_ada_b', 'm_norm_mix_g', 'm_norm_ffn_g', 'm_a_w_in', 'm_a_b_in', 'm_a_ln_g', 'm_a_ln_b', 'm_a_w_s', 'm_a_b_s', 'm_a_w_out', 'm_b_w_in', 'm_b_b_f', 'm_b_w_out', 'm_ffn_w_gate', 'm_ffn_w_up', 'm_ffn_w_down', 'm_final_g', 'v_ada_w', 'v_ada_b', 'v_norm_mix_g', 'v_norm_ffn_g', 'v_a_w_in', 'v_a_b_in', 'v_a_ln_g', 'v_a_ln_b', 'v_a_w_s', 'v_a_b_s', 'v_a_w_out', 'v_b_w_in', 'v_b_b_f', 'v_b_w_out', 'v_ffn_w_gate', 'v_ffn_w_up', 'v_ffn_w_down', 'v_final_g']
TWIN_OUTPUTS = ['loss', 'grad_x', 'grad_ada_w', 'grad_ada_b', 'grad_norm_mix_g', 'grad_norm_ffn_g', 'grad_a_w_in', 'grad_a_b_in', 'grad_a_ln_g', 'grad_a_ln_b', 'grad_a_w_s', 'grad_a_b_s', 'grad_a_w_out', 'grad_b_w_in', 'grad_b_b_f', 'grad_b_w_out', 'grad_ffn_w_gate', 'grad_ffn_w_up', 'grad_ffn_w_down', 'grad_final_g', 'delta_ada_w', 'delta_ada_b', 'delta_norm_mix_g', 'delta_norm_ffn_g', 'delta_a_w_in', 'delta_a_b_in', 'delta_a_ln_g', 'delta_a_ln_b', 'delta_a_w_s', 'delta_a_b_s', 'delta_a_w_out', 'delta_b_w_in', 'delta_b_b_f', 'delta_b_w_out', 'delta_ffn_w_gate', 'delta_ffn_w_up', 'delta_ffn_w_down', 'delta_final_g', 'new_m_ada_w', 'new_m_ada_b', 'new_m_norm_mix_g', 'new_m_norm_ffn_g', 'new_m_a_w_in', 'new_m_a_b_in', 'new_m_a_ln_g', 'new_m_a_ln_b', 'new_m_a_w_s', 'new_m_a_b_s', 'new_m_a_w_out', 'new_m_b_w_in', 'new_m_b_b_f', 'new_m_b_w_out', 'new_m_ffn_w_gate', 'new_m_ffn_w_up', 'new_m_ffn_w_down', 'new_m_final_g', 'new_v_ada_w', 'new_v_ada_b', 'new_v_norm_mix_g', 'new_v_norm_ffn_g', 'new_v_a_w_in', 'new_v_a_b_in', 'new_v_a_ln_g', 'new_v_a_ln_b', 'new_v_a_w_s', 'new_v_a_b_s', 'new_v_a_w_out', 'new_v_b_w_in', 'new_v_b_b_f', 'new_v_b_w_out', 'new_v_ffn_w_gate', 'new_v_ffn_w_up', 'new_v_ffn_w_down', 'new_v_final_g']
TWIN_LEAF_KINDS = {'loss': 'loss', 'grad_x': 'grad_x', 'grad_ada_w': 'grad_w', 'grad_ada_b': 'grad_w', 'grad_norm_mix_g': 'grad_w', 'grad_norm_ffn_g': 'grad_w', 'grad_a_w_in': 'grad_w', 'grad_a_b_in': 'grad_w', 'grad_a_ln_g': 'grad_w', 'grad_a_ln_b': 'grad_w', 'grad_a_w_s': 'grad_w', 'grad_a_b_s': 'grad_w', 'grad_a_w_out': 'grad_w', 'grad_b_w_in': 'grad_w', 'grad_b_b_f': 'grad_w', 'grad_b_w_out': 'grad_w', 'grad_ffn_w_gate': 'grad_w', 'grad_ffn_w_up': 'grad_w', 'grad_ffn_w_down': 'grad_w', 'grad_final_g': 'grad_w', 'delta_ada_w': 'delta_w', 'delta_ada_b': 'delta_w', 'delta_norm_mix_g': 'delta_w', 'delta_norm_ffn_g': 'delta_w', 'delta_a_w_in': 'delta_w', 'delta_a_b_in': 'delta_w', 'delta_a_ln_g': 'delta_w', 'delta_a_ln_b': 'delta_w', 'delta_a_w_s': 'delta_w', 'delta_a_b_s': 'delta_w', 'delta_a_w_out': 'delta_w', 'delta_b_w_in': 'delta_w', 'delta_b_b_f': 'delta_w', 'delta_b_w_out': 'delta_w', 'delta_ffn_w_gate': 'delta_w', 'delta_ffn_w_up': 'delta_w', 'delta_ffn_w_down': 'delta_w', 'delta_final_g': 'delta_w', 'new_m_ada_w': 'new_m', 'new_m_ada_b': 'new_m', 'new_m_norm_mix_g': 'new_m', 'new_m_norm_ffn_g': 'new_m', 'new_m_a_w_in': 'new_m', 'new_m_a_b_in': 'new_m', 'new_m_a_ln_g': 'new_m', 'new_m_a_ln_b': 'new_m', 'new_m_a_w_s': 'new_m', 'new_m_a_b_s': 'new_m', 'new_m_a_w_out': 'new_m', 'new_m_b_w_in': 'new_m', 'new_m_b_b_f': 'new_m', 'new_m_b_w_out': 'new_m', 'new_m_ffn_w_gate': 'new_m', 'new_m_ffn_w_up': 'new_m', 'new_m_ffn_w_down': 'new_m', 'new_m_final_g': 'new_m', 'new_v_ada_w': 'new_v', 'new_v_ada_b': 'new_v', 'new_v_norm_mix_g': 'new_v', 'new_v_norm_ffn_g': 'new_v', 'new_v_a_w_in': 'new_v', 'new_v_a_b_in': 'new_v', 'new_v_a_ln_g': 'new_v', 'new_v_a_ln_b': 'new_v', 'new_v_a_w_s': 'new_v', 'new_v_a_b_s': 'new_v', 'new_v_a_w_out': 'new_v', 'new_v_b_w_in': 'new_v', 'new_v_b_b_f': 'new_v', 'new_v_b_w_out': 'new_v', 'new_v_ffn_w_gate': 'new_v', 'new_v_ffn_w_up': 'new_v', 'new_v_ffn_w_down': 'new_v', 'new_v_final_g': 'new_v'}


def _forward(args):
    return _fwd_reference(*[args[k] for k in FWD_PARAMS])


def _output_shape():
    def fwd():
        inp = _fwd_setup_inputs(0)
        return _fwd_reference(*[inp[k] for k in FWD_PARAMS])
    out = _jax.eval_shape(fwd)
    return out.shape, out.dtype

N_MICROBATCH = 1
ADAM_LR = 0.001
ADAM_B1 = 0.9
ADAM_B2 = 0.999
ADAM_EPS = 1e-08
ADAM_WD = 0.01
ADAM_STEP = 10
PER_EXAMPLE_BATCH_AXIS = {'x': 0, 'c': 0, 'loss_target': 0}
SHARED_INPUTS = []
_WEIGHT_DTYPES = {'ada_w': _jnp.float32, 'ada_b': _jnp.float32, 'norm_mix_g': _jnp.float32, 'norm_ffn_g': _jnp.float32, 'a_w_in': _jnp.float32, 'a_b_in': _jnp.float32, 'a_ln_g': _jnp.float32, 'a_ln_b': _jnp.float32, 'a_w_s': _jnp.float32, 'a_b_s': _jnp.float32, 'a_w_out': _jnp.float32, 'b_w_in': _jnp.float32, 'b_b_f': _jnp.float32, 'b_w_out': _jnp.float32, 'ffn_w_gate': _jnp.float32, 'ffn_w_up': _jnp.float32, 'ffn_w_down': _jnp.float32, 'final_g': _jnp.float32}
MOMENT_SCALE = {'ada_w': 2.493646e-02, 'ada_b': 4.307930e-02, 'norm_mix_g': 2.119296e-02, 'norm_ffn_g': 2.602284e-02, 'a_w_in': 2.047511e-02, 'a_b_in': 1.939316e-02, 'a_ln_g': 1.375722e-02, 'a_ln_b': 1.309807e-02, 'a_w_s': 1.357023e-02, 'a_b_s': 1.913850e-02, 'a_w_out': 2.394727e-02, 'b_w_in': 8.475540e-03, 'b_b_f': 3.739111e-02, 'b_w_out': 1.102101e-02, 'ffn_w_gate': 1.123126e-02, 'ffn_w_up': 1.087492e-02, 'ffn_w_down': 1.806017e-02, 'final_g': 1.602019e+01}


def _to_microbatches(a, axis):
    t = _jnp.moveaxis(a, axis, 0)
    t = t.reshape((N_MICROBATCH, t.shape[0] // N_MICROBATCH) + t.shape[1:])
    return _jnp.moveaxis(t, 1, axis + 1)


def setup_inputs(seed: int = 0) -> dict:
    inp = _fwd_setup_inputs(seed)
    key = _jax.random.fold_in(_jax.random.key(seed), 7919)
    shape, _ = _output_shape()
    out = dict(inp)
    out["loss_target"] = _jax.random.normal(_jax.random.fold_in(key, 0), shape, _jnp.float32)
    for i, name in enumerate(TWIN_WEIGHTS):
        w = inp[name].astype(_jnp.float32)
        if MOMENT_SCALE is None:
            s = _jnp.sqrt(_jnp.mean(_jnp.square(w)) + 1e-30)
        else:
            s = MOMENT_SCALE[name]
        km, kv = _jax.random.split(_jax.random.fold_in(key, i + 1))
        out[name] = w
        out["m_" + name] = s * _jax.random.normal(km, w.shape, _jnp.float32)
        out["v_" + name] = (s * s) * _jax.random.uniform(kv, w.shape, _jnp.float32, 0.5, 1.5)
    if N_MICROBATCH > 1:
        for name, axis in PER_EXAMPLE_BATCH_AXIS.items():
            out[name] = _to_microbatches(out[name], axis)
    return {'x': out['x'], 'c': out['c'], 'ada_w': out['ada_w'], 'ada_b': out['ada_b'], 'norm_mix_g': out['norm_mix_g'], 'norm_ffn_g': out['norm_ffn_g'], 'a_w_in': out['a_w_in'], 'a_b_in': out['a_b_in'], 'a_ln_g': out['a_ln_g'], 'a_ln_b': out['a_ln_b'], 'a_w_s': out['a_w_s'], 'a_b_s': out['a_b_s'], 'a_w_out': out['a_w_out'], 'b_w_in': out['b_w_in'], 'b_b_f': out['b_b_f'], 'b_w_out': out['b_w_out'], 'ffn_w_gate': out['ffn_w_gate'], 'ffn_w_up': out['ffn_w_up'], 'ffn_w_down': out['ffn_w_down'], 'final_g': out['final_g'], 'loss_target': out['loss_target'], 'm_ada_w': out['m_ada_w'], 'm_ada_b': out['m_ada_b'], 'm_norm_mix_g': out['m_norm_mix_g'], 'm_norm_ffn_g': out['m_norm_ffn_g'], 'm_a_w_in': out['m_a_w_in'], 'm_a_b_in': out['m_a_b_in'], 'm_a_ln_g': out['m_a_ln_g'], 'm_a_ln_b': out['m_a_ln_b'], 'm_a_w_s': out['m_a_w_s'], 'm_a_b_s': out['m_a_b_s'], 'm_a_w_out': out['m_a_w_out'], 'm_b_w_in': out['m_b_w_in'], 'm_b_b_f': out['m_b_b_f'], 'm_b_w_out': out['m_b_w_out'], 'm_ffn_w_gate': out['m_ffn_w_gate'], 'm_ffn_w_up': out['m_ffn_w_up'], 'm_ffn_w_down': out['m_ffn_w_down'], 'm_final_g': out['m_final_g'], 'v_ada_w': out['v_ada_w'], 'v_ada_b': out['v_ada_b'], 'v_norm_mix_g': out['v_norm_mix_g'], 'v_norm_ffn_g': out['v_norm_ffn_g'], 'v_a_w_in': out['v_a_w_in'], 'v_a_b_in': out['v_a_b_in'], 'v_a_ln_g': out['v_a_ln_g'], 'v_a_ln_b': out['v_a_ln_b'], 'v_a_w_s': out['v_a_w_s'], 'v_a_b_s': out['v_a_b_s'], 'v_a_w_out': out['v_a_w_out'], 'v_b_w_in': out['v_b_w_in'], 'v_b_b_f': out['v_b_b_f'], 'v_b_w_out': out['v_b_w_out'], 'v_ffn_w_gate': out['v_ffn_w_gate'], 'v_ffn_w_up': out['v_ffn_w_up'], 'v_ffn_w_down': out['v_ffn_w_down'], 'v_final_g': out['v_final_g']}


def _loss(weights, diff, rest, loss_target):
    with _jax.named_scope("forward"):
        args = {**rest, TWIN_DIFF_INPUT: diff, **{k: w.astype(_WEIGHT_DTYPES[k]) for k, w in weights.items()}}
        y = _forward(args)
    with _jax.named_scope("loss_head"):
        err = _jnp.square(y.astype(_jnp.float32) - loss_target)
        return 0.5 * _jnp.sum(_jnp.mean(err, axis=-1)) if err.ndim else 0.5 * err


def _adamw(w, g, m, v):
    m = ADAM_B1 * m + (1.0 - ADAM_B1) * g
    v = ADAM_B2 * v + (1.0 - ADAM_B2) * _jnp.square(g)
    m_hat = m / (1.0 - ADAM_B1 ** ADAM_STEP)
    v_hat = v / (1.0 - ADAM_B2 ** ADAM_STEP)
    delta = -ADAM_LR * (m_hat / (_jnp.sqrt(v_hat) + ADAM_EPS) + ADAM_WD * w)
    return delta, m, v


def reference(x, c, ada_w, ada_b, norm_mix_g, norm_ffn_g, a_w_in, a_b_in, a_ln_g, a_ln_b, a_w_s, a_b_s, a_w_out, b_w_in, b_b_f, b_w_out, ffn_w_gate, ffn_w_up, ffn_w_down, final_g, loss_target, m_ada_w, m_ada_b, m_norm_mix_g, m_norm_ffn_g, m_a_w_in, m_a_b_in, m_a_ln_g, m_a_ln_b, m_a_w_s, m_a_b_s, m_a_w_out, m_b_w_in, m_b_b_f, m_b_w_out, m_ffn_w_gate, m_ffn_w_up, m_ffn_w_down, m_final_g, v_ada_w, v_ada_b, v_norm_mix_g, v_norm_ffn_g, v_a_w_in, v_a_b_in, v_a_ln_g, v_a_ln_b, v_a_w_s, v_a_b_s, v_a_w_out, v_b_w_in, v_b_b_f, v_b_w_out, v_ffn_w_gate, v_ffn_w_up, v_ffn_w_down, v_final_g):
    given = dict(x=x, c=c, ada_w=ada_w, ada_b=ada_b, norm_mix_g=norm_mix_g, norm_ffn_g=norm_ffn_g, a_w_in=a_w_in, a_b_in=a_b_in, a_ln_g=a_ln_g, a_ln_b=a_ln_b, a_w_s=a_w_s, a_b_s=a_b_s, a_w_out=a_w_out, b_w_in=b_w_in, b_b_f=b_b_f, b_w_out=b_w_out, ffn_w_gate=ffn_w_gate, ffn_w_up=ffn_w_up, ffn_w_down=ffn_w_down, final_g=final_g, loss_target=loss_target, m_ada_w=m_ada_w, m_ada_b=m_ada_b, m_norm_mix_g=m_norm_mix_g, m_norm_ffn_g=m_norm_ffn_g, m_a_w_in=m_a_w_in, m_a_b_in=m_a_b_in, m_a_ln_g=m_a_ln_g, m_a_ln_b=m_a_ln_b, m_a_w_s=m_a_w_s, m_a_b_s=m_a_b_s, m_a_w_out=m_a_w_out, m_b_w_in=m_b_w_in, m_b_b_f=m_b_b_f, m_b_w_out=m_b_w_out, m_ffn_w_gate=m_ffn_w_gate, m_ffn_w_up=m_ffn_w_up, m_ffn_w_down=m_ffn_w_down, m_final_g=m_final_g, v_ada_w=v_ada_w, v_ada_b=v_ada_b, v_norm_mix_g=v_norm_mix_g, v_norm_ffn_g=v_norm_ffn_g, v_a_w_in=v_a_w_in, v_a_b_in=v_a_b_in, v_a_ln_g=v_a_ln_g, v_a_ln_b=v_a_ln_b, v_a_w_s=v_a_w_s, v_a_b_s=v_a_b_s, v_a_w_out=v_a_w_out, v_b_w_in=v_b_w_in, v_b_b_f=v_b_b_f, v_b_w_out=v_b_w_out, v_ffn_w_gate=v_ffn_w_gate, v_ffn_w_up=v_ffn_w_up, v_ffn_w_down=v_ffn_w_down, v_final_g=v_final_g)
    weights = {n: given[n] for n in TWIN_WEIGHTS}
    shared = {n: given[n] for n in SHARED_INPUTS}
    per_example = {n: given[n] for n in ['x', 'c']}
    grad_fn = _jax.value_and_grad(_loss, argnums=(0, 1))

    def one_microbatch(ex, loss_target):
        ex = dict(ex)
        diff = ex.pop(TWIN_DIFF_INPUT)
        return grad_fn(weights, diff, {**shared, **ex}, loss_target)

    if N_MICROBATCH == 1:
        loss, (grad_w, grad_x) = one_microbatch(per_example, given["loss_target"])
    else:
        def body(carry, xs):
            loss_sum, grad_sum = carry
            l_k, (gw_k, gx_k) = one_microbatch(xs[0], xs[1])
            with _jax.named_scope("update"):
                return (loss_sum + l_k, _jax.tree.map(_jnp.add, grad_sum, gw_k)), gx_k

        init = (_jnp.zeros((), _jnp.float32), _jax.tree.map(_jnp.zeros_like, weights))
        (loss, grad_w), grad_x = _jax.lax.scan(body, init, (per_example, given["loss_target"]))
    with _jax.named_scope("update"):
        delta_w, new_m, new_v = {}, {}, {}
        for n in TWIN_WEIGHTS:
            delta_w[n], new_m[n], new_v[n] = _adamw(weights[n], grad_w[n], given["m_" + n], given["v_" + n])
    return (loss, grad_x, *[grad_w[n] for n in TWIN_WEIGHTS], *[delta_w[n] for n in TWIN_WEIGHTS],
            *[new_m[n] for n in TWIN_WEIGHTS], *[new_v[n] for n in TWIN_WEIGHTS])
```

```python
import functools
import math

import jax
import jax.numpy as jnp
from jax import lax
from jax.experimental import pallas as pl
from jax.experimental.pallas import tpu as pltpu

F32, BF16 = jnp.float32, jnp.bfloat16
LANE = 128
N_CHIPS = 4
N_DEV = 8
N_MOD = 6
EPS = 1e-6
VMEM_LIMIT = 60 * 1024 * 1024
ADAM_LR, ADAM_B1, ADAM_B2, ADAM_EPS, ADAM_WD, ADAM_STEP = 0.001, 0.9, 0.999, 1e-08, 0.01, 10
MESH = pl.DeviceIdType.MESH
ANY = pl.BlockSpec(memory_space=pl.ANY)
SDS = jax.ShapeDtypeStruct


def _pick(dim, pref, align):
    t = min(dim, pref)
    t -= t % align
    while t >= align:
        if dim % t == 0:
            return t
        t -= align
    return dim


def _params(sem=None):
    return pltpu.CompilerParams(dimension_semantics=sem, vmem_limit_bytes=VMEM_LIMIT)


def _mm(name, form, a_list, b_list, *, M, N, K, tm, tn, tk, b_stacked=False, out_stacked=False,
        b_koffs=None, acc_of=None, n_acc=1, extras=(), out_dtypes=(F32,), epilogue=None):
    assert M % tm == 0 and N % tn == 0 and K % tk == 0, (name, M, N, K, tm, tn, tk)
    nm, nn, nk = M // tm, N // tn, K // tk
    npairs = len(a_list)
    acc_of = acc_of or [0] * npairs
    b_koffs = b_koffs or [0] * npairs
    if epilogue is None:
        epilogue = lambda accs, ex: [accs[0]]

    in_specs = []
    for _ in a_list:
        if form == "tn":
            in_specs.append(pl.BlockSpec((tk, tm), lambda j, i, k: (k, i)))
        else:
            in_specs.append(pl.BlockSpec((tm, tk), lambda j, i, k: (i, k)))
    for off in b_koffs:
        if form == "nn":
            if b_stacked:
                assert tn * N_CHIPS == N
                in_specs.append(pl.BlockSpec((None, tk, tn), lambda j, i, k: (j, k, 0)))
            else:
                in_specs.append(pl.BlockSpec((tk, tn), lambda j, i, k: (k, j)))
        elif form == "nt":
            if b_stacked:
                assert tk * N_CHIPS == K
                in_specs.append(pl.BlockSpec((None, tn, tk), lambda j, i, k: (k, j, 0)))
            else:
                in_specs.append(pl.BlockSpec((tn, tk), lambda j, i, k, off=off: (j, off + k)))
        else:
            in_specs.append(pl.BlockSpec((tk, tn), lambda j, i, k: (k, j)))
    for kind, _ in extras:
        if kind == "tile":
            in_specs.append(pl.BlockSpec((tm, tn), lambda j, i, k: (i, j)))
        else:
            in_specs.append(pl.BlockSpec((1, tn), lambda j, i, k: (0, j)))
    if out_stacked:
        assert tn * N_CHIPS == N
        out_specs = [pl.BlockSpec((None, tm, tn), lambda j, i, k: (j, i, 0)) for _ in out_dtypes]
        out_shape = [SDS((N_CHIPS, M, tn), d) for d in out_dtypes]
    else:
        out_specs = [pl.BlockSpec((tm, tn), lambda j, i, k: (i, j)) for _ in out_dtypes]
        out_shape = [SDS((M, N), d) for d in out_dtypes]
    dims = {"nn": (((1,), (0,)), ((), ())), "nt": (((1,), (1,)), ((), ())), "tn": (((0,), (0,)), ((), ()))}[form]
    n_ex, n_out = len(extras), len(out_dtypes)

    def body(*refs):
        a_refs = refs[:npairs]
        b_refs = refs[npairs:2 * npairs]
        e_refs = refs[2 * npairs:2 * npairs + n_ex]
        o_refs = refs[2 * npairs + n_ex:2 * npairs + n_ex + n_out]
        acc_refs = refs[2 * npairs + n_ex + n_out:]

        tot = [None] * n_acc
        for p in range(npairs):
            d = lax.dot_general(a_refs[p][...].astype(BF16), b_refs[p][...].astype(BF16), dims,
                                preferred_element_type=F32)
            tot[acc_of[p]] = d if tot[acc_of[p]] is None else tot[acc_of[p]] + d

        def finish(accs):
            outs = epilogue(accs, [e[...] for e in e_refs])
            for o_ref, o in zip(o_refs, outs):
                o_ref[...] = o.astype(o_ref.dtype)

        if nk == 1:
            finish(tot)
        else:
            k = pl.program_id(2)

            @pl.when(k == 0)
            def _():
                for r, t in zip(acc_refs, tot):
                    r[...] = t

            @pl.when(k > 0)
            def _():
                for r, t in zip(acc_refs, tot):
                    r[...] += t

            @pl.when(k == nk - 1)
            def _():
                finish([r[...] for r in acc_refs])

    scratch = [pltpu.VMEM((tm, tn), F32) for _ in range(n_acc)] if nk > 1 else []
    outs = pl.pallas_call(
        body, name=name, grid=(nn, nm, nk), in_specs=in_specs, out_specs=out_specs, out_shape=out_shape,
        scratch_shapes=scratch, compiler_params=_params(("parallel", "parallel", "arbitrary")),
    )(*a_list, *b_list, *[e for _, e in extras])
    return outs


def _rowwise(name, fn, rows, vecs, row_outs, acc_widths, tr):
    S = rows[0].shape[0]
    assert S % tr == 0
    nr, nv, no, na = len(rows), len(vecs), len(row_outs), len(acc_widths)

    def body(*refs):
        r = [x[...] for x in refs[:nr]]
        v = [x[...] for x in refs[nr:nr + nv]]
        o_refs = refs[nr + nv:nr + nv + no]
        a_refs = refs[nr + nv + no:]
        outs, accs = fn(r, v)
        for o_ref, o in zip(o_refs, outs):
            o_ref[...] = o.astype(o_ref.dtype)
        first = pl.program_id(0) == 0

        @pl.when(first)
        def _():
            for a_ref, a in zip(a_refs, accs):
                a_ref[...] = a

        @pl.when(jnp.logical_not(first))
        def _():
            for a_ref, a in zip(a_refs, accs):
                a_ref[...] += a

    in_specs = [pl.BlockSpec((tr, x.shape[1]), lambda i: (i, 0)) for x in rows]
    in_specs += [pl.BlockSpec(x.shape, lambda i, nd=x.ndim: (0,) * nd) for x in vecs]
    out_specs = [pl.BlockSpec((tr, w), lambda i: (i, 0)) for w, _ in row_outs]
    out_specs += [pl.BlockSpec((1, w), lambda i: (0, 0)) for w in acc_widths]
    out_shape = [SDS((S, w), d) for w, d in row_outs] + [SDS((1, w), F32) for w in acc_widths]
    return pl.pallas_call(body, name=name, grid=(S // tr,), in_specs=in_specs, out_specs=out_specs,
                          out_shape=out_shape, compiler_params=_params(("arbitrary",)))(*rows, *vecs)


def _colsum(a):
    return jnp.sum(a, axis=0, keepdims=True)


def _rms_stats(x):
    rstd = lax.rsqrt(jnp.mean(x * x, axis=1, keepdims=True) + EPS)
    return x * rstd, rstd


def _normmod_fwd(name, x, g, sc, sh, tr):
    def fn(r, v):
        n, _ = _rms_stats(r[0])
        return [(n * v[0]) * (1.0 + v[1]) + v[2]], []
    return _rowwise(name, fn, [x], [g, sc, sh], [(x.shape[1], BF16)], [], tr)[0]


def _normmod_bwd(name, x, dh, dres, g, sc, tr):
    def fn(r, v):
        x_, dh_, dres_ = r
        g_, sc_ = v
        n, rstd = _rms_stats(x_)
        hn = n * g_
        dhn = dh_ * (1.0 + sc_)
        dn = dhn * g_
        dx = rstd * (dn - n * jnp.mean(dn * n, axis=1, keepdims=True))
        return [dres_ + dx], [_colsum(dh_), _colsum(dh_ * hn), _colsum(dhn * n)]
    D = x.shape[1]
    return _rowwise(name, fn, [x, dh, dres], [g, sc], [(D, F32)], [D, D, D], tr)


def _gate_bwd(name, dx, out, gate, tr):
    def fn(r, v):
        return [v[0] * r[0]], [_colsum(r[0] * r[1].astype(F32))]
    D = dx.shape[1]
    return _rowwise(name, fn, [dx, out], [gate], [(D, BF16)], [D], tr)


def _loss_head(name, x, target, g, tr):
    D = x.shape[1]

    def fn(r, v):
        n, rstd = _rms_stats(r[0])
        err = n * v[0] - r[1]
        loss = 0.5 * jnp.sum(jnp.mean(err * err, axis=1, keepdims=True), axis=0, keepdims=True)
        dy = err * (1.0 / D)
        dn = dy * v[0]
        dx = rstd * (dn - n * jnp.mean(dn * n, axis=1, keepdims=True))
        return [dx], [jnp.broadcast_to(loss, (1, LANE)), _colsum(dy * n)]
    return _rowwise(name, fn, [x, target], [g], [(D, F32)], [LANE, D], tr)


_INV_SQRT2 = 1.0 / math.sqrt(2.0)
_INV_SQRT2PI = 1.0 / math.sqrt(2.0 * math.pi)


def _gelu(a):
    return 0.5 * a * (1.0 + lax.erf(a * _INV_SQRT2))


def _gelu_grad(a):
    return 0.5 * (1.0 + lax.erf(a * _INV_SQRT2)) + a * jnp.exp(-0.5 * a * a) * _INV_SQRT2PI


def _ln_stats(v):
    mu = jnp.mean(v, axis=1, keepdims=True)
    vc = v - mu
    rstd = lax.rsqrt(jnp.mean(vc * vc, axis=1, keepdims=True) + EPS)
    return vc * rstd, rstd


def _chunk_mix(w_ref, vn, tr, G):
    rows = []
    for ch in range(tr // LANE):
        cols = []
        for g in range(G):
            blk = vn[ch * LANE:(ch + 1) * LANE, g * LANE:(g + 1) * LANE]
            cols.append(jnp.dot(w_ref[g], blk, preferred_element_type=F32))
        rows.append(jnp.concatenate(cols, axis=1))
    return jnp.concatenate(rows, axis=0)


def _sgu_mid_fwd(a, ln_g, ln_b, w_mask, bias_full, tr):
    S, D2 = a.shape
    D = D2 // 2
    G = D // LANE

    def body(a_ref, g_ref, b_ref, w_ref, bias_ref, y_ref):
        u = _gelu(a_ref[:, :D])
        v = _gelu(a_ref[:, D:])
        vhat, _ = _ln_stats(v)
        vn = (vhat * g_ref[...] + b_ref[...]).astype(BF16)
        sv = _chunk_mix(w_ref, vn, tr, G) + jnp.concatenate([bias_ref[...]] * (tr // LANE), axis=0)
        y_ref[...] = (u * sv).astype(BF16)

    return pl.pallas_call(
        body, name="sgu_mid_fwd", grid=(S // tr,),
        in_specs=[pl.BlockSpec((tr, D2), lambda i: (i, 0)), pl.BlockSpec((1, D), lambda i: (0, 0)),
                  pl.BlockSpec((1, D), lambda i: (0, 0)), pl.BlockSpec((G, LANE, LANE), lambda i: (0, 0, 0)),
                  pl.BlockSpec((LANE, D), lambda i: (0, 0))],
        out_specs=pl.BlockSpec((tr, D), lambda i: (i, 0)), out_shape=SDS((S, D), BF16),
        compiler_params=_params(("arbitrary",)))(a, ln_g, ln_b, w_mask, bias_full)


def _sgu_mid_bwd(a, dy, ln_g, ln_b, w_mask, bias_full, tr):
    S, D2 = a.shape
    D = D2 // 2
    G = D // LANE
    nch = tr // LANE

    def body(a_ref, dy_ref, g_ref, b_ref, w_ref, bias_ref, da_ref, dw_ref, dbias_ref, dg_ref, db_ref, dbin_ref):
        au, av = a_ref[:, :D], a_ref[:, D:]
        u = _gelu(au)
        v = _gelu(av)
        vhat, rstd = _ln_stats(v)
        vn = (vhat * g_ref[...] + b_ref[...]).astype(BF16)
        sv = _chunk_mix(w_ref, vn, tr, G) + jnp.concatenate([bias_ref[...]] * nch, axis=0)
        dy_ = dy_ref[...].astype(F32)
        du = dy_ * sv
        dsv = dy_ * u
        dsv_b = dsv.astype(BF16)
        first = pl.program_id(0) == 0

        @pl.when(first)
        def _():
            dw_ref[...] = jnp.zeros_like(dw_ref)
            dbias_ref[...] = jnp.zeros_like(dbias_ref)
            dg_ref[...] = jnp.zeros_like(dg_ref)
            db_ref[...] = jnp.zeros_like(db_ref)
            dbin_ref[...] = jnp.zeros_like(dbin_ref)

        rows = []
        dbias = None
        for ch in range(nch):
            r0 = ch * LANE
            cols = []
            for g in range(G):
                c0 = g * LANE
                ds_blk = dsv_b[r0:r0 + LANE, c0:c0 + LANE]
                vn_blk = vn[r0:r0 + LANE, c0:c0 + LANE]
                cols.append(lax.dot_general(w_ref[g], ds_blk, (((0,), (0,)), ((), ())),
                                            preferred_element_type=F32))
                dw_ref[g] += lax.dot_general(ds_blk, vn_blk, (((1,), (1,)), ((), ())),
                                             preferred_element_type=F32)
            rows.append(jnp.concatenate(cols, axis=1))
            blk = dsv[r0:r0 + LANE, :]
            dbias = blk if dbias is None else dbias + blk
        dvn = jnp.concatenate(rows, axis=0)
        dbias_ref[...] += dbias
        dg_ref[...] += _colsum(dvn * vhat)
        db_ref[...] += _colsum(dvn)
        dvh = dvn * g_ref[...]
        dv = rstd * (dvh - jnp.mean(dvh, axis=1, keepdims=True)
                     - vhat * jnp.mean(dvh * vhat, axis=1, keepdims=True))
        da_u = du * _gelu_grad(au)
        da_v = dv * _gelu_grad(av)
        da_ref[:, :D] = da_u.astype(BF16)
        da_ref[:, D:] = da_v.astype(BF16)
        dbin_ref[:, :D] += _colsum(da_u)
        dbin_ref[:, D:] += _colsum(da_v)

    full = lambda shp: pl.BlockSpec(shp, lambda i, nd=len(shp): (0,) * nd)
    return pl.pallas_call(
        body, name="sgu_mid_bwd", grid=(S // tr,),
        in_specs=[pl.BlockSpec((tr, D2), lambda i: (i, 0)), pl.BlockSpec((tr, D), lambda i: (i, 0)),
                  full((1, D)), full((1, D)), full((G, LANE, LANE)), full((LANE, D))],
        out_specs=[pl.BlockSpec((tr, D2), lambda i: (i, 0)), full((G, LANE, LANE)), full((LANE, D)),
                   full((1, D)), full((1, D)), full((1, D2))],
        out_shape=[SDS((S, D2), BF16), SDS((G, LANE, LANE), F32), SDS((LANE, D), F32),
                   SDS((1, D), F32), SDS((1, D), F32), SDS((1, D2), F32)],
        compiler_params=_params(("arbitrary",)))(a, dy, ln_g, ln_b, w_mask, bias_full)


def _tri(lower):
    r = lax.broadcasted_iota(jnp.int32, (LANE, LANE), 0)
    c = lax.broadcasted_iota(jnp.int32, (LANE, LANE), 1)
    return jnp.where((c <= r) if lower else (c >= r), 1.0, 0.0).astype(F32)


def _fox_gate_fwd(fl, bf_pad):
    S = fl.shape[0]
    nblk = S // LANE

    def body(fl_ref, b_ref, f_ref):
        tri = _tri(True)

        def step(i, carry):
            r0 = pl.multiple_of(i * LANE, LANE)
            z = fl_ref[pl.ds(r0, LANE), :] + b_ref[...]
            logf = jnp.minimum(z, 0.0) - jnp.log1p(jnp.exp(-jnp.abs(z)))
            f = jnp.dot(tri, logf, preferred_element_type=F32, precision=lax.Precision.HIGHEST) + carry
            f_ref[pl.ds(r0, LANE), :] = f
            return f[LANE - 1:LANE, :]
        lax.fori_loop(0, nblk, step, jnp.zeros((1, LANE), F32))

    return pl.pallas_call(body, name="fox_gate_fwd", out_shape=SDS((S, LANE), F32),
                          compiler_params=_params())(fl, bf_pad)


def _fox_gate_bwd(fl, dF, bf_pad):
    S = fl.shape[0]
    nblk = S // LANE

    def body(fl_ref, df_ref, b_ref, dfl_ref, db_ref):
        tri = _tri(True)

        def prefix(i, carry):
            r0 = pl.multiple_of(i * LANE, LANE)
            p = jnp.dot(tri, df_ref[pl.ds(r0, LANE), :], preferred_element_type=F32,
                        precision=lax.Precision.HIGHEST) + carry
            dfl_ref[pl.ds(r0, LANE), :] = p
            return p[LANE - 1:LANE, :]
        total = lax.fori_loop(0, nblk, prefix, jnp.zeros((1, LANE), F32))
        db_ref[...] = jnp.zeros_like(db_ref)

        def finish(i, carry):
            r0 = pl.multiple_of(i * LANE, LANE)
            dlogf = total - dfl_ref[pl.ds(r0, LANE), :] + df_ref[pl.ds(r0, LANE), :]
            z = fl_ref[pl.ds(r0, LANE), :] + b_ref[...]
            dfl = dlogf / (1.0 + jnp.exp(z))
            dfl_ref[pl.ds(r0, LANE), :] = dfl
            db_ref[...] += _colsum(dfl)
            return carry
        lax.fori_loop(0, nblk, finish, 0)

    return pl.pallas_call(body, name="fox_gate_bwd", out_shape=[SDS((S, LANE), F32), SDS((1, LANE), F32)],
                          compiler_params=_params())(fl, dF, bf_pad)


def _lane_pick(blk, h):
    lane = lax.broadcasted_iota(jnp.int32, blk.shape, 1)
    return jnp.sum(jnp.where(lane == h, blk, 0.0), axis=1, keepdims=True)


_NEG = -1e30


def _attn_fwd(qkv, F_sh, F_rows, H, tq):
    S = qkv.shape[0]
    D = H * LANE
    nq = S // tq
    scale = 1.0 / math.sqrt(LANE)

    def body(q_ref, k_ref, v_ref, fsh_ref, frow_ref, o_ref, lse_ref):
        h, i = pl.program_id(0), pl.program_id(1)
        q = q_ref[...]
        fq = _lane_pick(fsh_ref[...], h)
        row = lax.broadcasted_iota(jnp.int32, (tq, tq), 0)
        col = lax.broadcasted_iota(jnp.int32, (tq, tq), 1)

        def step(j, carry):
            m, l, acc = carry
            r0 = pl.multiple_of(j * tq, tq)
            k = k_ref[pl.ds(r0, tq), :]
            v = v_ref[pl.ds(r0, tq), :]
            s = lax.dot_general(q, k, (((1,), (1,)), ((), ())), preferred_element_type=F32) * scale
            s = s + fq - frow_ref[j]
            s = jnp.where(col + (j - i) * tq <= row, s, _NEG)
            m_new = jnp.maximum(m, jnp.max(s, axis=1, keepdims=True))
            alpha = jnp.exp(m - m_new)
            p = jnp.exp(s - m_new)
            l = alpha * l + jnp.sum(p, axis=1, keepdims=True)
            acc = alpha * acc + jnp.dot(p.astype(BF16), v, preferred_element_type=F32)
            return m_new, l, acc

        m, l, acc = lax.fori_loop(0, i + 1, step, (jnp.full((tq, 1), _NEG, F32), jnp.zeros((tq, 1), F32),
                                                   jnp.zeros((tq, LANE), F32)))
        o_ref[...] = (acc / l).astype(BF16)
        lse_ref[...] = m + jnp.log(l)

    return pl.pallas_call(
        body, name="attn_fwd", grid=(H, nq),
        in_specs=[pl.BlockSpec((tq, LANE), lambda h, i: (i, h)),
                  pl.BlockSpec((S, LANE), lambda h, i: (0, H + h)),
                  pl.BlockSpec((S, LANE), lambda h, i: (0, 2 * H + h)),
                  pl.BlockSpec((tq, LANE), lambda h, i: (i, 0)),
                  pl.BlockSpec((None, nq, 1, tq), lambda h, i: (h, 0, 0, 0))],
        out_specs=[pl.BlockSpec((tq, LANE), lambda h, i: (i, h)),
                   pl.BlockSpec((None, tq, 1), lambda h, i: (h, i, 0))],
        out_shape=[SDS((S, D), BF16), SDS((H, S, 1), F32)],
        compiler_params=_params(("parallel", "arbitrary")))(qkv, qkv, qkv, F_sh, F_rows)


def _attn_bwd(qkv, do, F_sh, A_rows, delta_rows, H, tq):
    S = qkv.shape[0]
    D = H * LANE
    nq = S // tq
    scale = 1.0 / math.sqrt(LANE)

    def body(q_ref, do_ref, k_ref, v_ref, fsh_ref, a_ref, dl_ref, dq_ref, dk_ref, dv_ref, df_ref, dfr_ref,
             dq_acc, dfr_acc):
        h, j = pl.program_id(0), pl.program_id(1)

        @pl.when(j == 0)
        def _():
            dq_acc[...] = jnp.zeros_like(dq_acc)
            dfr_acc[...] = jnp.zeros_like(dfr_acc)

        k = k_ref[...]
        v = v_ref[...]
        fk = _lane_pick(fsh_ref[...], h)
        row = lax.broadcasted_iota(jnp.int32, (tq, tq), 0)
        col = lax.broadcasted_iota(jnp.int32, (tq, tq), 1)

        def step(i, carry):
            dk, dv, df = carry
            r0 = pl.multiple_of(i * tq, tq)
            q = q_ref[pl.ds(r0, tq), :]
            do_ = do_ref[pl.ds(r0, tq), :]
            st = lax.dot_general(k, q, (((1,), (1,)), ((), ())), preferred_element_type=F32) * scale
            arg = st + a_ref[i] - fk
            arg = jnp.where(row <= col + (i - j) * tq, arg, _NEG)
            pt = jnp.exp(arg)
            dv = dv + jnp.dot(pt.astype(BF16), do_, preferred_element_type=F32)
            dpt = lax.dot_general(v, do_, (((1,), (1,)), ((), ())), preferred_element_type=F32)
            dst = pt * (dpt - dl_ref[i])
            df = df + jnp.sum(dst, axis=1, keepdims=True)
            dfr_acc[i] += jnp.sum(dst, axis=0, keepdims=True)
            dsb = dst.astype(BF16)
            dk = dk + jnp.dot(dsb, q, preferred_element_type=F32)
            dq_acc[pl.ds(r0, tq), :] += lax.dot_general(dsb, k, (((0,), (0,)), ((), ())),
                                                        preferred_element_type=F32)
            return dk, dv, df

        z = jnp.zeros((tq, LANE), F32)
        dk, dv, df = lax.fori_loop(j, nq, step, (z, z, jnp.zeros((tq, 1), F32)))
        dk_ref[...] = (dk * scale).astype(BF16)
        dv_ref[...] = dv.astype(BF16)
        df_ref[...] = -df

        @pl.when(j == nq - 1)
        def _():
            dq_ref[...] = (dq_acc[...] * scale).astype(BF16)
            dfr_ref[...] = dfr_acc[...]

    return pl.pallas_call(
        body, name="attn_bwd", grid=(H, nq),
        in_specs=[pl.BlockSpec((S, LANE), lambda h, j: (0, h)),
                  pl.BlockSpec((S, LANE), lambda h, j: (0, h)),
                  pl.BlockSpec((tq, LANE), lambda h, j: (j, H + h)),
                  pl.BlockSpec((tq, LANE), lambda h, j: (j, 2 * H + h)),
                  pl.BlockSpec((tq, LANE), lambda h, j: (j, 0)),
                  pl.BlockSpec((None, nq, 1, tq), lambda h, j: (h, 0, 0, 0)),
                  pl.BlockSpec((None, nq, 1, tq), lambda h, j: (h, 0, 0, 0))],
        out_specs=[pl.BlockSpec((S, LANE), lambda h, j: (0, h)),
                   pl.BlockSpec((tq, LANE), lambda h, j: (j, h)),
                   pl.BlockSpec((tq, LANE), lambda h, j: (j, h)),
                   pl.BlockSpec((None, tq, 1), lambda h, j: (h, j, 0)),
                   pl.BlockSpec((None, nq, 1, tq), lambda h, j: (h, 0, 0, 0))],
        out_shape=[SDS((S, D), BF16), SDS((S, D), BF16), SDS((S, D), BF16), SDS((H, S, 1), F32),
                   SDS((H, nq, 1, tq), F32)],
        scratch_shapes=[pltpu.VMEM((S, LANE), F32), pltpu.VMEM((nq, 1, tq), F32)],
        compiler_params=_params(("parallel", "arbitrary")))(qkv, do, qkv, qkv, F_sh, A_rows, delta_rows)


def _head_dots(do, o, H, tr):
    def fn(r, v):
        prod = r[0].astype(F32) * r[1].astype(F32)
        lane = lax.broadcasted_iota(jnp.int32, (prod.shape[0], LANE), 1)
        out = jnp.zeros((prod.shape[0], LANE), F32)
        for h in range(H):
            s = jnp.sum(prod[:, h * LANE:(h + 1) * LANE], axis=1, keepdims=True)
            out = jnp.where(lane == h, s, out)
        return [out], []
    return _rowwise("attn_delta", fn, [do, o], [], [(LANE, F32)], [], tr)[0]


def _adamw(name, w, g, m, v):
    R, C = w.shape
    tr = _pick(R, max(8, (512 * 1024) // max(C, 1) // 8 * 8), 8)
    c1 = 1.0 - ADAM_B1 ** ADAM_STEP
    c2 = 1.0 - ADAM_B2 ** ADAM_STEP

    def body(w_ref, g_ref, m_ref, v_ref, d_ref, mo_ref, vo_ref):
        g_ = g_ref[...]
        m_ = ADAM_B1 * m_ref[...] + (1.0 - ADAM_B1) * g_
        v_ = ADAM_B2 * v_ref[...] + (1.0 - ADAM_B2) * (g_ * g_)
        d_ref[...] = -ADAM_LR * ((m_ / c1) / (jnp.sqrt(v_ / c2) + ADAM_EPS) + ADAM_WD * w_ref[...])
        mo_ref[...] = m_
        vo_ref[...] = v_

    spec = pl.BlockSpec((tr, C), lambda i: (i, 0))
    return pl.pallas_call(body, name=name, grid=(R // tr,), in_specs=[spec] * 4, out_specs=[spec] * 3,
                          out_shape=[SDS((R, C), F32)] * 3, compiler_params=_params(("parallel",)))(w, g, m, v)


def _sum_slots(name, a):
    n, R, C = a.shape
    tr = _pick(R, 256, 8)

    def body(a_ref, o_ref):
        acc = a_ref[0]
        for k in range(1, n):
            acc = acc + a_ref[k]
        o_ref[...] = acc

    return pl.pallas_call(body, name=name, grid=(R // tr,),
                          in_specs=[pl.BlockSpec((n, tr, C), lambda i: (0, i, 0))],
                          out_specs=pl.BlockSpec((tr, C), lambda i: (i, 0)), out_shape=SDS((R, C), F32),
                          compiler_params=_params(("parallel",)))(a)


def _silu_rows(name, a):
    def body(a_ref, o_ref):
        z = a_ref[...]
        o_ref[...] = (z / (1.0 + jnp.exp(-z))).astype(BF16)
    return pl.pallas_call(body, name=name, out_shape=SDS(a.shape, BF16), compiler_params=_params())(a)


def _place():
    return lax.axis_index("x"), lax.axis_index("y"), lax.axis_index("c")


def _rcopy(src, dst, ssem, rsem, dev):
    return pltpu.make_async_remote_copy(src_ref=src, dst_ref=dst, send_sem=ssem, recv_sem=rsem,
                                        device_id=dev, device_id_type=MESH)


def _gather_devices(name, buf):
    R, C = buf.shape

    def body(b_ref, out_ref, ssem, rsem, lsem):
        x, y, c = _place()
        me = 4 * x + 2 * y + c
        mine = pltpu.make_async_copy(b_ref, out_ref.at[me], lsem)
        mine.start()
        peers = []
        for k in range(1, N_DEV):
            fx, fy, fc = (k >> 2) & 1, (k >> 1) & 1, k & 1
            peers.append((x ^ fx, y ^ fy, c ^ fc))
        sends = [_rcopy(b_ref, out_ref.at[me], ssem.at[k], rsem.at[k], p) for k, p in enumerate(peers)]
        for cp in sends:
            cp.start()
        for k, (px, py, pc) in enumerate(peers):
            _rcopy(b_ref, out_ref.at[4 * px + 2 * py + pc], ssem.at[k], rsem.at[k], (px, py, pc)).wait_recv()
        for cp in sends:
            cp.wait_send()
        mine.wait()

    return pl.pallas_call(body, name=name, in_specs=[ANY], out_specs=ANY, out_shape=SDS((N_DEV, R, C), buf.dtype),
                          scratch_shapes=[pltpu.SemaphoreType.DMA((N_DEV - 1,)), pltpu.SemaphoreType.DMA((N_DEV - 1,)),
                                          pltpu.SemaphoreType.DMA])(buf)


def _other_chips(x, y):
    return [(1 - x, y), (x, 1 - y), (1 - x, 1 - y)]


def _gather_chips(name, w):
    R, C = w.shape
    hR = R // 2
    assert hR * 2 == R

    def body(w_ref, out_ref, ssem, rsem, lsem):
        x, y, c = _place()
        sib = (x, y, 1 - c)
        chips = _other_chips(x, y)

        def half(cx, cy, hf):
            return out_ref.at[2 * cx + cy, pl.ds(hf * hR, hR), :]

        mine = pltpu.make_async_copy(w_ref, out_ref.at[2 * x + y], lsem)
        mine.start()
        my_half = w_ref.at[pl.ds(c * hR, hR), :]
        sends = [_rcopy(my_half, half(x, y, c), ssem.at[j], rsem.at[j], (cx, cy, c))
                 for j, (cx, cy) in enumerate(chips)]
        for cp in sends:
            cp.start()
        passed = [_rcopy(half(cx, cy, c), half(cx, cy, c), ssem.at[3 + j], rsem.at[3 + j], sib)
                  for j, (cx, cy) in enumerate(chips)]
        for j, (cx, cy) in enumerate(chips):
            _rcopy(my_half, half(cx, cy, c), ssem.at[j], rsem.at[j], (cx, cy, c)).wait_recv()
            passed[j].start()
        for j, (cx, cy) in enumerate(chips):
            _rcopy(half(cx, cy, 1 - c), half(cx, cy, 1 - c), ssem.at[3 + j], rsem.at[3 + j], sib).wait_recv()
        for cp in sends + passed:
            cp.wait_send()
        mine.wait()

    return pl.pallas_call(body, name=name, in_specs=[ANY], out_specs=ANY, out_shape=SDS((N_CHIPS, R, C), w.dtype),
                          scratch_shapes=[pltpu.SemaphoreType.DMA((6,)), pltpu.SemaphoreType.DMA((6,)),
                                          pltpu.SemaphoreType.DMA])(w)


def _add_halves(name, g4, recv, c_idx):
    _, R, C = g4.shape
    hR = R // 2
    tr = _pick(hR, max(8, (1024 * 1024) // C // 8 * 8), 8)
    nb = hR // tr

    def body(c_ref, a_ref, b_ref, o_ref):
        o_ref[...] = a_ref[...] + b_ref[...]

    gs = pltpu.PrefetchScalarGridSpec(
        num_scalar_prefetch=1, grid=(N_CHIPS, nb),
        in_specs=[pl.BlockSpec((None, tr, C), lambda k, i, c_ref: (k, c_ref[0] * nb + i, 0)),
                  pl.BlockSpec((None, tr, C), lambda k, i, c_ref: (k, i, 0))],
        out_specs=pl.BlockSpec((None, tr, C), lambda k, i, c_ref: (k, i, 0)))
    return pl.pallas_call(body, name=name, grid_spec=gs, out_shape=SDS((N_CHIPS, hR, C), F32),
                          compiler_params=_params(("parallel", "parallel")))(c_idx, g4, recv)


def _add_chips(name, p4, recv3, chip_idx):
    _, hR, C = p4.shape
    tr = _pick(hR, max(8, (1024 * 1024) // C // 8 * 8), 8)

    def body(k_ref, a_ref, b_ref, o_ref):
        o_ref[...] = ((a_ref[...] + b_ref[0]) + b_ref[1]) + b_ref[2]

    gs = pltpu.PrefetchScalarGridSpec(
        num_scalar_prefetch=1, grid=(hR // tr,),
        in_specs=[pl.BlockSpec((None, tr, C), lambda i, k_ref: (k_ref[0], i, 0)),
                  pl.BlockSpec((3, tr, C), lambda i, k_ref: (0, i, 0))],
        out_specs=pl.BlockSpec((tr, C), lambda i, k_ref: (i, 0)))
    return pl.pallas_call(body, name=name, grid_spec=gs, out_shape=SDS((hR, C), F32),
                          compiler_params=_params(("parallel",)))(chip_idx, p4, recv3)


def _reduce_scatter(tag, g4):
    _, R, C = g4.shape
    hR = R // 2
    assert hR * 2 == R

    def swap_body(g_ref, recv_ref, ssem, rsem):
        x, y, c = _place()
        cp = _rcopy(g_ref.at[:, pl.ds((1 - c) * hR, hR), :], recv_ref, ssem, rsem, (x, y, 1 - c))
        cp.start()
        cp.wait()

    recv = pl.pallas_call(swap_body, name=tag + "_rs_pair", in_specs=[ANY], out_specs=ANY,
                          out_shape=SDS((N_CHIPS, hR, C), F32),
                          scratch_shapes=[pltpu.SemaphoreType.DMA, pltpu.SemaphoreType.DMA])(g4)
    x, y, c = _place()
    c_idx = c.astype(jnp.int32).reshape(1)
    chip_idx = (2 * x + y).astype(jnp.int32).reshape(1)
    p4 = _add_halves(tag + "_rs_add2", g4, recv, c_idx)

    def chips_body(p_ref, recv_ref, ssem, rsem):
        x, y, c = _place()
        chips = _other_chips(x, y)
        sends = [_rcopy(p_ref.at[2 * cx + cy], recv_ref.at[j], ssem.at[j], rsem.at[j], (cx, cy, c))
                 for j, (cx, cy) in enumerate(chips)]
        for cp in sends:
            cp.start()
        for cp in sends:
            cp.wait()

    recv3 = pl.pallas_call(chips_body, name=tag + "_rs_chips", in_specs=[ANY], out_specs=ANY,
                           out_shape=SDS((3, hR, C), F32),
                           scratch_shapes=[pltpu.SemaphoreType.DMA((3,)), pltpu.SemaphoreType.DMA((3,))])(p4)
    r = _add_chips(tag + "_rs_add4", p4, recv3, chip_idx)

    def join_body(r_ref, out_ref, ssem, rsem, lsem):
        x, y, c = _place()
        dst = out_ref.at[pl.ds(c * hR, hR), :]
        mine = pltpu.make_async_copy(r_ref, dst, lsem)
        mine.start()
        cp = _rcopy(r_ref, dst, ssem, rsem, (x, y, 1 - c))
        cp.start()
        cp.wait_send()
        _rcopy(r_ref, out_ref.at[pl.ds((1 - c) * hR, hR), :], ssem, rsem, (x, y, 1 - c)).wait_recv()
        mine.wait()

    return pl.pallas_call(join_body, name=tag + "_rs_join", in_specs=[ANY], out_specs=ANY,
                          out_shape=SDS((R, C), F32),
                          scratch_shapes=[pltpu.SemaphoreType.DMA, pltpu.SemaphoreType.DMA,
                                          pltpu.SemaphoreType.DMA])(r)


def _resid_epilogue(accs, ex):
    return [ex[0] + ex[1] * accs[0], accs[0]]


def _swiglu_epilogue(accs, ex):
    gt, up = accs
    return [gt, up, gt / (1.0 + jnp.exp(-gt)) * up]


def _swiglu_bwd_epilogue(accs, ex):
    dact = accs[0]
    gt, up = ex[0].astype(F32), ex[1].astype(F32)
    sg = 1.0 / (1.0 + jnp.exp(-gt))
    return [dact * up * (sg * (1.0 + gt * (1.0 - sg))), dact * (gt * sg)]


def kernel(x, c, ada_w, ada_b, norm_mix_g, norm_ffn_g, a_w_in, a_b_in, a_ln_g, a_ln_b, a_w_s, a_b_s, a_w_out, b_w_in, b_b_f, b_w_out, ffn_w_gate, ffn_w_up, ffn_w_down, final_g, loss_target, m_ada_w, m_ada_b, m_norm_mix_g, m_norm_ffn_g, m_a_w_in, m_a_b_in, m_a_ln_g, m_a_ln_b, m_a_w_s, m_a_b_s, m_a_w_out, m_b_w_in, m_b_b_f, m_b_w_out, m_ffn_w_gate, m_ffn_w_up, m_ffn_w_down, m_final_g, v_ada_w, v_ada_b, v_norm_mix_g, v_norm_ffn_g, v_a_w_in, v_a_b_in, v_a_ln_g, v_a_ln_b, v_a_w_s, v_a_b_s, v_a_w_out, v_b_w_in, v_b_b_f, v_b_w_out, v_ffn_w_gate, v_ffn_w_up, v_ffn_w_down, v_final_g):
    S, D = x.shape[1], x.shape[2]
    H = D // LANE
    G = D // LANE
    FH = ffn_w_down.shape[1] * N_CHIPS
    depth = ada_w.shape[0]
    assert depth == 2 and a_w_in.shape[0] == 1 and b_w_in.shape[0] == 1
    mx, my, mc = _place()
    chip = 2 * mx + my
    dev = 4 * mx + 2 * my + mc
    x0 = x[0]
    target = loss_target[0]
    tr = _pick(S, 256, 8)
    tm = _pick(S, 512, 8)
    tq = _pick(S, 512, LANE)
    nq = S // tq

    wa_in4 = _gather_chips("ag_a_w_in", a_w_in[0].astype(BF16))
    wa_out = _gather_chips("ag_a_w_out", a_w_out[0].astype(BF16)).reshape(D, D)
    wb_in4 = _gather_chips("ag_b_w_in", b_w_in[0].astype(BF16))
    wb_out = _gather_chips("ag_b_w_out", b_w_out[0].astype(BF16)).reshape(D, D)
    wg4 = [_gather_chips(f"ag_w_gate{l}", ffn_w_gate[l].astype(BF16)) for l in range(depth)]
    wu4 = [_gather_chips(f"ag_w_up{l}", ffn_w_up[l].astype(BF16)) for l in range(depth)]
    wd = [_gather_chips(f"ag_w_down{l}", ffn_w_down[l].astype(BF16)).reshape(FH, D) for l in range(depth)]
    wb_in = jnp.concatenate([wb_in4[k] for k in range(N_CHIPS)], axis=1)
    w_qkv = wb_in[:, :3 * D]
    w_f = jnp.pad(wb_in[:, 3 * D:], ((0, 0), (0, LANE - H)))

    c_all = _gather_devices("ag_c", jnp.pad(c, ((0, 7), (0, 0)))).reshape(N_DEV, 8, D)[:, 0, :]
    c_act = _silu_rows("c_silu", c_all)
    n_loc = ada_w.shape[2]
    mods = []
    for l in range(depth):
        b_loc = lax.dynamic_slice_in_dim(ada_b[l:l + 1], chip * n_loc, n_loc, axis=1)
        mods.append(_mm(f"ada_fwd{l}", "nn", [c_act], [ada_w[l]], M=N_DEV, N=n_loc, K=D, tm=N_DEV,
                        tn=_pick(n_loc, 1024, LANE), tk=D, extras=[("row", b_loc)],
                        epilogue=lambda accs, ex: [accs[0] + ex[0]])[0])
    mod_all = _gather_devices("ag_mod", jnp.concatenate(mods, axis=1))
    mod_mine = lax.dynamic_index_in_dim(mod_all[0::2], dev, axis=1, keepdims=False)
    mod_mine = mod_mine.reshape(N_CHIPS, depth, n_loc).transpose(1, 0, 2).reshape(depth, 1, N_MOD * D)
    mod = [[mod_mine[l, :, i * D:(i + 1) * D] for i in range(N_MOD)] for l in range(depth)]

    row = lambda a: a.reshape(1, -1)

    tril = jnp.tril(jnp.ones((LANE, LANE), dtype=bool))
    w_mask = jnp.where(tril[None], a_w_s[0], 0.0).astype(BF16)
    bias_full = jnp.repeat(a_b_s[0].T, LANE, axis=1)
    bf_pad = jnp.pad(b_b_f, ((0, 0), (0, LANE - H)))

    saved = []
    xs = x0
    for l in range(depth):
        sh1, sc1, g1, sh2, sc2, g2 = mod[l]
        st = {"x_in": xs}
        h1 = _normmod_fwd(f"normmod_mix{l}", xs, row(norm_mix_g[l]), sc1, sh1, tr)
        st["h1"] = h1
        if l == 0:
            a = _mm("sgu_in", "nn", [h1], [wa_in4], M=S, N=2 * D, K=D, tm=tm, tn=2 * D // N_CHIPS, tk=D,
                    b_stacked=True, extras=[("row", a_b_in)],
                    epilogue=lambda accs, ex: [accs[0] + ex[0]])[0]
            y = _sgu_mid_fwd(a, a_ln_g, a_ln_b, w_mask, bias_full, tr)
            st["a"], st["y"] = a, y
            w_o, mix_out = wa_out, y
        else:
            qkv = _mm("fox_qkv", "nn", [h1], [w_qkv], M=S, N=3 * D, K=D, tm=tm, tn=_pick(3 * D, 1024, LANE),
                      tk=D, out_dtypes=(BF16,))[0]
            fl = _mm("fox_f", "nn", [h1], [w_f], M=S, N=LANE, K=D, tm=tm, tn=LANE, tk=D)[0]
            F_sh = _fox_gate_fwd(fl, bf_pad)
            F_rows = F_sh[:, :H].T.reshape(H, nq, 1, tq)
            o, lse = _attn_fwd(qkv, F_sh, F_rows, H, tq)
            st.update(qkv=qkv, fl=fl, F_sh=F_sh, F_rows=F_rows, o=o, lse=lse)
            w_o, mix_out = wb_out, o
        x1, out1 = _mm(f"mix_out{l}", "nn", [mix_out], [w_o], M=S, N=D, K=D, tm=tm, tn=_pick(D, 1024, LANE),
                       tk=D, extras=[("tile", xs), ("row", g1)], out_dtypes=(F32, BF16),
                       epilogue=_resid_epilogue)
        st["x_mid"], st["out1"] = x1, out1
        h2 = _normmod_fwd(f"normmod_ffn{l}", x1, row(norm_ffn_g[l]), sc2, sh2, tr)
        gt, up, act = _mm(f"ffn_in{l}", "nn", [h2, h2], [wg4[l], wu4[l]], M=S, N=FH, K=D, tm=tm,
                          tn=FH // N_CHIPS, tk=_pick(D, 1024, LANE), b_stacked=True, acc_of=[0, 1], n_acc=2,
                          out_dtypes=(BF16, BF16, BF16), epilogue=_swiglu_epilogue)
        x2, out2 = _mm(f"ffn_out{l}", "nn", [act], [wd[l]], M=S, N=D, K=FH, tm=tm, tn=_pick(D, 1024, LANE),
                       tk=FH // N_CHIPS, extras=[("tile", x1), ("row", g2)], out_dtypes=(F32, BF16),
                       epilogue=_resid_epilogue)
        st.update(h2=h2, gt=gt, up=up, act=act, out2=out2)
        saved.append(st)
        xs = x2

    dx, loss_vec, g_final = _loss_head("loss_head", xs, target, row(final_g), tr)
    loss = lax.psum(loss_vec[0, 0], ("x", "y", "c"))

    gw = {}
    dmods = [None] * depth
    gmix = [None] * depth
    gffn = [None] * depth
    for l in reversed(range(depth)):
        sh1, sc1, g1, sh2, sc2, g2 = mod[l]
        st = saved[l]
        dog, dg2 = _gate_bwd(f"gate_ffn{l}", dx, st["out2"], g2, tr)
        dgt, dup = _mm(f"ffn_dact{l}", "nt", [dog], [wd[l]], M=S, N=FH, K=D, tm=tm, tn=FH // N_CHIPS, tk=D,
                       extras=[("tile", st["gt"]), ("tile", st["up"])], out_dtypes=(BF16, BF16),
                       epilogue=_swiglu_bwd_epilogue)
        gw[f"wd{l}"] = _mm(f"ffn_dwd{l}", "tn", [st["act"]], [dog], M=FH, N=D, K=S, tm=FH // N_CHIPS,
                           tn=_pick(D, 1024, LANE), tk=_pick(S, 512, LANE))[0].reshape(N_CHIPS, FH // N_CHIPS, D)
        dh2 = _mm(f"ffn_dh{l}", "nt", [dgt, dup], [wg4[l], wu4[l]], M=S, N=D, K=FH, tm=tm,
                  tn=_pick(D, 1024, LANE), tk=FH // N_CHIPS, b_stacked=True)[0]
        gw[f"wg{l}"] = _mm(f"ffn_dwg{l}", "tn", [st["h2"]], [dgt], M=D, N=FH, K=S, tm=_pick(D, 512, LANE),
                           tn=FH // N_CHIPS, tk=_pick(S, 1024, LANE), out_stacked=True)[0]
        gw[f"wu{l}"] = _mm(f"ffn_dwu{l}", "tn", [st["h2"]], [dup], M=D, N=FH, K=S, tm=_pick(D, 512, LANE),
                           tn=FH // N_CHIPS, tk=_pick(S, 1024, LANE), out_stacked=True)[0]
        dx, dsh2, dsc2, gffn[l] = _normmod_bwd(f"normmod_ffn_bwd{l}", st["x_mid"], dh2, dx,
                                               row(norm_ffn_g[l]), sc2, tr)
        dog, dg1 = _gate_bwd(f"gate_mix{l}", dx, st["out1"], g1, tr)
        if l == 0:
            dy = _mm("sgu_dy", "nt", [dog], [wa_out], M=S, N=D, K=D, tm=tm, tn=_pick(D, 1024, LANE), tk=D,
                     out_dtypes=(BF16,))[0]
            gw["a_w_out"] = _mm("sgu_dwout", "tn", [st["y"]], [dog], M=D, N=D, K=S, tm=_pick(D, 512, LANE),
                                tn=_pick(D, 1024, LANE), tk=_pick(S, 1024, LANE))[0].reshape(N_CHIPS, D // N_CHIPS, D)
            da, dws, dbias, g_ln_g, g_ln_b, g_b_in = _sgu_mid_bwd(st["a"], dy, a_ln_g, a_ln_b, w_mask, bias_full, tr)
            dh1 = _mm("sgu_dh", "nt", [da], [wa_in4], M=S, N=D, K=2 * D, tm=tm, tn=_pick(D, 1024, LANE),
                      tk=2 * D // N_CHIPS, b_stacked=True)[0]
            gw["a_w_in"] = _mm("sgu_dwin", "tn", [st["h1"]], [da], M=D, N=2 * D, K=S, tm=_pick(D, 512, LANE),
                               tn=2 * D // N_CHIPS, tk=_pick(S, 1024, LANE), out_stacked=True)[0]
            g_w_s = jnp.where(tril[None], dws, 0.0)
            g_b_s = jnp.sum(dbias.reshape(LANE, G, LANE), axis=2).T
        else:
            do = _mm("fox_do", "nt", [dog], [wb_out], M=S, N=D, K=D, tm=tm, tn=_pick(D, 1024, LANE), tk=D,
                     out_dtypes=(BF16,))[0]
            gw["b_w_out"] = _mm("fox_dwout", "tn", [st["o"]], [dog], M=D, N=D, K=S, tm=_pick(D, 512, LANE),
                                tn=_pick(D, 1024, LANE), tk=_pick(S, 1024, LANE))[0].reshape(N_CHIPS, D // N_CHIPS, D)
            delta = _head_dots(do, st["o"], H, tr)
            delta_rows = delta[:, :H].T.reshape(H, nq, 1, tq)
            A_rows = st["F_rows"] - st["lse"].reshape(H, nq, 1, tq)
            dq, dk, dv, dF_k, dF_q = _attn_bwd(st["qkv"], do, st["F_sh"], A_rows, delta_rows, H, tq)
            dF_sh = jnp.pad((dF_k.reshape(H, S) + dF_q.reshape(H, S)).T, ((0, 0), (0, LANE - H)))
            dfl, db_f = _fox_gate_bwd(st["fl"], dF_sh, bf_pad)
            dfl_b = dfl.astype(BF16)
            dh_f = _mm("fox_dh_f", "nt", [dfl_b], [w_f], M=S, N=D, K=LANE, tm=tm, tn=_pick(D, 1024, LANE),
                       tk=LANE)[0]
            tkq = _pick(D, 1024, LANE)
            dh1 = _mm("fox_dh", "nt", [dq, dk, dv], [w_qkv, w_qkv, w_qkv], M=S, N=D, K=D, tm=tm,
                      tn=_pick(D, 1024, LANE), tk=tkq, b_koffs=[0, D // tkq, 2 * D // tkq],
                      extras=[("tile", dh_f)], epilogue=lambda accs, ex: [accs[0] + ex[0]])[0]
            parts = [_mm(f"fox_dw{nm}", "tn", [st["h1"]], [d_], M=D, N=D, K=S, tm=_pick(D, 512, LANE),
                         tn=_pick(D, 1024, LANE), tk=_pick(S, 1024, LANE))[0]
                     for nm, d_ in (("q", dq), ("k", dk), ("v", dv))]
            dwf = _mm("fox_dwf", "tn", [st["h1"]], [dfl_b], M=D, N=LANE, K=S, tm=_pick(D, 512, LANE), tn=LANE,
                      tk=_pick(S, 1024, LANE))[0]
            g_b_w_in = jnp.concatenate(parts + [dwf[:, :H]], axis=1)
            n_b = g_b_w_in.shape[1] // N_CHIPS
            gw["b_w_in"] = jnp.stack([g_b_w_in[:, k * n_b:(k + 1) * n_b] for k in range(N_CHIPS)])
            g_b_f = db_f[:, :H]
        dx, dsh1, dsc1, gmix[l] = _normmod_bwd(f"normmod_mix_bwd{l}", st["x_in"], dh1, dx,
                                               row(norm_mix_g[l]), sc1, tr)
        dmods[l] = jnp.concatenate([dsh1, dsc1, dg1, dsh2, dsc2, dg2], axis=1)
    grad_x = dx[None]

    small = [jnp.concatenate(dmods, axis=0), jnp.concatenate(gmix, axis=0), jnp.concatenate(gffn, axis=0),
             g_b_in, g_ln_g, g_ln_b, g_w_s[None], g_b_s[None], g_b_f, g_final[0]]
    small_w = [ada_b, norm_mix_g, norm_ffn_g, a_b_in, a_ln_g, a_ln_b, a_w_s, a_b_s, b_b_f, final_g]
    small_m = [m_ada_b, m_norm_mix_g, m_norm_ffn_g, m_a_b_in, m_a_ln_g, m_a_ln_b, m_a_w_s, m_a_b_s, m_b_b_f, m_final_g]
    small_v = [v_ada_b, v_norm_mix_g, v_norm_ffn_g, v_a_b_in, v_a_ln_g, v_a_ln_b, v_a_w_s, v_a_b_s, v_b_b_f, v_final_g]
    sizes = [w.size for w in small_w]
    total = sum(sizes)
    padded = -(-total // (8 * LANE)) * (8 * LANE)

    def pack(parts):
        flat = jnp.concatenate([p.reshape(-1) for p in parts])
        return jnp.pad(flat, (0, padded - total)).reshape(padded // LANE, LANE)

    def unpack(buf):
        flat = buf.reshape(-1)
        outs, off = [], 0
        for w, n in zip(small_w, sizes):
            outs.append(flat[off:off + n].reshape(w.shape))
            off += n
        return outs

    g_all = _gather_devices("ag_small_grads", pack(small))
    g_small = _sum_slots("sum_small_grads", g_all)
    d_small, m_small, v_small = _adamw("adamw_small", pack(small_w), g_small, pack(small_m), pack(small_v))
    sg, sd, sm, sv_ = unpack(g_small), unpack(d_small), unpack(m_small), unpack(v_small)

    n_dm = depth * N_MOD * D
    dmod_all = g_all.reshape(N_DEV, -1)[:, :n_dm].reshape(N_DEV, depth, N_MOD * D)
    g_ada = []
    for l in range(depth):
        dm_loc = lax.dynamic_slice_in_dim(dmod_all[:, l, :], chip * n_loc, n_loc, axis=1).astype(BF16)
        g_ada.append(_mm(f"ada_dw{l}", "tn", [c_act], [dm_loc], M=D, N=n_loc, K=N_DEV, tm=_pick(D, 512, LANE),
                         tn=_pick(n_loc, 1024, LANE), tk=N_DEV)[0])
    g_ada_w = jnp.stack(g_ada)

    def update(tag, g4, w, m, v):
        shp = w.shape
        g = _reduce_scatter(tag, g4)
        w2 = w.reshape(g.shape)
        d_, m_, v_ = _adamw("adamw_" + tag, w2, g, m.reshape(g.shape), v.reshape(g.shape))
        return [t.reshape(shp) for t in (g, d_, m_, v_)]

    def update2(tag, keys, w, m, v):
        res = [update(f"{tag}{l}", gw[k], w[l], m[l], v[l]) for l, k in enumerate(keys)]
        return [jnp.stack([r[i] for r in res]) for i in range(4)]

    u_a_in = update("a_w_in", gw["a_w_in"], a_w_in, m_a_w_in, v_a_w_in)
    u_a_out = update("a_w_out", gw["a_w_out"], a_w_out, m_a_w_out, v_a_w_out)
    u_b_in = update("b_w_in", gw["b_w_in"], b_w_in, m_b_w_in, v_b_w_in)
    u_b_out = update("b_w_out", gw["b_w_out"], b_w_out, m_b_w_out, v_b_w_out)
    u_wg = update2("wg", ["wg0", "wg1"], ffn_w_gate, m_ffn_w_gate, v_ffn_w_gate)
    u_wu = update2("wu", ["wu0", "wu1"], ffn_w_up, m_ffn_w_up, v_ffn_w_up)
    u_wd = update2("wd", ["wd0", "wd1"], ffn_w_down, m_ffn_w_down, v_ffn_w_down)
    flat2 = lambda a: a.reshape(-1, a.shape[-1])
    ada_upd = _adamw("adamw_ada_w", flat2(ada_w), flat2(g_ada_w), flat2(m_ada_w), flat2(v_ada_w))
    u_ada = [g_ada_w] + [t.reshape(ada_w.shape) for t in ada_upd]

    def leaves(i, small_list):
        s = small_list
        return [u_ada[i], s[0], s[1], s[2], u_a_in[i], s[3], s[4], s[5], s[6], s[7], u_a_out[i],
                u_b_in[i], s[8], u_b_out[i], u_wg[i], u_wu[i], u_wd[i], s[9]]

    return (loss, grad_x, *leaves(0, sg), *leaves(1, sd), *leaves(2, sm), *leaves(3, sv_))
```

```python
import functools
import math

import jax
import jax.numpy as jnp
from jax import lax
from jax.experimental import pallas as pl
from jax.experimental.pallas import tpu as pltpu

F32, BF16 = jnp.float32, jnp.bfloat16
LANE = 128
N_CHIPS = 4
N_DEV = 8
N_MOD = 6
EPS = 1e-6
VMEM_LIMIT = 60 * 1024 * 1024
ADAM_LR, ADAM_B1, ADAM_B2, ADAM_EPS, ADAM_WD, ADAM_STEP = 0.001, 0.9, 0.999, 1e-08, 0.01, 10
MESH = pl.DeviceIdType.MESH
ANY = pl.BlockSpec(memory_space=pl.ANY)
SDS = jax.ShapeDtypeStruct


def _pick(dim, pref, align):
    t = min(dim, pref)
    t -= t % align
    while t >= align:
        if dim % t == 0:
            return t
        t -= align
    return dim


def _params(sem=None):
    return pltpu.CompilerParams(dimension_semantics=sem, vmem_limit_bytes=VMEM_LIMIT)


class _Job:
    def __init__(self, ins, outs, aliases, nsem, start, wait):
        self.ins, self.outs, self.aliases, self.nsem, self.start, self.wait = ins, outs, aliases, nsem, start, wait


def _call(body, name, *, grid, in_specs, out_specs, out_shape, args, scratch=(), sem=None, comm=None):
    jobs, absorb = comm if comm else ([], None)
    n_in, n_out, n_scr = len(args), len(out_shape), len(scratch)
    c_ins = [a for j in jobs for a in j.ins]
    c_outs = [s for j in jobs for s in j.outs]
    aliases, c_scr = {}, []
    i_off = o_off = 0
    for j in jobs:
        for a, b in j.aliases.items():
            aliases[n_in + i_off + a] = n_out + o_off + b
        i_off += len(j.ins)
        o_off += len(j.outs)
        c_scr += [pltpu.SemaphoreType.DMA((j.nsem,)), pltpu.SemaphoreType.DMA((j.nsem,))]

    def wrapped(*refs):
        bounds = [n_in, len(c_ins), n_out, len(c_outs), n_scr]
        parts, p = [], 0
        for n in bounds:
            parts.append(refs[p:p + n])
            p += n
        main_in, cin, main_out, cout, main_scr = parts
        csem = refs[p:]

        def run(phase):
            a = b = 0
            for k, j in enumerate(jobs):
                fn = j.start if phase == 0 else j.wait
                fn(cin[a:a + len(j.ins)], cout[b:b + len(j.outs)], csem[2 * k], csem[2 * k + 1])
                a += len(j.ins)
                b += len(j.outs)

        if jobs and grid:
            first = functools.reduce(jnp.logical_and, [pl.program_id(d) == 0 for d in range(len(grid))])
            last = functools.reduce(jnp.logical_and, [pl.program_id(d) == grid[d] - 1 for d in range(len(grid))])
            pl.when(first)(lambda: run(0))
            body(*main_in, *main_out, *main_scr)
            pl.when(last)(lambda: run(1))
        elif jobs:
            run(0)
            body(*main_in, *main_out, *main_scr)
            run(1)
        else:
            body(*main_in, *main_out, *main_scr)

    outs = pl.pallas_call(
        wrapped, name=name, grid=grid, in_specs=list(in_specs) + [ANY] * len(c_ins),
        out_specs=list(out_specs) + [ANY] * len(c_outs), out_shape=list(out_shape) + c_outs,
        scratch_shapes=list(scratch) + c_scr, input_output_aliases=aliases,
        compiler_params=_params(("arbitrary",) * len(grid) if jobs else sem))(*args, *c_ins)
    if jobs:
        absorb(list(outs[n_out:]))
    return list(outs[:n_out])


def _comm_call(name, comm):
    _call(lambda: None, name, grid=(), in_specs=[], out_specs=[], out_shape=[], args=[], comm=comm)


def _mm(name, form, a_list, b_list, *, M, N, K, tm, tn, tk, b_stacked=False, out_stacked=False,
        b_koffs=None, acc_of=None, n_acc=1, extras=(), out_dtypes=(F32,), epilogue=None, comm=None):
    assert M % tm == 0 and N % tn == 0 and K % tk == 0, (name, M, N, K, tm, tn, tk)
    nm, nn, nk = M // tm, N // tn, K // tk
    npairs = len(a_list)
    acc_of = acc_of or [0] * npairs
    b_koffs = b_koffs or [0] * npairs
    if epilogue is None:
        epilogue = lambda accs, ex: [accs[0]]

    in_specs = []
    for _ in a_list:
        if form == "tn":
            in_specs.append(pl.BlockSpec((tk, tm), lambda j, i, k: (k, i)))
        else:
            in_specs.append(pl.BlockSpec((tm, tk), lambda j, i, k: (i, k)))
    for off in b_koffs:
        if form == "nn":
            if b_stacked:
                assert tn * N_CHIPS == N
                in_specs.append(pl.BlockSpec((None, tk, tn), lambda j, i, k: (j, k, 0)))
            else:
                in_specs.append(pl.BlockSpec((tk, tn), lambda j, i, k: (k, j)))
        elif form == "nt":
            if b_stacked:
                assert tk * N_CHIPS == K
                in_specs.append(pl.BlockSpec((None, tn, tk), lambda j, i, k: (k, j, 0)))
            else:
                in_specs.append(pl.BlockSpec((tn, tk), lambda j, i, k, off=off: (j, off + k)))
        else:
            in_specs.append(pl.BlockSpec((tk, tn), lambda j, i, k: (k, j)))
    for kind, _ in extras:
        if kind == "tile":
            in_specs.append(pl.BlockSpec((tm, tn), lambda j, i, k: (i, j)))
        else:
            in_specs.append(pl.BlockSpec((1, tn), lambda j, i, k: (0, j)))
    if out_stacked:
        assert tn * N_CHIPS == N
        out_specs = [pl.BlockSpec((None, tm, tn), lambda j, i, k: (j, i, 0)) for _ in out_dtypes]
        out_shape = [SDS((N_CHIPS, M, tn), d) for d in out_dtypes]
    else:
        out_specs = [pl.BlockSpec((tm, tn), lambda j, i, k: (i, j)) for _ in out_dtypes]
        out_shape = [SDS((M, N), d) for d in out_dtypes]
    dims = {"nn": (((1,), (0,)), ((), ())), "nt": (((1,), (1,)), ((), ())), "tn": (((0,), (0,)), ((), ()))}[form]
    n_ex, n_out = len(extras), len(out_dtypes)

    def body(*refs):
        a_refs = refs[:npairs]
        b_refs = refs[npairs:2 * npairs]
        e_refs = refs[2 * npairs:2 * npairs + n_ex]
        o_refs = refs[2 * npairs + n_ex:2 * npairs + n_ex + n_out]
        acc_refs = refs[2 * npairs + n_ex + n_out:]

        tot = [None] * n_acc
        for p in range(npairs):
            d = lax.dot_general(a_refs[p][...].astype(BF16), b_refs[p][...].astype(BF16), dims,
                                preferred_element_type=F32)
            tot[acc_of[p]] = d if tot[acc_of[p]] is None else tot[acc_of[p]] + d

        def finish(accs):
            outs = epilogue(accs, [e[...] for e in e_refs])
            for o_ref, o in zip(o_refs, outs):
                o_ref[...] = o.astype(o_ref.dtype)

        if nk == 1:
            finish(tot)
        else:
            k = pl.program_id(2)

            @pl.when(k == 0)
            def _():
                for r, t in zip(acc_refs, tot):
                    r[...] = t

            @pl.when(k > 0)
            def _():
                for r, t in zip(acc_refs, tot):
                    r[...] += t

            @pl.when(k == nk - 1)
            def _():
                finish([r[...] for r in acc_refs])

    scratch = [pltpu.VMEM((tm, tn), F32) for _ in range(n_acc)] if nk > 1 else []
    return _call(body, name, grid=(nn, nm, nk), in_specs=in_specs, out_specs=out_specs, out_shape=out_shape,
                 scratch=scratch, sem=("parallel", "parallel", "arbitrary"), comm=comm,
                 args=[*a_list, *b_list, *[e for _, e in extras]])


def _rowwise(name, fn, rows, vecs, row_outs, acc_widths, tr, comm=None):
    S = rows[0].shape[0]
    assert S % tr == 0
    nr, nv, no, na = len(rows), len(vecs), len(row_outs), len(acc_widths)

    def body(*refs):
        r = [x[...] for x in refs[:nr]]
        v = [x[...] for x in refs[nr:nr + nv]]
        o_refs = refs[nr + nv:nr + nv + no]
        a_refs = refs[nr + nv + no:]
        outs, accs = fn(r, v)
        for o_ref, o in zip(o_refs, outs):
            o_ref[...] = o.astype(o_ref.dtype)
        first = pl.program_id(0) == 0

        @pl.when(first)
        def _():
            for a_ref, a in zip(a_refs, accs):
                a_ref[...] = a

        @pl.when(jnp.logical_not(first))
        def _():
            for a_ref, a in zip(a_refs, accs):
                a_ref[...] += a

    in_specs = [pl.BlockSpec((tr, x.shape[1]), lambda i: (i, 0)) for x in rows]
    in_specs += [pl.BlockSpec(x.shape, lambda i, nd=x.ndim: (0,) * nd) for x in vecs]
    out_specs = [pl.BlockSpec((tr, w), lambda i: (i, 0)) for w, _ in row_outs]
    out_specs += [pl.BlockSpec((1, w), lambda i: (0, 0)) for w in acc_widths]
    out_shape = [SDS((S, w), d) for w, d in row_outs] + [SDS((1, w), F32) for w in acc_widths]
    return _call(body, name, grid=(S // tr,), in_specs=in_specs, out_specs=out_specs, out_shape=out_shape,
                 sem=("arbitrary",), comm=comm, args=[*rows, *vecs])


def _colsum(a):
    return jnp.sum(a, axis=0, keepdims=True)


def _rms_stats(x):
    rstd = lax.rsqrt(jnp.mean(x * x, axis=1, keepdims=True) + EPS)
    return x * rstd, rstd


def _normmod_fwd(name, x, g, sc, sh, tr):
    def fn(r, v):
        n, _ = _rms_stats(r[0])
        return [(n * v[0]) * (1.0 + v[1]) + v[2]], []
    return _rowwise(name, fn, [x], [g, sc, sh], [(x.shape[1], BF16)], [], tr)[0]


def _normmod_bwd(name, x, dh, dres, g, sc, tr, comm=None):
    def fn(r, v):
        x_, dh_, dres_ = r
        g_, sc_ = v
        n, rstd = _rms_stats(x_)
        hn = n * g_
        dhn = dh_ * (1.0 + sc_)
        dn = dhn * g_
        dx = rstd * (dn - n * jnp.mean(dn * n, axis=1, keepdims=True))
        return [dres_ + dx], [_colsum(dh_), _colsum(dh_ * hn), _colsum(dhn * n)]
    D = x.shape[1]
    return _rowwise(name, fn, [x, dh, dres], [g, sc], [(D, F32)], [D, D, D], tr, comm=comm)


def _gate_bwd(name, dx, out, gate, tr):
    def fn(r, v):
        return [v[0] * r[0]], [_colsum(r[0] * r[1].astype(F32))]
    D = dx.shape[1]
    return _rowwise(name, fn, [dx, out], [gate], [(D, BF16)], [D], tr)


def _loss_head(name, x, target, g, tr):
    D = x.shape[1]

    def fn(r, v):
        n, rstd = _rms_stats(r[0])
        err = n * v[0] - r[1]
        loss = 0.5 * jnp.sum(jnp.mean(err * err, axis=1, keepdims=True), axis=0, keepdims=True)
        dy = err * (1.0 / D)
        dn = dy * v[0]
        dx = rstd * (dn - n * jnp.mean(dn * n, axis=1, keepdims=True))
        return [dx], [jnp.broadcast_to(loss, (1, LANE)), _colsum(dy * n)]
    return _rowwise(name, fn, [x, target], [g], [(D, F32)], [LANE, D], tr)


_INV_SQRT2 = 1.0 / math.sqrt(2.0)
_INV_SQRT2PI = 1.0 / math.sqrt(2.0 * math.pi)


def _gelu(a):
    return 0.5 * a * (1.0 + lax.erf(a * _INV_SQRT2))


def _gelu_grad(a):
    return 0.5 * (1.0 + lax.erf(a * _INV_SQRT2)) + a * jnp.exp(-0.5 * a * a) * _INV_SQRT2PI


def _ln_stats(v):
    mu = jnp.mean(v, axis=1, keepdims=True)
    vc = v - mu
    rstd = lax.rsqrt(jnp.mean(vc * vc, axis=1, keepdims=True) + EPS)
    return vc * rstd, rstd


def _chunk_mix(w_ref, vn, tr, G):
    rows = []
    for ch in range(tr // LANE):
        cols = []
        for g in range(G):
            blk = vn[ch * LANE:(ch + 1) * LANE, g * LANE:(g + 1) * LANE]
            cols.append(jnp.dot(w_ref[g], blk, preferred_element_type=F32))
        rows.append(jnp.concatenate(cols, axis=1))
    return jnp.concatenate(rows, axis=0)


def _sgu_mid_fwd(a, ln_g, ln_b, w_mask, bias_full, tr):
    S, D2 = a.shape
    D = D2 // 2
    G = D // LANE

    def body(a_ref, g_ref, b_ref, w_ref, bias_ref, y_ref):
        u = _gelu(a_ref[:, :D])
        v = _gelu(a_ref[:, D:])
        vhat, _ = _ln_stats(v)
        vn = (vhat * g_ref[...] + b_ref[...]).astype(BF16)
        sv = _chunk_mix(w_ref, vn, tr, G) + jnp.concatenate([bias_ref[...]] * (tr // LANE), axis=0)
        y_ref[...] = (u * sv).astype(BF16)

    return pl.pallas_call(
        body, name="sgu_mid_fwd", grid=(S // tr,),
        in_specs=[pl.BlockSpec((tr, D2), lambda i: (i, 0)), pl.BlockSpec((1, D), lambda i: (0, 0)),
                  pl.BlockSpec((1, D), lambda i: (0, 0)), pl.BlockSpec((G, LANE, LANE), lambda i: (0, 0, 0)),
                  pl.BlockSpec((LANE, D), lambda i: (0, 0))],
        out_specs=pl.BlockSpec((tr, D), lambda i: (i, 0)), out_shape=SDS((S, D), BF16),
        compiler_params=_params(("arbitrary",)))(a, ln_g, ln_b, w_mask, bias_full)


def _sgu_mid_bwd(a, dy, ln_g, ln_b, w_mask, bias_full, tr):
    S, D2 = a.shape
    D = D2 // 2
    G = D // LANE
    nch = tr // LANE

    def body(a_ref, dy_ref, g_ref, b_ref, w_ref, bias_ref, da_ref, dw_ref, dbias_ref, dg_ref, db_ref, dbin_ref):
        au, av = a_ref[:, :D], a_ref[:, D:]
        u = _gelu(au)
        v = _gelu(av)
        vhat, rstd = _ln_stats(v)
        vn = (vhat * g_ref[...] + b_ref[...]).astype(BF16)
        sv = _chunk_mix(w_ref, vn, tr, G) + jnp.concatenate([bias_ref[...]] * nch, axis=0)
        dy_ = dy_ref[...].astype(F32)
        du = dy_ * sv
        dsv = dy_ * u
        dsv_b = dsv.astype(BF16)
        first = pl.program_id(0) == 0

        @pl.when(first)
        def _():
            dw_ref[...] = jnp.zeros_like(dw_ref)
            dbias_ref[...] = jnp.zeros_like(dbias_ref)
            dg_ref[...] = jnp.zeros_like(dg_ref)
            db_ref[...] = jnp.zeros_like(db_ref)
            dbin_ref[...] = jnp.zeros_like(dbin_ref)

        rows = []
        dbias = None
        for ch in range(nch):
            r0 = ch * LANE
            cols = []
            for g in range(G):
                c0 = g * LANE
                ds_blk = dsv_b[r0:r0 + LANE, c0:c0 + LANE]
                vn_blk = vn[r0:r0 + LANE, c0:c0 + LANE]
                cols.append(lax.dot_general(w_ref[g], ds_blk, (((0,), (0,)), ((), ())),
                                            preferred_element_type=F32))
                dw_ref[g] += lax.dot_general(ds_blk, vn_blk, (((1,), (1,)), ((), ())),
                                             preferred_element_type=F32)
            rows.append(jnp.concatenate(cols, axis=1))
            blk = dsv[r0:r0 + LANE, :]
            dbias = blk if dbias is None else dbias + blk
        dvn = jnp.concatenate(rows, axis=0)
        dbias_ref[...] += dbias
        dg_ref[...] += _colsum(dvn * vhat)
        db_ref[...] += _colsum(dvn)
        dvh = dvn * g_ref[...]
        dv = rstd * (dvh - jnp.mean(dvh, axis=1, keepdims=True)
                     - vhat * jnp.mean(dvh * vhat, axis=1, keepdims=True))
        da_u = du * _gelu_grad(au)
        da_v = dv * _gelu_grad(av)
        da_ref[:, :D] = da_u.astype(BF16)
        da_ref[:, D:] = da_v.astype(BF16)
        dbin_ref[:, :D] += _colsum(da_u)
        dbin_ref[:, D:] += _colsum(da_v)

    full = lambda shp: pl.BlockSpec(shp, lambda i, nd=len(shp): (0,) * nd)
    return pl.pallas_call(
        body, name="sgu_mid_bwd", grid=(S // tr,),
        in_specs=[pl.BlockSpec((tr, D2), lambda i: (i, 0)), pl.BlockSpec((tr, D), lambda i: (i, 0)),
                  full((1, D)), full((1, D)), full((G, LANE, LANE)), full((LANE, D))],
        out_specs=[pl.BlockSpec((tr, D2), lambda i: (i, 0)), full((G, LANE, LANE)), full((LANE, D)),
                   full((1, D)), full((1, D)), full((1, D2))],
        out_shape=[SDS((S, D2), BF16), SDS((G, LANE, LANE), F32), SDS((LANE, D), F32),
                   SDS((1, D), F32), SDS((1, D), F32), SDS((1, D2), F32)],
        compiler_params=_params(("arbitrary",)))(a, dy, ln_g, ln_b, w_mask, bias_full)


def _tri(lower):
    r = lax.broadcasted_iota(jnp.int32, (LANE, LANE), 0)
    c = lax.broadcasted_iota(jnp.int32, (LANE, LANE), 1)
    return jnp.where((c <= r) if lower else (c >= r), 1.0, 0.0).astype(F32)


def _fox_gate_fwd(fl, bf_pad):
    S = fl.shape[0]
    nblk = S // LANE

    def body(fl_ref, b_ref, f_ref):
        tri = _tri(True)

        def step(i, carry):
            r0 = pl.multiple_of(i * LANE, LANE)
            z = fl_ref[pl.ds(r0, LANE), :] + b_ref[...]
            logf = jnp.minimum(z, 0.0) - jnp.log1p(jnp.exp(-jnp.abs(z)))
            f = jnp.dot(tri, logf, preferred_element_type=F32, precision=lax.Precision.HIGHEST) + carry
            f_ref[pl.ds(r0, LANE), :] = f
            return f[LANE - 1:LANE, :]
        lax.fori_loop(0, nblk, step, jnp.zeros((1, LANE), F32))

    return pl.pallas_call(body, name="fox_gate_fwd", out_shape=SDS((S, LANE), F32),
                          compiler_params=_params())(fl, bf_pad)


def _fox_gate_bwd(fl, dF, bf_pad):
    S = fl.shape[0]
    nblk = S // LANE

    def body(fl_ref, df_ref, b_ref, dfl_ref, db_ref):
        tri = _tri(True)

        def prefix(i, carry):
            r0 = pl.multiple_of(i * LANE, LANE)
            p = jnp.dot(tri, df_ref[pl.ds(r0, LANE), :], preferred_element_type=F32,
                        precision=lax.Precision.HIGHEST) + carry
            dfl_ref[pl.ds(r0, LANE), :] = p
            return p[LANE - 1:LANE, :]
        total = lax.fori_loop(0, nblk, prefix, jnp.zeros((1, LANE), F32))
        db_ref[...] = jnp.zeros_like(db_ref)

        def finish(i, carry):
            r0 = pl.multiple_of(i * LANE, LANE)
            dlogf = total - dfl_ref[pl.ds(r0, LANE), :] + df_ref[pl.ds(r0, LANE), :]
            z = fl_ref[pl.ds(r0, LANE), :] + b_ref[...]
            dfl = dlogf / (1.0 + jnp.exp(z))
            dfl_ref[pl.ds(r0, LANE), :] = dfl
            db_ref[...] += _colsum(dfl)
            return carry
        lax.fori_loop(0, nblk, finish, 0)

    return pl.pallas_call(body, name="fox_gate_bwd", out_shape=[SDS((S, LANE), F32), SDS((1, LANE), F32)],
                          compiler_params=_params())(fl, dF, bf_pad)


def _lane_pick(blk, h):
    lane = lax.broadcasted_iota(jnp.int32, blk.shape, 1)
    return jnp.sum(jnp.where(lane == h, blk, 0.0), axis=1, keepdims=True)


_NEG = -1e30


def _attn_fwd(qkv, F_sh, F_rows, H, tq, comm=None):
    S = qkv.shape[0]
    D = H * LANE
    nq = S // tq
    scale = 1.0 / math.sqrt(LANE)

    def body(q_ref, k_ref, v_ref, fsh_ref, frow_ref, o_ref, lse_ref):
        h, i = pl.program_id(0), pl.program_id(1)
        q = q_ref[...]
        fq = _lane_pick(fsh_ref[...], h)
        row = lax.broadcasted_iota(jnp.int32, (tq, tq), 0)
        col = lax.broadcasted_iota(jnp.int32, (tq, tq), 1)

        def step(j, carry):
            m, l, acc = carry
            r0 = pl.multiple_of(j * tq, tq)
            k = k_ref[pl.ds(r0, tq), :]
            v = v_ref[pl.ds(r0, tq), :]
            s = lax.dot_general(q, k, (((1,), (1,)), ((), ())), preferred_element_type=F32) * scale
            s = s + fq - frow_ref[j]
            s = jnp.where(col + (j - i) * tq <= row, s, _NEG)
            m_new = jnp.maximum(m, jnp.max(s, axis=1, keepdims=True))
            alpha = jnp.exp(m - m_new)
            p = jnp.exp(s - m_new)
            l = alpha * l + jnp.sum(p, axis=1, keepdims=True)
            acc = alpha * acc + jnp.dot(p.astype(BF16), v, preferred_element_type=F32)
            return m_new, l, acc

        m, l, acc = lax.fori_loop(0, i + 1, step, (jnp.full((tq, 1), _NEG, F32), jnp.zeros((tq, 1), F32),
                                                   jnp.zeros((tq, LANE), F32)))
        o_ref[...] = (acc / l).astype(BF16)
        lse_ref[...] = m + jnp.log(l)

    return _call(
        body, "attn_fwd", grid=(H, nq),
        in_specs=[pl.BlockSpec((tq, LANE), lambda h, i: (i, h)),
                  pl.BlockSpec((S, LANE), lambda h, i: (0, H + h)),
                  pl.BlockSpec((S, LANE), lambda h, i: (0, 2 * H + h)),
                  pl.BlockSpec((tq, LANE), lambda h, i: (i, 0)),
                  pl.BlockSpec((None, nq, 1, tq), lambda h, i: (h, 0, 0, 0))],
        out_specs=[pl.BlockSpec((tq, LANE), lambda h, i: (i, h)),
                   pl.BlockSpec((None, tq, 1), lambda h, i: (h, i, 0))],
        out_shape=[SDS((S, D), BF16), SDS((H, S, 1), F32)],
        sem=("parallel", "arbitrary"), comm=comm, args=[qkv, qkv, qkv, F_sh, F_rows])


def _attn_bwd(qkv, do, F_sh, A_rows, delta_rows, H, tq, comm=None):
    S = qkv.shape[0]
    D = H * LANE
    nq = S // tq
    scale = 1.0 / math.sqrt(LANE)

    def body(q_ref, do_ref, k_ref, v_ref, fsh_ref, a_ref, dl_ref, dq_ref, dk_ref, dv_ref, df_ref, dfr_ref,
             dq_acc, dfr_acc):
        h, j = pl.program_id(0), pl.program_id(1)

        @pl.when(j == 0)
        def _():
            dq_acc[...] = jnp.zeros_like(dq_acc)
            dfr_acc[...] = jnp.zeros_like(dfr_acc)

        k = k_ref[...]
        v = v_ref[...]
        fk = _lane_pick(fsh_ref[...], h)
        row = lax.broadcasted_iota(jnp.int32, (tq, tq), 0)
        col = lax.broadcasted_iota(jnp.int32, (tq, tq), 1)

        def step(i, carry):
            dk, dv, df = carry
            r0 = pl.multiple_of(i * tq, tq)
            q = q_ref[pl.ds(r0, tq), :]
            do_ = do_ref[pl.ds(r0, tq), :]
            st = lax.dot_general(k, q, (((1,), (1,)), ((), ())), preferred_element_type=F32) * scale
            arg = st + a_ref[i] - fk
            arg = jnp.where(row <= col + (i - j) * tq, arg, _NEG)
            pt = jnp.exp(arg)
            dv = dv + jnp.dot(pt.astype(BF16), do_, preferred_element_type=F32)
            dpt = lax.dot_general(v, do_, (((1,), (1,)), ((), ())), preferred_element_type=F32)
            dst = pt * (dpt - dl_ref[i])
            df = df + jnp.sum(dst, axis=1, keepdims=True)
            dfr_acc[i] += jnp.sum(dst, axis=0, keepdims=True)
            dsb = dst.astype(BF16)
            dk = dk + jnp.dot(dsb, q, preferred_element_type=F32)
            dq_acc[pl.ds(r0, tq), :] += lax.dot_general(dsb, k, (((0,), (0,)), ((), ())),
                                                        preferred_element_type=F32)
            return dk, dv, df

        z = jnp.zeros((tq, LANE), F32)
        dk, dv, df = lax.fori_loop(j, nq, step, (z, z, jnp.zeros((tq, 1), F32)))
        dk_ref[...] = (dk * scale).astype(BF16)
        dv_ref[...] = dv.astype(BF16)
        df_ref[...] = -df

        @pl.when(j == nq - 1)
        def _():
            dq_ref[...] = (dq_acc[...] * scale).astype(BF16)
            dfr_ref[...] = dfr_acc[...]

    return _call(
        body, "attn_bwd", grid=(H, nq),
        in_specs=[pl.BlockSpec((S, LANE), lambda h, j: (0, h)),
                  pl.BlockSpec((S, LANE), lambda h, j: (0, h)),
                  pl.BlockSpec((tq, LANE), lambda h, j: (j, H + h)),
                  pl.BlockSpec((tq, LANE), lambda h, j: (j, 2 * H + h)),
                  pl.BlockSpec((tq, LANE), lambda h, j: (j, 0)),
                  pl.BlockSpec((None, nq, 1, tq), lambda h, j: (h, 0, 0, 0)),
                  pl.BlockSpec((None, nq, 1, tq), lambda h, j: (h, 0, 0, 0))],
        out_specs=[pl.BlockSpec((S, LANE), lambda h, j: (0, h)),
                   pl.BlockSpec((tq, LANE), lambda h, j: (j, h)),
                   pl.BlockSpec((tq, LANE), lambda h, j: (j, h)),
                   pl.BlockSpec((None, tq, 1), lambda h, j: (h, j, 0)),
                   pl.BlockSpec((None, nq, 1, tq), lambda h, j: (h, 0, 0, 0))],
        out_shape=[SDS((S, D), BF16), SDS((S, D), BF16), SDS((S, D), BF16), SDS((H, S, 1), F32),
                   SDS((H, nq, 1, tq), F32)],
        scratch=[pltpu.VMEM((S, LANE), F32), pltpu.VMEM((nq, 1, tq), F32)],
        sem=("parallel", "arbitrary"), comm=comm, args=[qkv, do, qkv, qkv, F_sh, A_rows, delta_rows])


def _head_dots(do, o, H, tr):
    def fn(r, v):
        prod = r[0].astype(F32) * r[1].astype(F32)
        lane = lax.broadcasted_iota(jnp.int32, (prod.shape[0], LANE), 1)
        out = jnp.zeros((prod.shape[0], LANE), F32)
        for h in range(H):
            s = jnp.sum(prod[:, h * LANE:(h + 1) * LANE], axis=1, keepdims=True)
            out = jnp.where(lane == h, s, out)
        return [out], []
    return _rowwise("attn_delta", fn, [do, o], [], [(LANE, F32)], [], tr)[0]


def _adamw(name, w, g, m, v, comm=None):
    R, C = w.shape
    tr = _pick(R, max(8, (512 * 1024) // max(C, 1) // 8 * 8), 8)
    c1 = 1.0 - ADAM_B1 ** ADAM_STEP
    c2 = 1.0 - ADAM_B2 ** ADAM_STEP

    def body(w_ref, g_ref, m_ref, v_ref, d_ref, mo_ref, vo_ref):
        g_ = g_ref[...]
        m_ = ADAM_B1 * m_ref[...] + (1.0 - ADAM_B1) * g_
        v_ = ADAM_B2 * v_ref[...] + (1.0 - ADAM_B2) * (g_ * g_)
        d_ref[...] = -ADAM_LR * ((m_ / c1) / (jnp.sqrt(v_ / c2) + ADAM_EPS) + ADAM_WD * w_ref[...])
        mo_ref[...] = m_
        vo_ref[...] = v_

    spec = pl.BlockSpec((tr, C), lambda i: (i, 0))
    return _call(body, name, grid=(R // tr,), in_specs=[spec] * 4, out_specs=[spec] * 3,
                 out_shape=[SDS((R, C), F32)] * 3, sem=("parallel",), comm=comm, args=[w, g, m, v])


def _sum_slots(name, a):
    n, R, C = a.shape
    tr = _pick(R, 256, 8)

    def body(a_ref, o_ref):
        acc = a_ref[0]
        for k in range(1, n):
            acc = acc + a_ref[k]
        o_ref[...] = acc

    return pl.pallas_call(body, name=name, grid=(R // tr,),
                          in_specs=[pl.BlockSpec((n, tr, C), lambda i: (0, i, 0))],
                          out_specs=pl.BlockSpec((tr, C), lambda i: (i, 0)), out_shape=SDS((R, C), F32),
                          compiler_params=_params(("parallel",)))(a)


def _silu_rows(name, a):
    def body(a_ref, o_ref):
        z = a_ref[...]
        o_ref[...] = (z / (1.0 + jnp.exp(-z))).astype(BF16)
    return pl.pallas_call(body, name=name, out_shape=SDS(a.shape, BF16), compiler_params=_params())(a)


def _place():
    return lax.axis_index("x"), lax.axis_index("y"), lax.axis_index("c")


def _rcopy(src, dst, ssem, rsem, dev):
    return pltpu.make_async_remote_copy(src_ref=src, dst_ref=dst, send_sem=ssem, recv_sem=rsem,
                                        device_id=dev, device_id_type=MESH)


def _gather_devices(name, buf):
    R, C = buf.shape

    def body(b_ref, out_ref, ssem, rsem, lsem):
        x, y, c = _place()
        me = 4 * x + 2 * y + c
        mine = pltpu.make_async_copy(b_ref, out_ref.at[me], lsem)
        mine.start()
        peers = []
        for k in range(1, N_DEV):
            fx, fy, fc = (k >> 2) & 1, (k >> 1) & 1, k & 1
            peers.append((x ^ fx, y ^ fy, c ^ fc))
        sends = [_rcopy(b_ref, out_ref.at[me], ssem.at[k], rsem.at[k], p) for k, p in enumerate(peers)]
        for cp in sends:
            cp.start()
        for k, (px, py, pc) in enumerate(peers):
            _rcopy(b_ref, out_ref.at[4 * px + 2 * py + pc], ssem.at[k], rsem.at[k], (px, py, pc)).wait_recv()
        for cp in sends:
            cp.wait_send()
        mine.wait()

    return pl.pallas_call(body, name=name, in_specs=[ANY], out_specs=ANY, out_shape=SDS((N_DEV, R, C), buf.dtype),
                          scratch_shapes=[pltpu.SemaphoreType.DMA((N_DEV - 1,)), pltpu.SemaphoreType.DMA((N_DEV - 1,)),
                                          pltpu.SemaphoreType.DMA])(buf)


def _other_chips(x, y):
    return [(1 - x, y), (x, 1 - y), (1 - x, 1 - y)]


def _cast_slab(name, w, idx):
    R, C = w.shape
    tr = _pick(R, max(16, (1 << 20) // C // 16 * 16), 16)

    def body(s_ref, w_ref, o_ref):
        o_ref[...] = w_ref[...].astype(BF16)

    gs = pltpu.PrefetchScalarGridSpec(
        num_scalar_prefetch=1, grid=(R // tr,),
        in_specs=[pl.BlockSpec((tr, C), lambda i, s: (i, 0))],
        out_specs=pl.BlockSpec((None, tr, C), lambda i, s: (s[1], i, 0)))
    return pl.pallas_call(body, name=name, grid_spec=gs, out_shape=SDS((N_CHIPS, R, C), BF16),
                          compiler_params=_params(("parallel",)))(idx, w)


def _job_gather_ici(buf):
    _, R, C = buf.shape
    hR = R // 2
    assert hR * 2 == R

    def views(o):
        x, y, c = _place()
        return c, _other_chips(x, y), o.at[2 * x + y, pl.ds(c * hR, hR), :]

    def start(i, o, ss, rs):
        c, chips, mine = views(o[0])
        for j, (cx, cy) in enumerate(chips):
            _rcopy(mine, mine, ss.at[j], rs.at[j], (cx, cy, c)).start()

    def wait(i, o, ss, rs):
        c, chips, mine = views(o[0])
        for j, (cx, cy) in enumerate(chips):
            cp = _rcopy(mine, o[0].at[2 * cx + cy, pl.ds(c * hR, hR), :], ss.at[j], rs.at[j], (cx, cy, c))
            cp.wait_send()
            cp.wait_recv()

    return _Job([buf], [SDS(buf.shape, buf.dtype)], {0: 0}, 3, start, wait)


def _job_gather_pair(buf):
    _, R, C = buf.shape
    hR = R // 2

    def views(o):
        x, y, c = _place()
        return c, (x, y, 1 - c), _other_chips(x, y)

    def start(i, o, ss, rs):
        c, sib, chips = views(o[0])
        for j, (cx, cy) in enumerate(chips):
            got = o[0].at[2 * cx + cy, pl.ds(c * hR, hR), :]
            _rcopy(got, got, ss.at[j], rs.at[j], sib).start()

    def wait(i, o, ss, rs):
        c, sib, chips = views(o[0])
        for j, (cx, cy) in enumerate(chips):
            cp = _rcopy(o[0].at[2 * cx + cy, pl.ds(c * hR, hR), :],
                        o[0].at[2 * cx + cy, pl.ds((1 - c) * hR, hR), :], ss.at[j], rs.at[j], sib)
            cp.wait_send()
            cp.wait_recv()

    return _Job([buf], [SDS(buf.shape, buf.dtype)], {0: 0}, 3, start, wait)


class _Gather:
    def __init__(self, bufs):
        self.todo, self.half, self.done = dict(bufs), {}, {}

    def comm(self, admit=()):
        second, first = list(self.half), list(admit)
        jobs = [_job_gather_pair(self.half[n]) for n in second] + [_job_gather_ici(self.todo[n]) for n in first]

        def absorb(outs):
            for n, o in zip(second + first, outs):
                if n in self.half:
                    del self.half[n]
                    self.done[n] = o
                else:
                    del self.todo[n]
                    self.half[n] = o
        return (jobs, absorb) if jobs else None


def _job_rs_pair(g4):
    _, R, C = g4.shape
    hR = R // 2
    assert hR * 2 == R

    def desc(i, o, ss, rs):
        x, y, c = _place()
        return _rcopy(i[0].at[:, pl.ds((1 - c) * hR, hR), :], o[0], ss.at[0], rs.at[0], (x, y, 1 - c))

    return _Job([g4], [SDS((N_CHIPS, hR, C), F32)], {}, 1,
                lambda i, o, ss, rs: desc(i, o, ss, rs).start(), lambda i, o, ss, rs: desc(i, o, ss, rs).wait())


def _job_rs_chips(p4):
    _, hR, C = p4.shape

    def descs(i, o, ss, rs):
        x, y, c = _place()
        return [_rcopy(i[0].at[2 * cx + cy], o[0].at[j], ss.at[j], rs.at[j], (cx, cy, c))
                for j, (cx, cy) in enumerate(_other_chips(x, y))]

    def start(i, o, ss, rs):
        for cp in descs(i, o, ss, rs):
            cp.start()

    def wait(i, o, ss, rs):
        for cp in descs(i, o, ss, rs):
            cp.wait_send()
            cp.wait_recv()

    return _Job([p4], [SDS((3, hR, C), p4.dtype)], {}, 3, start, wait)


def _job_rs_join(buf):
    R, C = buf.shape
    hR = R // 2

    def desc(o, ss, rs, recv):
        x, y, c = _place()
        mine = o[0].at[pl.ds(c * hR, hR), :]
        return _rcopy(mine, o[0].at[pl.ds((1 - c) * hR, hR), :] if recv else mine, ss.at[0], rs.at[0], (x, y, 1 - c))

    def wait(i, o, ss, rs):
        cp = desc(o, ss, rs, True)
        cp.wait_send()
        cp.wait_recv()

    return _Job([buf], [SDS(buf.shape, buf.dtype)], {0: 0}, 1,
                lambda i, o, ss, rs: desc(o, ss, rs, False).start(), wait)


def _rs_add_pair(name, g4, recv, idx):
    _, R, C = g4.shape
    hR = R // 2
    tr = _pick(hR, max(16, (1 << 20) // C // 16 * 16), 16)
    nb = hR // tr

    def body(s_ref, a_ref, b_ref, pb_ref, po_ref):
        s = a_ref[...] + b_ref[...]
        pb_ref[...] = s.astype(BF16)

        @pl.when(pl.program_id(1) == s_ref[1])
        def _():
            po_ref[...] = s

    gs = pltpu.PrefetchScalarGridSpec(
        num_scalar_prefetch=1, grid=(nb, N_CHIPS),
        in_specs=[pl.BlockSpec((None, tr, C), lambda i, k, s: (k, s[0] * nb + i, 0)),
                  pl.BlockSpec((None, tr, C), lambda i, k, s: (k, i, 0))],
        out_specs=[pl.BlockSpec((None, tr, C), lambda i, k, s: (k, i, 0)),
                   pl.BlockSpec((tr, C), lambda i, k, s: (i, 0))])
    return pl.pallas_call(body, name=name, grid_spec=gs, out_shape=[SDS((N_CHIPS, hR, C), BF16), SDS((hR, C), F32)],
                          compiler_params=_params(("parallel", "arbitrary")))(idx, g4, recv)


def _rs_add_chips(name, own, recv3, idx):
    hR, C = own.shape
    tr = _pick(hR, max(16, (1 << 20) // C // 16 * 16), 16)
    nb = hR // tr

    def body(s_ref, a_ref, b_ref, o_ref):
        o_ref[...] = ((a_ref[...] + b_ref[0].astype(F32)) + b_ref[1].astype(F32)) + b_ref[2].astype(F32)

    gs = pltpu.PrefetchScalarGridSpec(
        num_scalar_prefetch=1, grid=(nb,),
        in_specs=[pl.BlockSpec((tr, C), lambda i, s: (i, 0)), pl.BlockSpec((3, tr, C), lambda i, s: (0, i, 0))],
        out_specs=pl.BlockSpec((tr, C), lambda i, s: (s[0] * nb + i, 0)))
    return pl.pallas_call(body, name=name, grid_spec=gs, out_shape=SDS((2 * hR, C), F32),
                          compiler_params=_params(("parallel",)))(idx, own, recv3)


class _ReduceScatter:
    def __init__(self, idx):
        self.idx, self.items, self.done = idx, [], {}

    def push(self, tag, g4):
        self.items.append([tag, 0, g4])

    def comm(self):
        cur = list(self.items)
        jobs = []
        for tag, stage, data in cur:
            jobs.append(_job_rs_pair(data) if stage == 0 else _job_rs_chips(data[0]) if stage == 1
                        else _job_rs_join(data))

        def absorb(outs):
            for item, o in zip(cur, outs):
                tag, stage, data = item
                if stage == 0:
                    item[1:] = [1, _rs_add_pair(tag + "_rs_add2", data, o, self.idx)]
                elif stage == 1:
                    item[1:] = [2, _rs_add_chips(tag + "_rs_add4", data[1], o, self.idx)]
                else:
                    self.items.remove(item)
                    self.done[tag] = o
        return (jobs, absorb) if jobs else None


def _resid_epilogue(accs, ex):
    return [ex[0] + ex[1] * accs[0], accs[0]]


def _swiglu_epilogue(accs, ex):
    gt, up = accs
    return [gt, up, gt / (1.0 + jnp.exp(-gt)) * up]


def _swiglu_bwd_epilogue(accs, ex):
    dact = accs[0]
    gt, up = ex[0].astype(F32), ex[1].astype(F32)
    sg = 1.0 / (1.0 + jnp.exp(-gt))
    return [dact * up * (sg * (1.0 + gt * (1.0 - sg))), dact * (gt * sg)]


def kernel(x, c, ada_w, ada_b, norm_mix_g, norm_ffn_g, a_w_in, a_b_in, a_ln_g, a_ln_b, a_w_s, a_b_s, a_w_out, b_w_in, b_b_f, b_w_out, ffn_w_gate, ffn_w_up, ffn_w_down, final_g, loss_target, m_ada_w, m_ada_b, m_norm_mix_g, m_norm_ffn_g, m_a_w_in, m_a_b_in, m_a_ln_g, m_a_ln_b, m_a_w_s, m_a_b_s, m_a_w_out, m_b_w_in, m_b_b_f, m_b_w_out, m_ffn_w_gate, m_ffn_w_up, m_ffn_w_down, m_final_g, v_ada_w, v_ada_b, v_norm_mix_g, v_norm_ffn_g, v_a_w_in, v_a_b_in, v_a_ln_g, v_a_ln_b, v_a_w_s, v_a_b_s, v_a_w_out, v_b_w_in, v_b_b_f, v_b_w_out, v_ffn_w_gate, v_ffn_w_up, v_ffn_w_down, v_final_g):
    S, D = x.shape[1], x.shape[2]
    H = D // LANE
    G = D // LANE
    FH = ffn_w_down.shape[1] * N_CHIPS
    depth = ada_w.shape[0]
    assert depth == 2 and a_w_in.shape[0] == 1 and b_w_in.shape[0] == 1
    mx, my, mc = _place()
    chip = 2 * mx + my
    dev = 4 * mx + 2 * my + mc
    x0 = x[0]
    target = loss_target[0]
    tr = _pick(S, 256, 8)
    tm = _pick(S, 512, 8)
    tq = _pick(S, 512, LANE)
    nq = S // tq

    idx = jnp.stack([mc, chip]).astype(jnp.int32)
    shards = {"a_in": a_w_in[0], "a_out": a_w_out[0], "b_in": b_w_in[0], "b_out": b_w_out[0]}
    for l in range(depth):
        shards.update({f"wg{l}": ffn_w_gate[l], f"wu{l}": ffn_w_up[l], f"wd{l}": ffn_w_down[l]})
    ag = _Gather({n: _cast_slab("cast_" + n, w, idx) for n, w in shards.items()})
    _comm_call("ag_first_ici", ag.comm(["a_in", "a_out"]))
    _comm_call("ag_first_pair", ag.comm())
    wa_in4 = ag.done["a_in"]
    wa_out = ag.done["a_out"].reshape(D, D)

    c_all = _gather_devices("ag_c", jnp.pad(c, ((0, 7), (0, 0)))).reshape(N_DEV, 8, D)[:, 0, :]
    c_act = _silu_rows("c_silu", c_all)
    n_loc = ada_w.shape[2]
    mods = []
    for l in range(depth):
        b_loc = lax.dynamic_slice_in_dim(ada_b[l:l + 1], chip * n_loc, n_loc, axis=1)
        mods.append(_mm(f"ada_fwd{l}", "nn", [c_act], [ada_w[l]], M=N_DEV, N=n_loc, K=D, tm=N_DEV,
                        tn=_pick(n_loc, 1024, LANE), tk=D, extras=[("row", b_loc)],
                        epilogue=lambda accs, ex: [accs[0] + ex[0]])[0])
    mod_all = _gather_devices("ag_mod", jnp.concatenate(mods, axis=1))
    mod_mine = lax.dynamic_index_in_dim(mod_all[0::2], dev, axis=1, keepdims=False)
    mod_mine = mod_mine.reshape(N_CHIPS, depth, n_loc).transpose(1, 0, 2).reshape(depth, 1, N_MOD * D)
    mod = [[mod_mine[l, :, i * D:(i + 1) * D] for i in range(N_MOD)] for l in range(depth)]

    row = lambda a: a.reshape(1, -1)

    tril = jnp.tril(jnp.ones((LANE, LANE), dtype=bool))
    w_mask = jnp.where(tril[None], a_w_s[0], 0.0).astype(BF16)
    bias_full = jnp.repeat(a_b_s[0].T, LANE, axis=1)
    bf_pad = jnp.pad(b_b_f, ((0, 0), (0, LANE - H)))

    admit = {"sgu_in": ["wg0", "wu0"], "mix_out0": ["wd0"], "ffn_in0": ["b_in", "b_out"], "ffn_out0": ["wg1"],
             "fox_qkv": ["wu1"], "attn_fwd": ["wd1"], "mix_out1": []}
    saved = []
    xs = x0
    for l in range(depth):
        sh1, sc1, g1, sh2, sc2, g2 = mod[l]
        st = {"x_in": xs}
        h1 = _normmod_fwd(f"normmod_mix{l}", xs, row(norm_mix_g[l]), sc1, sh1, tr)
        st["h1"] = h1
        if l == 0:
            a = _mm("sgu_in", "nn", [h1], [wa_in4], M=S, N=2 * D, K=D, tm=tm, tn=2 * D // N_CHIPS, tk=D,
                    b_stacked=True, extras=[("row", a_b_in)], comm=ag.comm(admit["sgu_in"]),
                    epilogue=lambda accs, ex: [accs[0] + ex[0]])[0]
            y = _sgu_mid_fwd(a, a_ln_g, a_ln_b, w_mask, bias_full, tr)
            st["a"], st["y"] = a, y
            w_o, mix_out = wa_out, y
        else:
            wb_in4 = ag.done["b_in"]
            wb_in = jnp.concatenate([wb_in4[k] for k in range(N_CHIPS)], axis=1)
            w_qkv = wb_in[:, :3 * D]
            w_f = jnp.pad(wb_in[:, 3 * D:], ((0, 0), (0, LANE - H)))
            qkv = _mm("fox_qkv", "nn", [h1], [w_qkv], M=S, N=3 * D, K=D, tm=tm, tn=_pick(3 * D, 1024, LANE),
                      tk=D, out_dtypes=(BF16,), comm=ag.comm(admit["fox_qkv"]))[0]
            fl = _mm("fox_f", "nn", [h1], [w_f], M=S, N=LANE, K=D, tm=tm, tn=LANE, tk=D)[0]
            F_sh = _fox_gate_fwd(fl, bf_pad)
            F_rows = F_sh[:, :H].T.reshape(H, nq, 1, tq)
            o, lse = _attn_fwd(qkv, F_sh, F_rows, H, tq, comm=ag.comm(admit["attn_fwd"]))
            st.update(qkv=qkv, fl=fl, F_sh=F_sh, F_rows=F_rows, o=o, lse=lse, w_qkv=w_qkv, w_f=w_f)
            w_o, mix_out = ag.done["b_out"].reshape(D, D), o
        x1, out1 = _mm(f"mix_out{l}", "nn", [mix_out], [w_o], M=S, N=D, K=D, tm=tm, tn=_pick(D, 1024, LANE),
                       tk=D, extras=[("tile", xs), ("row", g1)], out_dtypes=(F32, BF16),
                       epilogue=_resid_epilogue, comm=ag.comm(admit[f"mix_out{l}"]))
        st["x_mid"], st["out1"] = x1, out1
        h2 = _normmod_fwd(f"normmod_ffn{l}", x1, row(norm_ffn_g[l]), sc2, sh2, tr)
        gt, up, act = _mm(f"ffn_in{l}", "nn", [h2, h2], [ag.done[f"wg{l}"], ag.done[f"wu{l}"]], M=S, N=FH, K=D,
                          tm=tm, tn=FH // N_CHIPS, tk=_pick(D, 1024, LANE), b_stacked=True, acc_of=[0, 1], n_acc=2,
                          out_dtypes=(BF16, BF16, BF16), epilogue=_swiglu_epilogue,
                          comm=ag.comm(admit.get(f"ffn_in{l}", ())))
        x2, out2 = _mm(f"ffn_out{l}", "nn", [act], [ag.done[f"wd{l}"].reshape(FH, D)], M=S, N=D, K=FH, tm=tm,
                       tn=_pick(D, 1024, LANE), tk=FH // N_CHIPS, extras=[("tile", x1), ("row", g2)],
                       out_dtypes=(F32, BF16), epilogue=_resid_epilogue,
                       comm=ag.comm(admit.get(f"ffn_out{l}", ())))
        st.update(h2=h2, gt=gt, up=up, act=act, out2=out2)
        saved.append(st)
        xs = x2
    assert not ag.todo and not ag.half
    wg4 = [ag.done[f"wg{l}"] for l in range(depth)]
    wu4 = [ag.done[f"wu{l}"] for l in range(depth)]
    wd = [ag.done[f"wd{l}"].reshape(FH, D) for l in range(depth)]
    wb_out = ag.done["b_out"].reshape(D, D)

    dx, loss_vec, g_final = _loss_head("loss_head", xs, target, row(final_g), tr)
    loss = lax.psum(loss_vec[0, 0], ("x", "y", "c"))

    rs = _ReduceScatter(idx)
    dmods = [None] * depth
    gmix = [None] * depth
    gffn = [None] * depth
    for l in reversed(range(depth)):
        sh1, sc1, g1, sh2, sc2, g2 = mod[l]
        st = saved[l]
        dog, dg2 = _gate_bwd(f"gate_ffn{l}", dx, st["out2"], g2, tr)
        dgt, dup = _mm(f"ffn_dact{l}", "nt", [dog], [wd[l]], M=S, N=FH, K=D, tm=tm, tn=FH // N_CHIPS, tk=D,
                       extras=[("tile", st["gt"]), ("tile", st["up"])], out_dtypes=(BF16, BF16),
                       epilogue=_swiglu_bwd_epilogue, comm=rs.comm())
        rs.push(f"wd{l}", _mm(f"ffn_dwd{l}", "tn", [st["act"]], [dog], M=FH, N=D, K=S, tm=FH // N_CHIPS,
                              tn=_pick(D, 1024, LANE), tk=_pick(S, 512, LANE),
                              comm=rs.comm())[0].reshape(N_CHIPS, FH // N_CHIPS, D))
        dh2 = _mm(f"ffn_dh{l}", "nt", [dgt, dup], [wg4[l], wu4[l]], M=S, N=D, K=FH, tm=tm,
                  tn=_pick(D, 1024, LANE), tk=FH // N_CHIPS, b_stacked=True, comm=rs.comm())[0]
        rs.push(f"wg{l}", _mm(f"ffn_dwg{l}", "tn", [st["h2"]], [dgt], M=D, N=FH, K=S, tm=_pick(D, 512, LANE),
                              tn=FH // N_CHIPS, tk=_pick(S, 1024, LANE), out_stacked=True, comm=rs.comm())[0])
        rs.push(f"wu{l}", _mm(f"ffn_dwu{l}", "tn", [st["h2"]], [dup], M=D, N=FH, K=S, tm=_pick(D, 512, LANE),
                              tn=FH // N_CHIPS, tk=_pick(S, 1024, LANE), out_stacked=True, comm=rs.comm())[0])
        dx, dsh2, dsc2, gffn[l] = _normmod_bwd(f"normmod_ffn_bwd{l}", st["x_mid"], dh2, dx,
                                               row(norm_ffn_g[l]), sc2, tr)
        dog, dg1 = _gate_bwd(f"gate_mix{l}", dx, st["out1"], g1, tr)
        if l == 0:
            dy = _mm("sgu_dy", "nt", [dog], [wa_out], M=S, N=D, K=D, tm=tm, tn=_pick(D, 1024, LANE), tk=D,
                     out_dtypes=(BF16,))[0]
            rs.push("a_out", _mm("sgu_dwout", "tn", [st["y"]], [dog], M=D, N=D, K=S, tm=_pick(D, 512, LANE),
                                 tn=_pick(D, 1024, LANE),
                                 tk=_pick(S, 1024, LANE))[0].reshape(N_CHIPS, D // N_CHIPS, D))
            da, dws, dbias, g_ln_g, g_ln_b, g_b_in = _sgu_mid_bwd(st["a"], dy, a_ln_g, a_ln_b, w_mask, bias_full, tr)
            rs.push("a_in", _mm("sgu_dwin", "tn", [st["h1"]], [da], M=D, N=2 * D, K=S, tm=_pick(D, 512, LANE),
                                tn=2 * D // N_CHIPS, tk=_pick(S, 1024, LANE), out_stacked=True, comm=rs.comm())[0])
            dh1 = _mm("sgu_dh", "nt", [da], [wa_in4], M=S, N=D, K=2 * D, tm=tm, tn=_pick(D, 1024, LANE),
                      tk=2 * D // N_CHIPS, b_stacked=True, comm=rs.comm())[0]
            g_w_s = jnp.where(tril[None], dws, 0.0)
            g_b_s = jnp.sum(dbias.reshape(LANE, G, LANE), axis=2).T
        else:
            do = _mm("fox_do", "nt", [dog], [wb_out], M=S, N=D, K=D, tm=tm, tn=_pick(D, 1024, LANE), tk=D,
                     out_dtypes=(BF16,))[0]
            rs.push("b_out", _mm("fox_dwout", "tn", [st["o"]], [dog], M=D, N=D, K=S, tm=_pick(D, 512, LANE),
                                 tn=_pick(D, 1024, LANE),
                                 tk=_pick(S, 1024, LANE))[0].reshape(N_CHIPS, D // N_CHIPS, D))
            delta = _head_dots(do, st["o"], H, tr)
            delta_rows = delta[:, :H].T.reshape(H, nq, 1, tq)
            A_rows = st["F_rows"] - st["lse"].reshape(H, nq, 1, tq)
            w_qkv, w_f = st["w_qkv"], st["w_f"]
            dq, dk, dv, dF_k, dF_q = _attn_bwd(st["qkv"], do, st["F_sh"], A_rows, delta_rows, H, tq,
                                               comm=rs.comm())
            dF_sh = jnp.pad((dF_k.reshape(H, S) + dF_q.reshape(H, S)).T, ((0, 0), (0, LANE - H)))
            dfl, db_f = _fox_gate_bwd(st["fl"], dF_sh, bf_pad)
            dfl_b = dfl.astype(BF16)
            dh_f = _mm("fox_dh_f", "nt", [dfl_b], [w_f], M=S, N=D, K=LANE, tm=tm, tn=_pick(D, 1024, LANE),
                       tk=LANE)[0]
            tkq = _pick(D, 1024, LANE)
            dh1 = _mm("fox_dh", "nt", [dq, dk, dv], [w_qkv, w_qkv, w_qkv], M=S, N=D, K=D, tm=tm,
                      tn=_pick(D, 1024, LANE), tk=tkq, b_koffs=[0, D // tkq, 2 * D // tkq],
                      extras=[("tile", dh_f)], epilogue=lambda accs, ex: [accs[0] + ex[0]], comm=rs.comm())[0]
            parts = [_mm(f"fox_dw{nm}", "tn", [st["h1"]], [d_], M=D, N=D, K=S, tm=_pick(D, 512, LANE),
                         tn=_pick(D, 1024, LANE), tk=_pick(S, 1024, LANE))[0]
                     for nm, d_ in (("q", dq), ("k", dk), ("v", dv))]
            dwf = _mm("fox_dwf", "tn", [st["h1"]], [dfl_b], M=D, N=LANE, K=S, tm=_pick(D, 512, LANE), tn=LANE,
                      tk=_pick(S, 1024, LANE))[0]
            g_b_w_in = jnp.concatenate(parts + [dwf[:, :H]], axis=1)
            n_b = g_b_w_in.shape[1] // N_CHIPS
            rs.push("b_in", jnp.stack([g_b_w_in[:, k * n_b:(k + 1) * n_b] for k in range(N_CHIPS)]))
            g_b_f = db_f[:, :H]
        dx, dsh1, dsc1, gmix[l] = _normmod_bwd(f"normmod_mix_bwd{l}", st["x_in"], dh1, dx,
                                               row(norm_mix_g[l]), sc1, tr)
        dmods[l] = jnp.concatenate([dsh1, dsc1, dg1, dsh2, dsc2, dg2], axis=1)
    grad_x = dx[None]

    small = [jnp.concatenate(dmods, axis=0), jnp.concatenate(gmix, axis=0), jnp.concatenate(gffn, axis=0),
             g_b_in, g_ln_g, g_ln_b, g_w_s[None], g_b_s[None], g_b_f, g_final[0]]
    small_w = [ada_b, norm_mix_g, norm_ffn_g, a_b_in, a_ln_g, a_ln_b, a_w_s, a_b_s, b_b_f, final_g]
    small_m = [m_ada_b, m_norm_mix_g, m_norm_ffn_g, m_a_b_in, m_a_ln_g, m_a_ln_b, m_a_w_s, m_a_b_s, m_b_b_f, m_final_g]
    small_v = [v_ada_b, v_norm_mix_g, v_norm_ffn_g, v_a_b_in, v_a_ln_g, v_a_ln_b, v_a_w_s, v_a_b_s, v_b_b_f, v_final_g]
    sizes = [w.size for w in small_w]
    total = sum(sizes)
    padded = -(-total // (8 * LANE)) * (8 * LANE)

    def pack(parts):
        flat = jnp.concatenate([p.reshape(-1) for p in parts])
        return jnp.pad(flat, (0, padded - total)).reshape(padded // LANE, LANE)

    def unpack(buf):
        flat = buf.reshape(-1)
        outs, off = [], 0
        for w, n in zip(small_w, sizes):
            outs.append(flat[off:off + n].reshape(w.shape))
            off += n
        return outs

    g_all = _gather_devices("ag_small_grads", pack(small))
    g_small = _sum_slots("sum_small_grads", g_all)
    d_small, m_small, v_small = _adamw("adamw_small", pack(small_w), g_small, pack(small_m), pack(small_v))
    sg, sd, sm, sv_ = unpack(g_small), unpack(d_small), unpack(m_small), unpack(v_small)

    n_dm = depth * N_MOD * D
    dmod_all = g_all.reshape(N_DEV, -1)[:, :n_dm].reshape(N_DEV, depth, N_MOD * D)
    g_ada = []
    for l in range(depth):
        dm_loc = lax.dynamic_slice_in_dim(dmod_all[:, l, :], chip * n_loc, n_loc, axis=1).astype(BF16)
        g_ada.append(_mm(f"ada_dw{l}", "tn", [c_act], [dm_loc], M=D, N=n_loc, K=N_DEV, tm=_pick(D, 512, LANE),
                         tn=_pick(n_loc, 1024, LANE), tk=N_DEV)[0])
    g_ada_w = jnp.stack(g_ada)

    flat2 = lambda a: a.reshape(-1, a.shape[-1])
    ada_upd = _adamw("adamw_ada_w", flat2(ada_w), flat2(g_ada_w), flat2(m_ada_w), flat2(v_ada_w), comm=rs.comm())
    u_ada = [g_ada_w] + [t.reshape(ada_w.shape) for t in ada_upd]
    state = {"a_in": (a_w_in[0], m_a_w_in[0], v_a_w_in[0]), "a_out": (a_w_out[0], m_a_w_out[0], v_a_w_out[0]),
             "b_in": (b_w_in[0], m_b_w_in[0], v_b_w_in[0]), "b_out": (b_w_out[0], m_b_w_out[0], v_b_w_out[0])}
    for l in range(depth):
        state[f"wg{l}"] = (ffn_w_gate[l], m_ffn_w_gate[l], v_ffn_w_gate[l])
        state[f"wu{l}"] = (ffn_w_up[l], m_ffn_w_up[l], v_ffn_w_up[l])
        state[f"wd{l}"] = (ffn_w_down[l], m_ffn_w_down[l], v_ffn_w_down[l])
    upd, flushes = {}, 0
    while len(upd) < len(state):
        ready = [t for t in rs.done if t not in upd]
        if ready:
            t = ready[0]
            upd[t] = [rs.done[t]] + _adamw("adamw_" + t, state[t][0], rs.done[t], state[t][1], state[t][2],
                                           comm=rs.comm())
        else:
            _comm_call(f"rs_flush{flushes}", rs.comm())
            flushes += 1
    u_a_in, u_a_out, u_b_in, u_b_out = ([t[None] for t in upd[n]] for n in ("a_in", "a_out", "b_in", "b_out"))
    u_wg, u_wu, u_wd = ([jnp.stack([upd[f"{n}{l}"][i] for l in range(depth)]) for i in range(4)]
                        for n in ("wg", "wu", "wd"))

    def leaves(i, small_list):
        s = small_list
        return [u_ada[i], s[0], s[1], s[2], u_a_in[i], s[3], s[4], s[5], s[6], s[7], u_a_out[i],
                u_b_in[i], s[8], u_b_out[i], u_wg[i], u_wu[i], u_wd[i], s[9]]

    return (loss, grad_x, *leaves(0, sg), *leaves(1, sd), *leaves(2, sm), *leaves(3, sv_))
```

```python
import functools
import math

import jax
import jax.numpy as jnp
from jax import lax
from jax.experimental import pallas as pl
from jax.experimental.pallas import tpu as pltpu

F32, BF16 = jnp.float32, jnp.bfloat16
LANE = 128
N_CHIPS = 4
N_DEV = 8
N_MOD = 6
EPS = 1e-6
VMEM_LIMIT = 60 * 1024 * 1024
ADAM_LR, ADAM_B1, ADAM_B2, ADAM_EPS, ADAM_WD, ADAM_STEP = 0.001, 0.9, 0.999, 1e-08, 0.01, 10
MESH = pl.DeviceIdType.MESH
ANY = pl.BlockSpec(memory_space=pl.ANY)
SDS = jax.ShapeDtypeStruct


def _pick(dim, pref, align):
    t = min(dim, pref)
    t -= t % align
    while t >= align:
        if dim % t == 0:
            return t
        t -= align
    return dim


def _params(sem=None):
    return pltpu.CompilerParams(dimension_semantics=sem, vmem_limit_bytes=VMEM_LIMIT)


class _Job:
    def __init__(self, ins, outs, aliases, nsem, start, wait):
        self.ins, self.outs, self.aliases, self.nsem, self.start, self.wait = ins, outs, aliases, nsem, start, wait


def _call(body, name, *, grid, in_specs, out_specs, out_shape, args, scratch=(), sem=None, comm=None,
          prefetch=None, keep=()):
    jobs, absorb = comm if comm else ([], None)
    n_pre = 0 if prefetch is None else 1
    n_in, n_out, n_scr = len(args), len(out_shape), len(scratch)
    c_ins = [a for a, _ in keep] + [a for j in jobs for a in j.ins]
    c_outs = [s for j in jobs for s in j.outs]
    aliases = {n_pre + n_in + k: o for k, (_, o) in enumerate(keep)}
    c_scr = []
    i_off, o_off = len(keep), 0
    for j in jobs:
        for a, b in j.aliases.items():
            aliases[n_pre + n_in + i_off + a] = n_out + o_off + b
        i_off += len(j.ins)
        o_off += len(j.outs)
        c_scr += [pltpu.SemaphoreType.DMA((j.nsem,)), pltpu.SemaphoreType.DMA((j.nsem,))]

    def wrapped(*refs):
        bounds = [n_pre + n_in, len(c_ins), n_out, len(c_outs), n_scr]
        parts, p = [], 0
        for n in bounds:
            parts.append(refs[p:p + n])
            p += n
        main_in, cin, main_out, cout, main_scr = parts
        cin = cin[len(keep):]
        csem = refs[p:]

        def run(phase):
            a = b = 0
            for k, j in enumerate(jobs):
                fn = j.start if phase == 0 else j.wait
                fn(cin[a:a + len(j.ins)], cout[b:b + len(j.outs)], csem[2 * k], csem[2 * k + 1])
                a += len(j.ins)
                b += len(j.outs)

        if jobs and grid:
            first = functools.reduce(jnp.logical_and, [pl.program_id(d) == 0 for d in range(len(grid))])
            last = functools.reduce(jnp.logical_and, [pl.program_id(d) == grid[d] - 1 for d in range(len(grid))])
            pl.when(first)(lambda: run(0))
            body(*main_in, *main_out, *main_scr)
            pl.when(last)(lambda: run(1))
        elif jobs:
            run(0)
            body(*main_in, *main_out, *main_scr)
            run(1)
        else:
            body(*main_in, *main_out, *main_scr)

    specs = dict(grid=grid, in_specs=list(in_specs) + [ANY] * len(c_ins),
                 out_specs=list(out_specs) + [ANY] * len(c_outs), scratch_shapes=list(scratch) + c_scr)
    if n_pre:
        specs = dict(grid_spec=pltpu.PrefetchScalarGridSpec(num_scalar_prefetch=1, **specs))
    outs = pl.pallas_call(
        wrapped, name=name, out_shape=list(out_shape) + c_outs, input_output_aliases=aliases,
        compiler_params=_params(("arbitrary",) * len(grid) if jobs else sem), **specs,
    )(*([prefetch] if n_pre else []), *args, *c_ins)
    if jobs:
        absorb(list(outs[n_out:]))
    return list(outs[:n_out])


def _comm_call(name, comm):
    _call(lambda: None, name, grid=(), in_specs=[], out_specs=[], out_shape=[], args=[], comm=comm)


def _mm(name, form, a_list, b_list, *, M, N, K, tm, tn, tk, b_stacked=False, out_stacked=False,
        b_koffs=None, acc_of=None, n_acc=1, extras=(), out_dtypes=(F32,), epilogue=None, comm=None):
    assert M % tm == 0 and N % tn == 0 and K % tk == 0, (name, M, N, K, tm, tn, tk)
    nm, nn, nk = M // tm, N // tn, K // tk
    npairs = len(a_list)
    acc_of = acc_of or [0] * npairs
    b_koffs = b_koffs or [0] * npairs
    if epilogue is None:
        epilogue = lambda accs, ex: [accs[0]]

    in_specs = []
    for _ in a_list:
        if form == "tn":
            in_specs.append(pl.BlockSpec((tk, tm), lambda j, i, k: (k, i)))
        else:
            in_specs.append(pl.BlockSpec((tm, tk), lambda j, i, k: (i, k)))
    for off in b_koffs:
        if form == "nn":
            if b_stacked:
                assert tn * N_CHIPS == N
                in_specs.append(pl.BlockSpec((None, tk, tn), lambda j, i, k: (j, k, 0)))
            else:
                in_specs.append(pl.BlockSpec((tk, tn), lambda j, i, k: (k, j)))
        elif form == "nt":
            if b_stacked:
                assert tk * N_CHIPS == K
                in_specs.append(pl.BlockSpec((None, tn, tk), lambda j, i, k: (k, j, 0)))
            else:
                in_specs.append(pl.BlockSpec((tn, tk), lambda j, i, k, off=off: (j, off + k)))
        else:
            in_specs.append(pl.BlockSpec((tk, tn), lambda j, i, k: (k, j)))
    for kind, _ in extras:
        if kind == "tile":
            in_specs.append(pl.BlockSpec((tm, tn), lambda j, i, k: (i, j)))
        else:
            in_specs.append(pl.BlockSpec((1, tn), lambda j, i, k: (0, j)))
    if out_stacked:
        assert tn * N_CHIPS == N
        out_specs = [pl.BlockSpec((None, tm, tn), lambda j, i, k: (j, i, 0)) for _ in out_dtypes]
        out_shape = [SDS((N_CHIPS, M, tn), d) for d in out_dtypes]
    else:
        out_specs = [pl.BlockSpec((tm, tn), lambda j, i, k: (i, j)) for _ in out_dtypes]
        out_shape = [SDS((M, N), d) for d in out_dtypes]
    dims = {"nn": (((1,), (0,)), ((), ())), "nt": (((1,), (1,)), ((), ())), "tn": (((0,), (0,)), ((), ()))}[form]
    n_ex, n_out = len(extras), len(out_dtypes)

    def body(*refs):
        a_refs = refs[:npairs]
        b_refs = refs[npairs:2 * npairs]
        e_refs = refs[2 * npairs:2 * npairs + n_ex]
        o_refs = refs[2 * npairs + n_ex:2 * npairs + n_ex + n_out]
        acc_refs = refs[2 * npairs + n_ex + n_out:]

        tot = [None] * n_acc
        for p in range(npairs):
            d = lax.dot_general(a_refs[p][...].astype(BF16), b_refs[p][...].astype(BF16), dims,
                                preferred_element_type=F32)
            tot[acc_of[p]] = d if tot[acc_of[p]] is None else tot[acc_of[p]] + d

        def finish(accs):
            outs = epilogue(accs, [e[...] for e in e_refs])
            for o_ref, o in zip(o_refs, outs):
                o_ref[...] = o.astype(o_ref.dtype)

        if nk == 1:
            finish(tot)
        else:
            k = pl.program_id(2)

            @pl.when(k == 0)
            def _():
                for r, t in zip(acc_refs, tot):
                    r[...] = t

            @pl.when(k > 0)
            def _():
                for r, t in zip(acc_refs, tot):
                    r[...] += t

            @pl.when(k == nk - 1)
            def _():
                finish([r[...] for r in acc_refs])

    scratch = [pltpu.VMEM((tm, tn), F32) for _ in range(n_acc)] if nk > 1 else []
    return _call(body, name, grid=(nn, nm, nk), in_specs=in_specs, out_specs=out_specs, out_shape=out_shape,
                 scratch=scratch, sem=("parallel", "parallel", "arbitrary"), comm=comm,
                 args=[*a_list, *b_list, *[e for _, e in extras]])


def _rowwise(name, fn, rows, vecs, row_outs, acc_widths, tr, comm=None):
    S = rows[0].shape[0]
    assert S % tr == 0
    nr, nv, no, na = len(rows), len(vecs), len(row_outs), len(acc_widths)

    def body(*refs):
        r = [x[...] for x in refs[:nr]]
        v = [x[...] for x in refs[nr:nr + nv]]
        o_refs = refs[nr + nv:nr + nv + no]
        a_refs = refs[nr + nv + no:]
        outs, accs = fn(r, v)
        for o_ref, o in zip(o_refs, outs):
            o_ref[...] = o.astype(o_ref.dtype)
        first = pl.program_id(0) == 0

        @pl.when(first)
        def _():
            for a_ref, a in zip(a_refs, accs):
                a_ref[...] = a

        @pl.when(jnp.logical_not(first))
        def _():
            for a_ref, a in zip(a_refs, accs):
                a_ref[...] += a

    in_specs = [pl.BlockSpec((tr, x.shape[1]), lambda i: (i, 0)) for x in rows]
    in_specs += [pl.BlockSpec(x.shape, lambda i, nd=x.ndim: (0,) * nd) for x in vecs]
    out_specs = [pl.BlockSpec((tr, w), lambda i: (i, 0)) for w, _ in row_outs]
    out_specs += [pl.BlockSpec((1, w), lambda i: (0, 0)) for w in acc_widths]
    out_shape = [SDS((S, w), d) for w, d in row_outs] + [SDS((1, w), F32) for w in acc_widths]
    return _call(body, name, grid=(S // tr,), in_specs=in_specs, out_specs=out_specs, out_shape=out_shape,
                 sem=("arbitrary",), comm=comm, args=[*rows, *vecs])


def _colsum(a):
    return jnp.sum(a, axis=0, keepdims=True)


def _rms_stats(x):
    rstd = lax.rsqrt(jnp.mean(x * x, axis=1, keepdims=True) + EPS)
    return x * rstd, rstd


def _normmod_fwd(name, x, g, sc, sh, tr):
    def fn(r, v):
        n, _ = _rms_stats(r[0])
        return [(n * v[0]) * (1.0 + v[1]) + v[2]], []
    return _rowwise(name, fn, [x], [g, sc, sh], [(x.shape[1], BF16)], [], tr)[0]


def _normmod_bwd(name, x, dh, dres, g, sc, tr, comm=None):
    def fn(r, v):
        x_, dh_, dres_ = r
        g_, sc_ = v
        n, rstd = _rms_stats(x_)
        hn = n * g_
        dhn = dh_ * (1.0 + sc_)
        dn = dhn * g_
        dx = rstd * (dn - n * jnp.mean(dn * n, axis=1, keepdims=True))
        return [dres_ + dx], [_colsum(dh_), _colsum(dh_ * hn), _colsum(dhn * n)]
    D = x.shape[1]
    return _rowwise(name, fn, [x, dh, dres], [g, sc], [(D, F32)], [D, D, D], tr, comm=comm)


def _gate_bwd(name, dx, out, gate, tr):
    def fn(r, v):
        return [v[0] * r[0]], [_colsum(r[0] * r[1].astype(F32))]
    D = dx.shape[1]
    return _rowwise(name, fn, [dx, out], [gate], [(D, BF16)], [D], tr)


def _loss_head(name, x, target, g, tr):
    D = x.shape[1]

    def fn(r, v):
        n, rstd = _rms_stats(r[0])
        err = n * v[0] - r[1]
        loss = 0.5 * jnp.sum(jnp.mean(err * err, axis=1, keepdims=True), axis=0, keepdims=True)
        dy = err * (1.0 / D)
        dn = dy * v[0]
        dx = rstd * (dn - n * jnp.mean(dn * n, axis=1, keepdims=True))
        return [dx], [jnp.broadcast_to(loss, (1, LANE)), _colsum(dy * n)]
    return _rowwise(name, fn, [x, target], [g], [(D, F32)], [LANE, D], tr)


_INV_SQRT2 = 1.0 / math.sqrt(2.0)
_INV_SQRT2PI = 1.0 / math.sqrt(2.0 * math.pi)


def _gelu(a):
    return 0.5 * a * (1.0 + lax.erf(a * _INV_SQRT2))


def _gelu_grad(a):
    return 0.5 * (1.0 + lax.erf(a * _INV_SQRT2)) + a * jnp.exp(-0.5 * a * a) * _INV_SQRT2PI


def _ln_stats(v):
    mu = jnp.mean(v, axis=1, keepdims=True)
    vc = v - mu
    rstd = lax.rsqrt(jnp.mean(vc * vc, axis=1, keepdims=True) + EPS)
    return vc * rstd, rstd


def _chunk_mix(w_ref, vn, tr, G):
    rows = []
    for ch in range(tr // LANE):
        cols = []
        for g in range(G):
            blk = vn[ch * LANE:(ch + 1) * LANE, g * LANE:(g + 1) * LANE]
            cols.append(jnp.dot(w_ref[g], blk, preferred_element_type=F32))
        rows.append(jnp.concatenate(cols, axis=1))
    return jnp.concatenate(rows, axis=0)


def _sgu_mid_fwd(a, ln_g, ln_b, w_mask, bias_full, tr):
    S, D2 = a.shape
    D = D2 // 2
    G = D // LANE

    def body(a_ref, g_ref, b_ref, w_ref, bias_ref, y_ref):
        u = _gelu(a_ref[:, :D])
        v = _gelu(a_ref[:, D:])
        vhat, _ = _ln_stats(v)
        vn = (vhat * g_ref[...] + b_ref[...]).astype(BF16)
        sv = _chunk_mix(w_ref, vn, tr, G) + jnp.concatenate([bias_ref[...]] * (tr // LANE), axis=0)
        y_ref[...] = (u * sv).astype(BF16)

    return pl.pallas_call(
        body, name="sgu_mid_fwd", grid=(S // tr,),
        in_specs=[pl.BlockSpec((tr, D2), lambda i: (i, 0)), pl.BlockSpec((1, D), lambda i: (0, 0)),
                  pl.BlockSpec((1, D), lambda i: (0, 0)), pl.BlockSpec((G, LANE, LANE), lambda i: (0, 0, 0)),
                  pl.BlockSpec((LANE, D), lambda i: (0, 0))],
        out_specs=pl.BlockSpec((tr, D), lambda i: (i, 0)), out_shape=SDS((S, D), BF16),
        compiler_params=_params(("arbitrary",)))(a, ln_g, ln_b, w_mask, bias_full)


def _sgu_mid_bwd(a, dy, ln_g, ln_b, w_mask, bias_full, tr):
    S, D2 = a.shape
    D = D2 // 2
    G = D // LANE
    nch = tr // LANE

    def body(a_ref, dy_ref, g_ref, b_ref, w_ref, bias_ref, da_ref, dw_ref, dbias_ref, dg_ref, db_ref, dbin_ref):
        au, av = a_ref[:, :D], a_ref[:, D:]
        u = _gelu(au)
        v = _gelu(av)
        vhat, rstd = _ln_stats(v)
        vn = (vhat * g_ref[...] + b_ref[...]).astype(BF16)
        sv = _chunk_mix(w_ref, vn, tr, G) + jnp.concatenate([bias_ref[...]] * nch, axis=0)
        dy_ = dy_ref[...].astype(F32)
        du = dy_ * sv
        dsv = dy_ * u
        dsv_b = dsv.astype(BF16)
        first = pl.program_id(0) == 0

        @pl.when(first)
        def _():
            dw_ref[...] = jnp.zeros_like(dw_ref)
            dbias_ref[...] = jnp.zeros_like(dbias_ref)
            dg_ref[...] = jnp.zeros_like(dg_ref)
            db_ref[...] = jnp.zeros_like(db_ref)
            dbin_ref[...] = jnp.zeros_like(dbin_ref)

        rows = []
        dbias = None
        for ch in range(nch):
            r0 = ch * LANE
            cols = []
            for g in range(G):
                c0 = g * LANE
                ds_blk = dsv_b[r0:r0 + LANE, c0:c0 + LANE]
                vn_blk = vn[r0:r0 + LANE, c0:c0 + LANE]
                cols.append(lax.dot_general(w_ref[g], ds_blk, (((0,), (0,)), ((), ())),
                                            preferred_element_type=F32))
                dw_ref[g] += lax.dot_general(ds_blk, vn_blk, (((1,), (1,)), ((), ())),
                                             preferred_element_type=F32)
            rows.append(jnp.concatenate(cols, axis=1))
            blk = dsv[r0:r0 + LANE, :]
            dbias = blk if dbias is None else dbias + blk
        dvn = jnp.concatenate(rows, axis=0)
        dbias_ref[...] += dbias
        dg_ref[...] += _colsum(dvn * vhat)
        db_ref[...] += _colsum(dvn)
        dvh = dvn * g_ref[...]
        dv = rstd * (dvh - jnp.mean(dvh, axis=1, keepdims=True)
                     - vhat * jnp.mean(dvh * vhat, axis=1, keepdims=True))
        da_u = du * _gelu_grad(au)
        da_v = dv * _gelu_grad(av)
        da_ref[:, :D] = da_u.astype(BF16)
        da_ref[:, D:] = da_v.astype(BF16)
        dbin_ref[:, :D] += _colsum(da_u)
        dbin_ref[:, D:] += _colsum(da_v)

    full = lambda shp: pl.BlockSpec(shp, lambda i, nd=len(shp): (0,) * nd)
    return pl.pallas_call(
        body, name="sgu_mid_bwd", grid=(S // tr,),
        in_specs=[pl.BlockSpec((tr, D2), lambda i: (i, 0)), pl.BlockSpec((tr, D), lambda i: (i, 0)),
                  full((1, D)), full((1, D)), full((G, LANE, LANE)), full((LANE, D))],
        out_specs=[pl.BlockSpec((tr, D2), lambda i: (i, 0)), full((G, LANE, LANE)), full((LANE, D)),
                   full((1, D)), full((1, D)), full((1, D2))],
        out_shape=[SDS((S, D2), BF16), SDS((G, LANE, LANE), F32), SDS((LANE, D), F32),
                   SDS((1, D), F32), SDS((1, D), F32), SDS((1, D2), F32)],
        compiler_params=_params(("arbitrary",)))(a, dy, ln_g, ln_b, w_mask, bias_full)


def _tri(lower):
    r = lax.broadcasted_iota(jnp.int32, (LANE, LANE), 0)
    c = lax.broadcasted_iota(jnp.int32, (LANE, LANE), 1)
    return jnp.where((c <= r) if lower else (c >= r), 1.0, 0.0).astype(F32)


def _fox_gate_fwd(fl, bf_pad):
    S = fl.shape[0]
    nblk = S // LANE

    def body(fl_ref, b_ref, f_ref):
        tri = _tri(True)

        def step(i, carry):
            r0 = pl.multiple_of(i * LANE, LANE)
            z = fl_ref[pl.ds(r0, LANE), :] + b_ref[...]
            logf = jnp.minimum(z, 0.0) - jnp.log1p(jnp.exp(-jnp.abs(z)))
            f = jnp.dot(tri, logf, preferred_element_type=F32, precision=lax.Precision.HIGHEST) + carry
            f_ref[pl.ds(r0, LANE), :] = f
            return f[LANE - 1:LANE, :]
        lax.fori_loop(0, nblk, step, jnp.zeros((1, LANE), F32))

    return pl.pallas_call(body, name="fox_gate_fwd", out_shape=SDS((S, LANE), F32),
                          compiler_params=_params())(fl, bf_pad)


def _fox_gate_bwd(fl, dF, bf_pad):
    S = fl.shape[0]
    nblk = S // LANE

    def body(fl_ref, df_ref, b_ref, dfl_ref, db_ref):
        tri = _tri(True)

        def prefix(i, carry):
            r0 = pl.multiple_of(i * LANE, LANE)
            p = jnp.dot(tri, df_ref[pl.ds(r0, LANE), :], preferred_element_type=F32,
                        precision=lax.Precision.HIGHEST) + carry
            dfl_ref[pl.ds(r0, LANE), :] = p
            return p[LANE - 1:LANE, :]
        total = lax.fori_loop(0, nblk, prefix, jnp.zeros((1, LANE), F32))
        db_ref[...] = jnp.zeros_like(db_ref)

        def finish(i, carry):
            r0 = pl.multiple_of(i * LANE, LANE)
            dlogf = total - dfl_ref[pl.ds(r0, LANE), :] + df_ref[pl.ds(r0, LANE), :]
            z = fl_ref[pl.ds(r0, LANE), :] + b_ref[...]
            dfl = dlogf / (1.0 + jnp.exp(z))
            dfl_ref[pl.ds(r0, LANE), :] = dfl
            db_ref[...] += _colsum(dfl)
            return carry
        lax.fori_loop(0, nblk, finish, 0)

    return pl.pallas_call(body, name="fox_gate_bwd", out_shape=[SDS((S, LANE), F32), SDS((1, LANE), F32)],
                          compiler_params=_params())(fl, dF, bf_pad)


def _lane_pick(blk, h):
    lane = lax.broadcasted_iota(jnp.int32, blk.shape, 1)
    return jnp.sum(jnp.where(lane == h, blk, 0.0), axis=1, keepdims=True)


_NEG = -1e30
_LOG2E = 1.0 / math.log(2.0)
_LN2 = math.log(2.0)


def _attn_fwd(qkv, F_sh, F_rows, H, tq, comm=None):
    S = qkv.shape[0]
    D = H * LANE
    nq = S // tq
    scale = 1.0 / math.sqrt(LANE)

    def body(q_ref, k_ref, v_ref, fsh_ref, frow_ref, o_ref, lse_ref):
        h, i = pl.program_id(0), pl.program_id(1)
        q = q_ref[...]
        fq = _lane_pick(fsh_ref[...], h) * _LOG2E

        def block(j, carry, diagonal):
            m, l, acc = carry
            r0 = pl.multiple_of(j * tq, tq)
            k = k_ref[pl.ds(r0, tq), :]
            v = v_ref[pl.ds(r0, tq), :]
            s = lax.dot_general(q, k, (((1,), (1,)), ((), ())), preferred_element_type=F32) * (scale * _LOG2E)
            s = s + (fq - frow_ref[j])
            if diagonal:
                row = lax.broadcasted_iota(jnp.int32, (tq, tq), 0)
                col = lax.broadcasted_iota(jnp.int32, (tq, tq), 1)
                s = jnp.where(col <= row, s, _NEG)
            m_new = jnp.maximum(m, jnp.max(s, axis=1, keepdims=True))
            alpha = jnp.exp2(m - m_new)
            p = jnp.exp2(s - m_new)
            l = alpha * l + jnp.sum(p, axis=1, keepdims=True)
            acc = alpha * acc + jnp.dot(p.astype(BF16), v, preferred_element_type=F32)
            return m_new, l, acc

        init = (jnp.full((tq, 1), _NEG, F32), jnp.zeros((tq, 1), F32), jnp.zeros((tq, LANE), F32))
        carry = lax.fori_loop(0, i, lambda j, cr: block(j, cr, False), init)
        m, l, acc = block(i, carry, True)
        o_ref[...] = (acc / l).astype(BF16)
        lse_ref[...] = (m + jnp.log2(l)) * _LN2

    return _call(
        body, "attn_fwd", grid=(H, nq),
        in_specs=[pl.BlockSpec((tq, LANE), lambda h, i: (i, h)),
                  pl.BlockSpec((S, LANE), lambda h, i: (0, H + h)),
                  pl.BlockSpec((S, LANE), lambda h, i: (0, 2 * H + h)),
                  pl.BlockSpec((tq, LANE), lambda h, i: (i, 0)),
                  pl.BlockSpec((None, nq, 1, tq), lambda h, i: (h, 0, 0, 0))],
        out_specs=[pl.BlockSpec((tq, LANE), lambda h, i: (i, h)),
                   pl.BlockSpec((None, tq, 1), lambda h, i: (h, i, 0))],
        out_shape=[SDS((S, D), BF16), SDS((H, S, 1), F32)],
        sem=("parallel", "arbitrary"), comm=comm, args=[qkv, qkv, qkv, F_sh, F_rows])


def _attn_bwd(qkv, do, F_sh, A_rows, delta_rows, H, tq, comm=None):
    S = qkv.shape[0]
    D = H * LANE
    nq = S // tq
    scale = 1.0 / math.sqrt(LANE)

    def body(q_ref, do_ref, k_ref, v_ref, fsh_ref, a_ref, dl_ref, dq_ref, dk_ref, dv_ref, df_ref, dfr_ref,
             dq_acc, dfr_acc):
        h, j = pl.program_id(0), pl.program_id(1)

        @pl.when(j == 0)
        def _():
            dq_acc[...] = jnp.zeros_like(dq_acc)
            dfr_acc[...] = jnp.zeros_like(dfr_acc)

        k = k_ref[...]
        v = v_ref[...]
        fk = _lane_pick(fsh_ref[...], h) * _LOG2E

        def block(i, carry, diagonal):
            dk, dv, df = carry
            r0 = pl.multiple_of(i * tq, tq)
            q = q_ref[pl.ds(r0, tq), :]
            do_ = do_ref[pl.ds(r0, tq), :]
            st = lax.dot_general(k, q, (((1,), (1,)), ((), ())), preferred_element_type=F32) * (scale * _LOG2E)
            arg = st + (a_ref[i] - fk)
            if diagonal:
                row = lax.broadcasted_iota(jnp.int32, (tq, tq), 0)
                col = lax.broadcasted_iota(jnp.int32, (tq, tq), 1)
                arg = jnp.where(row <= col, arg, _NEG)
            pt = jnp.exp2(arg)
            dv = dv + jnp.dot(pt.astype(BF16), do_, preferred_element_type=F32)
            dpt = lax.dot_general(v, do_, (((1,), (1,)), ((), ())), preferred_element_type=F32)
            dst = pt * (dpt - dl_ref[i])
            df = df + jnp.sum(dst, axis=1, keepdims=True)
            dfr_acc[i] += jnp.sum(dst, axis=0, keepdims=True)
            dsb = dst.astype(BF16)
            dk = dk + jnp.dot(dsb, q, preferred_element_type=F32)
            dq_acc[pl.ds(r0, tq), :] += lax.dot_general(dsb, k, (((0,), (0,)), ((), ())),
                                                        preferred_element_type=F32)
            return dk, dv, df

        z = jnp.zeros((tq, LANE), F32)
        carry = block(j, (z, z, jnp.zeros((tq, 1), F32)), True)
        dk, dv, df = lax.fori_loop(j + 1, nq, lambda i, cr: block(i, cr, False), carry)
        dk_ref[...] = (dk * scale).astype(BF16)
        dv_ref[...] = dv.astype(BF16)
        df_ref[...] = -df

        @pl.when(j == nq - 1)
        def _():
            dq_ref[...] = (dq_acc[...] * scale).astype(BF16)
            dfr_ref[...] = dfr_acc[...]

    return _call(
        body, "attn_bwd", grid=(H, nq),
        in_specs=[pl.BlockSpec((S, LANE), lambda h, j: (0, h)),
                  pl.BlockSpec((S, LANE), lambda h, j: (0, h)),
                  pl.BlockSpec((tq, LANE), lambda h, j: (j, H + h)),
                  pl.BlockSpec((tq, LANE), lambda h, j: (j, 2 * H + h)),
                  pl.BlockSpec((tq, LANE), lambda h, j: (j, 0)),
                  pl.BlockSpec((None, nq, 1, tq), lambda h, j: (h, 0, 0, 0)),
                  pl.BlockSpec((None, nq, 1, tq), lambda h, j: (h, 0, 0, 0))],
        out_specs=[pl.BlockSpec((S, LANE), lambda h, j: (0, h)),
                   pl.BlockSpec((tq, LANE), lambda h, j: (j, h)),
                   pl.BlockSpec((tq, LANE), lambda h, j: (j, h)),
                   pl.BlockSpec((None, tq, 1), lambda h, j: (h, j, 0)),
                   pl.BlockSpec((None, nq, 1, tq), lambda h, j: (h, 0, 0, 0))],
        out_shape=[SDS((S, D), BF16), SDS((S, D), BF16), SDS((S, D), BF16), SDS((H, S, 1), F32),
                   SDS((H, nq, 1, tq), F32)],
        scratch=[pltpu.VMEM((S, LANE), F32), pltpu.VMEM((nq, 1, tq), F32)],
        sem=("parallel", "arbitrary"), comm=comm, args=[qkv, do, qkv, qkv, F_sh, A_rows, delta_rows])


def _head_dots(do, o, H, tr):
    def fn(r, v):
        prod = r[0].astype(F32) * r[1].astype(F32)
        lane = lax.broadcasted_iota(jnp.int32, (prod.shape[0], LANE), 1)
        out = jnp.zeros((prod.shape[0], LANE), F32)
        for h in range(H):
            s = jnp.sum(prod[:, h * LANE:(h + 1) * LANE], axis=1, keepdims=True)
            out = jnp.where(lane == h, s, out)
        return [out], []
    return _rowwise("attn_delta", fn, [do, o], [], [(LANE, F32)], [], tr)[0]


def _adamw(name, w, g, m, v, layer=None, prev=None):
    R, C = g.shape
    tr = _pick(R, max(8, (512 * 1024) // max(C, 1) // 8 * 8), 8)
    c1 = 1.0 - ADAM_B1 ** ADAM_STEP
    c2 = 1.0 - ADAM_B2 ** ADAM_STEP

    def body(w_ref, g_ref, m_ref, v_ref, go_ref, d_ref, mo_ref, vo_ref):
        g_ = g_ref[...]
        m_ = ADAM_B1 * m_ref[...] + (1.0 - ADAM_B1) * g_
        v_ = ADAM_B2 * v_ref[...] + (1.0 - ADAM_B2) * (g_ * g_)
        go_ref[...] = g_
        d_ref[...] = -ADAM_LR * ((m_ / c1) / (jnp.sqrt(v_ / c2) + ADAM_EPS) + ADAM_WD * w_ref[...])
        mo_ref[...] = m_
        vo_ref[...] = v_

    gspec = pl.BlockSpec((tr, C), lambda i: (i, 0))
    pspec = gspec if layer is None else pl.BlockSpec((None, tr, C), lambda i: (layer, i, 0))
    return _call(body, name, grid=(R // tr,), in_specs=[pspec, gspec, pspec, pspec], out_specs=[pspec] * 4,
                 out_shape=[SDS(w.shape, F32)] * 4, sem=("parallel",), args=[w, g, m, v],
                 keep=[(p, k) for k, p in enumerate(prev)] if prev else ())


def _sum_slots(name, a):
    n, R, C = a.shape
    tr = _pick(R, 256, 8)

    def body(a_ref, o_ref):
        acc = a_ref[0]
        for k in range(1, n):
            acc = acc + a_ref[k]
        o_ref[...] = acc

    return pl.pallas_call(body, name=name, grid=(R // tr,),
                          in_specs=[pl.BlockSpec((n, tr, C), lambda i: (0, i, 0))],
                          out_specs=pl.BlockSpec((tr, C), lambda i: (i, 0)), out_shape=SDS((R, C), F32),
                          compiler_params=_params(("parallel",)))(a)


def _silu_rows(name, a):
    def body(a_ref, o_ref):
        z = a_ref[...]
        o_ref[...] = (z / (1.0 + jnp.exp(-z))).astype(BF16)
    return pl.pallas_call(body, name=name, out_shape=SDS(a.shape, BF16), compiler_params=_params())(a)


def _place():
    return lax.axis_index("x"), lax.axis_index("y"), lax.axis_index("c")


def _rcopy(src, dst, ssem, rsem, dev):
    return pltpu.make_async_remote_copy(src_ref=src, dst_ref=dst, send_sem=ssem, recv_sem=rsem,
                                        device_id=dev, device_id_type=MESH)


def _gather_devices(name, buf):
    R, C = buf.shape

    def body(b_ref, out_ref, ssem, rsem, lsem):
        x, y, c = _place()
        me = 4 * x + 2 * y + c
        mine = pltpu.make_async_copy(b_ref, out_ref.at[me], lsem)
        mine.start()
        peers = []
        for k in range(1, N_DEV):
            fx, fy, fc = (k >> 2) & 1, (k >> 1) & 1, k & 1
            peers.append((x ^ fx, y ^ fy, c ^ fc))
        sends = [_rcopy(b_ref, out_ref.at[me], ssem.at[k], rsem.at[k], p) for k, p in enumerate(peers)]
        for cp in sends:
            cp.start()
        for k, (px, py, pc) in enumerate(peers):
            _rcopy(b_ref, out_ref.at[4 * px + 2 * py + pc], ssem.at[k], rsem.at[k], (px, py, pc)).wait_recv()
        for cp in sends:
            cp.wait_send()
        mine.wait()

    return pl.pallas_call(body, name=name, in_specs=[ANY], out_specs=ANY, out_shape=SDS((N_DEV, R, C), buf.dtype),
                          scratch_shapes=[pltpu.SemaphoreType.DMA((N_DEV - 1,)), pltpu.SemaphoreType.DMA((N_DEV - 1,)),
                                          pltpu.SemaphoreType.DMA])(buf)


def _other_chips(x, y):
    return [(1 - x, y), (x, 1 - y), (1 - x, 1 - y)]


def _cast_slabs(name, ws, idx, comm=None):
    n = len(ws)
    steps = next(s for s in (8, 4, 2, 1) if all(w.shape[0] % (16 * s) == 0 for w in ws))

    def body(s_ref, *refs):
        for w_ref, o_ref in zip(refs[:n], refs[n:]):
            o_ref[...] = w_ref[...].astype(BF16)

    return _call(body, name, grid=(steps,), prefetch=idx, comm=comm, sem=("parallel",), args=list(ws),
                 in_specs=[pl.BlockSpec((w.shape[0] // steps, w.shape[1]), lambda i, s: (i, 0)) for w in ws],
                 out_specs=[pl.BlockSpec((None, w.shape[0] // steps, w.shape[1]), lambda i, s: (s[1], i, 0))
                            for w in ws],
                 out_shape=[SDS((N_CHIPS,) + w.shape, BF16) for w in ws])


def _job_gather_ici(buf):
    _, R, C = buf.shape
    hR = R // 2
    assert hR * 2 == R

    def views(o):
        x, y, c = _place()
        return c, _other_chips(x, y), o.at[2 * x + y, pl.ds(c * hR, hR), :]

    def start(i, o, ss, rs):
        c, chips, mine = views(o[0])
        for j, (cx, cy) in enumerate(chips):
            _rcopy(mine, mine, ss.at[j], rs.at[j], (cx, cy, c)).start()

    def wait(i, o, ss, rs):
        c, chips, mine = views(o[0])
        for j, (cx, cy) in enumerate(chips):
            cp = _rcopy(mine, o[0].at[2 * cx + cy, pl.ds(c * hR, hR), :], ss.at[j], rs.at[j], (cx, cy, c))
            cp.wait_send()
            cp.wait_recv()

    return _Job([buf], [SDS(buf.shape, buf.dtype)], {0: 0}, 3, start, wait)


def _job_gather_pair(buf):
    _, R, C = buf.shape
    hR = R // 2

    def views(o):
        x, y, c = _place()
        return c, (x, y, 1 - c), _other_chips(x, y)

    def start(i, o, ss, rs):
        c, sib, chips = views(o[0])
        for j, (cx, cy) in enumerate(chips):
            got = o[0].at[2 * cx + cy, pl.ds(c * hR, hR), :]
            _rcopy(got, got, ss.at[j], rs.at[j], sib).start()

    def wait(i, o, ss, rs):
        c, sib, chips = views(o[0])
        for j, (cx, cy) in enumerate(chips):
            cp = _rcopy(o[0].at[2 * cx + cy, pl.ds(c * hR, hR), :],
                        o[0].at[2 * cx + cy, pl.ds((1 - c) * hR, hR), :], ss.at[j], rs.at[j], sib)
            cp.wait_send()
            cp.wait_recv()

    return _Job([buf], [SDS(buf.shape, buf.dtype)], {0: 0}, 3, start, wait)


class _Gather:
    def __init__(self, bufs):
        self.todo, self.half, self.done = dict(bufs), {}, {}

    def comm(self, admit=()):
        second, first = list(self.half), list(admit)
        jobs = [_job_gather_pair(self.half[n]) for n in second] + [_job_gather_ici(self.todo[n]) for n in first]

        def absorb(outs):
            for n, o in zip(second + first, outs):
                if n in self.half:
                    del self.half[n]
                    self.done[n] = o
                else:
                    del self.todo[n]
                    self.half[n] = o
        return (jobs, absorb) if jobs else None


def _job_rs_pair(g4):
    _, R, C = g4.shape
    hR = R // 2
    assert hR * 2 == R

    def desc(i, o, ss, rs):
        x, y, c = _place()
        return _rcopy(i[0].at[:, pl.ds((1 - c) * hR, hR), :], o[0], ss.at[0], rs.at[0], (x, y, 1 - c))

    return _Job([g4], [SDS((N_CHIPS, hR, C), F32)], {}, 1,
                lambda i, o, ss, rs: desc(i, o, ss, rs).start(), lambda i, o, ss, rs: desc(i, o, ss, rs).wait())


def _job_rs_chips(p4):
    _, hR, C = p4.shape

    def descs(i, o, ss, rs):
        x, y, c = _place()
        return [_rcopy(i[0].at[2 * cx + cy], o[0].at[j], ss.at[j], rs.at[j], (cx, cy, c))
                for j, (cx, cy) in enumerate(_other_chips(x, y))]

    def start(i, o, ss, rs):
        for cp in descs(i, o, ss, rs):
            cp.start()

    def wait(i, o, ss, rs):
        for cp in descs(i, o, ss, rs):
            cp.wait_send()
            cp.wait_recv()

    return _Job([p4], [SDS((3, hR, C), p4.dtype)], {}, 3, start, wait)


def _job_rs_join(buf):
    R, C = buf.shape
    hR = R // 2

    def desc(o, ss, rs, recv):
        x, y, c = _place()
        mine = o[0].at[pl.ds(c * hR, hR), :]
        return _rcopy(mine, o[0].at[pl.ds((1 - c) * hR, hR), :] if recv else mine, ss.at[0], rs.at[0], (x, y, 1 - c))

    def wait(i, o, ss, rs):
        cp = desc(o, ss, rs, True)
        cp.wait_send()
        cp.wait_recv()

    return _Job([buf], [SDS(buf.shape, buf.dtype)], {0: 0}, 1,
                lambda i, o, ss, rs: desc(o, ss, rs, False).start(), wait)


def _rs_add_pair(name, g4, recv, idx):
    _, R, C = g4.shape
    hR = R // 2
    tr = _pick(hR, max(16, (1 << 20) // C // 16 * 16), 16)
    nb = hR // tr

    def body(s_ref, a_ref, b_ref, pb_ref, po_ref):
        s = a_ref[...] + b_ref[...]
        pb_ref[...] = s.astype(BF16)

        @pl.when(pl.program_id(1) == s_ref[1])
        def _():
            po_ref[...] = s

    gs = pltpu.PrefetchScalarGridSpec(
        num_scalar_prefetch=1, grid=(nb, N_CHIPS),
        in_specs=[pl.BlockSpec((None, tr, C), lambda i, k, s: (k, s[0] * nb + i, 0)),
                  pl.BlockSpec((None, tr, C), lambda i, k, s: (k, i, 0))],
        out_specs=[pl.BlockSpec((None, tr, C), lambda i, k, s: (k, i, 0)),
                   pl.BlockSpec((tr, C), lambda i, k, s: (i, 0))])
    return pl.pallas_call(body, name=name, grid_spec=gs, out_shape=[SDS((N_CHIPS, hR, C), BF16), SDS((hR, C), F32)],
                          compiler_params=_params(("parallel", "arbitrary")))(idx, g4, recv)


def _rs_add_chips(name, own, recv3, idx):
    hR, C = own.shape
    tr = _pick(hR, max(16, (1 << 20) // C // 16 * 16), 16)
    nb = hR // tr

    def body(s_ref, a_ref, b_ref, o_ref):
        o_ref[...] = ((a_ref[...] + b_ref[0].astype(F32)) + b_ref[1].astype(F32)) + b_ref[2].astype(F32)

    gs = pltpu.PrefetchScalarGridSpec(
        num_scalar_prefetch=1, grid=(nb,),
        in_specs=[pl.BlockSpec((tr, C), lambda i, s: (i, 0)), pl.BlockSpec((3, tr, C), lambda i, s: (0, i, 0))],
        out_specs=pl.BlockSpec((tr, C), lambda i, s: (s[0] * nb + i, 0)))
    return pl.pallas_call(body, name=name, grid_spec=gs, out_shape=SDS((2 * hR, C), F32),
                          compiler_params=_params(("parallel",)))(idx, own, recv3)


class _ReduceScatter:
    def __init__(self, idx):
        self.idx, self.items, self.done = idx, [], {}

    def push(self, tag, g4):
        self.items.append([tag, 0, g4])

    def comm(self):
        cur = list(self.items)
        jobs = []
        for tag, stage, data in cur:
            jobs.append(_job_rs_pair(data) if stage == 0 else _job_rs_chips(data[0]) if stage == 1
                        else _job_rs_join(data))

        def absorb(outs):
            for item, o in zip(cur, outs):
                tag, stage, data = item
                if stage == 0:
                    item[1:] = [1, _rs_add_pair(tag + "_rs_add2", data, o, self.idx)]
                elif stage == 1:
                    item[1:] = [2, _rs_add_chips(tag + "_rs_add4", data[1], o, self.idx)]
                else:
                    self.items.remove(item)
                    self.done[tag] = o
        return (jobs, absorb) if jobs else None


def _resid_epilogue(accs, ex):
    return [ex[0] + ex[1] * accs[0], accs[0]]


def _swiglu_epilogue(accs, ex):
    gt, up = accs
    return [gt, up, gt / (1.0 + jnp.exp(-gt)) * up]


def _swiglu_bwd_epilogue(accs, ex):
    dact = accs[0]
    gt, up = ex[0].astype(F32), ex[1].astype(F32)
    sg = 1.0 / (1.0 + jnp.exp(-gt))
    return [dact * up * (sg * (1.0 + gt * (1.0 - sg))), dact * (gt * sg)]


def kernel(x, c, ada_w, ada_b, norm_mix_g, norm_ffn_g, a_w_in, a_b_in, a_ln_g, a_ln_b, a_w_s, a_b_s, a_w_out, b_w_in, b_b_f, b_w_out, ffn_w_gate, ffn_w_up, ffn_w_down, final_g, loss_target, m_ada_w, m_ada_b, m_norm_mix_g, m_norm_ffn_g, m_a_w_in, m_a_b_in, m_a_ln_g, m_a_ln_b, m_a_w_s, m_a_b_s, m_a_w_out, m_b_w_in, m_b_b_f, m_b_w_out, m_ffn_w_gate, m_ffn_w_up, m_ffn_w_down, m_final_g, v_ada_w, v_ada_b, v_norm_mix_g, v_norm_ffn_g, v_a_w_in, v_a_b_in, v_a_ln_g, v_a_ln_b, v_a_w_s, v_a_b_s, v_a_w_out, v_b_w_in, v_b_b_f, v_b_w_out, v_ffn_w_gate, v_ffn_w_up, v_ffn_w_down, v_final_g):
    S, D = x.shape[1], x.shape[2]
    H = D // LANE
    G = D // LANE
    FH = ffn_w_down.shape[1] * N_CHIPS
    depth = ada_w.shape[0]
    assert depth == 2 and a_w_in.shape[0] == 1 and b_w_in.shape[0] == 1
    mx, my, mc = _place()
    chip = 2 * mx + my
    dev = 4 * mx + 2 * my + mc
    x0 = x[0]
    target = loss_target[0]
    tr = _pick(S, 256, 8)
    tm = _pick(S, 512, 8)
    tq = _pick(S, 512, LANE)
    nq = S // tq

    idx = jnp.stack([mc, chip]).astype(jnp.int32)
    shards = {"a_in": a_w_in[0], "a_out": a_w_out[0], "b_in": b_w_in[0], "b_out": b_w_out[0]}
    for l in range(depth):
        shards.update({f"wg{l}": ffn_w_gate[l], f"wu{l}": ffn_w_up[l], f"wd{l}": ffn_w_down[l]})
    first = ["a_in", "a_out"]
    rest = [n for n in shards if n not in first]
    ag = _Gather(dict(zip(first, _cast_slabs("cast_first", [shards[n] for n in first], idx))))
    ag.todo.update(zip(rest, _cast_slabs("cast_rest", [shards[n] for n in rest], idx, comm=ag.comm(first))))
    _comm_call("ag_first_pair", ag.comm(["wg0"]))
    wa_in4 = ag.done["a_in"]
    wa_out = ag.done["a_out"].reshape(D, D)

    c_all = _gather_devices("ag_c", jnp.pad(c, ((0, 7), (0, 0)))).reshape(N_DEV, 8, D)[:, 0, :]
    c_act = _silu_rows("c_silu", c_all)
    n_loc = ada_w.shape[2]
    mods = []
    for l in range(depth):
        b_loc = lax.dynamic_slice_in_dim(ada_b[l:l + 1], chip * n_loc, n_loc, axis=1)
        mods.append(_mm(f"ada_fwd{l}", "nn", [c_act], [ada_w[l]], M=N_DEV, N=n_loc, K=D, tm=N_DEV,
                        tn=_pick(n_loc, 1024, LANE), tk=D, extras=[("row", b_loc)],
                        epilogue=lambda accs, ex: [accs[0] + ex[0]])[0])
    mod_all = _gather_devices("ag_mod", jnp.concatenate(mods, axis=1))
    mod_mine = lax.dynamic_index_in_dim(mod_all[0::2], dev, axis=1, keepdims=False)
    mod_mine = mod_mine.reshape(N_CHIPS, depth, n_loc).transpose(1, 0, 2).reshape(depth, 1, N_MOD * D)
    mod = [[mod_mine[l, :, i * D:(i + 1) * D] for i in range(N_MOD)] for l in range(depth)]

    row = lambda a: a.reshape(1, -1)

    tril = jnp.tril(jnp.ones((LANE, LANE), dtype=bool))
    w_mask = jnp.where(tril[None], a_w_s[0], 0.0).astype(BF16)
    bias_full = jnp.repeat(a_b_s[0].T, LANE, axis=1)
    bf_pad = jnp.pad(b_b_f, ((0, 0), (0, LANE - H)))

    admit = {"sgu_in": ["wu0"], "mix_out0": ["wd0"], "ffn_in0": ["b_in", "b_out"], "ffn_out0": ["wg1"],
             "fox_qkv": ["wu1"], "attn_fwd": ["wd1"], "mix_out1": []}
    saved = []
    xs = x0
    for l in range(depth):
        sh1, sc1, g1, sh2, sc2, g2 = mod[l]
        st = {"x_in": xs}
        h1 = _normmod_fwd(f"normmod_mix{l}", xs, row(norm_mix_g[l]), sc1, sh1, tr)
        st["h1"] = h1
        if l == 0:
            a = _mm("sgu_in", "nn", [h1], [wa_in4], M=S, N=2 * D, K=D, tm=tm, tn=2 * D // N_CHIPS, tk=D,
                    b_stacked=True, extras=[("row", a_b_in)], comm=ag.comm(admit["sgu_in"]),
                    epilogue=lambda accs, ex: [accs[0] + ex[0]])[0]
            y = _sgu_mid_fwd(a, a_ln_g, a_ln_b, w_mask, bias_full, tr)
            st["a"], st["y"] = a, y
            w_o, mix_out = wa_out, y
        else:
            wb_in4 = ag.done["b_in"]
            n_last = wb_in4.shape[2] - H
            w_qkv = jnp.concatenate([wb_in4[k] for k in range(N_CHIPS - 1)] + [wb_in4[-1][:, :n_last]], axis=1)
            w_f = jnp.pad(wb_in4[-1][:, n_last:], ((0, 0), (0, LANE - H)))
            qkv = _mm("fox_qkv", "nn", [h1], [w_qkv], M=S, N=3 * D, K=D, tm=tm, tn=_pick(3 * D, 1024, LANE),
                      tk=D, out_dtypes=(BF16,), comm=ag.comm(admit["fox_qkv"]))[0]
            fl = _mm("fox_f", "nn", [h1], [w_f], M=S, N=LANE, K=D, tm=tm, tn=LANE, tk=D)[0]
            F_sh = _fox_gate_fwd(fl, bf_pad)
            F_rows = F_sh[:, :H].T.reshape(H, nq, 1, tq)
            o, lse = _attn_fwd(qkv, F_sh, F_rows * _LOG2E, H, tq, comm=ag.comm(admit["attn_fwd"]))
            st.update(qkv=qkv, fl=fl, F_sh=F_sh, F_rows=F_rows, o=o, lse=lse, w_qkv=w_qkv, w_f=w_f)
            w_o, mix_out = ag.done["b_out"].reshape(D, D), o
        x1, out1 = _mm(f"mix_out{l}", "nn", [mix_out], [w_o], M=S, N=D, K=D, tm=tm, tn=_pick(D, 1024, LANE),
                       tk=D, extras=[("tile", xs), ("row", g1)], out_dtypes=(F32, BF16),
                       epilogue=_resid_epilogue, comm=ag.comm(admit[f"mix_out{l}"]))
        st["x_mid"], st["out1"] = x1, out1
        h2 = _normmod_fwd(f"normmod_ffn{l}", x1, row(norm_ffn_g[l]), sc2, sh2, tr)
        gt, up, act = _mm(f"ffn_in{l}", "nn", [h2, h2], [ag.done[f"wg{l}"], ag.done[f"wu{l}"]], M=S, N=FH, K=D,
                          tm=tm, tn=FH // N_CHIPS, tk=_pick(D, 1024, LANE), b_stacked=True, acc_of=[0, 1], n_acc=2,
                          out_dtypes=(BF16, BF16, BF16), epilogue=_swiglu_epilogue,
                          comm=ag.comm(admit.get(f"ffn_in{l}", ())))
        x2, out2 = _mm(f"ffn_out{l}", "nn", [act], [ag.done[f"wd{l}"].reshape(FH, D)], M=S, N=D, K=FH, tm=tm,
                       tn=_pick(D, 1024, LANE), tk=FH // N_CHIPS, extras=[("tile", x1), ("row", g2)],
                       out_dtypes=(F32, BF16), epilogue=_resid_epilogue,
                       comm=ag.comm(admit.get(f"ffn_out{l}", ())))
        st.update(h2=h2, gt=gt, up=up, act=act, out2=out2)
        saved.append(st)
        xs = x2
    assert not ag.todo and not ag.half
    wg4 = [ag.done[f"wg{l}"] for l in range(depth)]
    wu4 = [ag.done[f"wu{l}"] for l in range(depth)]
    wd = [ag.done[f"wd{l}"].reshape(FH, D) for l in range(depth)]
    wb_out = ag.done["b_out"].reshape(D, D)

    dx, loss_vec, g_final = _loss_head("loss_head", xs, target, row(final_g), tr)
    loss = lax.psum(loss_vec[0, 0], ("x", "y", "c"))

    rs = _ReduceScatter(idx)
    dmods = [None] * depth
    gmix = [None] * depth
    gffn = [None] * depth
    for l in reversed(range(depth)):
        sh1, sc1, g1, sh2, sc2, g2 = mod[l]
        st = saved[l]
        dog, dg2 = _gate_bwd(f"gate_ffn{l}", dx, st["out2"], g2, tr)
        dgt, dup = _mm(f"ffn_dact{l}", "nt", [dog], [wd[l]], M=S, N=FH, K=D, tm=tm, tn=FH // N_CHIPS, tk=D,
                       extras=[("tile", st["gt"]), ("tile", st["up"])], out_dtypes=(BF16, BF16),
                       epilogue=_swiglu_bwd_epilogue, comm=rs.comm())
        rs.push(f"wd{l}", _mm(f"ffn_dwd{l}", "tn", [st["act"]], [dog], M=FH, N=D, K=S, tm=FH // N_CHIPS,
                              tn=_pick(D, 1024, LANE), tk=_pick(S, 512, LANE),
                              comm=rs.comm())[0].reshape(N_CHIPS, FH // N_CHIPS, D))
        dh2 = _mm(f"ffn_dh{l}", "nt", [dgt, dup], [wg4[l], wu4[l]], M=S, N=D, K=FH, tm=tm,
                  tn=_pick(D, 1024, LANE), tk=FH // N_CHIPS, b_stacked=True, comm=rs.comm())[0]
        rs.push(f"wg{l}", _mm(f"ffn_dwg{l}", "tn", [st["h2"]], [dgt], M=D, N=FH, K=S, tm=_pick(D, 512, LANE),
                              tn=FH // N_CHIPS, tk=_pick(S, 1024, LANE), out_stacked=True, comm=rs.comm())[0])
        rs.push(f"wu{l}", _mm(f"ffn_dwu{l}", "tn", [st["h2"]], [dup], M=D, N=FH, K=S, tm=_pick(D, 512, LANE),
                              tn=FH // N_CHIPS, tk=_pick(S, 1024, LANE), out_stacked=True, comm=rs.comm())[0])
        dx, dsh2, dsc2, gffn[l] = _normmod_bwd(f"normmod_ffn_bwd{l}", st["x_mid"], dh2, dx,
                                               row(norm_ffn_g[l]), sc2, tr)
        dog, dg1 = _gate_bwd(f"gate_mix{l}", dx, st["out1"], g1, tr)
        if l == 0:
            dy = _mm("sgu_dy", "nt", [dog], [wa_out], M=S, N=D, K=D, tm=tm, tn=_pick(D, 1024, LANE), tk=D,
                     out_dtypes=(BF16,))[0]
            rs.push("a_out", _mm("sgu_dwout", "tn", [st["y"]], [dog], M=D, N=D, K=S, tm=_pick(D, 512, LANE),
                                 tn=_pick(D, 1024, LANE),
                                 tk=_pick(S, 1024, LANE))[0].reshape(N_CHIPS, D // N_CHIPS, D))
            da, dws, dbias, g_ln_g, g_ln_b, g_b_in = _sgu_mid_bwd(st["a"], dy, a_ln_g, a_ln_b, w_mask, bias_full, tr)
            rs.push("a_in", _mm("sgu_dwin", "tn", [st["h1"]], [da], M=D, N=2 * D, K=S, tm=_pick(D, 512, LANE),
                                tn=2 * D // N_CHIPS, tk=_pick(S, 1024, LANE), out_stacked=True, comm=rs.comm())[0])
            dh1 = _mm("sgu_dh", "nt", [da], [wa_in4], M=S, N=D, K=2 * D, tm=tm, tn=_pick(D, 1024, LANE),
                      tk=2 * D // N_CHIPS, b_stacked=True, comm=rs.comm())[0]
            g_w_s = jnp.where(tril[None], dws, 0.0)
            g_b_s = jnp.sum(dbias.reshape(LANE, G, LANE), axis=2).T
        else:
            do = _mm("fox_do", "nt", [dog], [wb_out], M=S, N=D, K=D, tm=tm, tn=_pick(D, 1024, LANE), tk=D,
                     out_dtypes=(BF16,))[0]
            rs.push("b_out", _mm("fox_dwout", "tn", [st["o"]], [dog], M=D, N=D, K=S, tm=_pick(D, 512, LANE),
                                 tn=_pick(D, 1024, LANE),
                                 tk=_pick(S, 1024, LANE))[0].reshape(N_CHIPS, D // N_CHIPS, D))
            delta = _head_dots(do, st["o"], H, tr)
            delta_rows = delta[:, :H].T.reshape(H, nq, 1, tq)
            A_rows = (st["F_rows"] - st["lse"].reshape(H, nq, 1, tq)) * _LOG2E
            w_qkv, w_f = st["w_qkv"], st["w_f"]
            dq, dk, dv, dF_k, dF_q = _attn_bwd(st["qkv"], do, st["F_sh"], A_rows, delta_rows, H, tq,
                                               comm=rs.comm())
            dF_sh = jnp.pad((dF_k.reshape(H, S) + dF_q.reshape(H, S)).T, ((0, 0), (0, LANE - H)))
            dfl, db_f = _fox_gate_bwd(st["fl"], dF_sh, bf_pad)
            dfl_b = dfl.astype(BF16)
            dh_f = _mm("fox_dh_f", "nt", [dfl_b], [w_f], M=S, N=D, K=LANE, tm=tm, tn=_pick(D, 1024, LANE),
                       tk=LANE)[0]
            tkq = _pick(D, 1024, LANE)
            dh1 = _mm("fox_dh", "nt", [dq, dk, dv], [w_qkv, w_qkv, w_qkv], M=S, N=D, K=D, tm=tm,
                      tn=_pick(D, 1024, LANE), tk=tkq, b_koffs=[0, D // tkq, 2 * D // tkq],
                      extras=[("tile", dh_f)], epilogue=lambda accs, ex: [accs[0] + ex[0]], comm=rs.comm())[0]
            parts = [_mm(f"fox_dw{nm}", "tn", [st["h1"]], [d_], M=D, N=D, K=S, tm=_pick(D, 512, LANE),
                         tn=_pick(D, 1024, LANE), tk=_pick(S, 1024, LANE))[0]
                     for nm, d_ in (("q", dq), ("k", dk), ("v", dv))]
            dwf = _mm("fox_dwf", "tn", [st["h1"]], [dfl_b], M=D, N=LANE, K=S, tm=_pick(D, 512, LANE), tn=LANE,
                      tk=_pick(S, 1024, LANE))[0]
            g_b_w_in = jnp.concatenate(parts + [dwf[:, :H]], axis=1)
            n_b = g_b_w_in.shape[1] // N_CHIPS
            rs.push("b_in", g_b_w_in.reshape(D, N_CHIPS, n_b).transpose(1, 0, 2))
            g_b_f = db_f[:, :H]
        dx, dsh1, dsc1, gmix[l] = _normmod_bwd(f"normmod_mix_bwd{l}", st["x_in"], dh1, dx,
                                               row(norm_mix_g[l]), sc1, tr)
        dmods[l] = jnp.concatenate([dsh1, dsc1, dg1, dsh2, dsc2, dg2], axis=1)
    grad_x = dx[None]

    small = [jnp.concatenate(dmods, axis=0), jnp.concatenate(gmix, axis=0), jnp.concatenate(gffn, axis=0),
             g_b_in, g_ln_g, g_ln_b, g_w_s[None], g_b_s[None], g_b_f, g_final[0]]
    small_w = [ada_b, norm_mix_g, norm_ffn_g, a_b_in, a_ln_g, a_ln_b, a_w_s, a_b_s, b_b_f, final_g]
    small_m = [m_ada_b, m_norm_mix_g, m_norm_ffn_g, m_a_b_in, m_a_ln_g, m_a_ln_b, m_a_w_s, m_a_b_s, m_b_b_f, m_final_g]
    small_v = [v_ada_b, v_norm_mix_g, v_norm_ffn_g, v_a_b_in, v_a_ln_g, v_a_ln_b, v_a_w_s, v_a_b_s, v_b_b_f, v_final_g]
    sizes = [w.size for w in small_w]
    total = sum(sizes)
    padded = -(-total // (8 * LANE)) * (8 * LANE)

    def pack(parts):
        flat = jnp.concatenate([p.reshape(-1) for p in parts])
        return jnp.pad(flat, (0, padded - total)).reshape(padded // LANE, LANE)

    def unpack(buf):
        flat = buf.reshape(-1)
        outs, off = [], 0
        for w, n in zip(small_w, sizes):
            outs.append(flat[off:off + n].reshape(w.shape))
            off += n
        return outs

    g_all = _gather_devices("ag_small_grads", pack(small))
    g_small = _sum_slots("sum_small_grads", g_all)
    _, d_small, m_small, v_small = _adamw("adamw_small", pack(small_w), g_small, pack(small_m), pack(small_v))
    sg, sd, sm, sv_ = unpack(g_small), unpack(d_small), unpack(m_small), unpack(v_small)

    n_dm = depth * N_MOD * D
    dmod_all = g_all.reshape(N_DEV, -1)[:, :n_dm].reshape(N_DEV, depth, N_MOD * D)
    u_ada = None
    for l in range(depth):
        dm_loc = lax.dynamic_slice_in_dim(dmod_all[:, l, :], chip * n_loc, n_loc, axis=1).astype(BF16)
        g_ada = _mm(f"ada_dw{l}", "tn", [c_act], [dm_loc], M=D, N=n_loc, K=N_DEV, tm=_pick(D, 512, LANE),
                    tn=_pick(n_loc, 1024, LANE), tk=N_DEV)[0]
        u_ada = _adamw(f"adamw_ada_w{l}", ada_w, g_ada, m_ada_w, v_ada_w, layer=l, prev=u_ada)

    flushes = 0
    while rs.items:
        _comm_call(f"rs_flush{flushes}", rs.comm())
        flushes += 1
    groups = {"a_in": (a_w_in, m_a_w_in, v_a_w_in), "a_out": (a_w_out, m_a_w_out, v_a_w_out),
              "b_in": (b_w_in, m_b_w_in, v_b_w_in), "b_out": (b_w_out, m_b_w_out, v_b_w_out),
              "wg": (ffn_w_gate, m_ffn_w_gate, v_ffn_w_gate), "wu": (ffn_w_up, m_ffn_w_up, v_ffn_w_up),
              "wd": (ffn_w_down, m_ffn_w_down, v_ffn_w_down)}
    upd = {}
    for n, (w_, m_, v_) in groups.items():
        for l in range(w_.shape[0]):
            tag = n if n in rs.done else f"{n}{l}"
            upd[n] = _adamw("adamw_" + tag, w_, rs.done[tag], m_, v_, layer=l, prev=upd.get(n))
    u_a_in, u_a_out, u_b_in, u_b_out, u_wg, u_wu, u_wd = (upd[n] for n in groups)

    def leaves(i, small_list):
        s = small_list
        return [u_ada[i], s[0], s[1], s[2], u_a_in[i], s[3], s[4], s[5], s[6], s[7], u_a_out[i],
                u_b_in[i], s[8], u_b_out[i], u_wg[i], u_wu[i], u_wd[i], s[9]]

    return (loss, grad_x, *leaves(0, sg), *leaves(1, sd), *leaves(2, sm), *leaves(3, sv_))
```

```python
import functools
import math

import jax
import jax.numpy as jnp
from jax import lax
from jax.experimental import pallas as pl
from jax.experimental.pallas import tpu as pltpu

F32, BF16 = jnp.float32, jnp.bfloat16
LANE = 128
N_CHIPS = 4
N_DEV = 8
N_MOD = 6
EPS = 1e-6
VMEM_LIMIT = 60 * 1024 * 1024
ADAM_LR, ADAM_B1, ADAM_B2, ADAM_EPS, ADAM_WD, ADAM_STEP = 0.001, 0.9, 0.999, 1e-08, 0.01, 10
MESH = pl.DeviceIdType.MESH
ANY = pl.BlockSpec(memory_space=pl.ANY)
SDS = jax.ShapeDtypeStruct


def _pick(dim, pref, align):
    t = min(dim, pref)
    t -= t % align
    while t >= align:
        if dim % t == 0:
            return t
        t -= align
    return dim


def _params(sem=None):
    return pltpu.CompilerParams(dimension_semantics=sem, vmem_limit_bytes=VMEM_LIMIT)


class _Job:
    def __init__(self, ins, outs, aliases, nsem, start, wait):
        self.ins, self.outs, self.aliases, self.nsem, self.start, self.wait = ins, outs, aliases, nsem, start, wait


def _call(body, name, *, grid, in_specs, out_specs, out_shape, args, scratch=(), sem=None, comm=None,
          prefetch=None, keep=()):
    jobs, absorb = comm if comm else ([], None)
    n_pre = 0 if prefetch is None else 1
    n_in, n_out, n_scr = len(args), len(out_shape), len(scratch)
    c_ins = [a for a, _ in keep] + [a for j in jobs for a in j.ins]
    c_outs = [s for j in jobs for s in j.outs]
    aliases = {n_pre + n_in + k: o for k, (_, o) in enumerate(keep)}
    c_scr = []
    i_off, o_off = len(keep), 0
    for j in jobs:
        for a, b in j.aliases.items():
            aliases[n_pre + n_in + i_off + a] = n_out + o_off + b
        i_off += len(j.ins)
        o_off += len(j.outs)
        c_scr += [pltpu.SemaphoreType.DMA((j.nsem,)), pltpu.SemaphoreType.DMA((j.nsem,))]

    def wrapped(*refs):
        bounds = [n_pre + n_in, len(c_ins), n_out, len(c_outs), n_scr]
        parts, p = [], 0
        for n in bounds:
            parts.append(refs[p:p + n])
            p += n
        main_in, cin, main_out, cout, main_scr = parts
        cin = cin[len(keep):]
        csem = refs[p:]

        def run(phase):
            a = b = 0
            for k, j in enumerate(jobs):
                fn = j.start if phase == 0 else j.wait
                fn(cin[a:a + len(j.ins)], cout[b:b + len(j.outs)], csem[2 * k], csem[2 * k + 1])
                a += len(j.ins)
                b += len(j.outs)

        if jobs and grid:
            first = functools.reduce(jnp.logical_and, [pl.program_id(d) == 0 for d in range(len(grid))])
            last = functools.reduce(jnp.logical_and, [pl.program_id(d) == grid[d] - 1 for d in range(len(grid))])
            pl.when(first)(lambda: run(0))
            body(*main_in, *main_out, *main_scr)
            pl.when(last)(lambda: run(1))
        elif jobs:
            run(0)
            body(*main_in, *main_out, *main_scr)
            run(1)
        else:
            body(*main_in, *main_out, *main_scr)

    specs = dict(grid=grid, in_specs=list(in_specs) + [ANY] * len(c_ins),
                 out_specs=list(out_specs) + [ANY] * len(c_outs), scratch_shapes=list(scratch) + c_scr)
    if n_pre:
        specs = dict(grid_spec=pltpu.PrefetchScalarGridSpec(num_scalar_prefetch=1, **specs))
    outs = pl.pallas_call(
        wrapped, name=name, out_shape=list(out_shape) + c_outs, input_output_aliases=aliases,
        compiler_params=_params(("arbitrary",) * len(grid) if jobs else sem), **specs,
    )(*([prefetch] if n_pre else []), *args, *c_ins)
    if jobs:
        absorb(list(outs[n_out:]))
    return list(outs[:n_out])


def _comm_call(name, comm):
    _call(lambda: None, name, grid=(), in_specs=[], out_specs=[], out_shape=[], args=[], comm=comm)


def _mm(name, form, a_list, b_list, *, M, N, K, tm, tn, tk, b_stacked=False, out_stacked=False,
        b_koffs=None, acc_of=None, n_acc=1, extras=(), out_dtypes=(F32,), epilogue=None, comm=None,
        outer="j", b_layer=None):
    assert M % tm == 0 and N % tn == 0 and K % tk == 0, (name, M, N, K, tm, tn, tk)
    nm, nn, nk = M // tm, N // tn, K // tk
    npairs = len(a_list)
    acc_of = acc_of or [0] * npairs
    b_koffs = b_koffs or [0] * npairs
    if epilogue is None:
        epilogue = lambda accs, ex: [accs[0]]

    def spec(shape, fn):
        return pl.BlockSpec(shape, fn if outer == "j" else (lambda i, j, k: fn(j, i, k)))

    a_uniq = [a for p, a in enumerate(a_list) if all(a is not b for b in a_list[:p])]
    a_pos = [next(u for u, b in enumerate(a_uniq) if b is a) for a in a_list]
    in_specs = []
    for _ in a_uniq:
        if form == "tn":
            in_specs.append(spec((tk, tm), lambda j, i, k: (k, i)))
        else:
            in_specs.append(spec((tm, tk), lambda j, i, k: (i, k)))
    for off in b_koffs:
        if b_layer is not None:
            assert form == "nn" and not b_stacked
            in_specs.append(spec((None, tk, tn), lambda j, i, k: (b_layer, k, j)))
        elif form == "nn":
            if b_stacked:
                assert tn * N_CHIPS == N
                in_specs.append(spec((None, tk, tn), lambda j, i, k: (j, k, 0)))
            else:
                in_specs.append(spec((tk, tn), lambda j, i, k, off=off: (off + k, j)))
        elif form == "nt":
            if b_stacked:
                assert tk * N_CHIPS == K
                in_specs.append(spec((None, tn, tk), lambda j, i, k: (k, j, 0)))
            else:
                in_specs.append(spec((tn, tk), lambda j, i, k, off=off: (j, off + k)))
        else:
            in_specs.append(spec((tk, tn), lambda j, i, k: (k, j)))
    for kind, _ in extras:
        if kind == "tile":
            in_specs.append(spec((tm, tn), lambda j, i, k: (i, j)))
        else:
            in_specs.append(spec((1, tn), lambda j, i, k: (0, j)))
    if out_stacked:
        assert tn * N_CHIPS == N
        out_specs = [spec((None, tm, tn), lambda j, i, k: (j, i, 0)) for _ in out_dtypes]
        out_shape = [SDS((N_CHIPS, M, tn), d) for d in out_dtypes]
    else:
        out_specs = [spec((tm, tn), lambda j, i, k: (i, j)) for _ in out_dtypes]
        out_shape = [SDS((M, N), d) for d in out_dtypes]
    dims = {"nn": (((1,), (0,)), ((), ())), "nt": (((1,), (1,)), ((), ())), "tn": (((0,), (0,)), ((), ()))}[form]
    n_a, n_ex, n_out = len(a_uniq), len(extras), len(out_dtypes)

    def body(*refs):
        a_vals = [refs[u][...].astype(BF16) for u in range(n_a)]
        b_refs = refs[n_a:n_a + npairs]
        e_refs = refs[n_a + npairs:n_a + npairs + n_ex]
        o_refs = refs[n_a + npairs + n_ex:n_a + npairs + n_ex + n_out]
        acc_refs = refs[n_a + npairs + n_ex + n_out:]

        tot = [None] * n_acc
        for p in range(npairs):
            d = lax.dot_general(a_vals[a_pos[p]], b_refs[p][...].astype(BF16), dims, preferred_element_type=F32)
            tot[acc_of[p]] = d if tot[acc_of[p]] is None else tot[acc_of[p]] + d

        def finish(accs):
            outs = epilogue(accs, [e[...] for e in e_refs])
            for o_ref, o in zip(o_refs, outs):
                o_ref[...] = o.astype(o_ref.dtype)

        if nk == 1:
            finish(tot)
        else:
            k = pl.program_id(2)

            @pl.when(k == 0)
            def _():
                for r, t in zip(acc_refs, tot):
                    r[...] = t

            @pl.when(k > 0)
            def _():
                for r, t in zip(acc_refs, tot):
                    r[...] += t

            @pl.when(k == nk - 1)
            def _():
                finish([r[...] for r in acc_refs])

    scratch = [pltpu.VMEM((tm, tn), F32) for _ in range(n_acc)] if nk > 1 else []
    return _call(body, name, grid=(nn, nm, nk) if outer == "j" else (nm, nn, nk), in_specs=in_specs,
                 out_specs=out_specs, out_shape=out_shape, scratch=scratch,
                 sem=("parallel", "parallel", "arbitrary"), comm=comm,
                 args=[*a_uniq, *b_list, *[e for _, e in extras]])


def _rowwise(name, fn, rows, vecs, row_outs, acc_widths, tr, comm=None):
    S = rows[0].shape[0]
    assert S % tr == 0
    nr, nv, no, na = len(rows), len(vecs), len(row_outs), len(acc_widths)

    def body(*refs):
        r = [x[...] for x in refs[:nr]]
        v = [x[...] for x in refs[nr:nr + nv]]
        o_refs = refs[nr + nv:nr + nv + no]
        a_refs = refs[nr + nv + no:]
        outs, accs = fn(r, v)
        for o_ref, o in zip(o_refs, outs):
            o_ref[...] = o.astype(o_ref.dtype)
        first = pl.program_id(0) == 0

        @pl.when(first)
        def _():
            for a_ref, a in zip(a_refs, accs):
                a_ref[...] = a

        @pl.when(jnp.logical_not(first))
        def _():
            for a_ref, a in zip(a_refs, accs):
                a_ref[...] += a

    in_specs = [pl.BlockSpec((tr, x.shape[1]), lambda i: (i, 0)) for x in rows]
    in_specs += [pl.BlockSpec(x.shape, lambda i, nd=x.ndim: (0,) * nd) for x in vecs]
    out_specs = [pl.BlockSpec((tr, w), lambda i: (i, 0)) for w, _ in row_outs]
    out_specs += [pl.BlockSpec((1, w), lambda i: (0, 0)) for w in acc_widths]
    out_shape = [SDS((S, w), d) for w, d in row_outs] + [SDS((1, w), F32) for w in acc_widths]
    return _call(body, name, grid=(S // tr,), in_specs=in_specs, out_specs=out_specs, out_shape=out_shape,
                 sem=("arbitrary",), comm=comm, args=[*rows, *vecs])


def _colsum(a):
    return jnp.sum(a, axis=0, keepdims=True)


def _rms_stats(x):
    rstd = lax.rsqrt(jnp.mean(x * x, axis=1, keepdims=True) + EPS)
    return x * rstd, rstd


def _normmod_fwd(name, x, g, sc, sh, tr):
    def fn(r, v):
        n, _ = _rms_stats(r[0])
        return [(n * v[0]) * (1.0 + v[1]) + v[2]], []
    return _rowwise(name, fn, [x], [g, sc, sh], [(x.shape[1], BF16)], [], tr)[0]


def _normmod_bwd(name, x, dh, dres, g, sc, tr, comm=None):
    def fn(r, v):
        x_, dh_, dres_ = r
        g_, sc_ = v
        n, rstd = _rms_stats(x_)
        hn = n * g_
        dhn = dh_ * (1.0 + sc_)
        dn = dhn * g_
        dx = rstd * (dn - n * jnp.mean(dn * n, axis=1, keepdims=True))
        return [dres_ + dx], [_colsum(dh_), _colsum(dh_ * hn), _colsum(dhn * n)]
    D = x.shape[1]
    return _rowwise(name, fn, [x, dh, dres], [g, sc], [(D, F32)], [D, D, D], tr, comm=comm)


def _gate_bwd(name, dx, out, gate, tr):
    def fn(r, v):
        return [v[0] * r[0]], [_colsum(r[0] * r[1].astype(F32))]
    D = dx.shape[1]
    return _rowwise(name, fn, [dx, out], [gate], [(D, BF16)], [D], tr)


def _loss_head(name, x, target, g, tr):
    D = x.shape[1]

    def fn(r, v):
        n, rstd = _rms_stats(r[0])
        err = n * v[0] - r[1]
        loss = 0.5 * jnp.sum(jnp.mean(err * err, axis=1, keepdims=True), axis=0, keepdims=True)
        dy = err * (1.0 / D)
        dn = dy * v[0]
        dx = rstd * (dn - n * jnp.mean(dn * n, axis=1, keepdims=True))
        return [dx], [jnp.broadcast_to(loss, (1, LANE)), _colsum(dy * n)]
    return _rowwise(name, fn, [x, target], [g], [(D, F32)], [LANE, D], tr)


_INV_SQRT2 = 1.0 / math.sqrt(2.0)
_INV_SQRT2PI = 1.0 / math.sqrt(2.0 * math.pi)


def _gelu(a):
    return 0.5 * a * (1.0 + lax.erf(a * _INV_SQRT2))


def _gelu_grad(a):
    return 0.5 * (1.0 + lax.erf(a * _INV_SQRT2)) + a * jnp.exp(-0.5 * a * a) * _INV_SQRT2PI


def _ln_stats(v):
    mu = jnp.mean(v, axis=1, keepdims=True)
    vc = v - mu
    rstd = lax.rsqrt(jnp.mean(vc * vc, axis=1, keepdims=True) + EPS)
    return vc * rstd, rstd


def _chunk_mix(w_ref, vn, tr, G):
    rows = []
    for ch in range(tr // LANE):
        cols = []
        for g in range(G):
            blk = vn[ch * LANE:(ch + 1) * LANE, g * LANE:(g + 1) * LANE]
            cols.append(jnp.dot(w_ref[g], blk, preferred_element_type=F32))
        rows.append(jnp.concatenate(cols, axis=1))
    return jnp.concatenate(rows, axis=0)


def _sgu_mid_fwd(a, ln_g, ln_b, w_mask, bias_full, tr):
    S, D2 = a.shape
    D = D2 // 2
    G = D // LANE

    def body(a_ref, g_ref, b_ref, w_ref, bias_ref, y_ref):
        u = _gelu(a_ref[:, :D])
        v = _gelu(a_ref[:, D:])
        vhat, _ = _ln_stats(v)
        vn = (vhat * g_ref[...] + b_ref[...]).astype(BF16)
        sv = _chunk_mix(w_ref, vn, tr, G) + jnp.concatenate([bias_ref[...]] * (tr // LANE), axis=0)
        y_ref[...] = (u * sv).astype(BF16)

    return pl.pallas_call(
        body, name="sgu_mid_fwd", grid=(S // tr,),
        in_specs=[pl.BlockSpec((tr, D2), lambda i: (i, 0)), pl.BlockSpec((1, D), lambda i: (0, 0)),
                  pl.BlockSpec((1, D), lambda i: (0, 0)), pl.BlockSpec((G, LANE, LANE), lambda i: (0, 0, 0)),
                  pl.BlockSpec((LANE, D), lambda i: (0, 0))],
        out_specs=pl.BlockSpec((tr, D), lambda i: (i, 0)), out_shape=SDS((S, D), BF16),
        compiler_params=_params(("arbitrary",)))(a, ln_g, ln_b, w_mask, bias_full)


def _sgu_mid_bwd(a, dy, ln_g, ln_b, w_mask, bias_full, tr):
    S, D2 = a.shape
    D = D2 // 2
    G = D // LANE
    nch = tr // LANE

    def body(a_ref, dy_ref, g_ref, b_ref, w_ref, bias_ref, da_ref, dw_ref, dbias_ref, dg_ref, db_ref, dbin_ref):
        au, av = a_ref[:, :D], a_ref[:, D:]
        u = _gelu(au)
        v = _gelu(av)
        vhat, rstd = _ln_stats(v)
        vn = (vhat * g_ref[...] + b_ref[...]).astype(BF16)
        sv = _chunk_mix(w_ref, vn, tr, G) + jnp.concatenate([bias_ref[...]] * nch, axis=0)
        dy_ = dy_ref[...].astype(F32)
        du = dy_ * sv
        dsv = dy_ * u
        dsv_b = dsv.astype(BF16)
        first = pl.program_id(0) == 0

        @pl.when(first)
        def _():
            dw_ref[...] = jnp.zeros_like(dw_ref)
            dbias_ref[...] = jnp.zeros_like(dbias_ref)
            dg_ref[...] = jnp.zeros_like(dg_ref)
            db_ref[...] = jnp.zeros_like(db_ref)
            dbin_ref[...] = jnp.zeros_like(dbin_ref)

        rows = []
        dbias = None
        for ch in range(nch):
            r0 = ch * LANE
            cols = []
            for g in range(G):
                c0 = g * LANE
                ds_blk = dsv_b[r0:r0 + LANE, c0:c0 + LANE]
                vn_blk = vn[r0:r0 + LANE, c0:c0 + LANE]
                cols.append(lax.dot_general(w_ref[g], ds_blk, (((0,), (0,)), ((), ())),
                                            preferred_element_type=F32))
                dw_ref[g] += lax.dot_general(ds_blk, vn_blk, (((1,), (1,)), ((), ())),
                                             preferred_element_type=F32)
            rows.append(jnp.concatenate(cols, axis=1))
            blk = dsv[r0:r0 + LANE, :]
            dbias = blk if dbias is None else dbias + blk
        dvn = jnp.concatenate(rows, axis=0)
        dbias_ref[...] += dbias
        dg_ref[...] += _colsum(dvn * vhat)
        db_ref[...] += _colsum(dvn)
        dvh = dvn * g_ref[...]
        dv = rstd * (dvh - jnp.mean(dvh, axis=1, keepdims=True)
                     - vhat * jnp.mean(dvh * vhat, axis=1, keepdims=True))
        da_u = du * _gelu_grad(au)
        da_v = dv * _gelu_grad(av)
        da_ref[:, :D] = da_u.astype(BF16)
        da_ref[:, D:] = da_v.astype(BF16)
        dbin_ref[:, :D] += _colsum(da_u)
        dbin_ref[:, D:] += _colsum(da_v)

    full = lambda shp: pl.BlockSpec(shp, lambda i, nd=len(shp): (0,) * nd)
    return pl.pallas_call(
        body, name="sgu_mid_bwd", grid=(S // tr,),
        in_specs=[pl.BlockSpec((tr, D2), lambda i: (i, 0)), pl.BlockSpec((tr, D), lambda i: (i, 0)),
                  full((1, D)), full((1, D)), full((G, LANE, LANE)), full((LANE, D))],
        out_specs=[pl.BlockSpec((tr, D2), lambda i: (i, 0)), full((G, LANE, LANE)), full((LANE, D)),
                   full((1, D)), full((1, D)), full((1, D2))],
        out_shape=[SDS((S, D2), BF16), SDS((G, LANE, LANE), F32), SDS((LANE, D), F32),
                   SDS((1, D), F32), SDS((1, D), F32), SDS((1, D2), F32)],
        compiler_params=_params(("arbitrary",)))(a, dy, ln_g, ln_b, w_mask, bias_full)


def _tri(lower):
    r = lax.broadcasted_iota(jnp.int32, (LANE, LANE), 0)
    c = lax.broadcasted_iota(jnp.int32, (LANE, LANE), 1)
    return jnp.where((c <= r) if lower else (c >= r), 1.0, 0.0).astype(F32)


def _fox_gate_fwd(fl, bf_pad):
    S = fl.shape[0]
    nblk = S // LANE

    def body(fl_ref, b_ref, f_ref):
        tri = _tri(True)

        def step(i, carry):
            r0 = pl.multiple_of(i * LANE, LANE)
            z = fl_ref[pl.ds(r0, LANE), :] + b_ref[...]
            logf = jnp.minimum(z, 0.0) - jnp.log1p(jnp.exp(-jnp.abs(z)))
            f = jnp.dot(tri, logf, preferred_element_type=F32, precision=lax.Precision.HIGHEST) + carry
            f_ref[pl.ds(r0, LANE), :] = f
            return f[LANE - 1:LANE, :]
        lax.fori_loop(0, nblk, step, jnp.zeros((1, LANE), F32))

    return pl.pallas_call(body, name="fox_gate_fwd", out_shape=SDS((S, LANE), F32),
                          compiler_params=_params())(fl, bf_pad)


def _fox_gate_bwd(fl, dF, bf_pad):
    S = fl.shape[0]
    nblk = S // LANE

    def body(fl_ref, df_ref, b_ref, dfl_ref, db_ref):
        tri = _tri(True)

        def prefix(i, carry):
            r0 = pl.multiple_of(i * LANE, LANE)
            p = jnp.dot(tri, df_ref[pl.ds(r0, LANE), :], preferred_element_type=F32,
                        precision=lax.Precision.HIGHEST) + carry
            dfl_ref[pl.ds(r0, LANE), :] = p
            return p[LANE - 1:LANE, :]
        total = lax.fori_loop(0, nblk, prefix, jnp.zeros((1, LANE), F32))
        db_ref[...] = jnp.zeros_like(db_ref)

        def finish(i, carry):
            r0 = pl.multiple_of(i * LANE, LANE)
            dlogf = total - dfl_ref[pl.ds(r0, LANE), :] + df_ref[pl.ds(r0, LANE), :]
            z = fl_ref[pl.ds(r0, LANE), :] + b_ref[...]
            dfl = dlogf / (1.0 + jnp.exp(z))
            dfl_ref[pl.ds(r0, LANE), :] = dfl
            db_ref[...] += _colsum(dfl)
            return carry
        lax.fori_loop(0, nblk, finish, 0)

    return pl.pallas_call(body, name="fox_gate_bwd", out_shape=[SDS((S, LANE), F32), SDS((1, LANE), F32)],
                          compiler_params=_params())(fl, dF, bf_pad)


def _lane_pick(blk, h):
    lane = lax.broadcasted_iota(jnp.int32, blk.shape, 1)
    return jnp.sum(jnp.where(lane == h, blk, 0.0), axis=1, keepdims=True)


_NEG = -1e30
_LOG2E = 1.0 / math.log(2.0)
_LN2 = math.log(2.0)


def _attn_fwd(qkv, F_sh, F_rows, H, tq, comm=None):
    S = qkv.shape[0]
    D = H * LANE
    nq = S // tq
    scale = 1.0 / math.sqrt(LANE)

    def body(q_ref, k_ref, v_ref, fsh_ref, frow_ref, o_ref, lse_ref):
        h, i = pl.program_id(0), pl.program_id(1)
        q = q_ref[...]
        fq = _lane_pick(fsh_ref[...], h) * _LOG2E

        def block(j, carry, diagonal):
            m, l, acc = carry
            r0 = pl.multiple_of(j * tq, tq)
            k = k_ref[pl.ds(r0, tq), :]
            v = v_ref[pl.ds(r0, tq), :]
            s = lax.dot_general(q, k, (((1,), (1,)), ((), ())), preferred_element_type=F32) * (scale * _LOG2E)
            s = s + (fq - frow_ref[j])
            if diagonal:
                row = lax.broadcasted_iota(jnp.int32, (tq, tq), 0)
                col = lax.broadcasted_iota(jnp.int32, (tq, tq), 1)
                s = jnp.where(col <= row, s, _NEG)
            m_new = jnp.maximum(m, jnp.max(s, axis=1, keepdims=True))
            alpha = jnp.exp2(m - m_new)
            p = jnp.exp2(s - m_new)
            l = alpha * l + jnp.sum(p, axis=1, keepdims=True)
            acc = alpha * acc + jnp.dot(p.astype(BF16), v, preferred_element_type=F32)
            return m_new, l, acc

        init = (jnp.full((tq, 1), _NEG, F32), jnp.zeros((tq, 1), F32), jnp.zeros((tq, LANE), F32))
        carry = lax.fori_loop(0, i, lambda j, cr: block(j, cr, False), init)
        m, l, acc = block(i, carry, True)
        o_ref[...] = (acc / l).astype(BF16)
        lse_ref[...] = (m + jnp.log2(l)) * _LN2

    return _call(
        body, "attn_fwd", grid=(H, nq),
        in_specs=[pl.BlockSpec((tq, LANE), lambda h, i: (i, h)),
                  pl.BlockSpec((S, LANE), lambda h, i: (0, H + h)),
                  pl.BlockSpec((S, LANE), lambda h, i: (0, 2 * H + h)),
                  pl.BlockSpec((tq, LANE), lambda h, i: (i, 0)),
                  pl.BlockSpec((None, nq, 1, tq), lambda h, i: (h, 0, 0, 0))],
        out_specs=[pl.BlockSpec((tq, LANE), lambda h, i: (i, h)),
                   pl.BlockSpec((None, tq, 1), lambda h, i: (h, i, 0))],
        out_shape=[SDS((S, D), BF16), SDS((H, S, 1), F32)],
        sem=("parallel", "arbitrary"), comm=comm, args=[qkv, qkv, qkv, F_sh, F_rows])


def _attn_bwd(qkv, do, F_sh, A_rows, delta_rows, H, tq, comm=None):
    S = qkv.shape[0]
    D = H * LANE
    nq = S // tq
    scale = 1.0 / math.sqrt(LANE)

    def body(q_ref, do_ref, k_ref, v_ref, fsh_ref, a_ref, dl_ref, dq_ref, dk_ref, dv_ref, df_ref, dfr_ref,
             dq_acc, dfr_acc):
        h, j = pl.program_id(0), pl.program_id(1)

        @pl.when(j == 0)
        def _():
            dq_acc[...] = jnp.zeros_like(dq_acc)
            dfr_acc[...] = jnp.zeros_like(dfr_acc)

        k = k_ref[...]
        v = v_ref[...]
        fk = _lane_pick(fsh_ref[...], h) * _LOG2E

        def block(i, carry, diagonal):
            dk, dv, df = carry
            r0 = pl.multiple_of(i * tq, tq)
            q = q_ref[pl.ds(r0, tq), :]
            do_ = do_ref[pl.ds(r0, tq), :]
            st = lax.dot_general(k, q, (((1,), (1,)), ((), ())), preferred_element_type=F32) * (scale * _LOG2E)
            arg = st + (a_ref[i] - fk)
            if diagonal:
                row = lax.broadcasted_iota(jnp.int32, (tq, tq), 0)
                col = lax.broadcasted_iota(jnp.int32, (tq, tq), 1)
                arg = jnp.where(row <= col, arg, _NEG)
            pt = jnp.exp2(arg)
            dv = dv + jnp.dot(pt.astype(BF16), do_, preferred_element_type=F32)
            dpt = lax.dot_general(v, do_, (((1,), (1,)), ((), ())), preferred_element_type=F32)
            dst = pt * (dpt - dl_ref[i])
            df = df + jnp.sum(dst, axis=1, keepdims=True)
            dfr_acc[i] += jnp.sum(dst, axis=0, keepdims=True)
            dsb = dst.astype(BF16)
            dk = dk + jnp.dot(dsb, q, preferred_element_type=F32)
            dq_acc[pl.ds(r0, tq), :] += lax.dot_general(dsb, k, (((0,), (0,)), ((), ())),
                                                        preferred_element_type=F32)
            return dk, dv, df

        z = jnp.zeros((tq, LANE), F32)
        carry = block(j, (z, z, jnp.zeros((tq, 1), F32)), True)
        dk, dv, df = lax.fori_loop(j + 1, nq, lambda i, cr: block(i, cr, False), carry)
        dk_ref[...] = (dk * scale).astype(BF16)
        dv_ref[...] = dv.astype(BF16)
        df_ref[...] = -df

        @pl.when(j == nq - 1)
        def _():
            dq_ref[...] = (dq_acc[...] * scale).astype(BF16)
            dfr_ref[...] = dfr_acc[...]

    return _call(
        body, "attn_bwd", grid=(H, nq),
        in_specs=[pl.BlockSpec((S, LANE), lambda h, j: (0, h)),
                  pl.BlockSpec((S, LANE), lambda h, j: (0, h)),
                  pl.BlockSpec((tq, LANE), lambda h, j: (j, H + h)),
                  pl.BlockSpec((tq, LANE), lambda h, j: (j, 2 * H + h)),
                  pl.BlockSpec((tq, LANE), lambda h, j: (j, 0)),
                  pl.BlockSpec((None, nq, 1, tq), lambda h, j: (h, 0, 0, 0)),
                  pl.BlockSpec((None, nq, 1, tq), lambda h, j: (h, 0, 0, 0))],
        out_specs=[pl.BlockSpec((S, LANE), lambda h, j: (0, h)),
                   pl.BlockSpec((tq, LANE), lambda h, j: (j, h)),
                   pl.BlockSpec((tq, LANE), lambda h, j: (j, h)),
                   pl.BlockSpec((None, tq, 1), lambda h, j: (h, j, 0)),
                   pl.BlockSpec((None, nq, 1, tq), lambda h, j: (h, 0, 0, 0))],
        out_shape=[SDS((S, D), BF16), SDS((S, D), BF16), SDS((S, D), BF16), SDS((H, S, 1), F32),
                   SDS((H, nq, 1, tq), F32)],
        scratch=[pltpu.VMEM((S, LANE), F32), pltpu.VMEM((nq, 1, tq), F32)],
        sem=("parallel", "arbitrary"), comm=comm, args=[qkv, do, qkv, qkv, F_sh, A_rows, delta_rows])


def _head_dots(do, o, H, tr):
    def fn(r, v):
        prod = r[0].astype(F32) * r[1].astype(F32)
        lane = lax.broadcasted_iota(jnp.int32, (prod.shape[0], LANE), 1)
        out = jnp.zeros((prod.shape[0], LANE), F32)
        for h in range(H):
            s = jnp.sum(prod[:, h * LANE:(h + 1) * LANE], axis=1, keepdims=True)
            out = jnp.where(lane == h, s, out)
        return [out], []
    return _rowwise("attn_delta", fn, [do, o], [], [(LANE, F32)], [], tr)[0]


def _adamw(name, w, g, m, v, layer=None, prev=None):
    R, C = g.shape
    tr = _pick(R, max(8, (512 * 1024) // max(C, 1) // 8 * 8), 8)
    c1 = 1.0 - ADAM_B1 ** ADAM_STEP
    c2 = 1.0 - ADAM_B2 ** ADAM_STEP

    def body(w_ref, g_ref, m_ref, v_ref, go_ref, d_ref, mo_ref, vo_ref):
        g_ = g_ref[...]
        m_ = ADAM_B1 * m_ref[...] + (1.0 - ADAM_B1) * g_
        v_ = ADAM_B2 * v_ref[...] + (1.0 - ADAM_B2) * (g_ * g_)
        go_ref[...] = g_
        d_ref[...] = -ADAM_LR * ((m_ / c1) / (jnp.sqrt(v_ / c2) + ADAM_EPS) + ADAM_WD * w_ref[...])
        mo_ref[...] = m_
        vo_ref[...] = v_

    gspec = pl.BlockSpec((tr, C), lambda i: (i, 0))
    pspec = gspec if layer is None else pl.BlockSpec((None, tr, C), lambda i: (layer, i, 0))
    return _call(body, name, grid=(R // tr,), in_specs=[pspec, gspec, pspec, pspec], out_specs=[pspec] * 4,
                 out_shape=[SDS(w.shape, F32)] * 4, sem=("parallel",), args=[w, g, m, v],
                 keep=[(p, k) for k, p in enumerate(prev)] if prev else ())


def _sum_slots(name, a):
    n, R, C = a.shape
    tr = _pick(R, 256, 8)

    def body(a_ref, o_ref):
        acc = a_ref[0]
        for k in range(1, n):
            acc = acc + a_ref[k]
        o_ref[...] = acc

    return pl.pallas_call(body, name=name, grid=(R // tr,),
                          in_specs=[pl.BlockSpec((n, tr, C), lambda i: (0, i, 0))],
                          out_specs=pl.BlockSpec((tr, C), lambda i: (i, 0)), out_shape=SDS((R, C), F32),
                          compiler_params=_params(("parallel",)))(a)


def _silu_rows(name, a):
    def body(a_ref, o_ref):
        z = a_ref[...]
        o_ref[...] = (z / (1.0 + jnp.exp(-z))).astype(BF16)
    return pl.pallas_call(body, name=name, out_shape=SDS(a.shape, BF16), compiler_params=_params())(a)


def _place():
    return lax.axis_index("x"), lax.axis_index("y"), lax.axis_index("c")


def _rcopy(src, dst, ssem, rsem, dev):
    return pltpu.make_async_remote_copy(src_ref=src, dst_ref=dst, send_sem=ssem, recv_sem=rsem,
                                        device_id=dev, device_id_type=MESH)


def _gather_devices(name, buf):
    R, C = buf.shape

    def body(b_ref, out_ref, ssem, rsem, lsem):
        x, y, c = _place()
        me = 4 * x + 2 * y + c
        mine = pltpu.make_async_copy(b_ref, out_ref.at[me], lsem)
        mine.start()
        peers = []
        for k in range(1, N_DEV):
            fx, fy, fc = (k >> 2) & 1, (k >> 1) & 1, k & 1
            peers.append((x ^ fx, y ^ fy, c ^ fc))
        sends = [_rcopy(b_ref, out_ref.at[me], ssem.at[k], rsem.at[k], p) for k, p in enumerate(peers)]
        for cp in sends:
            cp.start()
        for k, (px, py, pc) in enumerate(peers):
            _rcopy(b_ref, out_ref.at[4 * px + 2 * py + pc], ssem.at[k], rsem.at[k], (px, py, pc)).wait_recv()
        for cp in sends:
            cp.wait_send()
        mine.wait()

    return pl.pallas_call(body, name=name, in_specs=[ANY], out_specs=ANY, out_shape=SDS((N_DEV, R, C), buf.dtype),
                          scratch_shapes=[pltpu.SemaphoreType.DMA((N_DEV - 1,)), pltpu.SemaphoreType.DMA((N_DEV - 1,)),
                                          pltpu.SemaphoreType.DMA])(buf)


def _other_chips(x, y):
    return [(1 - x, y), (x, 1 - y), (1 - x, 1 - y)]


def _cast_slabs(name, ws, idx, comm=None):
    n = len(ws)
    steps = next(s for s in (8, 4, 2, 1) if all(w.shape[1] % (16 * s) == 0 for w, _ in ws))

    def body(s_ref, *refs):
        for w_ref, o_ref in zip(refs[:n], refs[n:]):
            o_ref[...] = w_ref[...].astype(BF16)

    return _call(body, name, grid=(steps,), prefetch=idx, comm=comm, sem=("parallel",), args=[w for w, _ in ws],
                 in_specs=[pl.BlockSpec((None, w.shape[1] // steps, w.shape[2]), lambda i, s, l=l: (l, i, 0))
                           for w, l in ws],
                 out_specs=[pl.BlockSpec((None, w.shape[1] // steps, w.shape[2]), lambda i, s: (s[1], i, 0))
                            for w, _ in ws],
                 out_shape=[SDS((N_CHIPS,) + w.shape[1:], BF16) for w, _ in ws])


def _job_gather_ici(buf):
    _, R, C = buf.shape
    hR = R // 2
    assert hR * 2 == R

    def views(o):
        x, y, c = _place()
        return c, _other_chips(x, y), o.at[2 * x + y, pl.ds(c * hR, hR), :]

    def start(i, o, ss, rs):
        c, chips, mine = views(o[0])
        for j, (cx, cy) in enumerate(chips):
            _rcopy(mine, mine, ss.at[j], rs.at[j], (cx, cy, c)).start()

    def wait(i, o, ss, rs):
        c, chips, mine = views(o[0])
        for j, (cx, cy) in enumerate(chips):
            cp = _rcopy(mine, o[0].at[2 * cx + cy, pl.ds(c * hR, hR), :], ss.at[j], rs.at[j], (cx, cy, c))
            cp.wait_send()
            cp.wait_recv()

    return _Job([buf], [SDS(buf.shape, buf.dtype)], {0: 0}, 3, start, wait)


def _job_gather_pair(buf):
    _, R, C = buf.shape
    hR = R // 2

    def views(o):
        x, y, c = _place()
        return c, (x, y, 1 - c), _other_chips(x, y)

    def start(i, o, ss, rs):
        c, sib, chips = views(o[0])
        for j, (cx, cy) in enumerate(chips):
            got = o[0].at[2 * cx + cy, pl.ds(c * hR, hR), :]
            _rcopy(got, got, ss.at[j], rs.at[j], sib).start()

    def wait(i, o, ss, rs):
        c, sib, chips = views(o[0])
        for j, (cx, cy) in enumerate(chips):
            cp = _rcopy(o[0].at[2 * cx + cy, pl.ds(c * hR, hR), :],
                        o[0].at[2 * cx + cy, pl.ds((1 - c) * hR, hR), :], ss.at[j], rs.at[j], sib)
            cp.wait_send()
            cp.wait_recv()

    return _Job([buf], [SDS(buf.shape, buf.dtype)], {0: 0}, 3, start, wait)


class _Gather:
    def __init__(self, bufs):
        self.todo, self.half, self.done = dict(bufs), {}, {}

    def comm(self, admit=()):
        second, first = list(self.half), list(admit)
        jobs = [_job_gather_pair(self.half[n]) for n in second] + [_job_gather_ici(self.todo[n]) for n in first]

        def absorb(outs):
            for n, o in zip(second + first, outs):
                if n in self.half:
                    del self.half[n]
                    self.done[n] = o
                else:
                    del self.todo[n]
                    self.half[n] = o
        return (jobs, absorb) if jobs else None


def _job_rs_pair(g4):
    _, R, C = g4.shape
    hR = R // 2
    assert hR * 2 == R

    def desc(i, o, ss, rs):
        x, y, c = _place()
        return _rcopy(i[0].at[:, pl.ds((1 - c) * hR, hR), :], o[0], ss.at[0], rs.at[0], (x, y, 1 - c))

    return _Job([g4], [SDS((N_CHIPS, hR, C), F32)], {}, 1,
                lambda i, o, ss, rs: desc(i, o, ss, rs).start(), lambda i, o, ss, rs: desc(i, o, ss, rs).wait())


def _job_rs_chips(p4):
    _, hR, C = p4.shape

    def descs(i, o, ss, rs):
        x, y, c = _place()
        return [_rcopy(i[0].at[2 * cx + cy], o[0].at[j], ss.at[j], rs.at[j], (cx, cy, c))
                for j, (cx, cy) in enumerate(_other_chips(x, y))]

    def start(i, o, ss, rs):
        for cp in descs(i, o, ss, rs):
            cp.start()

    def wait(i, o, ss, rs):
        for cp in descs(i, o, ss, rs):
            cp.wait_send()
            cp.wait_recv()

    return _Job([p4], [SDS((3, hR, C), p4.dtype)], {}, 3, start, wait)


def _job_rs_join(buf):
    R, C = buf.shape
    hR = R // 2

    def desc(o, ss, rs, recv):
        x, y, c = _place()
        mine = o[0].at[pl.ds(c * hR, hR), :]
        return _rcopy(mine, o[0].at[pl.ds((1 - c) * hR, hR), :] if recv else mine, ss.at[0], rs.at[0], (x, y, 1 - c))

    def wait(i, o, ss, rs):
        cp = desc(o, ss, rs, True)
        cp.wait_send()
        cp.wait_recv()

    return _Job([buf], [SDS(buf.shape, buf.dtype)], {0: 0}, 1,
                lambda i, o, ss, rs: desc(o, ss, rs, False).start(), wait)


def _rs_add_pair(name, g4, recv, idx):
    _, R, C = g4.shape
    hR = R // 2
    tr = _pick(hR, max(16, (1 << 20) // C // 16 * 16), 16)
    nb = hR // tr

    def body(s_ref, a_ref, b_ref, pb_ref, po_ref):
        s = a_ref[...] + b_ref[...]
        pb_ref[...] = s.astype(BF16)

        @pl.when(pl.program_id(1) == s_ref[1])
        def _():
            po_ref[...] = s

    gs = pltpu.PrefetchScalarGridSpec(
        num_scalar_prefetch=1, grid=(nb, N_CHIPS),
        in_specs=[pl.BlockSpec((None, tr, C), lambda i, k, s: (k, s[0] * nb + i, 0)),
                  pl.BlockSpec((None, tr, C), lambda i, k, s: (k, i, 0))],
        out_specs=[pl.BlockSpec((None, tr, C), lambda i, k, s: (k, i, 0)),
                   pl.BlockSpec((tr, C), lambda i, k, s: (i, 0))])
    return pl.pallas_call(body, name=name, grid_spec=gs, out_shape=[SDS((N_CHIPS, hR, C), BF16), SDS((hR, C), F32)],
                          compiler_params=_params(("parallel", "arbitrary")))(idx, g4, recv)


def _rs_add_chips(name, own, recv3, idx):
    hR, C = own.shape
    tr = _pick(hR, max(16, (1 << 20) // C // 16 * 16), 16)
    nb = hR // tr

    def body(s_ref, a_ref, b_ref, o_ref):
        o_ref[...] = ((a_ref[...] + b_ref[0].astype(F32)) + b_ref[1].astype(F32)) + b_ref[2].astype(F32)

    gs = pltpu.PrefetchScalarGridSpec(
        num_scalar_prefetch=1, grid=(nb,),
        in_specs=[pl.BlockSpec((tr, C), lambda i, s: (i, 0)), pl.BlockSpec((3, tr, C), lambda i, s: (0, i, 0))],
        out_specs=pl.BlockSpec((tr, C), lambda i, s: (s[0] * nb + i, 0)))
    return pl.pallas_call(body, name=name, grid_spec=gs, out_shape=SDS((2 * hR, C), F32),
                          compiler_params=_params(("parallel",)))(idx, own, recv3)


class _ReduceScatter:
    def __init__(self, idx):
        self.idx, self.items, self.done = idx, [], {}

    def push(self, tag, g4):
        self.items.append([tag, 0, g4])

    def comm(self):
        cur = list(self.items)
        jobs = []
        for tag, stage, data in cur:
            jobs.append(_job_rs_pair(data) if stage == 0 else _job_rs_chips(data[0]) if stage == 1
                        else _job_rs_join(data))

        def absorb(outs):
            for item, o in zip(cur, outs):
                tag, stage, data = item
                if stage == 0:
                    item[1:] = [1, _rs_add_pair(tag + "_rs_add2", data, o, self.idx)]
                elif stage == 1:
                    item[1:] = [2, _rs_add_chips(tag + "_rs_add4", data[1], o, self.idx)]
                else:
                    self.items.remove(item)
                    self.done[tag] = o
        return (jobs, absorb) if jobs else None


def _resid_epilogue(accs, ex):
    return [ex[0] + ex[1] * accs[0], accs[0]]


def _swiglu_epilogue(accs, ex):
    gt, up = accs
    return [gt, up, gt / (1.0 + jnp.exp(-gt)) * up]


def _swiglu_bwd_epilogue(accs, ex):
    dact = accs[0]
    gt, up = ex[0].astype(F32), ex[1].astype(F32)
    sg = 1.0 / (1.0 + jnp.exp(-gt))
    return [dact * up * (sg * (1.0 + gt * (1.0 - sg))), dact * (gt * sg)]


def kernel(x, c, ada_w, ada_b, norm_mix_g, norm_ffn_g, a_w_in, a_b_in, a_ln_g, a_ln_b, a_w_s, a_b_s, a_w_out, b_w_in, b_b_f, b_w_out, ffn_w_gate, ffn_w_up, ffn_w_down, final_g, loss_target, m_ada_w, m_ada_b, m_norm_mix_g, m_norm_ffn_g, m_a_w_in, m_a_b_in, m_a_ln_g, m_a_ln_b, m_a_w_s, m_a_b_s, m_a_w_out, m_b_w_in, m_b_b_f, m_b_w_out, m_ffn_w_gate, m_ffn_w_up, m_ffn_w_down, m_final_g, v_ada_w, v_ada_b, v_norm_mix_g, v_norm_ffn_g, v_a_w_in, v_a_b_in, v_a_ln_g, v_a_ln_b, v_a_w_s, v_a_b_s, v_a_w_out, v_b_w_in, v_b_b_f, v_b_w_out, v_ffn_w_gate, v_ffn_w_up, v_ffn_w_down, v_final_g):
    S, D = x.shape[1], x.shape[2]
    H = D // LANE
    G = D // LANE
    FH = ffn_w_down.shape[1] * N_CHIPS
    depth = ada_w.shape[0]
    assert depth == 2 and a_w_in.shape[0] == 1 and b_w_in.shape[0] == 1
    mx, my, mc = _place()
    chip = 2 * mx + my
    dev = 4 * mx + 2 * my + mc
    x0 = x[0]
    target = loss_target[0]
    tr = _pick(S, 256, 8)
    tm = _pick(S, 512, 8)
    tq = _pick(S, 512, LANE)
    nq = S // tq

    idx = jnp.stack([mc, chip]).astype(jnp.int32)
    shards = {"a_in": (a_w_in, 0), "a_out": (a_w_out, 0), "b_in": (b_w_in, 0), "b_out": (b_w_out, 0)}
    for l in range(depth):
        shards.update({f"wg{l}": (ffn_w_gate, l), f"wu{l}": (ffn_w_up, l), f"wd{l}": (ffn_w_down, l)})
    first = ["a_in", "a_out"]
    rest = [n for n in shards if n not in first]
    ag = _Gather(dict(zip(first, _cast_slabs("cast_first", [shards[n] for n in first], idx))))
    ag.todo.update(zip(rest, _cast_slabs("cast_rest", [shards[n] for n in rest], idx, comm=ag.comm(first))))
    _comm_call("ag_first_pair", ag.comm(["wg0"]))
    wa_in4 = ag.done["a_in"]
    wa_out = ag.done["a_out"].reshape(D, D)

    c_all = _gather_devices("ag_c", jnp.pad(c, ((0, 7), (0, 0)))).reshape(N_DEV, 8, D)[:, 0, :]
    c_act = _silu_rows("c_silu", c_all)
    n_loc = ada_w.shape[2]
    mods = []
    for l in range(depth):
        b_loc = lax.dynamic_slice_in_dim(ada_b[l:l + 1], chip * n_loc, n_loc, axis=1)
        mods.append(_mm(f"ada_fwd{l}", "nn", [c_act], [ada_w], M=N_DEV, N=n_loc, K=D, tm=N_DEV, b_layer=l,
                        tn=_pick(n_loc, 1024, LANE), tk=D, extras=[("row", b_loc)],
                        epilogue=lambda accs, ex: [accs[0] + ex[0]])[0])
    mod_all = _gather_devices("ag_mod", jnp.concatenate(mods, axis=1))
    mod_mine = lax.dynamic_index_in_dim(mod_all[0::2], dev, axis=1, keepdims=False)
    mod_mine = mod_mine.reshape(N_CHIPS, depth, n_loc).transpose(1, 0, 2).reshape(depth, 1, N_MOD * D)
    mod = [[mod_mine[l, :, i * D:(i + 1) * D] for i in range(N_MOD)] for l in range(depth)]

    row = lambda a: a.reshape(1, -1)

    tril = jnp.tril(jnp.ones((LANE, LANE), dtype=bool))
    w_mask = jnp.where(tril[None], a_w_s[0], 0.0).astype(BF16)
    bias_full = jnp.repeat(a_b_s[0].T, LANE, axis=1)
    bf_pad = jnp.pad(b_b_f, ((0, 0), (0, LANE - H)))

    admit = {"sgu_in": ["wu0"], "mix_out0": ["wd0"], "ffn_in0": ["b_in", "b_out"], "ffn_out0": ["wg1"],
             "fox_qkv": ["wu1"], "attn_fwd": ["wd1"], "mix_out1": []}
    saved = []
    xs = x0
    for l in range(depth):
        sh1, sc1, g1, sh2, sc2, g2 = mod[l]
        st = {"x_in": xs}
        h1 = _normmod_fwd(f"normmod_mix{l}", xs, row(norm_mix_g[l]), sc1, sh1, tr)
        st["h1"] = h1
        if l == 0:
            a = _mm("sgu_in", "nn", [h1], [wa_in4], M=S, N=2 * D, K=D, tm=tm, tn=2 * D // N_CHIPS, tk=D,
                    b_stacked=True, extras=[("row", a_b_in)], comm=ag.comm(admit["sgu_in"]),
                    epilogue=lambda accs, ex: [accs[0] + ex[0]])[0]
            y = _sgu_mid_fwd(a, a_ln_g, a_ln_b, w_mask, bias_full, tr)
            st["a"], st["y"] = a, y
            w_o, mix_out = wa_out, y
        else:
            wb_in4 = ag.done["b_in"]
            n_last = wb_in4.shape[2] - H
            w_qkv = jnp.concatenate([wb_in4[k] for k in range(N_CHIPS - 1)] + [wb_in4[-1][:, :n_last]], axis=1)
            w_f = jnp.pad(wb_in4[-1][:, n_last:], ((0, 0), (0, LANE - H)))
            qkv = _mm("fox_qkv", "nn", [h1], [w_qkv], M=S, N=3 * D, K=D, tm=tm, tn=_pick(3 * D, 1024, LANE),
                      tk=D, out_dtypes=(BF16,), comm=ag.comm(admit["fox_qkv"]))[0]
            fl = _mm("fox_f", "nn", [h1], [w_f], M=S, N=LANE, K=D, tm=tm, tn=LANE, tk=D)[0]
            F_sh = _fox_gate_fwd(fl, bf_pad)
            F_rows = F_sh[:, :H].T.reshape(H, nq, 1, tq)
            o, lse = _attn_fwd(qkv, F_sh, F_rows * _LOG2E, H, tq, comm=ag.comm(admit["attn_fwd"]))
            st.update(qkv=qkv, fl=fl, F_sh=F_sh, F_rows=F_rows, o=o, lse=lse, w_qkv=w_qkv, w_f=w_f)
            w_o, mix_out = ag.done["b_out"].reshape(D, D), o
        x1, out1 = _mm(f"mix_out{l}", "nn", [mix_out], [w_o], M=S, N=D, K=D, tm=tm, tn=_pick(D, 1024, LANE),
                       tk=D, extras=[("tile", xs), ("row", g1)], out_dtypes=(F32, BF16),
                       epilogue=_resid_epilogue, comm=ag.comm(admit[f"mix_out{l}"]))
        st["x_mid"], st["out1"] = x1, out1
        h2 = _normmod_fwd(f"normmod_ffn{l}", x1, row(norm_ffn_g[l]), sc2, sh2, tr)
        gt, up, act = _mm(f"ffn_in{l}", "nn", [h2, h2], [ag.done[f"wg{l}"], ag.done[f"wu{l}"]], M=S, N=FH, K=D,
                          tm=tm, tn=FH // N_CHIPS, tk=D, b_stacked=True, acc_of=[0, 1], n_acc=2,
                          out_dtypes=(BF16, BF16, BF16), epilogue=_swiglu_epilogue,
                          comm=ag.comm(admit.get(f"ffn_in{l}", ())))
        x2, out2 = _mm(f"ffn_out{l}", "nn", [act], [ag.done[f"wd{l}"].reshape(FH, D)], M=S, N=D, K=FH, tm=tm,
                       tn=_pick(D, 1024, LANE), tk=FH, extras=[("tile", x1), ("row", g2)],
                       out_dtypes=(F32, BF16), epilogue=_resid_epilogue,
                       comm=ag.comm(admit.get(f"ffn_out{l}", ())))
        st.update(h2=h2, gt=gt, up=up, act=act, out2=out2)
        saved.append(st)
        xs = x2
    assert not ag.todo and not ag.half
    wg4 = [ag.done[f"wg{l}"] for l in range(depth)]
    wu4 = [ag.done[f"wu{l}"] for l in range(depth)]
    wd = [ag.done[f"wd{l}"].reshape(FH, D) for l in range(depth)]
    wb_out = ag.done["b_out"].reshape(D, D)

    dx, loss_vec, g_final = _loss_head("loss_head", xs, target, row(final_g), tr)
    loss = lax.psum(loss_vec[0, 0], ("x", "y", "c"))

    rs = _ReduceScatter(idx)
    dmods = [None] * depth
    gmix = [None] * depth
    gffn = [None] * depth
    for l in reversed(range(depth)):
        sh1, sc1, g1, sh2, sc2, g2 = mod[l]
        st = saved[l]
        dog, dg2 = _gate_bwd(f"gate_ffn{l}", dx, st["out2"], g2, tr)
        dgt, dup = _mm(f"ffn_dact{l}", "nt", [dog], [wd[l]], M=S, N=FH, K=D, tm=tm, tn=FH // N_CHIPS, tk=D,
                       extras=[("tile", st["gt"]), ("tile", st["up"])], out_dtypes=(BF16, BF16),
                       epilogue=_swiglu_bwd_epilogue, comm=rs.comm())
        rs.push(f"wd{l}", _mm(f"ffn_dwd{l}", "tn", [st["act"]], [dog], M=FH, N=D, K=S, tm=FH // N_CHIPS,
                              tn=_pick(D, 512, LANE), tk=S, outer="i",
                              comm=rs.comm())[0].reshape(N_CHIPS, FH // N_CHIPS, D))
        dh2 = _mm(f"ffn_dh{l}", "nt", [dgt, dup], [wg4[l], wu4[l]], M=S, N=D, K=FH, tm=_pick(S, 1024, 8),
                  tn=_pick(D, 1024, LANE), tk=FH // N_CHIPS, b_stacked=True, comm=rs.comm())[0]
        rs.push(f"wg{l}", _mm(f"ffn_dwg{l}", "tn", [st["h2"]], [dgt], M=D, N=FH, K=S, tm=_pick(D, 512, LANE),
                              tn=FH // N_CHIPS, tk=S, out_stacked=True, comm=rs.comm())[0])
        rs.push(f"wu{l}", _mm(f"ffn_dwu{l}", "tn", [st["h2"]], [dup], M=D, N=FH, K=S, tm=_pick(D, 512, LANE),
                              tn=FH // N_CHIPS, tk=S, out_stacked=True, comm=rs.comm())[0])
        dx, dsh2, dsc2, gffn[l] = _normmod_bwd(f"normmod_ffn_bwd{l}", st["x_mid"], dh2, dx,
                                               row(norm_ffn_g[l]), sc2, tr)
        dog, dg1 = _gate_bwd(f"gate_mix{l}", dx, st["out1"], g1, tr)
        if l == 0:
            dy = _mm("sgu_dy", "nt", [dog], [wa_out], M=S, N=D, K=D, tm=tm, tn=_pick(D, 1024, LANE), tk=D,
                     out_dtypes=(BF16,))[0]
            rs.push("a_out", _mm("sgu_dwout", "tn", [st["y"]], [dog], M=D, N=D, K=S, tm=_pick(D, 512, LANE),
                                 tn=_pick(D, 1024, LANE),
                                 tk=S)[0].reshape(N_CHIPS, D // N_CHIPS, D))
            da, dws, dbias, g_ln_g, g_ln_b, g_b_in = _sgu_mid_bwd(st["a"], dy, a_ln_g, a_ln_b, w_mask, bias_full, tr)
            rs.push("a_in", _mm("sgu_dwin", "tn", [st["h1"]], [da], M=D, N=2 * D, K=S, tm=_pick(D, 512, LANE),
                                tn=2 * D // N_CHIPS, tk=S, out_stacked=True, comm=rs.comm())[0])
            dh1 = _mm("sgu_dh", "nt", [da], [wa_in4], M=S, N=D, K=2 * D, tm=_pick(S, 1024, 8), tn=_pick(D, 1024, LANE),
                      tk=2 * D // N_CHIPS, b_stacked=True, comm=rs.comm())[0]
            g_w_s = jnp.where(tril[None], dws, 0.0)
            g_b_s = jnp.sum(dbias.reshape(LANE, G, LANE), axis=2).T
        else:
            do = _mm("fox_do", "nt", [dog], [wb_out], M=S, N=D, K=D, tm=tm, tn=_pick(D, 1024, LANE), tk=D,
                     out_dtypes=(BF16,))[0]
            rs.push("b_out", _mm("fox_dwout", "tn", [st["o"]], [dog], M=D, N=D, K=S, tm=_pick(D, 512, LANE),
                                 tn=_pick(D, 1024, LANE),
                                 tk=S)[0].reshape(N_CHIPS, D // N_CHIPS, D))
            delta = _head_dots(do, st["o"], H, tr)
            delta_rows = delta[:, :H].T.reshape(H, nq, 1, tq)
            A_rows = (st["F_rows"] - st["lse"].reshape(H, nq, 1, tq)) * _LOG2E
            w_qkv, w_f = st["w_qkv"], st["w_f"]
            dq, dk, dv, dF_k, dF_q = _attn_bwd(st["qkv"], do, st["F_sh"], A_rows, delta_rows, H, tq,
                                               comm=rs.comm())
            dF_sh = jnp.pad((dF_k.reshape(H, S) + dF_q.reshape(H, S)).T, ((0, 0), (0, LANE - H)))
            dfl, db_f = _fox_gate_bwd(st["fl"], dF_sh, bf_pad)
            dfl_b = dfl.astype(BF16)
            dh_f = _mm("fox_dh_f", "nt", [dfl_b], [w_f], M=S, N=D, K=LANE, tm=tm, tn=_pick(D, 1024, LANE),
                       tk=LANE)[0]
            dh1 = _mm("fox_dh", "nt", [dq, dk, dv], [w_qkv, w_qkv, w_qkv], M=S, N=D, K=D, tm=_pick(S, 256, 8),
                      tn=_pick(D, 1024, LANE), tk=D, b_koffs=[0, 1, 2],
                      extras=[("tile", dh_f)], epilogue=lambda accs, ex: [accs[0] + ex[0]], comm=rs.comm())[0]
            parts = [_mm(f"fox_dw{nm}", "tn", [st["h1"]], [d_], M=D, N=D, K=S, tm=_pick(D, 512, LANE),
                         tn=_pick(D, 1024, LANE), tk=S)[0]
                     for nm, d_ in (("q", dq), ("k", dk), ("v", dv))]
            dwf = _mm("fox_dwf", "tn", [st["h1"]], [dfl_b], M=D, N=LANE, K=S, tm=_pick(D, 512, LANE), tn=LANE,
                      tk=S)[0]
            g_b_w_in = jnp.concatenate(parts + [dwf[:, :H]], axis=1)
            n_b = g_b_w_in.shape[1] // N_CHIPS
            rs.push("b_in", g_b_w_in.reshape(D, N_CHIPS, n_b).transpose(1, 0, 2))
            g_b_f = db_f[:, :H]
        dx, dsh1, dsc1, gmix[l] = _normmod_bwd(f"normmod_mix_bwd{l}", st["x_in"], dh1, dx,
                                               row(norm_mix_g[l]), sc1, tr)
        dmods[l] = jnp.concatenate([dsh1, dsc1, dg1, dsh2, dsc2, dg2], axis=1)
    grad_x = dx[None]

    small = [jnp.concatenate(dmods, axis=0), jnp.concatenate(gmix, axis=0), jnp.concatenate(gffn, axis=0),
             g_b_in, g_ln_g, g_ln_b, g_w_s[None], g_b_s[None], g_b_f, g_final[0]]
    small_w = [ada_b, norm_mix_g, norm_ffn_g, a_b_in, a_ln_g, a_ln_b, a_w_s, a_b_s, b_b_f, final_g]
    small_m = [m_ada_b, m_norm_mix_g, m_norm_ffn_g, m_a_b_in, m_a_ln_g, m_a_ln_b, m_a_w_s, m_a_b_s, m_b_b_f, m_final_g]
    small_v = [v_ada_b, v_norm_mix_g, v_norm_ffn_g, v_a_b_in, v_a_ln_g, v_a_ln_b, v_a_w_s, v_a_b_s, v_b_b_f, v_final_g]
    sizes = [w.size for w in small_w]
    total = sum(sizes)
    padded = -(-total // (8 * LANE)) * (8 * LANE)

    def pack(parts):
        flat = jnp.concatenate([p.reshape(-1) for p in parts])
        return jnp.pad(flat, (0, padded - total)).reshape(padded // LANE, LANE)

    def unpack(buf):
        flat = buf.reshape(-1)
        outs, off = [], 0
        for w, n in zip(small_w, sizes):
            outs.append(flat[off:off + n].reshape(w.shape))
            off += n
        return outs

    g_all = _gather_devices("ag_small_grads", pack(small))
    g_small = _sum_slots("sum_small_grads", g_all)
    _, d_small, m_small, v_small = _adamw("adamw_small", pack(small_w), g_small, pack(small_m), pack(small_v))
    sg, sd, sm, sv_ = unpack(g_small), unpack(d_small), unpack(m_small), unpack(v_small)

    n_dm = depth * N_MOD * D
    dmod_all = g_all.reshape(N_DEV, -1)[:, :n_dm].reshape(N_DEV, depth, N_MOD * D)
    u_ada = None
    for l in range(depth):
        dm_loc = lax.dynamic_slice_in_dim(dmod_all[:, l, :], chip * n_loc, n_loc, axis=1).astype(BF16)
        g_ada = _mm(f"ada_dw{l}", "tn", [c_act], [dm_loc], M=D, N=n_loc, K=N_DEV, tm=_pick(D, 512, LANE),
                    tn=_pick(n_loc, 1024, LANE), tk=N_DEV)[0]
        u_ada = _adamw(f"adamw_ada_w{l}", ada_w, g_ada, m_ada_w, v_ada_w, layer=l, prev=u_ada)

    flushes = 0
    while rs.items:
        _comm_call(f"rs_flush{flushes}", rs.comm())
        flushes += 1
    groups = {"a_in": (a_w_in, m_a_w_in, v_a_w_in), "a_out": (a_w_out, m_a_w_out, v_a_w_out),
              "b_in": (b_w_in, m_b_w_in, v_b_w_in), "b_out": (b_w_out, m_b_w_out, v_b_w_out),
              "wg": (ffn_w_gate, m_ffn_w_gate, v_ffn_w_gate), "wu": (ffn_w_up, m_ffn_w_up, v_ffn_w_up),
              "wd": (ffn_w_down, m_ffn_w_down, v_ffn_w_down)}
    upd = {}
    for n, (w_, m_, v_) in groups.items():
        for l in range(w_.shape[0]):
            tag = n if n in rs.done else f"{n}{l}"
            upd[n] = _adamw("adamw_" + tag, w_, rs.done[tag], m_, v_, layer=l, prev=upd.get(n))
    u_a_in, u_a_out, u_b_in, u_b_out, u_wg, u_wu, u_wd = (upd[n] for n in groups)

    def leaves(i, small_list):
        s = small_list
        return [u_ada[i], s[0], s[1], s[2], u_a_in[i], s[3], s[4], s[5], s[6], s[7], u_a_out[i],
                u_b_in[i], s[8], u_b_out[i], u_wg[i], u_wu[i], u_wd[i], s[9]]

    return (loss, grad_x, *leaves(0, sg), *leaves(1, sd), *leaves(2, sm), *leaves(3, sv_))
```

```python
import functools
import math

import jax
import jax.numpy as jnp
from jax import lax
from jax.experimental import pallas as pl
from jax.experimental.pallas import tpu as pltpu

F32, BF16 = jnp.float32, jnp.bfloat16
LANE = 128
N_CHIPS = 4
N_DEV = 8
N_MOD = 6
EPS = 1e-6
VMEM_LIMIT = 60 * 1024 * 1024
ADAM_LR, ADAM_B1, ADAM_B2, ADAM_EPS, ADAM_WD, ADAM_STEP = 0.001, 0.9, 0.999, 1e-08, 0.01, 10
MESH = pl.DeviceIdType.MESH
ANY = pl.BlockSpec(memory_space=pl.ANY)
SDS = jax.ShapeDtypeStruct


def _pick(dim, pref, align):
    t = min(dim, pref)
    t -= t % align
    while t >= align:
        if dim % t == 0:
            return t
        t -= align
    return dim


def _params(sem=None):
    return pltpu.CompilerParams(dimension_semantics=sem, vmem_limit_bytes=VMEM_LIMIT)


class _Job:
    def __init__(self, ins, outs, aliases, nsem, start, wait):
        self.ins, self.outs, self.aliases, self.nsem, self.start, self.wait = ins, outs, aliases, nsem, start, wait


def _call(body, name, *, grid, in_specs, out_specs, out_shape, args, scratch=(), sem=None, comm=None,
          prefetch=None, keep=()):
    jobs, absorb = comm if comm else ([], None)
    n_pre = 0 if prefetch is None else 1
    n_in, n_out, n_scr = len(args), len(out_shape), len(scratch)
    c_ins = [a for a, _ in keep] + [a for j in jobs for a in j.ins]
    c_outs = [s for j in jobs for s in j.outs]
    aliases = {n_pre + n_in + k: o for k, (_, o) in enumerate(keep)}
    c_scr = []
    i_off, o_off = len(keep), 0
    for j in jobs:
        for a, b in j.aliases.items():
            aliases[n_pre + n_in + i_off + a] = n_out + o_off + b
        i_off += len(j.ins)
        o_off += len(j.outs)
        c_scr += [pltpu.SemaphoreType.DMA((j.nsem,)), pltpu.SemaphoreType.DMA((j.nsem,))]

    def wrapped(*refs):
        bounds = [n_pre + n_in, len(c_ins), n_out, len(c_outs), n_scr]
        parts, p = [], 0
        for n in bounds:
            parts.append(refs[p:p + n])
            p += n
        main_in, cin, main_out, cout, main_scr = parts
        cin = cin[len(keep):]
        csem = refs[p:]

        def run(phase):
            a = b = 0
            for k, j in enumerate(jobs):
                fn = j.start if phase == 0 else j.wait
                fn(cin[a:a + len(j.ins)], cout[b:b + len(j.outs)], csem[2 * k], csem[2 * k + 1])
                a += len(j.ins)
                b += len(j.outs)

        if jobs and grid:
            first = functools.reduce(jnp.logical_and, [pl.program_id(d) == 0 for d in range(len(grid))])
            last = functools.reduce(jnp.logical_and, [pl.program_id(d) == grid[d] - 1 for d in range(len(grid))])
            pl.when(first)(lambda: run(0))
            body(*main_in, *main_out, *main_scr)
            pl.when(last)(lambda: run(1))
        elif jobs:
            run(0)
            body(*main_in, *main_out, *main_scr)
            run(1)
        else:
            body(*main_in, *main_out, *main_scr)

    specs = dict(grid=grid, in_specs=list(in_specs) + [ANY] * len(c_ins),
                 out_specs=list(out_specs) + [ANY] * len(c_outs), scratch_shapes=list(scratch) + c_scr)
    if n_pre:
        specs = dict(grid_spec=pltpu.PrefetchScalarGridSpec(num_scalar_prefetch=1, **specs))
    outs = pl.pallas_call(
        wrapped, name=name, out_shape=list(out_shape) + c_outs, input_output_aliases=aliases,
        compiler_params=_params(("arbitrary",) * len(grid) if jobs else sem), **specs,
    )(*([prefetch] if n_pre else []), *args, *c_ins)
    if jobs:
        absorb(list(outs[n_out:]))
    return list(outs[:n_out])


def _comm_call(name, comm):
    _call(lambda: None, name, grid=(), in_specs=[], out_specs=[], out_shape=[], args=[], comm=comm)


def _mm(name, form, a_list, b_list, *, M, N, K, tm, tn, tk, b_stacked=False, out_stacked=False,
        b_koffs=None, acc_of=None, n_acc=1, extras=(), out_dtypes=(F32,), epilogue=None, comm=None,
        outer="j", b_layer=None):
    assert M % tm == 0 and N % tn == 0 and K % tk == 0, (name, M, N, K, tm, tn, tk)
    nm, nn, nk = M // tm, N // tn, K // tk
    npairs = len(a_list)
    acc_of = acc_of or [0] * npairs
    b_koffs = b_koffs or [0] * npairs
    if epilogue is None:
        epilogue = lambda accs, ex: [accs[0]]

    def spec(shape, fn):
        return pl.BlockSpec(shape, fn if outer == "j" else (lambda i, j, k: fn(j, i, k)))

    a_uniq = [a for p, a in enumerate(a_list) if all(a is not b for b in a_list[:p])]
    a_pos = [next(u for u, b in enumerate(a_uniq) if b is a) for a in a_list]
    in_specs = []
    for _ in a_uniq:
        if form == "tn":
            in_specs.append(spec((tk, tm), lambda j, i, k: (k, i)))
        else:
            in_specs.append(spec((tm, tk), lambda j, i, k: (i, k)))
    for off in b_koffs:
        if b_layer is not None:
            assert form == "nn" and not b_stacked
            in_specs.append(spec((None, tk, tn), lambda j, i, k: (b_layer, k, j)))
        elif form == "nn":
            if b_stacked:
                assert tn * N_CHIPS == N
                in_specs.append(spec((None, tk, tn), lambda j, i, k: (j, k, 0)))
            else:
                in_specs.append(spec((tk, tn), lambda j, i, k, off=off: (off + k, j)))
        elif form == "nt":
            if b_stacked:
                assert tk * N_CHIPS == K
                in_specs.append(spec((None, tn, tk), lambda j, i, k: (k, j, 0)))
            else:
                in_specs.append(spec((tn, tk), lambda j, i, k, off=off: (j, off + k)))
        else:
            in_specs.append(spec((tk, tn), lambda j, i, k: (k, j)))
    for kind, _ in extras:
        if kind == "tile":
            in_specs.append(spec((tm, tn), lambda j, i, k: (i, j)))
        else:
            in_specs.append(spec((1, tn), lambda j, i, k: (0, j)))
    if out_stacked:
        assert tn * N_CHIPS == N
        out_specs = [spec((None, tm, tn), lambda j, i, k: (j, i, 0)) for _ in out_dtypes]
        out_shape = [SDS((N_CHIPS, M, tn), d) for d in out_dtypes]
    else:
        out_specs = [spec((tm, tn), lambda j, i, k: (i, j)) for _ in out_dtypes]
        out_shape = [SDS((M, N), d) for d in out_dtypes]
    dims = {"nn": (((1,), (0,)), ((), ())), "nt": (((1,), (1,)), ((), ())), "tn": (((0,), (0,)), ((), ()))}[form]
    n_a, n_ex, n_out = len(a_uniq), len(extras), len(out_dtypes)

    def body(*refs):
        a_vals = [refs[u][...].astype(BF16) for u in range(n_a)]
        b_refs = refs[n_a:n_a + npairs]
        e_refs = refs[n_a + npairs:n_a + npairs + n_ex]
        o_refs = refs[n_a + npairs + n_ex:n_a + npairs + n_ex + n_out]
        acc_refs = refs[n_a + npairs + n_ex + n_out:]

        tot = [None] * n_acc
        for p in range(npairs):
            d = lax.dot_general(a_vals[a_pos[p]], b_refs[p][...].astype(BF16), dims, preferred_element_type=F32)
            tot[acc_of[p]] = d if tot[acc_of[p]] is None else tot[acc_of[p]] + d

        def finish(accs):
            outs = epilogue(accs, [e[...] for e in e_refs])
            for o_ref, o in zip(o_refs, outs):
                o_ref[...] = o.astype(o_ref.dtype)

        if nk == 1:
            finish(tot)
        else:
            k = pl.program_id(2)

            @pl.when(k == 0)
            def _():
                for r, t in zip(acc_refs, tot):
                    r[...] = t

            @pl.when(k > 0)
            def _():
                for r, t in zip(acc_refs, tot):
                    r[...] += t

            @pl.when(k == nk - 1)
            def _():
                finish([r[...] for r in acc_refs])

    scratch = [pltpu.VMEM((tm, tn), F32) for _ in range(n_acc)] if nk > 1 else []
    return _call(body, name, grid=(nn, nm, nk) if outer == "j" else (nm, nn, nk), in_specs=in_specs,
                 out_specs=out_specs, out_shape=out_shape, scratch=scratch,
                 sem=("parallel", "parallel", "arbitrary"), comm=comm,
                 args=[*a_uniq, *b_list, *[e for _, e in extras]])


def _rowwise(name, fn, rows, vecs, row_outs, acc_widths, tr, comm=None):
    S = rows[0].shape[0]
    assert S % tr == 0
    nr, nv, no, na = len(rows), len(vecs), len(row_outs), len(acc_widths)

    def body(*refs):
        r = [x[...] for x in refs[:nr]]
        v = [x[...] for x in refs[nr:nr + nv]]
        o_refs = refs[nr + nv:nr + nv + no]
        a_refs = refs[nr + nv + no:]
        outs, accs = fn(r, v)
        for o_ref, o in zip(o_refs, outs):
            o_ref[...] = o.astype(o_ref.dtype)
        first = pl.program_id(0) == 0

        @pl.when(first)
        def _():
            for a_ref, a in zip(a_refs, accs):
                a_ref[...] = a

        @pl.when(jnp.logical_not(first))
        def _():
            for a_ref, a in zip(a_refs, accs):
                a_ref[...] += a

    in_specs = [pl.BlockSpec((tr, x.shape[1]), lambda i: (i, 0)) for x in rows]
    in_specs += [pl.BlockSpec(x.shape, lambda i, nd=x.ndim: (0,) * nd) for x in vecs]
    out_specs = [pl.BlockSpec((tr, w), lambda i: (i, 0)) for w, _ in row_outs]
    out_specs += [pl.BlockSpec((1, w), lambda i: (0, 0)) for w in acc_widths]
    out_shape = [SDS((S, w), d) for w, d in row_outs] + [SDS((1, w), F32) for w in acc_widths]
    return _call(body, name, grid=(S // tr,), in_specs=in_specs, out_specs=out_specs, out_shape=out_shape,
                 sem=("arbitrary",), comm=comm, args=[*rows, *vecs])


def _colsum(a):
    return jnp.sum(a, axis=0, keepdims=True)


def _rms_stats(x):
    rstd = lax.rsqrt(jnp.mean(x * x, axis=1, keepdims=True) + EPS)
    return x * rstd, rstd


def _normmod_fwd(name, x, g, sc, sh, tr):
    def fn(r, v):
        n, _ = _rms_stats(r[0])
        return [(n * v[0]) * (1.0 + v[1]) + v[2]], []
    return _rowwise(name, fn, [x], [g, sc, sh], [(x.shape[1], BF16)], [], tr)[0]


def _normmod_bwd(name, x, dh, dres, g, sc, tr, comm=None):
    def fn(r, v):
        x_, dh_, dres_ = r
        g_, sc_ = v
        n, rstd = _rms_stats(x_)
        hn = n * g_
        dhn = dh_ * (1.0 + sc_)
        dn = dhn * g_
        dx = rstd * (dn - n * jnp.mean(dn * n, axis=1, keepdims=True))
        return [dres_ + dx], [_colsum(dh_), _colsum(dh_ * hn), _colsum(dhn * n)]
    D = x.shape[1]
    return _rowwise(name, fn, [x, dh, dres], [g, sc], [(D, F32)], [D, D, D], tr, comm=comm)


def _gate_bwd(name, dx, out, gate, tr):
    def fn(r, v):
        return [v[0] * r[0]], [_colsum(r[0] * r[1].astype(F32))]
    D = dx.shape[1]
    return _rowwise(name, fn, [dx, out], [gate], [(D, BF16)], [D], tr)


def _loss_head(name, x, target, g, tr):
    D = x.shape[1]

    def fn(r, v):
        n, rstd = _rms_stats(r[0])
        err = n * v[0] - r[1]
        loss = 0.5 * jnp.sum(jnp.mean(err * err, axis=1, keepdims=True), axis=0, keepdims=True)
        dy = err * (1.0 / D)
        dn = dy * v[0]
        dx = rstd * (dn - n * jnp.mean(dn * n, axis=1, keepdims=True))
        return [dx], [jnp.broadcast_to(loss, (1, LANE)), _colsum(dy * n)]
    return _rowwise(name, fn, [x, target], [g], [(D, F32)], [LANE, D], tr)


_INV_SQRT2 = 1.0 / math.sqrt(2.0)
_INV_SQRT2PI = 1.0 / math.sqrt(2.0 * math.pi)


def _gelu(a):
    return 0.5 * a * (1.0 + lax.erf(a * _INV_SQRT2))


def _gelu_grad(a):
    return 0.5 * (1.0 + lax.erf(a * _INV_SQRT2)) + a * jnp.exp(-0.5 * a * a) * _INV_SQRT2PI


def _ln_stats(v):
    mu = jnp.mean(v, axis=1, keepdims=True)
    vc = v - mu
    rstd = lax.rsqrt(jnp.mean(vc * vc, axis=1, keepdims=True) + EPS)
    return vc * rstd, rstd


def _chunk_mix(w_ref, vn, tr, G):
    rows = []
    for ch in range(tr // LANE):
        cols = []
        for g in range(G):
            blk = vn[ch * LANE:(ch + 1) * LANE, g * LANE:(g + 1) * LANE]
            cols.append(jnp.dot(w_ref[g], blk, preferred_element_type=F32))
        rows.append(jnp.concatenate(cols, axis=1))
    return jnp.concatenate(rows, axis=0)


def _sgu_mid_fwd(a, ln_g, ln_b, w_mask, bias_full, tr):
    S, D2 = a.shape
    D = D2 // 2
    G = D // LANE

    def body(a_ref, g_ref, b_ref, w_ref, bias_ref, y_ref):
        u = _gelu(a_ref[:, :D])
        v = _gelu(a_ref[:, D:])
        vhat, _ = _ln_stats(v)
        vn = (vhat * g_ref[...] + b_ref[...]).astype(BF16)
        sv = _chunk_mix(w_ref, vn, tr, G) + jnp.concatenate([bias_ref[...]] * (tr // LANE), axis=0)
        y_ref[...] = (u * sv).astype(BF16)

    return pl.pallas_call(
        body, name="sgu_mid_fwd", grid=(S // tr,),
        in_specs=[pl.BlockSpec((tr, D2), lambda i: (i, 0)), pl.BlockSpec((1, D), lambda i: (0, 0)),
                  pl.BlockSpec((1, D), lambda i: (0, 0)), pl.BlockSpec((G, LANE, LANE), lambda i: (0, 0, 0)),
                  pl.BlockSpec((LANE, D), lambda i: (0, 0))],
        out_specs=pl.BlockSpec((tr, D), lambda i: (i, 0)), out_shape=SDS((S, D), BF16),
        compiler_params=_params(("arbitrary",)))(a, ln_g, ln_b, w_mask, bias_full)


def _sgu_mid_bwd(a, dy, ln_g, ln_b, w_mask, bias_full, tr, comm=None):
    S, D2 = a.shape
    D = D2 // 2
    G = D // LANE
    nch = tr // LANE

    def body(a_ref, dy_ref, g_ref, b_ref, w_ref, bias_ref, da_ref, dw_ref, dbias_ref, dg_ref, db_ref, dbin_ref):
        au, av = a_ref[:, :D], a_ref[:, D:]
        u = _gelu(au)
        v = _gelu(av)
        vhat, rstd = _ln_stats(v)
        vn = (vhat * g_ref[...] + b_ref[...]).astype(BF16)
        sv = _chunk_mix(w_ref, vn, tr, G) + jnp.concatenate([bias_ref[...]] * nch, axis=0)
        dy_ = dy_ref[...].astype(F32)
        du = dy_ * sv
        dsv = dy_ * u
        dsv_b = dsv.astype(BF16)
        first = pl.program_id(0) == 0

        @pl.when(first)
        def _():
            dw_ref[...] = jnp.zeros_like(dw_ref)
            dbias_ref[...] = jnp.zeros_like(dbias_ref)
            dg_ref[...] = jnp.zeros_like(dg_ref)
            db_ref[...] = jnp.zeros_like(db_ref)
            dbin_ref[...] = jnp.zeros_like(dbin_ref)

        rows = []
        dbias = None
        for ch in range(nch):
            r0 = ch * LANE
            cols = []
            for g in range(G):
                c0 = g * LANE
                ds_blk = dsv_b[r0:r0 + LANE, c0:c0 + LANE]
                vn_blk = vn[r0:r0 + LANE, c0:c0 + LANE]
                cols.append(lax.dot_general(w_ref[g], ds_blk, (((0,), (0,)), ((), ())),
                                            preferred_element_type=F32))
                dw_ref[g] += lax.dot_general(ds_blk, vn_blk, (((1,), (1,)), ((), ())),
                                             preferred_element_type=F32)
            rows.append(jnp.concatenate(cols, axis=1))
            blk = dsv[r0:r0 + LANE, :]
            dbias = blk if dbias is None else dbias + blk
        dvn = jnp.concatenate(rows, axis=0)
        dbias_ref[...] += dbias
        dg_ref[...] += _colsum(dvn * vhat)
        db_ref[...] += _colsum(dvn)
        dvh = dvn * g_ref[...]
        dv = rstd * (dvh - jnp.mean(dvh, axis=1, keepdims=True)
                     - vhat * jnp.mean(dvh * vhat, axis=1, keepdims=True))
        da_u = du * _gelu_grad(au)
        da_v = dv * _gelu_grad(av)
        da_ref[:, :D] = da_u.astype(BF16)
        da_ref[:, D:] = da_v.astype(BF16)
        dbin_ref[:, :D] += _colsum(da_u)
        dbin_ref[:, D:] += _colsum(da_v)

    full = lambda shp: pl.BlockSpec(shp, lambda i, nd=len(shp): (0,) * nd)
    return _call(
        body, "sgu_mid_bwd", grid=(S // tr,),
        in_specs=[pl.BlockSpec((tr, D2), lambda i: (i, 0)), pl.BlockSpec((tr, D), lambda i: (i, 0)),
                  full((1, D)), full((1, D)), full((G, LANE, LANE)), full((LANE, D))],
        out_specs=[pl.BlockSpec((tr, D2), lambda i: (i, 0)), full((G, LANE, LANE)), full((LANE, D)),
                   full((1, D)), full((1, D)), full((1, D2))],
        out_shape=[SDS((S, D2), BF16), SDS((G, LANE, LANE), F32), SDS((LANE, D), F32),
                   SDS((1, D), F32), SDS((1, D), F32), SDS((1, D2), F32)],
        sem=("arbitrary",), comm=comm, args=[a, dy, ln_g, ln_b, w_mask, bias_full])


def _tri(lower):
    r = lax.broadcasted_iota(jnp.int32, (LANE, LANE), 0)
    c = lax.broadcasted_iota(jnp.int32, (LANE, LANE), 1)
    return jnp.where((c <= r) if lower else (c >= r), 1.0, 0.0).astype(F32)


def _fox_gate_fwd(fl, bf_pad):
    S = fl.shape[0]
    nblk = S // LANE

    def body(fl_ref, b_ref, f_ref):
        tri = _tri(True)

        def step(i, carry):
            r0 = pl.multiple_of(i * LANE, LANE)
            z = fl_ref[pl.ds(r0, LANE), :] + b_ref[...]
            logf = jnp.minimum(z, 0.0) - jnp.log1p(jnp.exp(-jnp.abs(z)))
            f = jnp.dot(tri, logf, preferred_element_type=F32, precision=lax.Precision.HIGHEST) + carry
            f_ref[pl.ds(r0, LANE), :] = f
            return f[LANE - 1:LANE, :]
        lax.fori_loop(0, nblk, step, jnp.zeros((1, LANE), F32))

    return pl.pallas_call(body, name="fox_gate_fwd", out_shape=SDS((S, LANE), F32),
                          compiler_params=_params())(fl, bf_pad)


def _fox_gate_bwd(fl, dF, bf_pad):
    S = fl.shape[0]
    nblk = S // LANE

    def body(fl_ref, df_ref, b_ref, dfl_ref, db_ref):
        tri = _tri(True)

        def prefix(i, carry):
            r0 = pl.multiple_of(i * LANE, LANE)
            p = jnp.dot(tri, df_ref[pl.ds(r0, LANE), :], preferred_element_type=F32,
                        precision=lax.Precision.HIGHEST) + carry
            dfl_ref[pl.ds(r0, LANE), :] = p
            return p[LANE - 1:LANE, :]
        total = lax.fori_loop(0, nblk, prefix, jnp.zeros((1, LANE), F32))
        db_ref[...] = jnp.zeros_like(db_ref)

        def finish(i, carry):
            r0 = pl.multiple_of(i * LANE, LANE)
            dlogf = total - dfl_ref[pl.ds(r0, LANE), :] + df_ref[pl.ds(r0, LANE), :]
            z = fl_ref[pl.ds(r0, LANE), :] + b_ref[...]
            dfl = dlogf / (1.0 + jnp.exp(z))
            dfl_ref[pl.ds(r0, LANE), :] = dfl
            db_ref[...] += _colsum(dfl)
            return carry
        lax.fori_loop(0, nblk, finish, 0)

    return pl.pallas_call(body, name="fox_gate_bwd", out_shape=[SDS((S, LANE), F32), SDS((1, LANE), F32)],
                          compiler_params=_params())(fl, dF, bf_pad)


def _lane_pick(blk, h):
    lane = lax.broadcasted_iota(jnp.int32, blk.shape, 1)
    return jnp.sum(jnp.where(lane == h, blk, 0.0), axis=1, keepdims=True)


_NEG = -1e30
_LOG2E = 1.0 / math.log(2.0)
_LN2 = math.log(2.0)
_SUB = 32


def _attn_fwd(qkv, F_sh, F_rows, H, tq, comm=None):
    S = qkv.shape[0]
    D = H * LANE
    nq = S // tq
    scale = 1.0 / math.sqrt(LANE)

    def body(q_ref, k_ref, v_ref, fsh_ref, frow_ref, o_ref, lse_ref):
        h, i = pl.program_id(0), pl.program_id(1)
        q = q_ref[...]
        fq = _lane_pick(fsh_ref[...], h) * _LOG2E

        def block(j, carry, diagonal):
            m, l, acc = carry
            r0 = pl.multiple_of(j * tq, tq)
            k = k_ref[pl.ds(r0, tq), :]
            v = v_ref[pl.ds(r0, tq), :]
            s = lax.dot_general(q, k, (((1,), (1,)), ((), ())), preferred_element_type=F32) * (scale * _LOG2E)
            s = s + (fq - frow_ref[j])
            if diagonal:
                row = lax.broadcasted_iota(jnp.int32, (tq, tq), 0)
                col = lax.broadcasted_iota(jnp.int32, (tq, tq), 1)
                s = jnp.where(col <= row, s, _NEG)
            m_new = jnp.maximum(m, jnp.max(s, axis=1, keepdims=True))
            alpha = jnp.exp2(m - m_new)
            p = jnp.exp2(s - m_new)
            l = alpha * l + jnp.sum(p, axis=1, keepdims=True)
            acc = alpha * acc + jnp.dot(p.astype(BF16), v, preferred_element_type=F32)
            return m_new, l, acc

        init = (jnp.full((tq, 1), _NEG, F32), jnp.zeros((tq, 1), F32), jnp.zeros((tq, LANE), F32))
        carry = lax.fori_loop(0, i, lambda j, cr: block(j, cr, False), init)
        m, l, acc = block(i, carry, True)
        o_ref[...] = (acc / l).astype(BF16)
        lse_ref[...] = (m + jnp.log2(l)) * _LN2

    return _call(
        body, "attn_fwd", grid=(H, nq),
        in_specs=[pl.BlockSpec((tq, LANE), lambda h, i: (i, h)),
                  pl.BlockSpec((S, LANE), lambda h, i: (0, H + h)),
                  pl.BlockSpec((S, LANE), lambda h, i: (0, 2 * H + h)),
                  pl.BlockSpec((tq, LANE), lambda h, i: (i, 0)),
                  pl.BlockSpec((None, nq, 1, tq), lambda h, i: (h, 0, 0, 0))],
        out_specs=[pl.BlockSpec((tq, LANE), lambda h, i: (i, h)),
                   pl.BlockSpec((None, tq, 1), lambda h, i: (h, i, 0))],
        out_shape=[SDS((S, D), BF16), SDS((H, S, 1), F32)],
        sem=("parallel", "arbitrary"), comm=comm, args=[qkv, qkv, qkv, F_sh, F_rows])


def _attn_bwd(qkv, do, F_sh, A_rows, delta_rows, H, tq, comm=None):
    S = qkv.shape[0]
    D = H * LANE
    nq = S // tq
    scale = 1.0 / math.sqrt(LANE)

    def body(q_ref, do_ref, k_ref, v_ref, fsh_ref, a_ref, dl_ref, dq_ref, dk_ref, dv_ref, df_ref, dfr_ref,
             dq_acc, dfr_acc, st_scr, dp_scr, pt_scr, ds_scr, fk_scr, df_scr, dk_scr, dv_scr):
        h, j = pl.program_id(0), pl.program_id(1)

        @pl.when(j == 0)
        def _():
            dq_acc[...] = jnp.zeros_like(dq_acc)
            dfr_acc[...] = jnp.zeros_like(dfr_acc)

        k = k_ref[...]
        v = v_ref[...]
        fk_scr[...] = _lane_pick(fsh_ref[...], h) * _LOG2E
        df_scr[...] = jnp.zeros((tq, 1), F32)
        dk_scr[...] = jnp.zeros((tq, LANE), F32)
        dv_scr[...] = jnp.zeros((tq, LANE), F32)

        def block(i, diagonal):
            r0 = pl.multiple_of(i * tq, tq)
            q = q_ref[pl.ds(r0, tq), :]
            do_ = do_ref[pl.ds(r0, tq), :]
            st_scr[...] = lax.dot_general(k, q, (((1,), (1,)), ((), ())), preferred_element_type=F32)
            dp_scr[...] = lax.dot_general(v, do_, (((1,), (1,)), ((), ())), preferred_element_type=F32)
            a_row, dl_row = a_ref[i], dl_ref[i]
            dfr = jnp.zeros((1, tq), F32)
            for r in range(tq // _SUB):
                rows = slice(r * _SUB, (r + 1) * _SUB)
                arg = st_scr[rows, :] * (scale * _LOG2E) + (a_row - fk_scr[rows, :])
                if diagonal:
                    row = lax.broadcasted_iota(jnp.int32, (_SUB, tq), 0) + r * _SUB
                    col = lax.broadcasted_iota(jnp.int32, (_SUB, tq), 1)
                    arg = jnp.where(row <= col, arg, _NEG)
                pt = jnp.exp2(arg)
                dst = pt * (dp_scr[rows, :] - dl_row)
                df_scr[rows, :] += jnp.sum(dst, axis=1, keepdims=True)
                dfr = dfr + jnp.sum(dst, axis=0, keepdims=True)
                pt_scr[rows, :] = pt.astype(BF16)
                ds_scr[rows, :] = dst.astype(BF16)
            dfr_acc[i] += dfr
            dsb = ds_scr[...]
            dv_scr[...] += jnp.dot(pt_scr[...], do_, preferred_element_type=F32)
            dk_scr[...] += jnp.dot(dsb, q, preferred_element_type=F32)
            dq_acc[pl.ds(r0, tq), :] += lax.dot_general(dsb, k, (((0,), (0,)), ((), ())),
                                                        preferred_element_type=F32)

        def full_block(i, carry):
            block(i, False)
            return carry

        block(j, True)
        lax.fori_loop(j + 1, nq, full_block, 0)
        dk_ref[...] = (dk_scr[...] * scale).astype(BF16)
        dv_ref[...] = dv_scr[...].astype(BF16)
        df_ref[...] = -df_scr[...]

        @pl.when(j == nq - 1)
        def _():
            dq_ref[...] = (dq_acc[...] * scale).astype(BF16)
            dfr_ref[...] = dfr_acc[...]

    return _call(
        body, "attn_bwd", grid=(H, nq),
        in_specs=[pl.BlockSpec((S, LANE), lambda h, j: (0, h)),
                  pl.BlockSpec((S, LANE), lambda h, j: (0, h)),
                  pl.BlockSpec((tq, LANE), lambda h, j: (j, H + h)),
                  pl.BlockSpec((tq, LANE), lambda h, j: (j, 2 * H + h)),
                  pl.BlockSpec((tq, LANE), lambda h, j: (j, 0)),
                  pl.BlockSpec((None, nq, 1, tq), lambda h, j: (h, 0, 0, 0)),
                  pl.BlockSpec((None, nq, 1, tq), lambda h, j: (h, 0, 0, 0))],
        out_specs=[pl.BlockSpec((S, LANE), lambda h, j: (0, h)),
                   pl.BlockSpec((tq, LANE), lambda h, j: (j, h)),
                   pl.BlockSpec((tq, LANE), lambda h, j: (j, h)),
                   pl.BlockSpec((None, tq, 1), lambda h, j: (h, j, 0)),
                   pl.BlockSpec((None, nq, 1, tq), lambda h, j: (h, 0, 0, 0))],
        out_shape=[SDS((S, D), BF16), SDS((S, D), BF16), SDS((S, D), BF16), SDS((H, S, 1), F32),
                   SDS((H, nq, 1, tq), F32)],
        scratch=[pltpu.VMEM((S, LANE), F32), pltpu.VMEM((nq, 1, tq), F32),
                 pltpu.VMEM((tq, tq), F32), pltpu.VMEM((tq, tq), F32), pltpu.VMEM((tq, tq), BF16),
                 pltpu.VMEM((tq, tq), BF16), pltpu.VMEM((tq, 1), F32), pltpu.VMEM((tq, 1), F32),
                 pltpu.VMEM((tq, LANE), F32), pltpu.VMEM((tq, LANE), F32)],
        sem=("parallel", "arbitrary"), comm=comm, args=[qkv, do, qkv, qkv, F_sh, A_rows, delta_rows])


def _head_dots(do, o, H, tr):
    def fn(r, v):
        prod = r[0].astype(F32) * r[1].astype(F32)
        lane = lax.broadcasted_iota(jnp.int32, (prod.shape[0], LANE), 1)
        out = jnp.zeros((prod.shape[0], LANE), F32)
        for h in range(H):
            s = jnp.sum(prod[:, h * LANE:(h + 1) * LANE], axis=1, keepdims=True)
            out = jnp.where(lane == h, s, out)
        return [out], []
    return _rowwise("attn_delta", fn, [do, o], [], [(LANE, F32)], [], tr)[0]


def _adamw(name, w, g, m, v, layer=None, prev=None):
    R, C = g.shape
    tr = _pick(R, max(8, (512 * 1024) // max(C, 1) // 8 * 8), 8)
    c1 = 1.0 - ADAM_B1 ** ADAM_STEP
    c2 = 1.0 - ADAM_B2 ** ADAM_STEP

    def body(w_ref, g_ref, m_ref, v_ref, go_ref, d_ref, mo_ref, vo_ref):
        g_ = g_ref[...]
        m_ = ADAM_B1 * m_ref[...] + (1.0 - ADAM_B1) * g_
        v_ = ADAM_B2 * v_ref[...] + (1.0 - ADAM_B2) * (g_ * g_)
        go_ref[...] = g_
        d_ref[...] = -ADAM_LR * ((m_ / c1) / (jnp.sqrt(v_ / c2) + ADAM_EPS) + ADAM_WD * w_ref[...])
        mo_ref[...] = m_
        vo_ref[...] = v_

    gspec = pl.BlockSpec((tr, C), lambda i: (i, 0))
    pspec = gspec if layer is None else pl.BlockSpec((None, tr, C), lambda i: (layer, i, 0))
    return _call(body, name, grid=(R // tr,), in_specs=[pspec, gspec, pspec, pspec], out_specs=[pspec] * 4,
                 out_shape=[SDS(w.shape, F32)] * 4, sem=("parallel",), args=[w, g, m, v],
                 keep=[(p, k) for k, p in enumerate(prev)] if prev else ())


def _sum_slots(name, a):
    n, R, C = a.shape
    tr = _pick(R, 256, 8)

    def body(a_ref, o_ref):
        acc = a_ref[0]
        for k in range(1, n):
            acc = acc + a_ref[k]
        o_ref[...] = acc

    return pl.pallas_call(body, name=name, grid=(R // tr,),
                          in_specs=[pl.BlockSpec((n, tr, C), lambda i: (0, i, 0))],
                          out_specs=pl.BlockSpec((tr, C), lambda i: (i, 0)), out_shape=SDS((R, C), F32),
                          compiler_params=_params(("parallel",)))(a)


def _silu_rows(name, a):
    def body(a_ref, o_ref):
        z = a_ref[...]
        o_ref[...] = (z / (1.0 + jnp.exp(-z))).astype(BF16)
    return pl.pallas_call(body, name=name, out_shape=SDS(a.shape, BF16), compiler_params=_params())(a)


def _place():
    return lax.axis_index("x"), lax.axis_index("y"), lax.axis_index("c")


def _rcopy(src, dst, ssem, rsem, dev):
    return pltpu.make_async_remote_copy(src_ref=src, dst_ref=dst, send_sem=ssem, recv_sem=rsem,
                                        device_id=dev, device_id_type=MESH)


def _gather_devices(name, buf):
    R, C = buf.shape

    def body(b_ref, out_ref, ssem, rsem, lsem):
        x, y, c = _place()
        me = 4 * x + 2 * y + c
        mine = pltpu.make_async_copy(b_ref, out_ref.at[me], lsem)
        mine.start()
        peers = []
        for k in range(1, N_DEV):
            fx, fy, fc = (k >> 2) & 1, (k >> 1) & 1, k & 1
            peers.append((x ^ fx, y ^ fy, c ^ fc))
        sends = [_rcopy(b_ref, out_ref.at[me], ssem.at[k], rsem.at[k], p) for k, p in enumerate(peers)]
        for cp in sends:
            cp.start()
        for k, (px, py, pc) in enumerate(peers):
            _rcopy(b_ref, out_ref.at[4 * px + 2 * py + pc], ssem.at[k], rsem.at[k], (px, py, pc)).wait_recv()
        for cp in sends:
            cp.wait_send()
        mine.wait()

    return pl.pallas_call(body, name=name, in_specs=[ANY], out_specs=ANY, out_shape=SDS((N_DEV, R, C), buf.dtype),
                          scratch_shapes=[pltpu.SemaphoreType.DMA((N_DEV - 1,)), pltpu.SemaphoreType.DMA((N_DEV - 1,)),
                                          pltpu.SemaphoreType.DMA])(buf)


def _other_chips(x, y):
    return [(1 - x, y), (x, 1 - y), (1 - x, 1 - y)]


def _cast_slabs(name, ws, idx, comm=None):
    n = len(ws)
    steps = next(s for s in (8, 4, 2, 1) if all(w.shape[1] % (16 * s) == 0 for w, _ in ws))

    def body(s_ref, *refs):
        for w_ref, o_ref in zip(refs[:n], refs[n:]):
            o_ref[...] = w_ref[...].astype(BF16)

    return _call(body, name, grid=(steps,), prefetch=idx, comm=comm, sem=("parallel",), args=[w for w, _ in ws],
                 in_specs=[pl.BlockSpec((None, w.shape[1] // steps, w.shape[2]), lambda i, s, l=l: (l, i, 0))
                           for w, l in ws],
                 out_specs=[pl.BlockSpec((None, w.shape[1] // steps, w.shape[2]), lambda i, s: (s[1], i, 0))
                            for w, _ in ws],
                 out_shape=[SDS((N_CHIPS,) + w.shape[1:], BF16) for w, _ in ws])


def _job_gather_ici(buf):
    _, R, C = buf.shape
    hR = R // 2
    assert hR * 2 == R

    def views(o):
        x, y, c = _place()
        return c, _other_chips(x, y), o.at[2 * x + y, pl.ds(c * hR, hR), :]

    def start(i, o, ss, rs):
        c, chips, mine = views(o[0])
        for j, (cx, cy) in enumerate(chips):
            _rcopy(mine, mine, ss.at[j], rs.at[j], (cx, cy, c)).start()

    def wait(i, o, ss, rs):
        c, chips, mine = views(o[0])
        for j, (cx, cy) in enumerate(chips):
            cp = _rcopy(mine, o[0].at[2 * cx + cy, pl.ds(c * hR, hR), :], ss.at[j], rs.at[j], (cx, cy, c))
            cp.wait_send()
            cp.wait_recv()

    return _Job([buf], [SDS(buf.shape, buf.dtype)], {0: 0}, 3, start, wait)


def _job_gather_pair(buf):
    _, R, C = buf.shape
    hR = R // 2

    def views(o):
        x, y, c = _place()
        return c, (x, y, 1 - c), _other_chips(x, y)

    def start(i, o, ss, rs):
        c, sib, chips = views(o[0])
        for j, (cx, cy) in enumerate(chips):
            got = o[0].at[2 * cx + cy, pl.ds(c * hR, hR), :]
            _rcopy(got, got, ss.at[j], rs.at[j], sib).start()

    def wait(i, o, ss, rs):
        c, sib, chips = views(o[0])
        for j, (cx, cy) in enumerate(chips):
            cp = _rcopy(o[0].at[2 * cx + cy, pl.ds(c * hR, hR), :],
                        o[0].at[2 * cx + cy, pl.ds((1 - c) * hR, hR), :], ss.at[j], rs.at[j], sib)
            cp.wait_send()
            cp.wait_recv()

    return _Job([buf], [SDS(buf.shape, buf.dtype)], {0: 0}, 3, start, wait)


class _Gather:
    def __init__(self, bufs):
        self.todo, self.half, self.done = dict(bufs), {}, {}

    def comm(self, admit=()):
        second, first = list(self.half), list(admit)
        jobs = [_job_gather_pair(self.half[n]) for n in second] + [_job_gather_ici(self.todo[n]) for n in first]

        def absorb(outs):
            for n, o in zip(second + first, outs):
                if n in self.half:
                    del self.half[n]
                    self.done[n] = o
                else:
                    del self.todo[n]
                    self.half[n] = o
        return (jobs, absorb) if jobs else None


def _job_rs_pair(g4):
    _, R, C = g4.shape
    hR = R // 2
    assert hR * 2 == R

    def desc(i, o, ss, rs):
        x, y, c = _place()
        return _rcopy(i[0].at[:, pl.ds((1 - c) * hR, hR), :], o[0], ss.at[0], rs.at[0], (x, y, 1 - c))

    return _Job([g4], [SDS((N_CHIPS, hR, C), F32)], {}, 1,
                lambda i, o, ss, rs: desc(i, o, ss, rs).start(), lambda i, o, ss, rs: desc(i, o, ss, rs).wait())


def _job_rs_chips(p4):
    _, hR, C = p4.shape

    def descs(i, o, ss, rs):
        x, y, c = _place()
        return [_rcopy(i[0].at[2 * cx + cy], o[0].at[j], ss.at[j], rs.at[j], (cx, cy, c))
                for j, (cx, cy) in enumerate(_other_chips(x, y))]

    def start(i, o, ss, rs):
        for cp in descs(i, o, ss, rs):
            cp.start()

    def wait(i, o, ss, rs):
        for cp in descs(i, o, ss, rs):
            cp.wait_send()
            cp.wait_recv()

    return _Job([p4], [SDS((3, hR, C), p4.dtype)], {}, 3, start, wait)


def _job_rs_join(buf):
    R, C = buf.shape
    hR = R // 2

    def desc(o, ss, rs, recv):
        x, y, c = _place()
        mine = o[0].at[pl.ds(c * hR, hR), :]
        return _rcopy(mine, o[0].at[pl.ds((1 - c) * hR, hR), :] if recv else mine, ss.at[0], rs.at[0], (x, y, 1 - c))

    def wait(i, o, ss, rs):
        cp = desc(o, ss, rs, True)
        cp.wait_send()
        cp.wait_recv()

    return _Job([buf], [SDS(buf.shape, buf.dtype)], {0: 0}, 1,
                lambda i, o, ss, rs: desc(o, ss, rs, False).start(), wait)


def _rs_add_pair(name, g4, recv, idx):
    _, R, C = g4.shape
    hR = R // 2
    tr = _pick(hR, max(16, (1 << 20) // C // 16 * 16), 16)
    nb = hR // tr

    def body(s_ref, a_ref, b_ref, pb_ref, po_ref):
        s = a_ref[...] + b_ref[...]
        pb_ref[...] = s.astype(BF16)

        @pl.when(pl.program_id(1) == s_ref[1])
        def _():
            po_ref[...] = s

    gs = pltpu.PrefetchScalarGridSpec(
        num_scalar_prefetch=1, grid=(nb, N_CHIPS),
        in_specs=[pl.BlockSpec((None, tr, C), lambda i, k, s: (k, s[0] * nb + i, 0)),
                  pl.BlockSpec((None, tr, C), lambda i, k, s: (k, i, 0))],
        out_specs=[pl.BlockSpec((None, tr, C), lambda i, k, s: (k, i, 0)),
                   pl.BlockSpec((tr, C), lambda i, k, s: (i, 0))])
    return pl.pallas_call(body, name=name, grid_spec=gs, out_shape=[SDS((N_CHIPS, hR, C), BF16), SDS((hR, C), F32)],
                          compiler_params=_params(("parallel", "arbitrary")))(idx, g4, recv)


def _rs_add_chips(name, own, recv3, idx):
    hR, C = own.shape
    tr = _pick(hR, max(16, (1 << 20) // C // 16 * 16), 16)
    nb = hR // tr

    def body(s_ref, a_ref, b_ref, o_ref):
        o_ref[...] = ((a_ref[...] + b_ref[0].astype(F32)) + b_ref[1].astype(F32)) + b_ref[2].astype(F32)

    gs = pltpu.PrefetchScalarGridSpec(
        num_scalar_prefetch=1, grid=(nb,),
        in_specs=[pl.BlockSpec((tr, C), lambda i, s: (i, 0)), pl.BlockSpec((3, tr, C), lambda i, s: (0, i, 0))],
        out_specs=pl.BlockSpec((tr, C), lambda i, s: (s[0] * nb + i, 0)))
    return pl.pallas_call(body, name=name, grid_spec=gs, out_shape=SDS((2 * hR, C), F32),
                          compiler_params=_params(("parallel",)))(idx, own, recv3)


class _ReduceScatter:
    def __init__(self, idx):
        self.idx, self.items, self.done = idx, [], {}

    def push(self, tag, g4):
        self.items.append([tag, 0, g4])

    def comm(self, ici=1):
        cur, jobs = [], []
        for item in self.items:
            tag, stage, data = item
            if stage == 1:
                if ici == 0:
                    continue
                ici -= 1
            cur.append(item)
            jobs.append(_job_rs_pair(data) if stage == 0 else _job_rs_chips(data[0]) if stage == 1
                        else _job_rs_join(data))

        def absorb(outs):
            for item, o in zip(cur, outs):
                tag, stage, data = item
                if stage == 0:
                    item[1:] = [1, _rs_add_pair(tag + "_rs_add2", data, o, self.idx)]
                elif stage == 1:
                    item[1:] = [2, _rs_add_chips(tag + "_rs_add4", data[1], o, self.idx)]
                else:
                    self.items.remove(item)
                    self.done[tag] = o
        return (jobs, absorb) if jobs else None


def _resid_epilogue(accs, ex):
    return [ex[0] + ex[1] * accs[0], accs[0]]


def _swiglu_epilogue(accs, ex):
    gt, up = accs
    return [gt, up, gt / (1.0 + jnp.exp(-gt)) * up]


def _swiglu_bwd_epilogue(accs, ex):
    dact = accs[0]
    gt, up = ex[0].astype(F32), ex[1].astype(F32)
    sg = 1.0 / (1.0 + jnp.exp(-gt))
    return [dact * up * (sg * (1.0 + gt * (1.0 - sg))), dact * (gt * sg)]


def kernel(x, c, ada_w, ada_b, norm_mix_g, norm_ffn_g, a_w_in, a_b_in, a_ln_g, a_ln_b, a_w_s, a_b_s, a_w_out, b_w_in, b_b_f, b_w_out, ffn_w_gate, ffn_w_up, ffn_w_down, final_g, loss_target, m_ada_w, m_ada_b, m_norm_mix_g, m_norm_ffn_g, m_a_w_in, m_a_b_in, m_a_ln_g, m_a_ln_b, m_a_w_s, m_a_b_s, m_a_w_out, m_b_w_in, m_b_b_f, m_b_w_out, m_ffn_w_gate, m_ffn_w_up, m_ffn_w_down, m_final_g, v_ada_w, v_ada_b, v_norm_mix_g, v_norm_ffn_g, v_a_w_in, v_a_b_in, v_a_ln_g, v_a_ln_b, v_a_w_s, v_a_b_s, v_a_w_out, v_b_w_in, v_b_b_f, v_b_w_out, v_ffn_w_gate, v_ffn_w_up, v_ffn_w_down, v_final_g):
    S, D = x.shape[1], x.shape[2]
    H = D // LANE
    G = D // LANE
    FH = ffn_w_down.shape[1] * N_CHIPS
    depth = ada_w.shape[0]
    assert depth == 2 and a_w_in.shape[0] == 1 and b_w_in.shape[0] == 1
    mx, my, mc = _place()
    chip = 2 * mx + my
    dev = 4 * mx + 2 * my + mc
    x0 = x[0]
    target = loss_target[0]
    tr = _pick(S, 256, 8)
    tm = _pick(S, 512, 8)
    tq_f = _pick(S, 1024, LANE)
    tq = _pick(S, 512, LANE)
    nq = S // tq

    idx = jnp.stack([mc, chip]).astype(jnp.int32)
    shards = {"a_in": (a_w_in, 0), "a_out": (a_w_out, 0), "b_in": (b_w_in, 0), "b_out": (b_w_out, 0)}
    for l in range(depth):
        shards.update({f"wg{l}": (ffn_w_gate, l), f"wu{l}": (ffn_w_up, l), f"wd{l}": (ffn_w_down, l)})
    first = ["a_in", "a_out"]
    rest = [n for n in shards if n not in first]
    ag = _Gather(dict(zip(first, _cast_slabs("cast_first", [shards[n] for n in first], idx))))
    ag.todo.update(zip(rest, _cast_slabs("cast_rest", [shards[n] for n in rest], idx, comm=ag.comm(first))))
    _comm_call("ag_first_pair", ag.comm(["wg0"]))
    wa_in4 = ag.done["a_in"]
    wa_out = ag.done["a_out"].reshape(D, D)

    c_all = _gather_devices("ag_c", jnp.pad(c, ((0, 7), (0, 0)))).reshape(N_DEV, 8, D)[:, 0, :]
    c_act = _silu_rows("c_silu", c_all)
    n_loc = ada_w.shape[2]
    mods = []
    for l in range(depth):
        b_loc = lax.dynamic_slice_in_dim(ada_b[l:l + 1], chip * n_loc, n_loc, axis=1)
        mods.append(_mm(f"ada_fwd{l}", "nn", [c_act], [ada_w], M=N_DEV, N=n_loc, K=D, tm=N_DEV, b_layer=l,
                        tn=_pick(n_loc, 1024, LANE), tk=D, extras=[("row", b_loc)],
                        epilogue=lambda accs, ex: [accs[0] + ex[0]])[0])
    mod_all = _gather_devices("ag_mod", jnp.concatenate(mods, axis=1))
    mod_mine = lax.dynamic_index_in_dim(mod_all[0::2], dev, axis=1, keepdims=False)
    mod_mine = mod_mine.reshape(N_CHIPS, depth, n_loc).transpose(1, 0, 2).reshape(depth, 1, N_MOD * D)
    mod = [[mod_mine[l, :, i * D:(i + 1) * D] for i in range(N_MOD)] for l in range(depth)]

    row = lambda a: a.reshape(1, -1)

    tril = jnp.tril(jnp.ones((LANE, LANE), dtype=bool))
    w_mask = jnp.where(tril[None], a_w_s[0], 0.0).astype(BF16)
    bias_full = jnp.repeat(a_b_s[0].T, LANE, axis=1)
    bf_pad = jnp.pad(b_b_f, ((0, 0), (0, LANE - H)))

    admit = {"sgu_in": ["wu0"], "mix_out0": ["wd0"], "ffn_in0": ["b_in", "b_out"], "ffn_out0": ["wg1"],
             "fox_qkv": ["wu1"], "attn_fwd": ["wd1"], "mix_out1": []}
    saved = []
    xs = x0
    for l in range(depth):
        sh1, sc1, g1, sh2, sc2, g2 = mod[l]
        st = {"x_in": xs}
        h1 = _normmod_fwd(f"normmod_mix{l}", xs, row(norm_mix_g[l]), sc1, sh1, tr)
        st["h1"] = h1
        if l == 0:
            a = _mm("sgu_in", "nn", [h1], [wa_in4], M=S, N=2 * D, K=D, tm=tm, tn=2 * D // N_CHIPS, tk=D,
                    b_stacked=True, extras=[("row", a_b_in)], comm=ag.comm(admit["sgu_in"]),
                    epilogue=lambda accs, ex: [accs[0] + ex[0]])[0]
            y = _sgu_mid_fwd(a, a_ln_g, a_ln_b, w_mask, bias_full, tr)
            st["a"], st["y"] = a, y
            w_o, mix_out = wa_out, y
        else:
            wb_in4 = ag.done["b_in"]
            n_last = wb_in4.shape[2] - H
            w_qkv = jnp.concatenate([wb_in4[k] for k in range(N_CHIPS - 1)] + [wb_in4[-1][:, :n_last]], axis=1)
            w_f = jnp.pad(wb_in4[-1][:, n_last:], ((0, 0), (0, LANE - H)))
            qkv = _mm("fox_qkv", "nn", [h1], [w_qkv], M=S, N=3 * D, K=D, tm=tm, tn=_pick(3 * D, 1024, LANE),
                      tk=D, out_dtypes=(BF16,), comm=ag.comm(admit["fox_qkv"]))[0]
            fl = _mm("fox_f", "nn", [h1], [w_f], M=S, N=LANE, K=D, tm=tm, tn=LANE, tk=D)[0]
            F_sh = _fox_gate_fwd(fl, bf_pad)
            F_hs = F_sh[:, :H].T
            F_rows = F_hs.reshape(H, nq, 1, tq)
            o, lse = _attn_fwd(qkv, F_sh, F_hs.reshape(H, S // tq_f, 1, tq_f) * _LOG2E, H, tq_f,
                               comm=ag.comm(admit["attn_fwd"]))
            st.update(qkv=qkv, fl=fl, F_sh=F_sh, F_rows=F_rows, o=o, lse=lse, w_qkv=w_qkv, w_f=w_f)
            w_o, mix_out = ag.done["b_out"].reshape(D, D), o
        x1, out1 = _mm(f"mix_out{l}", "nn", [mix_out], [w_o], M=S, N=D, K=D, tm=tm, tn=_pick(D, 1024, LANE),
                       tk=D, extras=[("tile", xs), ("row", g1)], out_dtypes=(F32, BF16),
                       epilogue=_resid_epilogue, comm=ag.comm(admit[f"mix_out{l}"]))
        st["x_mid"], st["out1"] = x1, out1
        h2 = _normmod_fwd(f"normmod_ffn{l}", x1, row(norm_ffn_g[l]), sc2, sh2, tr)
        gt, up, act = _mm(f"ffn_in{l}", "nn", [h2, h2], [ag.done[f"wg{l}"], ag.done[f"wu{l}"]], M=S, N=FH, K=D,
                          tm=tm, tn=FH // N_CHIPS, tk=D, b_stacked=True, acc_of=[0, 1], n_acc=2,
                          out_dtypes=(BF16, BF16, BF16), epilogue=_swiglu_epilogue,
                          comm=ag.comm(admit.get(f"ffn_in{l}", ())))
        x2, out2 = _mm(f"ffn_out{l}", "nn", [act], [ag.done[f"wd{l}"].reshape(FH, D)], M=S, N=D, K=FH, tm=tm,
                       tn=_pick(D, 1024, LANE), tk=FH, extras=[("tile", x1), ("row", g2)],
                       out_dtypes=(F32, BF16), epilogue=_resid_epilogue,
                       comm=ag.comm(admit.get(f"ffn_out{l}", ())))
        st.update(h2=h2, gt=gt, up=up, act=act, out2=out2)
        saved.append(st)
        xs = x2
    assert not ag.todo and not ag.half
    wg4 = [ag.done[f"wg{l}"] for l in range(depth)]
    wu4 = [ag.done[f"wu{l}"] for l in range(depth)]
    wd = [ag.done[f"wd{l}"].reshape(FH, D) for l in range(depth)]
    wb_out = ag.done["b_out"].reshape(D, D)

    dx, loss_vec, g_final = _loss_head("loss_head", xs, target, row(final_g), tr)
    loss = lax.psum(loss_vec[0, 0], ("x", "y", "c"))

    rs = _ReduceScatter(idx)
    dmods = [None] * depth
    gmix = [None] * depth
    gffn = [None] * depth
    for l in reversed(range(depth)):
        sh1, sc1, g1, sh2, sc2, g2 = mod[l]
        st = saved[l]
        dog, dg2 = _gate_bwd(f"gate_ffn{l}", dx, st["out2"], g2, tr)
        dgt, dup = _mm(f"ffn_dact{l}", "nt", [dog], [wd[l]], M=S, N=FH, K=D, tm=tm, tn=FH // N_CHIPS, tk=D,
                       extras=[("tile", st["gt"]), ("tile", st["up"])], out_dtypes=(BF16, BF16),
                       epilogue=_swiglu_bwd_epilogue, comm=rs.comm())
        rs.push(f"wd{l}", _mm(f"ffn_dwd{l}", "tn", [st["act"]], [dog], M=FH, N=D, K=S, tm=FH // N_CHIPS,
                              tn=_pick(D, 512, LANE), tk=S, outer="i",
                              comm=rs.comm())[0].reshape(N_CHIPS, FH // N_CHIPS, D))
        dh2 = _mm(f"ffn_dh{l}", "nt", [dgt, dup], [wg4[l], wu4[l]], M=S, N=D, K=FH, tm=_pick(S, 1024, 8),
                  tn=_pick(D, 1024, LANE), tk=FH // N_CHIPS, b_stacked=True, comm=rs.comm())[0]
        rs.push(f"wg{l}", _mm(f"ffn_dwg{l}", "tn", [st["h2"]], [dgt], M=D, N=FH, K=S, tm=_pick(D, 512, LANE),
                              tn=FH // N_CHIPS, tk=S, out_stacked=True, comm=rs.comm())[0])
        rs.push(f"wu{l}", _mm(f"ffn_dwu{l}", "tn", [st["h2"]], [dup], M=D, N=FH, K=S, tm=_pick(D, 512, LANE),
                              tn=FH // N_CHIPS, tk=S, out_stacked=True, comm=rs.comm())[0])
        dx, dsh2, dsc2, gffn[l] = _normmod_bwd(f"normmod_ffn_bwd{l}", st["x_mid"], dh2, dx,
                                               row(norm_ffn_g[l]), sc2, tr, comm=rs.comm(ici=0))
        dog, dg1 = _gate_bwd(f"gate_mix{l}", dx, st["out1"], g1, tr)
        if l == 0:
            dy = _mm("sgu_dy", "nt", [dog], [wa_out], M=S, N=D, K=D, tm=tm, tn=_pick(D, 1024, LANE), tk=D,
                     out_dtypes=(BF16,))[0]
            rs.push("a_out", _mm("sgu_dwout", "tn", [st["y"]], [dog], M=D, N=D, K=S, tm=_pick(D, 512, LANE),
                                 tn=_pick(D, 1024, LANE),
                                 tk=S)[0].reshape(N_CHIPS, D // N_CHIPS, D))
            da, dws, dbias, g_ln_g, g_ln_b, g_b_in = _sgu_mid_bwd(st["a"], dy, a_ln_g, a_ln_b, w_mask, bias_full, tr,
                                                                  comm=rs.comm())
            rs.push("a_in", _mm("sgu_dwin", "tn", [st["h1"]], [da], M=D, N=2 * D, K=S, tm=_pick(D, 512, LANE),
                                tn=2 * D // N_CHIPS, tk=S, out_stacked=True, comm=rs.comm())[0])
            dh1 = _mm("sgu_dh", "nt", [da], [wa_in4], M=S, N=D, K=2 * D, tm=_pick(S, 1024, 8), tn=_pick(D, 1024, LANE),
                      tk=2 * D // N_CHIPS, b_stacked=True, comm=rs.comm())[0]
            g_w_s = jnp.where(tril[None], dws, 0.0)
            g_b_s = jnp.sum(dbias.reshape(LANE, G, LANE), axis=2).T
        else:
            do = _mm("fox_do", "nt", [dog], [wb_out], M=S, N=D, K=D, tm=tm, tn=_pick(D, 1024, LANE), tk=D,
                     out_dtypes=(BF16,))[0]
            rs.push("b_out", _mm("fox_dwout", "tn", [st["o"]], [dog], M=D, N=D, K=S, tm=_pick(D, 512, LANE),
                                 tn=_pick(D, 1024, LANE),
                                 tk=S)[0].reshape(N_CHIPS, D // N_CHIPS, D))
            delta = _head_dots(do, st["o"], H, tr)
            delta_rows = delta[:, :H].T.reshape(H, nq, 1, tq)
            A_rows = (st["F_rows"] - st["lse"].reshape(H, nq, 1, tq)) * _LOG2E
            w_qkv, w_f = st["w_qkv"], st["w_f"]
            dq, dk, dv, dF_k, dF_q = _attn_bwd(st["qkv"], do, st["F_sh"], A_rows, delta_rows, H, tq,
                                               comm=rs.comm(ici=2))
            dF_sh = jnp.pad((dF_k.reshape(H, S) + dF_q.reshape(H, S)).T, ((0, 0), (0, LANE - H)))
            dfl, db_f = _fox_gate_bwd(st["fl"], dF_sh, bf_pad)
            dfl_b = dfl.astype(BF16)
            dh_f = _mm("fox_dh_f", "nt", [dfl_b], [w_f], M=S, N=D, K=LANE, tm=tm, tn=_pick(D, 1024, LANE),
                       tk=LANE)[0]
            dh1 = _mm("fox_dh", "nt", [dq, dk, dv], [w_qkv, w_qkv, w_qkv], M=S, N=D, K=D, tm=_pick(S, 256, 8),
                      tn=_pick(D, 1024, LANE), tk=D, b_koffs=[0, 1, 2],
                      extras=[("tile", dh_f)], epilogue=lambda accs, ex: [accs[0] + ex[0]], comm=rs.comm())[0]
            parts = [_mm(f"fox_dw{nm}", "tn", [st["h1"]], [d_], M=D, N=D, K=S, tm=_pick(D, 512, LANE),
                         tn=_pick(D, 1024, LANE), tk=S, comm=rs.comm(ici=0))[0]
                     for nm, d_ in (("q", dq), ("k", dk), ("v", dv))]
            dwf = _mm("fox_dwf", "tn", [st["h1"]], [dfl_b], M=D, N=LANE, K=S, tm=_pick(D, 512, LANE), tn=LANE,
                      tk=S)[0]
            g_b_w_in = jnp.concatenate(parts + [dwf[:, :H]], axis=1)
            n_b = g_b_w_in.shape[1] // N_CHIPS
            rs.push("b_in", g_b_w_in.reshape(D, N_CHIPS, n_b).transpose(1, 0, 2))
            g_b_f = db_f[:, :H]
        dx, dsh1, dsc1, gmix[l] = _normmod_bwd(f"normmod_mix_bwd{l}", st["x_in"], dh1, dx,
                                               row(norm_mix_g[l]), sc1, tr, comm=rs.comm(ici=1 - l))
        dmods[l] = jnp.concatenate([dsh1, dsc1, dg1, dsh2, dsc2, dg2], axis=1)
    grad_x = dx[None]

    small = [jnp.concatenate(dmods, axis=0), jnp.concatenate(gmix, axis=0), jnp.concatenate(gffn, axis=0),
             g_b_in, g_ln_g, g_ln_b, g_w_s[None], g_b_s[None], g_b_f, g_final[0]]
    small_w = [ada_b, norm_mix_g, norm_ffn_g, a_b_in, a_ln_g, a_ln_b, a_w_s, a_b_s, b_b_f, final_g]
    small_m = [m_ada_b, m_norm_mix_g, m_norm_ffn_g, m_a_b_in, m_a_ln_g, m_a_ln_b, m_a_w_s, m_a_b_s, m_b_b_f, m_final_g]
    small_v = [v_ada_b, v_norm_mix_g, v_norm_ffn_g, v_a_b_in, v_a_ln_g, v_a_ln_b, v_a_w_s, v_a_b_s, v_b_b_f, v_final_g]
    sizes = [w.size for w in small_w]
    total = sum(sizes)
    padded = -(-total // (8 * LANE)) * (8 * LANE)

    def pack(parts):
        flat = jnp.concatenate([p.reshape(-1) for p in parts])
        return jnp.pad(flat, (0, padded - total)).reshape(padded // LANE, LANE)

    def unpack(buf):
        flat = buf.reshape(-1)
        outs, off = [], 0
        for w, n in zip(small_w, sizes):
            outs.append(flat[off:off + n].reshape(w.shape))
            off += n
        return outs

    g_all = _gather_devices("ag_small_grads", pack(small))
    g_small = _sum_slots("sum_small_grads", g_all)
    _, d_small, m_small, v_small = _adamw("adamw_small", pack(small_w), g_small, pack(small_m), pack(small_v))
    sg, sd, sm, sv_ = unpack(g_small), unpack(d_small), unpack(m_small), unpack(v_small)

    n_dm = depth * N_MOD * D
    dmod_all = g_all.reshape(N_DEV, -1)[:, :n_dm].reshape(N_DEV, depth, N_MOD * D)
    u_ada = None
    for l in range(depth):
        dm_loc = lax.dynamic_slice_in_dim(dmod_all[:, l, :], chip * n_loc, n_loc, axis=1).astype(BF16)
        g_ada = _mm(f"ada_dw{l}", "tn", [c_act], [dm_loc], M=D, N=n_loc, K=N_DEV, tm=_pick(D, 512, LANE),
                    tn=_pick(n_loc, 1024, LANE), tk=N_DEV)[0]
        u_ada = _adamw(f"adamw_ada_w{l}", ada_w, g_ada, m_ada_w, v_ada_w, layer=l, prev=u_ada)

    flushes = 0
    while rs.items:
        _comm_call(f"rs_flush{flushes}", rs.comm())
        flushes += 1
    groups = {"a_in": (a_w_in, m_a_w_in, v_a_w_in), "a_out": (a_w_out, m_a_w_out, v_a_w_out),
              "b_in": (b_w_in, m_b_w_in, v_b_w_in), "b_out": (b_w_out, m_b_w_out, v_b_w_out),
              "wg": (ffn_w_gate, m_ffn_w_gate, v_ffn_w_gate), "wu": (ffn_w_up, m_ffn_w_up, v_ffn_w_up),
              "wd": (ffn_w_down, m_ffn_w_down, v_ffn_w_down)}
    upd = {}
    for n, (w_, m_, v_) in groups.items():
        for l in range(w_.shape[0]):
            tag = n if n in rs.done else f"{n}{l}"
            upd[n] = _adamw("adamw_" + tag, w_, rs.done[tag], m_, v_, layer=l, prev=upd.get(n))
    u_a_in, u_a_out, u_b_in, u_b_out, u_wg, u_wu, u_wd = (upd[n] for n in groups)

    def leaves(i, small_list):
        s = small_list
        return [u_ada[i], s[0], s[1], s[2], u_a_in[i], s[3], s[4], s[5], s[6], s[7], u_a_out[i],
                u_b_in[i], s[8], u_b_out[i], u_wg[i], u_wu[i], u_wd[i], s[9]]

    return (loss, grad_x, *leaves(0, sg), *leaves(1, sd), *leaves(2, sm), *leaves(3, sv_))
```

```python
import functools
import math

import jax
import jax.numpy as jnp
from jax import lax
from jax.experimental import pallas as pl
from jax.experimental.pallas import tpu as pltpu

F32, BF16 = jnp.float32, jnp.bfloat16
LANE = 128
N_CHIPS = 4
N_DEV = 8
N_MOD = 6
EPS = 1e-6
VMEM_LIMIT = 60 * 1024 * 1024
ADAM_LR, ADAM_B1, ADAM_B2, ADAM_EPS, ADAM_WD, ADAM_STEP = 0.001, 0.9, 0.999, 1e-08, 0.01, 10
MESH = pl.DeviceIdType.MESH
ANY = pl.BlockSpec(memory_space=pl.ANY)
SDS = jax.ShapeDtypeStruct


def _pick(dim, pref, align):
    t = min(dim, pref)
    t -= t % align
    while t >= align:
        if dim % t == 0:
            return t
        t -= align
    return dim


def _params(sem=None):
    return pltpu.CompilerParams(dimension_semantics=sem, vmem_limit_bytes=VMEM_LIMIT)


class _Job:
    def __init__(self, ins, outs, aliases, nsem, start, wait):
        self.ins, self.outs, self.aliases, self.nsem, self.start, self.wait = ins, outs, aliases, nsem, start, wait


def _call(body, name, *, grid, in_specs, out_specs, out_shape, args, scratch=(), sem=None, comm=None,
          prefetch=None, keep=()):
    jobs, absorb = comm if comm else ([], None)
    n_pre = 0 if prefetch is None else 1
    n_in, n_out, n_scr = len(args), len(out_shape), len(scratch)
    c_ins = [a for a, _ in keep] + [a for j in jobs for a in j.ins]
    c_outs = [s for j in jobs for s in j.outs]
    aliases = {n_pre + n_in + k: o for k, (_, o) in enumerate(keep)}
    c_scr = []
    i_off, o_off = len(keep), 0
    for j in jobs:
        for a, b in j.aliases.items():
            aliases[n_pre + n_in + i_off + a] = n_out + o_off + b
        i_off += len(j.ins)
        o_off += len(j.outs)
        c_scr += [pltpu.SemaphoreType.DMA((j.nsem,)), pltpu.SemaphoreType.DMA((j.nsem,))]

    def wrapped(*refs):
        bounds = [n_pre + n_in, len(c_ins), n_out, len(c_outs), n_scr]
        parts, p = [], 0
        for n in bounds:
            parts.append(refs[p:p + n])
            p += n
        main_in, cin, main_out, cout, main_scr = parts
        cin = cin[len(keep):]
        csem = refs[p:]

        def run(phase):
            a = b = 0
            for k, j in enumerate(jobs):
                fn = j.start if phase == 0 else j.wait
                fn(cin[a:a + len(j.ins)], cout[b:b + len(j.outs)], csem[2 * k], csem[2 * k + 1])
                a += len(j.ins)
                b += len(j.outs)

        if jobs and grid:
            first = functools.reduce(jnp.logical_and, [pl.program_id(d) == 0 for d in range(len(grid))])
            last = functools.reduce(jnp.logical_and, [pl.program_id(d) == grid[d] - 1 for d in range(len(grid))])
            pl.when(first)(lambda: run(0))
            body(*main_in, *main_out, *main_scr)
            pl.when(last)(lambda: run(1))
        elif jobs:
            run(0)
            body(*main_in, *main_out, *main_scr)
            run(1)
        else:
            body(*main_in, *main_out, *main_scr)

    specs = dict(grid=grid, in_specs=list(in_specs) + [ANY] * len(c_ins),
                 out_specs=list(out_specs) + [ANY] * len(c_outs), scratch_shapes=list(scratch) + c_scr)
    if n_pre:
        specs = dict(grid_spec=pltpu.PrefetchScalarGridSpec(num_scalar_prefetch=1, **specs))
    outs = pl.pallas_call(
        wrapped, name=name, out_shape=list(out_shape) + c_outs, input_output_aliases=aliases,
        compiler_params=_params(("arbitrary",) * len(grid) if jobs else sem), **specs,
    )(*([prefetch] if n_pre else []), *args, *c_ins)
    if jobs:
        absorb(list(outs[n_out:]))
    return list(outs[:n_out])


def _comm_call(name, comm):
    _call(lambda: None, name, grid=(), in_specs=[], out_specs=[], out_shape=[], args=[], comm=comm)


def _mm(name, form, a_list, b_list, *, M, N, K, tm, tn, tk, b_stacked=False, out_stacked=False,
        b_koffs=None, acc_of=None, n_acc=1, extras=(), out_dtypes=(F32,), epilogue=None, comm=None,
        outer="j", b_layer=None, out_into=None):
    assert M % tm == 0 and N % tn == 0 and K % tk == 0, (name, M, N, K, tm, tn, tk)
    nm, nn, nk = M // tm, N // tn, K // tk
    npairs = len(a_list)
    acc_of = acc_of or [0] * npairs
    b_koffs = b_koffs or [0] * npairs
    if epilogue is None:
        epilogue = lambda accs, ex: [accs[0]]

    def spec(shape, fn):
        return pl.BlockSpec(shape, fn if outer == "j" else (lambda i, j, k: fn(j, i, k)))

    a_uniq = [a for p, a in enumerate(a_list) if all(a is not b for b in a_list[:p])]
    a_pos = [next(u for u, b in enumerate(a_uniq) if b is a) for a in a_list]
    in_specs = []
    for _ in a_uniq:
        if form == "tn":
            in_specs.append(spec((tk, tm), lambda j, i, k: (k, i)))
        else:
            in_specs.append(spec((tm, tk), lambda j, i, k: (i, k)))
    for off in b_koffs:
        if b_layer is not None:
            assert form == "nn" and not b_stacked
            in_specs.append(spec((None, tk, tn), lambda j, i, k: (b_layer, k, j)))
        elif form == "nn":
            if b_stacked:
                assert tn * N_CHIPS == N
                in_specs.append(spec((None, tk, tn), lambda j, i, k: (j, k, 0)))
            else:
                in_specs.append(spec((tk, tn), lambda j, i, k, off=off: (off + k, j)))
        elif form == "nt":
            if b_stacked:
                assert tk * N_CHIPS == K
                in_specs.append(spec((None, tn, tk), lambda j, i, k: (k, j, 0)))
            else:
                in_specs.append(spec((tn, tk), lambda j, i, k, off=off: (j, off + k)))
        else:
            in_specs.append(spec((tk, tn), lambda j, i, k: (k, j)))
    for kind, _ in extras:
        if kind == "tile":
            in_specs.append(spec((tm, tn), lambda j, i, k: (i, j)))
        else:
            in_specs.append(spec((1, tn), lambda j, i, k: (0, j)))
    if out_stacked:
        assert tn * N_CHIPS == N
        out_specs = [spec((None, tm, tn), lambda j, i, k: (j, i, 0)) for _ in out_dtypes]
        out_shape = [SDS((N_CHIPS, M, tn), d) for d in out_dtypes]
    else:
        rows, roff, prev = out_into if out_into else (M, 0, None)
        out_specs = [spec((tm, tn), lambda j, i, k: (roff + i, j)) for _ in out_dtypes]
        out_shape = [SDS((rows, N), d) for d in out_dtypes]
    dims = {"nn": (((1,), (0,)), ((), ())), "nt": (((1,), (1,)), ((), ())), "tn": (((0,), (0,)), ((), ()))}[form]
    n_a, n_ex, n_out = len(a_uniq), len(extras), len(out_dtypes)

    def body(*refs):
        a_vals = [refs[u][...].astype(BF16) for u in range(n_a)]
        b_refs = refs[n_a:n_a + npairs]
        e_refs = refs[n_a + npairs:n_a + npairs + n_ex]
        o_refs = refs[n_a + npairs + n_ex:n_a + npairs + n_ex + n_out]
        acc_refs = refs[n_a + npairs + n_ex + n_out:]

        tot = [None] * n_acc
        for p in range(npairs):
            d = lax.dot_general(a_vals[a_pos[p]], b_refs[p][...].astype(BF16), dims, preferred_element_type=F32)
            tot[acc_of[p]] = d if tot[acc_of[p]] is None else tot[acc_of[p]] + d

        def finish(accs):
            outs = epilogue(accs, [e[...] for e in e_refs])
            for o_ref, o in zip(o_refs, outs):
                o_ref[...] = o.astype(o_ref.dtype)

        if nk == 1:
            finish(tot)
        else:
            k = pl.program_id(2)

            @pl.when(k == 0)
            def _():
                for r, t in zip(acc_refs, tot):
                    r[...] = t

            @pl.when(k > 0)
            def _():
                for r, t in zip(acc_refs, tot):
                    r[...] += t

            @pl.when(k == nk - 1)
            def _():
                finish([r[...] for r in acc_refs])

    scratch = [pltpu.VMEM((tm, tn), F32) for _ in range(n_acc)] if nk > 1 else []
    return _call(body, name, grid=(nn, nm, nk) if outer == "j" else (nm, nn, nk), in_specs=in_specs,
                 out_specs=out_specs, out_shape=out_shape, scratch=scratch,
                 sem=("parallel", "parallel", "arbitrary"), comm=comm,
                 keep=[(out_into[2], 0)] if out_into and out_into[2] is not None else (),
                 args=[*a_uniq, *b_list, *[e for _, e in extras]])


def _rowwise(name, fn, rows, vecs, row_outs, acc_widths, tr, comm=None):
    S = rows[0].shape[0]
    assert S % tr == 0
    nr, nv, no, na = len(rows), len(vecs), len(row_outs), len(acc_widths)

    def body(*refs):
        r = [x[...] for x in refs[:nr]]
        v = [x[...] for x in refs[nr:nr + nv]]
        o_refs = refs[nr + nv:nr + nv + no]
        a_refs = refs[nr + nv + no:]
        outs, accs = fn(r, v)
        for o_ref, o in zip(o_refs, outs):
            o_ref[...] = o.astype(o_ref.dtype)
        first = pl.program_id(0) == 0

        @pl.when(first)
        def _():
            for a_ref, a in zip(a_refs, accs):
                a_ref[...] = a

        @pl.when(jnp.logical_not(first))
        def _():
            for a_ref, a in zip(a_refs, accs):
                a_ref[...] += a

    in_specs = [pl.BlockSpec((tr, x.shape[1]), lambda i: (i, 0)) for x in rows]
    in_specs += [pl.BlockSpec(x.shape, lambda i, nd=x.ndim: (0,) * nd) for x in vecs]
    out_specs = [pl.BlockSpec((tr, w), lambda i: (i, 0)) for w, _ in row_outs]
    out_specs += [pl.BlockSpec((1, w), lambda i: (0, 0)) for w in acc_widths]
    out_shape = [SDS((S, w), d) for w, d in row_outs] + [SDS((1, w), F32) for w in acc_widths]
    return _call(body, name, grid=(S // tr,), in_specs=in_specs, out_specs=out_specs, out_shape=out_shape,
                 sem=("arbitrary",), comm=comm, args=[*rows, *vecs])


def _colsum(a):
    return jnp.sum(a, axis=0, keepdims=True)


def _rms_stats(x):
    rstd = lax.rsqrt(jnp.mean(x * x, axis=1, keepdims=True) + EPS)
    return x * rstd, rstd


def _normmod_fwd(name, x, g, sc, sh, tr):
    def fn(r, v):
        n, _ = _rms_stats(r[0])
        return [(n * v[0]) * (1.0 + v[1]) + v[2]], []
    return _rowwise(name, fn, [x], [g, sc, sh], [(x.shape[1], BF16)], [], tr)[0]


def _normmod_bwd(name, x, dh, dres, g, sc, tr, comm=None):
    def fn(r, v):
        x_, dh_, dres_ = r
        g_, sc_ = v
        n, rstd = _rms_stats(x_)
        hn = n * g_
        dhn = dh_ * (1.0 + sc_)
        dn = dhn * g_
        dx = rstd * (dn - n * jnp.mean(dn * n, axis=1, keepdims=True))
        return [dres_ + dx], [_colsum(dh_), _colsum(dh_ * hn), _colsum(dhn * n)]
    D = x.shape[1]
    return _rowwise(name, fn, [x, dh, dres], [g, sc], [(D, F32)], [D, D, D], tr, comm=comm)


def _gate_bwd(name, dx, out, gate, tr):
    def fn(r, v):
        return [v[0] * r[0]], [_colsum(r[0] * r[1].astype(F32))]
    D = dx.shape[1]
    return _rowwise(name, fn, [dx, out], [gate], [(D, BF16)], [D], tr)


def _loss_head(name, x, target, g, tr):
    D = x.shape[1]

    def fn(r, v):
        n, rstd = _rms_stats(r[0])
        err = n * v[0] - r[1]
        loss = 0.5 * jnp.sum(jnp.mean(err * err, axis=1, keepdims=True), axis=0, keepdims=True)
        dy = err * (1.0 / D)
        dn = dy * v[0]
        dx = rstd * (dn - n * jnp.mean(dn * n, axis=1, keepdims=True))
        return [dx], [jnp.broadcast_to(loss, (1, LANE)), _colsum(dy * n)]
    return _rowwise(name, fn, [x, target], [g], [(D, F32)], [LANE, D], tr)


_INV_SQRT2 = 1.0 / math.sqrt(2.0)
_INV_SQRT2PI = 1.0 / math.sqrt(2.0 * math.pi)


def _gelu(a):
    return 0.5 * a * (1.0 + lax.erf(a * _INV_SQRT2))


def _gelu_grad(a):
    return 0.5 * (1.0 + lax.erf(a * _INV_SQRT2)) + a * jnp.exp(-0.5 * a * a) * _INV_SQRT2PI


def _ln_stats(v):
    mu = jnp.mean(v, axis=1, keepdims=True)
    vc = v - mu
    rstd = lax.rsqrt(jnp.mean(vc * vc, axis=1, keepdims=True) + EPS)
    return vc * rstd, rstd


def _chunk_mix(w_ref, vn, tr, G):
    rows = []
    for ch in range(tr // LANE):
        cols = []
        for g in range(G):
            blk = vn[ch * LANE:(ch + 1) * LANE, g * LANE:(g + 1) * LANE]
            cols.append(jnp.dot(w_ref[g], blk, preferred_element_type=F32))
        rows.append(jnp.concatenate(cols, axis=1))
    return jnp.concatenate(rows, axis=0)


def _sgu_mid_fwd(a, ln_g, ln_b, w_mask, bias_full, tr):
    S, D2 = a.shape
    D = D2 // 2
    G = D // LANE

    def body(a_ref, g_ref, b_ref, w_ref, bias_ref, y_ref):
        u = _gelu(a_ref[:, :D])
        v = _gelu(a_ref[:, D:])
        vhat, _ = _ln_stats(v)
        vn = (vhat * g_ref[...] + b_ref[...]).astype(BF16)
        sv = _chunk_mix(w_ref, vn, tr, G) + jnp.concatenate([bias_ref[...]] * (tr // LANE), axis=0)
        y_ref[...] = (u * sv).astype(BF16)

    return pl.pallas_call(
        body, name="sgu_mid_fwd", grid=(S // tr,),
        in_specs=[pl.BlockSpec((tr, D2), lambda i: (i, 0)), pl.BlockSpec((1, D), lambda i: (0, 0)),
                  pl.BlockSpec((1, D), lambda i: (0, 0)), pl.BlockSpec((G, LANE, LANE), lambda i: (0, 0, 0)),
                  pl.BlockSpec((LANE, D), lambda i: (0, 0))],
        out_specs=pl.BlockSpec((tr, D), lambda i: (i, 0)), out_shape=SDS((S, D), BF16),
        compiler_params=_params(("arbitrary",)))(a, ln_g, ln_b, w_mask, bias_full)


def _sgu_mid_bwd(a, dy, ln_g, ln_b, w_mask, bias_full, tr, comm=None):
    S, D2 = a.shape
    D = D2 // 2
    G = D // LANE
    nch = tr // LANE

    def body(a_ref, dy_ref, g_ref, b_ref, w_ref, bias_ref, da_ref, dw_ref, dbias_ref, dg_ref, db_ref, dbin_ref):
        au, av = a_ref[:, :D], a_ref[:, D:]
        u = _gelu(au)
        v = _gelu(av)
        vhat, rstd = _ln_stats(v)
        vn = (vhat * g_ref[...] + b_ref[...]).astype(BF16)
        sv = _chunk_mix(w_ref, vn, tr, G) + jnp.concatenate([bias_ref[...]] * nch, axis=0)
        dy_ = dy_ref[...].astype(F32)
        du = dy_ * sv
        dsv = dy_ * u
        dsv_b = dsv.astype(BF16)
        first = pl.program_id(0) == 0

        @pl.when(first)
        def _():
            dw_ref[...] = jnp.zeros_like(dw_ref)
            dbias_ref[...] = jnp.zeros_like(dbias_ref)
            dg_ref[...] = jnp.zeros_like(dg_ref)
            db_ref[...] = jnp.zeros_like(db_ref)
            dbin_ref[...] = jnp.zeros_like(dbin_ref)

        rows = []
        dbias = None
        for ch in range(nch):
            r0 = ch * LANE
            cols = []
            for g in range(G):
                c0 = g * LANE
                ds_blk = dsv_b[r0:r0 + LANE, c0:c0 + LANE]
                vn_blk = vn[r0:r0 + LANE, c0:c0 + LANE]
                cols.append(lax.dot_general(w_ref[g], ds_blk, (((0,), (0,)), ((), ())),
                                            preferred_element_type=F32))
                dw_ref[g] += lax.dot_general(ds_blk, vn_blk, (((1,), (1,)), ((), ())),
                                             preferred_element_type=F32)
            rows.append(jnp.concatenate(cols, axis=1))
            blk = dsv[r0:r0 + LANE, :]
            dbias = blk if dbias is None else dbias + blk
        dvn = jnp.concatenate(rows, axis=0)
        dbias_ref[...] += dbias
        dg_ref[...] += _colsum(dvn * vhat)
        db_ref[...] += _colsum(dvn)
        dvh = dvn * g_ref[...]
        dv = rstd * (dvh - jnp.mean(dvh, axis=1, keepdims=True)
                     - vhat * jnp.mean(dvh * vhat, axis=1, keepdims=True))
        da_u = du * _gelu_grad(au)
        da_v = dv * _gelu_grad(av)
        da_ref[:, :D] = da_u.astype(BF16)
        da_ref[:, D:] = da_v.astype(BF16)
        dbin_ref[:, :D] += _colsum(da_u)
        dbin_ref[:, D:] += _colsum(da_v)

    full = lambda shp: pl.BlockSpec(shp, lambda i, nd=len(shp): (0,) * nd)
    return _call(
        body, "sgu_mid_bwd", grid=(S // tr,),
        in_specs=[pl.BlockSpec((tr, D2), lambda i: (i, 0)), pl.BlockSpec((tr, D), lambda i: (i, 0)),
                  full((1, D)), full((1, D)), full((G, LANE, LANE)), full((LANE, D))],
        out_specs=[pl.BlockSpec((tr, D2), lambda i: (i, 0)), full((G, LANE, LANE)), full((LANE, D)),
                   full((1, D)), full((1, D)), full((1, D2))],
        out_shape=[SDS((S, D2), BF16), SDS((G, LANE, LANE), F32), SDS((LANE, D), F32),
                   SDS((1, D), F32), SDS((1, D), F32), SDS((1, D2), F32)],
        sem=("arbitrary",), comm=comm, args=[a, dy, ln_g, ln_b, w_mask, bias_full])


def _tri(lower):
    r = lax.broadcasted_iota(jnp.int32, (LANE, LANE), 0)
    c = lax.broadcasted_iota(jnp.int32, (LANE, LANE), 1)
    return jnp.where((c <= r) if lower else (c >= r), 1.0, 0.0).astype(F32)


def _fox_gate_fwd(fl, bf_pad):
    S = fl.shape[0]
    nblk = S // LANE

    def body(fl_ref, b_ref, f_ref):
        tri = _tri(True)

        def step(i, carry):
            r0 = pl.multiple_of(i * LANE, LANE)
            z = fl_ref[pl.ds(r0, LANE), :] + b_ref[...]
            logf = jnp.minimum(z, 0.0) - jnp.log1p(jnp.exp(-jnp.abs(z)))
            f = jnp.dot(tri, logf, preferred_element_type=F32, precision=lax.Precision.HIGHEST) + carry
            f_ref[pl.ds(r0, LANE), :] = f
            return f[LANE - 1:LANE, :]
        lax.fori_loop(0, nblk, step, jnp.zeros((1, LANE), F32))

    return pl.pallas_call(body, name="fox_gate_fwd", out_shape=SDS((S, LANE), F32),
                          compiler_params=_params())(fl, bf_pad)


def _fox_gate_bwd(fl, dF, bf_pad):
    S = fl.shape[0]
    nblk = S // LANE

    def body(fl_ref, df_ref, b_ref, dfl_ref, db_ref):
        tri = _tri(True)

        def prefix(i, carry):
            r0 = pl.multiple_of(i * LANE, LANE)
            p = jnp.dot(tri, df_ref[pl.ds(r0, LANE), :], preferred_element_type=F32,
                        precision=lax.Precision.HIGHEST) + carry
            dfl_ref[pl.ds(r0, LANE), :] = p
            return p[LANE - 1:LANE, :]
        total = lax.fori_loop(0, nblk, prefix, jnp.zeros((1, LANE), F32))
        db_ref[...] = jnp.zeros_like(db_ref)

        def finish(i, carry):
            r0 = pl.multiple_of(i * LANE, LANE)
            dlogf = total - dfl_ref[pl.ds(r0, LANE), :] + df_ref[pl.ds(r0, LANE), :]
            z = fl_ref[pl.ds(r0, LANE), :] + b_ref[...]
            dfl = dlogf / (1.0 + jnp.exp(z))
            dfl_ref[pl.ds(r0, LANE), :] = dfl
            db_ref[...] += _colsum(dfl)
            return carry
        lax.fori_loop(0, nblk, finish, 0)

    return pl.pallas_call(body, name="fox_gate_bwd", out_shape=[SDS((S, LANE), F32), SDS((1, LANE), F32)],
                          compiler_params=_params())(fl, dF, bf_pad)


def _lane_pick(blk, h):
    lane = lax.broadcasted_iota(jnp.int32, blk.shape, 1)
    return jnp.sum(jnp.where(lane == h, blk, 0.0), axis=1, keepdims=True)


_NEG = -1e30
_LOG2E = 1.0 / math.log(2.0)
_LN2 = math.log(2.0)
_SUB = 32


def _attn_fwd(qkv, F_sh, F_rows, H, tq, comm=None):
    S = qkv.shape[0]
    D = H * LANE
    nq = S // tq
    scale = 1.0 / math.sqrt(LANE)

    def body(q_ref, k_ref, v_ref, fsh_ref, frow_ref, o_ref, lse_ref):
        h, i = pl.program_id(0), pl.program_id(1)
        q = q_ref[...]
        fq = _lane_pick(fsh_ref[...], h) * _LOG2E

        def block(j, carry, diagonal):
            m, l, acc = carry
            r0 = pl.multiple_of(j * tq, tq)
            k = k_ref[pl.ds(r0, tq), :]
            v = v_ref[pl.ds(r0, tq), :]
            s = lax.dot_general(q, k, (((1,), (1,)), ((), ())), preferred_element_type=F32) * (scale * _LOG2E)
            s = s + (fq - frow_ref[j])
            if diagonal:
                row = lax.broadcasted_iota(jnp.int32, (tq, tq), 0)
                col = lax.broadcasted_iota(jnp.int32, (tq, tq), 1)
                s = jnp.where(col <= row, s, _NEG)
            m_new = jnp.maximum(m, jnp.max(s, axis=1, keepdims=True))
            alpha = jnp.exp2(m - m_new)
            p = jnp.exp2(s - m_new)
            l = alpha * l + jnp.sum(p, axis=1, keepdims=True)
            acc = alpha * acc + jnp.dot(p.astype(BF16), v, preferred_element_type=F32)
            return m_new, l, acc

        init = (jnp.full((tq, 1), _NEG, F32), jnp.zeros((tq, 1), F32), jnp.zeros((tq, LANE), F32))
        carry = lax.fori_loop(0, i, lambda j, cr: block(j, cr, False), init)
        m, l, acc = block(i, carry, True)
        o_ref[...] = (acc / l).astype(BF16)
        lse_ref[...] = (m + jnp.log2(l)) * _LN2

    return _call(
        body, "attn_fwd", grid=(H, nq),
        in_specs=[pl.BlockSpec((tq, LANE), lambda h, i: (i, h)),
                  pl.BlockSpec((S, LANE), lambda h, i: (0, H + h)),
                  pl.BlockSpec((S, LANE), lambda h, i: (0, 2 * H + h)),
                  pl.BlockSpec((tq, LANE), lambda h, i: (i, 0)),
                  pl.BlockSpec((None, nq, 1, tq), lambda h, i: (h, 0, 0, 0))],
        out_specs=[pl.BlockSpec((tq, LANE), lambda h, i: (i, h)),
                   pl.BlockSpec((None, tq, 1), lambda h, i: (h, i, 0))],
        out_shape=[SDS((S, D), BF16), SDS((H, S, 1), F32)],
        sem=("parallel", "arbitrary"), comm=comm, args=[qkv, qkv, qkv, F_sh, F_rows])


def _attn_bwd(qkv, do, F_sh, A_rows, delta_rows, H, tq, comm=None):
    S = qkv.shape[0]
    D = H * LANE
    nq = S // tq
    scale = 1.0 / math.sqrt(LANE)

    def body(q_ref, do_ref, k_ref, v_ref, fsh_ref, a_ref, dl_ref, dq_ref, dk_ref, dv_ref, df_ref, dfr_ref,
             dq_acc, dfr_acc, st_scr, dp_scr, pt_scr, ds_scr, fk_scr, df_scr, dk_scr, dv_scr):
        h, j = pl.program_id(0), pl.program_id(1)

        @pl.when(j == 0)
        def _():
            dq_acc[...] = jnp.zeros_like(dq_acc)
            dfr_acc[...] = jnp.zeros_like(dfr_acc)

        k = k_ref[...]
        v = v_ref[...]
        fk_scr[...] = _lane_pick(fsh_ref[...], h) * _LOG2E
        df_scr[...] = jnp.zeros((tq, 1), F32)
        dk_scr[...] = jnp.zeros((tq, LANE), F32)
        dv_scr[...] = jnp.zeros((tq, LANE), F32)

        def block(i, diagonal):
            r0 = pl.multiple_of(i * tq, tq)
            q = q_ref[pl.ds(r0, tq), :]
            do_ = do_ref[pl.ds(r0, tq), :]
            st_scr[...] = lax.dot_general(k, q, (((1,), (1,)), ((), ())), preferred_element_type=F32)
            dp_scr[...] = lax.dot_general(v, do_, (((1,), (1,)), ((), ())), preferred_element_type=F32)
            a_row, dl_row = a_ref[i], dl_ref[i]
            dfr = jnp.zeros((1, tq), F32)
            for r in range(tq // _SUB):
                rows = slice(r * _SUB, (r + 1) * _SUB)
                arg = st_scr[rows, :] * (scale * _LOG2E) + (a_row - fk_scr[rows, :])
                if diagonal:
                    row = lax.broadcasted_iota(jnp.int32, (_SUB, tq), 0) + r * _SUB
                    col = lax.broadcasted_iota(jnp.int32, (_SUB, tq), 1)
                    arg = jnp.where(row <= col, arg, _NEG)
                pt = jnp.exp2(arg)
                dst = pt * (dp_scr[rows, :] - dl_row)
                df_scr[rows, :] += jnp.sum(dst, axis=1, keepdims=True)
                dfr = dfr + jnp.sum(dst, axis=0, keepdims=True)
                pt_scr[rows, :] = pt.astype(BF16)
                ds_scr[rows, :] = dst.astype(BF16)
            dfr_acc[i] += dfr
            dsb = ds_scr[...]
            dv_scr[...] += jnp.dot(pt_scr[...], do_, preferred_element_type=F32)
            dk_scr[...] += jnp.dot(dsb, q, preferred_element_type=F32)
            dq_acc[pl.ds(r0, tq), :] += lax.dot_general(dsb, k, (((0,), (0,)), ((), ())),
                                                        preferred_element_type=F32)

        def full_block(i, carry):
            block(i, False)
            return carry

        block(j, True)
        lax.fori_loop(j + 1, nq, full_block, 0)
        dk_ref[...] = (dk_scr[...] * scale).astype(BF16)
        dv_ref[...] = dv_scr[...].astype(BF16)
        df_ref[...] = -df_scr[...]

        @pl.when(j == nq - 1)
        def _():
            dq_ref[...] = (dq_acc[...] * scale).astype(BF16)
            dfr_ref[...] = dfr_acc[...]

    return _call(
        body, "attn_bwd", grid=(H, nq),
        in_specs=[pl.BlockSpec((S, LANE), lambda h, j: (0, h)),
                  pl.BlockSpec((S, LANE), lambda h, j: (0, h)),
                  pl.BlockSpec((tq, LANE), lambda h, j: (j, H + h)),
                  pl.BlockSpec((tq, LANE), lambda h, j: (j, 2 * H + h)),
                  pl.BlockSpec((tq, LANE), lambda h, j: (j, 0)),
                  pl.BlockSpec((None, nq, 1, tq), lambda h, j: (h, 0, 0, 0)),
                  pl.BlockSpec((None, nq, 1, tq), lambda h, j: (h, 0, 0, 0))],
        out_specs=[pl.BlockSpec((S, LANE), lambda h, j: (0, h)),
                   pl.BlockSpec((tq, LANE), lambda h, j: (j, h)),
                   pl.BlockSpec((tq, LANE), lambda h, j: (j, h)),
                   pl.BlockSpec((None, tq, 1), lambda h, j: (h, j, 0)),
                   pl.BlockSpec((None, nq, 1, tq), lambda h, j: (h, 0, 0, 0))],
        out_shape=[SDS((S, D), BF16), SDS((S, D), BF16), SDS((S, D), BF16), SDS((H, S, 1), F32),
                   SDS((H, nq, 1, tq), F32)],
        scratch=[pltpu.VMEM((S, LANE), F32), pltpu.VMEM((nq, 1, tq), F32),
                 pltpu.VMEM((tq, tq), F32), pltpu.VMEM((tq, tq), F32), pltpu.VMEM((tq, tq), BF16),
                 pltpu.VMEM((tq, tq), BF16), pltpu.VMEM((tq, 1), F32), pltpu.VMEM((tq, 1), F32),
                 pltpu.VMEM((tq, LANE), F32), pltpu.VMEM((tq, LANE), F32)],
        sem=("parallel", "arbitrary"), comm=comm, args=[qkv, do, qkv, qkv, F_sh, A_rows, delta_rows])


def _head_dots(do, o, H, tr):
    def fn(r, v):
        prod = r[0].astype(F32) * r[1].astype(F32)
        lane = lax.broadcasted_iota(jnp.int32, (prod.shape[0], LANE), 1)
        out = jnp.zeros((prod.shape[0], LANE), F32)
        for h in range(H):
            s = jnp.sum(prod[:, h * LANE:(h + 1) * LANE], axis=1, keepdims=True)
            out = jnp.where(lane == h, s, out)
        return [out], []
    return _rowwise("attn_delta", fn, [do, o], [], [(LANE, F32)], [], tr)[0]


def _adamw(name, w, g, m, v, layer=None, prev=None):
    R, C = g.shape
    tr = _pick(R, max(8, (512 * 1024) // max(C, 1) // 8 * 8), 8)
    c1 = 1.0 - ADAM_B1 ** ADAM_STEP
    c2 = 1.0 - ADAM_B2 ** ADAM_STEP

    def body(w_ref, g_ref, m_ref, v_ref, go_ref, d_ref, mo_ref, vo_ref):
        g_ = g_ref[...]
        m_ = ADAM_B1 * m_ref[...] + (1.0 - ADAM_B1) * g_
        v_ = ADAM_B2 * v_ref[...] + (1.0 - ADAM_B2) * (g_ * g_)
        go_ref[...] = g_
        d_ref[...] = -ADAM_LR * ((m_ / c1) / (jnp.sqrt(v_ / c2) + ADAM_EPS) + ADAM_WD * w_ref[...])
        mo_ref[...] = m_
        vo_ref[...] = v_

    if w.ndim == 3 and w.shape[1] == 1 and layer is None:
        tc = _pick(C, 2 * LANE, LANE)
        gspec = pl.BlockSpec((R, tc), lambda i: (0, i))
        pspec = pl.BlockSpec((R, None, tc), lambda i: (0, 0, i))
        grid = (C // tc,)
    else:
        gspec = pl.BlockSpec((tr, C), lambda i: (i, 0))
        pspec = gspec if layer is None else pl.BlockSpec((None, tr, C), lambda i: (layer, i, 0))
        grid = (R // tr,)
    return _call(body, name, grid=grid, in_specs=[pspec, gspec, pspec, pspec], out_specs=[pspec] * 4,
                 out_shape=[SDS(w.shape, F32)] * 4, sem=("parallel",), args=[w, g, m, v],
                 keep=[(p, k) for k, p in enumerate(prev)] if prev else ())


def _sum_slots(name, a):
    n, R, C = a.shape
    tr = _pick(R, 256, 8)

    def body(a_ref, o_ref):
        acc = a_ref[0]
        for k in range(1, n):
            acc = acc + a_ref[k]
        o_ref[...] = acc

    return pl.pallas_call(body, name=name, grid=(R // tr,),
                          in_specs=[pl.BlockSpec((n, tr, C), lambda i: (0, i, 0))],
                          out_specs=pl.BlockSpec((tr, C), lambda i: (i, 0)), out_shape=SDS((R, C), F32),
                          compiler_params=_params(("parallel",)))(a)


def _silu_rows(name, a):
    def body(a_ref, o_ref):
        z = a_ref[...]
        o_ref[...] = (z / (1.0 + jnp.exp(-z))).astype(BF16)
    return pl.pallas_call(body, name=name, out_shape=SDS(a.shape, BF16), compiler_params=_params())(a)


def _place():
    return lax.axis_index("x"), lax.axis_index("y"), lax.axis_index("c")


def _rcopy(src, dst, ssem, rsem, dev):
    return pltpu.make_async_remote_copy(src_ref=src, dst_ref=dst, send_sem=ssem, recv_sem=rsem,
                                        device_id=dev, device_id_type=MESH)


def _gather_devices(name, buf):
    R, C = buf.shape

    def body(b_ref, out_ref, ssem, rsem, lsem):
        x, y, c = _place()
        me = 4 * x + 2 * y + c
        mine = pltpu.make_async_copy(b_ref, out_ref.at[me], lsem)
        mine.start()
        peers = []
        for k in range(1, N_DEV):
            fx, fy, fc = (k >> 2) & 1, (k >> 1) & 1, k & 1
            peers.append((x ^ fx, y ^ fy, c ^ fc))
        sends = [_rcopy(b_ref, out_ref.at[me], ssem.at[k], rsem.at[k], p) for k, p in enumerate(peers)]
        for cp in sends:
            cp.start()
        for k, (px, py, pc) in enumerate(peers):
            _rcopy(b_ref, out_ref.at[4 * px + 2 * py + pc], ssem.at[k], rsem.at[k], (px, py, pc)).wait_recv()
        for cp in sends:
            cp.wait_send()
        mine.wait()

    return pl.pallas_call(body, name=name, in_specs=[ANY], out_specs=ANY, out_shape=SDS((N_DEV, R, C), buf.dtype),
                          scratch_shapes=[pltpu.SemaphoreType.DMA((N_DEV - 1,)), pltpu.SemaphoreType.DMA((N_DEV - 1,)),
                                          pltpu.SemaphoreType.DMA])(buf)


def _other_chips(x, y):
    return [(1 - x, y), (x, 1 - y), (1 - x, 1 - y)]


def _cast_slabs(name, ws, idx, comm=None):
    n = len(ws)
    steps = next(s for s in (8, 4, 2, 1)
                 if all((w.shape[2] % (LANE * s) if l is None else w.shape[1] % (16 * s)) == 0 for w, l in ws))

    def body(s_ref, *refs):
        for w_ref, o_ref in zip(refs[:n], refs[n:]):
            o_ref[...] = w_ref[...].astype(BF16)

    in_specs, out_specs, out_shape = [], [], []
    for w, l in ws:
        if l is None:
            R, _, C = w.shape
            in_specs.append(pl.BlockSpec((R, None, C // steps), lambda i, s: (0, 0, i)))
            out_specs.append(pl.BlockSpec((None, R, C // steps), lambda i, s: (s[1], 0, i)))
        else:
            _, R, C = w.shape
            in_specs.append(pl.BlockSpec((None, R // steps, C), lambda i, s, l=l: (l, i, 0)))
            out_specs.append(pl.BlockSpec((None, R // steps, C), lambda i, s: (s[1], i, 0)))
        out_shape.append(SDS((N_CHIPS, R, C), BF16))
    return _call(body, name, grid=(steps,), prefetch=idx, comm=comm, sem=("parallel",), args=[w for w, _ in ws],
                 in_specs=in_specs, out_specs=out_specs, out_shape=out_shape)


def _half(ref, lead, hf, cols):
    n = ref.shape[-1 if cols else -2] // 2
    assert 2 * n == ref.shape[-1 if cols else -2]
    cut = pl.ds(hf * n, n)
    return ref.at[(*lead, slice(None), cut) if cols else (*lead, cut, slice(None))]


def _job_gather_ici(buf, cols=False):
    def views(o):
        x, y, c = _place()
        return c, _other_chips(x, y), _half(o, (2 * x + y,), c, cols)

    def start(i, o, ss, rs):
        c, chips, mine = views(o[0])
        for j, (cx, cy) in enumerate(chips):
            _rcopy(mine, mine, ss.at[j], rs.at[j], (cx, cy, c)).start()

    def wait(i, o, ss, rs):
        c, chips, mine = views(o[0])
        for j, (cx, cy) in enumerate(chips):
            cp = _rcopy(mine, _half(o[0], (2 * cx + cy,), c, cols), ss.at[j], rs.at[j], (cx, cy, c))
            cp.wait_send()
            cp.wait_recv()

    return _Job([buf], [SDS(buf.shape, buf.dtype)], {0: 0}, 3, start, wait)


def _job_gather_pair(buf, cols=False):
    def views(o):
        x, y, c = _place()
        return c, (x, y, 1 - c), _other_chips(x, y)

    def start(i, o, ss, rs):
        c, sib, chips = views(o[0])
        for j, (cx, cy) in enumerate(chips):
            got = _half(o[0], (2 * cx + cy,), c, cols)
            _rcopy(got, got, ss.at[j], rs.at[j], sib).start()

    def wait(i, o, ss, rs):
        c, sib, chips = views(o[0])
        for j, (cx, cy) in enumerate(chips):
            cp = _rcopy(_half(o[0], (2 * cx + cy,), c, cols), _half(o[0], (2 * cx + cy,), 1 - c, cols),
                        ss.at[j], rs.at[j], sib)
            cp.wait_send()
            cp.wait_recv()

    return _Job([buf], [SDS(buf.shape, buf.dtype)], {0: 0}, 3, start, wait)


class _Gather:
    def __init__(self, bufs, by_cols=()):
        self.todo, self.half, self.done, self.by_cols = dict(bufs), {}, {}, set(by_cols)

    def comm(self, admit=()):
        second, first = list(self.half), list(admit)
        jobs = [_job_gather_pair(self.half[n], n in self.by_cols) for n in second]
        jobs += [_job_gather_ici(self.todo[n], n in self.by_cols) for n in first]

        def absorb(outs):
            for n, o in zip(second + first, outs):
                if n in self.half:
                    del self.half[n]
                    self.done[n] = o
                else:
                    del self.todo[n]
                    self.half[n] = o
        return (jobs, absorb) if jobs else None


def _half_shape(R, C, cols):
    return (R, C // 2) if cols else (R // 2, C)


def _job_rs_pair(g4, cols=False):
    _, R, C = g4.shape

    def desc(i, o, ss, rs):
        x, y, c = _place()
        return _rcopy(_half(i[0], (slice(None),), 1 - c, cols), o[0], ss.at[0], rs.at[0], (x, y, 1 - c))

    return _Job([g4], [SDS((N_CHIPS,) + _half_shape(R, C, cols), F32)], {}, 1,
                lambda i, o, ss, rs: desc(i, o, ss, rs).start(), lambda i, o, ss, rs: desc(i, o, ss, rs).wait())


def _job_rs_chips(p4):
    _, hR, C = p4.shape

    def descs(i, o, ss, rs):
        x, y, c = _place()
        return [_rcopy(i[0].at[2 * cx + cy], o[0].at[j], ss.at[j], rs.at[j], (cx, cy, c))
                for j, (cx, cy) in enumerate(_other_chips(x, y))]

    def start(i, o, ss, rs):
        for cp in descs(i, o, ss, rs):
            cp.start()

    def wait(i, o, ss, rs):
        for cp in descs(i, o, ss, rs):
            cp.wait_send()
            cp.wait_recv()

    return _Job([p4], [SDS((3, hR, C), p4.dtype)], {}, 3, start, wait)


def _job_rs_join(buf, cols=False):
    def desc(o, ss, rs, recv):
        x, y, c = _place()
        mine = _half(o[0], (), c, cols)
        return _rcopy(mine, _half(o[0], (), 1 - c, cols) if recv else mine, ss.at[0], rs.at[0], (x, y, 1 - c))

    def wait(i, o, ss, rs):
        cp = desc(o, ss, rs, True)
        cp.wait_send()
        cp.wait_recv()

    return _Job([buf], [SDS(buf.shape, buf.dtype)], {0: 0}, 1,
                lambda i, o, ss, rs: desc(o, ss, rs, False).start(), wait)


def _walk(R, C, cols):
    if cols:
        tc = _pick(C, 2 * LANE, LANE)
        return (R, tc), C // tc
    tr = _pick(R, max(16, (1 << 20) // C // 16 * 16), 16)
    return (tr, C), R // tr


def _rs_add_pair(name, g4, recv, idx, cols=False):
    _, R, C = g4.shape
    hshape = _half_shape(R, C, cols)
    blk, nb = _walk(*hshape, cols)
    at = (lambda k, i: (k, 0, i)) if cols else (lambda k, i: (k, i, 0))

    def body(s_ref, a_ref, b_ref, pb_ref, po_ref):
        s = a_ref[...] + b_ref[...]
        pb_ref[...] = s.astype(BF16)

        @pl.when(pl.program_id(1) == s_ref[1])
        def _():
            po_ref[...] = s

    gs = pltpu.PrefetchScalarGridSpec(
        num_scalar_prefetch=1, grid=(nb, N_CHIPS),
        in_specs=[pl.BlockSpec((None,) + blk, lambda i, k, s: at(k, s[0] * nb + i)),
                  pl.BlockSpec((None,) + blk, lambda i, k, s: at(k, i))],
        out_specs=[pl.BlockSpec((None,) + blk, lambda i, k, s: at(k, i)),
                   pl.BlockSpec(blk, lambda i, k, s: at(k, i)[1:])])
    return pl.pallas_call(body, name=name, grid_spec=gs,
                          out_shape=[SDS((N_CHIPS,) + hshape, BF16), SDS(hshape, F32)],
                          compiler_params=_params(("parallel", "arbitrary")))(idx, g4, recv)


def _rs_add_chips(name, own, recv3, idx, cols=False):
    hR, hC = own.shape
    blk, nb = _walk(hR, hC, cols)
    at = (lambda i: (0, i)) if cols else (lambda i: (i, 0))

    def body(s_ref, a_ref, b_ref, o_ref):
        o_ref[...] = ((a_ref[...] + b_ref[0].astype(F32)) + b_ref[1].astype(F32)) + b_ref[2].astype(F32)

    gs = pltpu.PrefetchScalarGridSpec(
        num_scalar_prefetch=1, grid=(nb,),
        in_specs=[pl.BlockSpec(blk, lambda i, s: at(i)), pl.BlockSpec((3,) + blk, lambda i, s: (0,) + at(i))],
        out_specs=pl.BlockSpec(blk, lambda i, s: at(s[0] * nb + i)))
    return pl.pallas_call(body, name=name, grid_spec=gs,
                          out_shape=SDS((hR, 2 * hC) if cols else (2 * hR, hC), F32),
                          compiler_params=_params(("parallel",)))(idx, own, recv3)


class _ReduceScatter:
    def __init__(self, idx):
        self.idx, self.items, self.done = idx, [], {}

    def push(self, tag, g4, cols=False):
        self.items.append([tag, 0, g4, cols])

    def comm(self, ici=1):
        cur, jobs = [], []
        for item in self.items:
            tag, stage, data, cols = item
            if stage == 1:
                if ici == 0:
                    continue
                ici -= 1
            cur.append(item)
            jobs.append(_job_rs_pair(data, cols) if stage == 0 else _job_rs_chips(data[0]) if stage == 1
                        else _job_rs_join(data, cols))

        def absorb(outs):
            for item, o in zip(cur, outs):
                tag, stage, data, cols = item
                if stage == 0:
                    item[1:3] = [1, _rs_add_pair(tag + "_rs_add2", data, o, self.idx, cols)]
                elif stage == 1:
                    item[1:3] = [2, _rs_add_chips(tag + "_rs_add4", data[1], o, self.idx, cols)]
                else:
                    self.items.remove(item)
                    self.done[tag] = o
        return (jobs, absorb) if jobs else None


def _resid_epilogue(accs, ex):
    return [ex[0] + ex[1] * accs[0], accs[0]]


def _swiglu_epilogue(accs, ex):
    gt, up = accs
    return [gt, up, gt / (1.0 + jnp.exp(-gt)) * up]


def _swiglu_bwd_epilogue(accs, ex):
    dact = accs[0]
    gt, up = ex[0].astype(F32), ex[1].astype(F32)
    sg = 1.0 / (1.0 + jnp.exp(-gt))
    return [dact * up * (sg * (1.0 + gt * (1.0 - sg))), dact * (gt * sg)]


def kernel(x, c, ada_w, ada_b, norm_mix_g, norm_ffn_g, a_w_in, a_b_in, a_ln_g, a_ln_b, a_w_s, a_b_s, a_w_out, b_w_in, b_b_f, b_w_out, ffn_w_gate, ffn_w_up, ffn_w_down, final_g, loss_target, m_ada_w, m_ada_b, m_norm_mix_g, m_norm_ffn_g, m_a_w_in, m_a_b_in, m_a_ln_g, m_a_ln_b, m_a_w_s, m_a_b_s, m_a_w_out, m_b_w_in, m_b_b_f, m_b_w_out, m_ffn_w_gate, m_ffn_w_up, m_ffn_w_down, m_final_g, v_ada_w, v_ada_b, v_norm_mix_g, v_norm_ffn_g, v_a_w_in, v_a_b_in, v_a_ln_g, v_a_ln_b, v_a_w_s, v_a_b_s, v_a_w_out, v_b_w_in, v_b_b_f, v_b_w_out, v_ffn_w_gate, v_ffn_w_up, v_ffn_w_down, v_final_g):
    S, D = x.shape[1], x.shape[2]
    H = D // LANE
    G = D // LANE
    FH = ffn_w_down.shape[1] * N_CHIPS
    depth = ada_w.shape[0]
    assert depth == 2 and a_w_in.shape[0] == 1 and b_w_in.shape[0] == 1
    mx, my, mc = _place()
    chip = 2 * mx + my
    dev = 4 * mx + 2 * my + mc
    x0 = x[0]
    target = loss_target[0]
    tr = _pick(S, 256, 8)
    tm = _pick(S, 512, 8)
    tq_f = _pick(S, 1024, LANE)
    tq = _pick(S, 512, LANE)
    nq = S // tq

    idx = jnp.stack([mc, chip]).astype(jnp.int32)
    t3 = lambda a: jnp.transpose(a, (2, 0, 1))
    shards = {"a_in": (a_w_in, 0), "a_out": (a_w_out, 0), "b_in": (t3(b_w_in), None), "b_out": (b_w_out, 0)}
    for l in range(depth):
        shards.update({f"wg{l}": (ffn_w_gate, l), f"wu{l}": (ffn_w_up, l), f"wd{l}": (ffn_w_down, l)})
    first = ["a_in", "a_out"]
    rest = [n for n in shards if n not in first]
    ag = _Gather(dict(zip(first, _cast_slabs("cast_first", [shards[n] for n in first], idx))), by_cols=["b_in"])
    ag.todo.update(zip(rest, _cast_slabs("cast_rest", [shards[n] for n in rest], idx, comm=ag.comm(first))))
    _comm_call("ag_first_pair", ag.comm(["wg0"]))
    wa_in4 = ag.done["a_in"]
    wa_out = ag.done["a_out"].reshape(D, D)

    c_all = _gather_devices("ag_c", jnp.pad(c, ((0, 7), (0, 0)))).reshape(N_DEV, 8, D)[:, 0, :]
    c_act = _silu_rows("c_silu", c_all)
    n_loc = ada_w.shape[2]
    mods = []
    for l in range(depth):
        b_loc = lax.dynamic_slice_in_dim(ada_b[l:l + 1], chip * n_loc, n_loc, axis=1)
        mods.append(_mm(f"ada_fwd{l}", "nn", [c_act], [ada_w], M=N_DEV, N=n_loc, K=D, tm=N_DEV, b_layer=l,
                        tn=_pick(n_loc, 1024, LANE), tk=D, extras=[("row", b_loc)],
                        epilogue=lambda accs, ex: [accs[0] + ex[0]])[0])
    mod_all = _gather_devices("ag_mod", jnp.concatenate(mods, axis=1))
    mod_mine = lax.dynamic_index_in_dim(mod_all[0::2], dev, axis=1, keepdims=False)
    mod_mine = mod_mine.reshape(N_CHIPS, depth, n_loc).transpose(1, 0, 2).reshape(depth, 1, N_MOD * D)
    mod = [[mod_mine[l, :, i * D:(i + 1) * D] for i in range(N_MOD)] for l in range(depth)]

    row = lambda a: a.reshape(1, -1)

    tril = jnp.tril(jnp.ones((LANE, LANE), dtype=bool))
    w_mask = jnp.where(tril[None], a_w_s[0], 0.0).astype(BF16)
    bias_full = jnp.repeat(a_b_s[0].T, LANE, axis=1)
    bf_pad = jnp.pad(b_b_f, ((0, 0), (0, LANE - H)))

    admit = {"sgu_in": ["wu0"], "mix_out0": ["wd0"], "ffn_in0": ["b_in", "b_out"], "ffn_out0": ["wg1"],
             "fox_qkv": ["wu1"], "attn_fwd": ["wd1"], "mix_out1": []}
    saved = []
    xs = x0
    for l in range(depth):
        sh1, sc1, g1, sh2, sc2, g2 = mod[l]
        st = {"x_in": xs}
        h1 = _normmod_fwd(f"normmod_mix{l}", xs, row(norm_mix_g[l]), sc1, sh1, tr)
        st["h1"] = h1
        if l == 0:
            a = _mm("sgu_in", "nn", [h1], [wa_in4], M=S, N=2 * D, K=D, tm=tm, tn=2 * D // N_CHIPS, tk=D,
                    b_stacked=True, extras=[("row", a_b_in)], comm=ag.comm(admit["sgu_in"]),
                    epilogue=lambda accs, ex: [accs[0] + ex[0]])[0]
            y = _sgu_mid_fwd(a, a_ln_g, a_ln_b, w_mask, bias_full, tr)
            st["a"], st["y"] = a, y
            w_o, mix_out = wa_out, y
        else:
            w_qkv = ag.done["b_in"].reshape(-1, D)
            w_f = jnp.pad(w_qkv[3 * D:], ((0, LANE - H), (0, 0)))
            qkv = _mm("fox_qkv", "nt", [h1], [w_qkv], M=S, N=3 * D, K=D, tm=tm, tn=_pick(3 * D, 1024, LANE),
                      tk=D, out_dtypes=(BF16,), comm=ag.comm(admit["fox_qkv"]))[0]
            fl = _mm("fox_f", "nt", [h1], [w_f], M=S, N=LANE, K=D, tm=tm, tn=LANE, tk=D)[0]
            F_sh = _fox_gate_fwd(fl, bf_pad)
            F_hs = F_sh[:, :H].T
            F_rows = F_hs.reshape(H, nq, 1, tq)
            o, lse = _attn_fwd(qkv, F_sh, F_hs.reshape(H, S // tq_f, 1, tq_f) * _LOG2E, H, tq_f,
                               comm=ag.comm(admit["attn_fwd"]))
            st.update(qkv=qkv, fl=fl, F_sh=F_sh, F_rows=F_rows, o=o, lse=lse, w_qkv=w_qkv, w_f=w_f)
            w_o, mix_out = ag.done["b_out"].reshape(D, D), o
        x1, out1 = _mm(f"mix_out{l}", "nn", [mix_out], [w_o], M=S, N=D, K=D, tm=tm, tn=_pick(D, 1024, LANE),
                       tk=D, extras=[("tile", xs), ("row", g1)], out_dtypes=(F32, BF16),
                       epilogue=_resid_epilogue, comm=ag.comm(admit[f"mix_out{l}"]))
        st["x_mid"], st["out1"] = x1, out1
        h2 = _normmod_fwd(f"normmod_ffn{l}", x1, row(norm_ffn_g[l]), sc2, sh2, tr)
        gt, up, act = _mm(f"ffn_in{l}", "nn", [h2, h2], [ag.done[f"wg{l}"], ag.done[f"wu{l}"]], M=S, N=FH, K=D,
                          tm=tm, tn=FH // N_CHIPS, tk=D, b_stacked=True, acc_of=[0, 1], n_acc=2,
                          out_dtypes=(BF16, BF16, BF16), epilogue=_swiglu_epilogue,
                          comm=ag.comm(admit.get(f"ffn_in{l}", ())))
        x2, out2 = _mm(f"ffn_out{l}", "nn", [act], [ag.done[f"wd{l}"].reshape(FH, D)], M=S, N=D, K=FH, tm=tm,
                       tn=_pick(D, 1024, LANE), tk=FH, extras=[("tile", x1), ("row", g2)],
                       out_dtypes=(F32, BF16), epilogue=_resid_epilogue,
                       comm=ag.comm(admit.get(f"ffn_out{l}", ())))
        st.update(h2=h2, gt=gt, up=up, act=act, out2=out2)
        saved.append(st)
        xs = x2
    assert not ag.todo and not ag.half
    wg4 = [ag.done[f"wg{l}"] for l in range(depth)]
    wu4 = [ag.done[f"wu{l}"] for l in range(depth)]
    wd = [ag.done[f"wd{l}"].reshape(FH, D) for l in range(depth)]
    wb_out = ag.done["b_out"].reshape(D, D)

    dx, loss_vec, g_final = _loss_head("loss_head", xs, target, row(final_g), tr)
    loss = lax.psum(loss_vec[0, 0], ("x", "y", "c"))

    rs = _ReduceScatter(idx)
    dmods = [None] * depth
    gmix = [None] * depth
    gffn = [None] * depth
    for l in reversed(range(depth)):
        sh1, sc1, g1, sh2, sc2, g2 = mod[l]
        st = saved[l]
        dog, dg2 = _gate_bwd(f"gate_ffn{l}", dx, st["out2"], g2, tr)
        dgt, dup = _mm(f"ffn_dact{l}", "nt", [dog], [wd[l]], M=S, N=FH, K=D, tm=tm, tn=FH // N_CHIPS, tk=D,
                       extras=[("tile", st["gt"]), ("tile", st["up"])], out_dtypes=(BF16, BF16),
                       epilogue=_swiglu_bwd_epilogue, comm=rs.comm())
        rs.push(f"wd{l}", _mm(f"ffn_dwd{l}", "tn", [st["act"]], [dog], M=FH, N=D, K=S, tm=FH // N_CHIPS,
                              tn=_pick(D, 512, LANE), tk=S, outer="i",
                              comm=rs.comm())[0].reshape(N_CHIPS, FH // N_CHIPS, D))
        dh2 = _mm(f"ffn_dh{l}", "nt", [dgt, dup], [wg4[l], wu4[l]], M=S, N=D, K=FH, tm=_pick(S, 1024, 8),
                  tn=_pick(D, 1024, LANE), tk=FH // N_CHIPS, b_stacked=True, comm=rs.comm())[0]
        rs.push(f"wg{l}", _mm(f"ffn_dwg{l}", "tn", [st["h2"]], [dgt], M=D, N=FH, K=S, tm=_pick(D, 512, LANE),
                              tn=FH // N_CHIPS, tk=S, out_stacked=True, comm=rs.comm())[0])
        rs.push(f"wu{l}", _mm(f"ffn_dwu{l}", "tn", [st["h2"]], [dup], M=D, N=FH, K=S, tm=_pick(D, 512, LANE),
                              tn=FH // N_CHIPS, tk=S, out_stacked=True, comm=rs.comm())[0])
        dx, dsh2, dsc2, gffn[l] = _normmod_bwd(f"normmod_ffn_bwd{l}", st["x_mid"], dh2, dx,
                                               row(norm_ffn_g[l]), sc2, tr, comm=rs.comm(ici=0))
        dog, dg1 = _gate_bwd(f"gate_mix{l}", dx, st["out1"], g1, tr)
        if l == 0:
            dy = _mm("sgu_dy", "nt", [dog], [wa_out], M=S, N=D, K=D, tm=tm, tn=_pick(D, 1024, LANE), tk=D,
                     out_dtypes=(BF16,))[0]
            rs.push("a_out", _mm("sgu_dwout", "tn", [st["y"]], [dog], M=D, N=D, K=S, tm=_pick(D, 512, LANE),
                                 tn=_pick(D, 1024, LANE),
                                 tk=S)[0].reshape(N_CHIPS, D // N_CHIPS, D))
            da, dws, dbias, g_ln_g, g_ln_b, g_b_in = _sgu_mid_bwd(st["a"], dy, a_ln_g, a_ln_b, w_mask, bias_full, tr,
                                                                  comm=rs.comm())
            rs.push("a_in", _mm("sgu_dwin", "tn", [st["h1"]], [da], M=D, N=2 * D, K=S, tm=_pick(D, 512, LANE),
                                tn=2 * D // N_CHIPS, tk=S, out_stacked=True, comm=rs.comm())[0])
            dh1 = _mm("sgu_dh", "nt", [da], [wa_in4], M=S, N=D, K=2 * D, tm=_pick(S, 1024, 8), tn=_pick(D, 1024, LANE),
                      tk=2 * D // N_CHIPS, b_stacked=True, comm=rs.comm())[0]
            g_w_s = jnp.where(tril[None], dws, 0.0)
            g_b_s = jnp.sum(dbias.reshape(LANE, G, LANE), axis=2).T
        else:
            do = _mm("fox_do", "nt", [dog], [wb_out], M=S, N=D, K=D, tm=tm, tn=_pick(D, 1024, LANE), tk=D,
                     out_dtypes=(BF16,))[0]
            rs.push("b_out", _mm("fox_dwout", "tn", [st["o"]], [dog], M=D, N=D, K=S, tm=_pick(D, 512, LANE),
                                 tn=_pick(D, 1024, LANE),
                                 tk=S)[0].reshape(N_CHIPS, D // N_CHIPS, D))
            delta = _head_dots(do, st["o"], H, tr)
            delta_rows = delta[:, :H].T.reshape(H, nq, 1, tq)
            A_rows = (st["F_rows"] - st["lse"].reshape(H, nq, 1, tq)) * _LOG2E
            w_qkv, w_f = st["w_qkv"], st["w_f"]
            dq, dk, dv, dF_k, dF_q = _attn_bwd(st["qkv"], do, st["F_sh"], A_rows, delta_rows, H, tq,
                                               comm=rs.comm(ici=2))
            dF_sh = jnp.pad((dF_k.reshape(H, S) + dF_q.reshape(H, S)).T, ((0, 0), (0, LANE - H)))
            dfl, db_f = _fox_gate_bwd(st["fl"], dF_sh, bf_pad)
            dfl_b = dfl.astype(BF16)
            dh_f = _mm("fox_dh_f", "nn", [dfl_b], [w_f], M=S, N=D, K=LANE, tm=tm, tn=_pick(D, 1024, LANE),
                       tk=LANE)[0]
            dh1 = _mm("fox_dh", "nn", [dq, dk, dv], [w_qkv, w_qkv, w_qkv], M=S, N=D, K=D, tm=_pick(S, 256, 8),
                      tn=_pick(D, 1024, LANE), tk=D, b_koffs=[0, 1, 2],
                      extras=[("tile", dh_f)], epilogue=lambda accs, ex: [accs[0] + ex[0]], comm=rs.comm())[0]
            tmw = _pick(D, 512, LANE)
            g_bT = None
            for p, (nm, d_) in enumerate((("q", dq), ("k", dk), ("v", dv))):
                g_bT = _mm(f"fox_dw{nm}", "tn", [d_], [st["h1"]], M=D, N=D, K=S, tm=tmw, tn=_pick(D, 1024, LANE),
                           tk=S, out_into=(3 * D + H, p * (D // tmw), g_bT), comm=rs.comm(ici=0))[0]
            dwf = _mm("fox_dwf", "tn", [dfl_b], [st["h1"]], M=LANE, N=D, K=S, tm=LANE, tn=_pick(D, 1024, LANE),
                      tk=S)[0]
            g_bT = lax.dynamic_update_slice(g_bT, dwf[:H], (3 * D, 0))
            rs.push("b_in", g_bT.reshape(N_CHIPS, -1, D), cols=True)
            g_b_f = db_f[:, :H]
        dx, dsh1, dsc1, gmix[l] = _normmod_bwd(f"normmod_mix_bwd{l}", st["x_in"], dh1, dx,
                                               row(norm_mix_g[l]), sc1, tr, comm=rs.comm(ici=1 - l))
        dmods[l] = jnp.concatenate([dsh1, dsc1, dg1, dsh2, dsc2, dg2], axis=1)
    grad_x = dx[None]

    small = [jnp.concatenate(dmods, axis=0), jnp.concatenate(gmix, axis=0), jnp.concatenate(gffn, axis=0),
             g_b_in, g_ln_g, g_ln_b, g_w_s[None], g_b_s[None], g_b_f, g_final[0]]
    small_w = [ada_b, norm_mix_g, norm_ffn_g, a_b_in, a_ln_g, a_ln_b, a_w_s, a_b_s, b_b_f, final_g]
    small_m = [m_ada_b, m_norm_mix_g, m_norm_ffn_g, m_a_b_in, m_a_ln_g, m_a_ln_b, m_a_w_s, m_a_b_s, m_b_b_f, m_final_g]
    small_v = [v_ada_b, v_norm_mix_g, v_norm_ffn_g, v_a_b_in, v_a_ln_g, v_a_ln_b, v_a_w_s, v_a_b_s, v_b_b_f, v_final_g]
    sizes = [w.size for w in small_w]
    total = sum(sizes)
    padded = -(-total // (8 * LANE)) * (8 * LANE)

    def pack(parts):
        flat = jnp.concatenate([p.reshape(-1) for p in parts])
        return jnp.pad(flat, (0, padded - total)).reshape(padded // LANE, LANE)

    def unpack(buf):
        flat = buf.reshape(-1)
        outs, off = [], 0
        for w, n in zip(small_w, sizes):
            outs.append(flat[off:off + n].reshape(w.shape))
            off += n
        return outs

    g_all = _gather_devices("ag_small_grads", pack(small))
    g_small = _sum_slots("sum_small_grads", g_all)
    _, d_small, m_small, v_small = _adamw("adamw_small", pack(small_w), g_small, pack(small_m), pack(small_v))
    sg, sd, sm, sv_ = unpack(g_small), unpack(d_small), unpack(m_small), unpack(v_small)

    n_dm = depth * N_MOD * D
    dmod_all = g_all.reshape(N_DEV, -1)[:, :n_dm].reshape(N_DEV, depth, N_MOD * D)
    u_ada = None
    for l in range(depth):
        dm_loc = lax.dynamic_slice_in_dim(dmod_all[:, l, :], chip * n_loc, n_loc, axis=1).astype(BF16)
        g_ada = _mm(f"ada_dw{l}", "tn", [c_act], [dm_loc], M=D, N=n_loc, K=N_DEV, tm=_pick(D, 512, LANE),
                    tn=_pick(n_loc, 1024, LANE), tk=N_DEV)[0]
        u_ada = _adamw(f"adamw_ada_w{l}", ada_w, g_ada, m_ada_w, v_ada_w, layer=l, prev=u_ada)

    flushes = 0
    while rs.items:
        _comm_call(f"rs_flush{flushes}", rs.comm())
        flushes += 1
    groups = {"a_in": (a_w_in, m_a_w_in, v_a_w_in), "a_out": (a_w_out, m_a_w_out, v_a_w_out),
              "b_out": (b_w_out, m_b_w_out, v_b_w_out),
              "wg": (ffn_w_gate, m_ffn_w_gate, v_ffn_w_gate), "wu": (ffn_w_up, m_ffn_w_up, v_ffn_w_up),
              "wd": (ffn_w_down, m_ffn_w_down, v_ffn_w_down)}
    upd = {}
    for n, (w_, m_, v_) in groups.items():
        for l in range(w_.shape[0]):
            tag = n if n in rs.done else f"{n}{l}"
            upd[n] = _adamw("adamw_" + tag, w_, rs.done[tag], m_, v_, layer=l, prev=upd.get(n))
    u_a_in, u_a_out, u_b_out, u_wg, u_wu, u_wd = (upd[n] for n in groups)
    u_b_in = [jnp.transpose(t, (1, 2, 0)) for t in
              _adamw("adamw_b_in", t3(b_w_in), rs.done["b_in"], t3(m_b_w_in), t3(v_b_w_in))]

    def leaves(i, small_list):
        s = small_list
        return [u_ada[i], s[0], s[1], s[2], u_a_in[i], s[3], s[4], s[5], s[6], s[7], u_a_out[i],
                u_b_in[i], s[8], u_b_out[i], u_wg[i], u_wu[i], u_wd[i], s[9]]

    return (loss, grad_x, *leaves(0, sg), *leaves(1, sd), *leaves(2, sm), *leaves(3, sv_))
```

```python
import functools
import math

import jax
import jax.numpy as jnp
from jax import lax
from jax.experimental import pallas as pl
from jax.experimental.pallas import tpu as pltpu

F32, BF16 = jnp.float32, jnp.bfloat16
LANE = 128
N_CHIPS = 4
N_DEV = 8
N_MOD = 6
EPS = 1e-6
VMEM_LIMIT = 60 * 1024 * 1024
ADAM_LR, ADAM_B1, ADAM_B2, ADAM_EPS, ADAM_WD, ADAM_STEP = 0.001, 0.9, 0.999, 1e-08, 0.01, 10
MESH = pl.DeviceIdType.MESH
ANY = pl.BlockSpec(memory_space=pl.ANY)
SDS = jax.ShapeDtypeStruct


def _pick(dim, pref, align):
    t = min(dim, pref)
    t -= t % align
    while t >= align:
        if dim % t == 0:
            return t
        t -= align
    return dim


def _params(sem=None):
    return pltpu.CompilerParams(dimension_semantics=sem, vmem_limit_bytes=VMEM_LIMIT)


class _Job:
    def __init__(self, ins, outs, aliases, nsem, start, wait):
        self.ins, self.outs, self.aliases, self.nsem, self.start, self.wait = ins, outs, aliases, nsem, start, wait


def _call(body, name, *, grid, in_specs, out_specs, out_shape, args, scratch=(), sem=None, comm=None,
          prefetch=None, keep=()):
    jobs, absorb = comm if comm else ([], None)
    n_pre = 0 if prefetch is None else 1
    n_in, n_out, n_scr = len(args), len(out_shape), len(scratch)
    c_ins = [a for a, _ in keep] + [a for j in jobs for a in j.ins]
    c_outs = [s for j in jobs for s in j.outs]
    aliases = {n_pre + n_in + k: o for k, (_, o) in enumerate(keep)}
    c_scr = []
    i_off, o_off = len(keep), 0
    for j in jobs:
        for a, b in j.aliases.items():
            aliases[n_pre + n_in + i_off + a] = n_out + o_off + b
        i_off += len(j.ins)
        o_off += len(j.outs)
        c_scr += [pltpu.SemaphoreType.DMA((j.nsem,)), pltpu.SemaphoreType.DMA((j.nsem,))]

    def wrapped(*refs):
        bounds = [n_pre + n_in, len(c_ins), n_out, len(c_outs), n_scr]
        parts, p = [], 0
        for n in bounds:
            parts.append(refs[p:p + n])
            p += n
        main_in, cin, main_out, cout, main_scr = parts
        cin = cin[len(keep):]
        csem = refs[p:]

        def run(phase):
            a = b = 0
            for k, j in enumerate(jobs):
                fn = j.start if phase == 0 else j.wait
                fn(cin[a:a + len(j.ins)], cout[b:b + len(j.outs)], csem[2 * k], csem[2 * k + 1])
                a += len(j.ins)
                b += len(j.outs)

        if jobs and grid:
            first = functools.reduce(jnp.logical_and, [pl.program_id(d) == 0 for d in range(len(grid))])
            last = functools.reduce(jnp.logical_and, [pl.program_id(d) == grid[d] - 1 for d in range(len(grid))])
            pl.when(first)(lambda: run(0))
            body(*main_in, *main_out, *main_scr)
            pl.when(last)(lambda: run(1))
        elif jobs:
            run(0)
            body(*main_in, *main_out, *main_scr)
            run(1)
        else:
            body(*main_in, *main_out, *main_scr)

    specs = dict(grid=grid, in_specs=list(in_specs) + [ANY] * len(c_ins),
                 out_specs=list(out_specs) + [ANY] * len(c_outs), scratch_shapes=list(scratch) + c_scr)
    if n_pre:
        specs = dict(grid_spec=pltpu.PrefetchScalarGridSpec(num_scalar_prefetch=1, **specs))
    outs = pl.pallas_call(
        wrapped, name=name, out_shape=list(out_shape) + c_outs, input_output_aliases=aliases,
        compiler_params=_params(("arbitrary",) * len(grid) if jobs else sem), **specs,
    )(*([prefetch] if n_pre else []), *args, *c_ins)
    if jobs:
        absorb(list(outs[n_out:]))
    return list(outs[:n_out])


def _comm_call(name, comm):
    _call(lambda: None, name, grid=(), in_specs=[], out_specs=[], out_shape=[], args=[], comm=comm)


def _mm(name, form, a_list, b_list, *, M, N, K, tm, tn, tk, b_stacked=False, out_stacked=False,
        b_koffs=None, acc_of=None, n_acc=1, extras=(), out_dtypes=(F32,), epilogue=None, comm=None,
        outer="j", b_layer=None, out_into=None):
    assert M % tm == 0 and N % tn == 0 and K % tk == 0, (name, M, N, K, tm, tn, tk)
    nm, nn, nk = M // tm, N // tn, K // tk
    npairs = len(a_list)
    acc_of = acc_of or [0] * npairs
    b_koffs = b_koffs or [0] * npairs
    if epilogue is None:
        epilogue = lambda accs, ex: [accs[0]]

    def spec(shape, fn):
        return pl.BlockSpec(shape, fn if outer == "j" else (lambda i, j, k: fn(j, i, k)))

    a_uniq = [a for p, a in enumerate(a_list) if all(a is not b for b in a_list[:p])]
    a_pos = [next(u for u, b in enumerate(a_uniq) if b is a) for a in a_list]
    in_specs = []
    for _ in a_uniq:
        if form == "tn":
            in_specs.append(spec((tk, tm), lambda j, i, k: (k, i)))
        else:
            in_specs.append(spec((tm, tk), lambda j, i, k: (i, k)))
    for off in b_koffs:
        if b_layer is not None:
            assert form == "nn" and not b_stacked
            in_specs.append(spec((None, tk, tn), lambda j, i, k: (b_layer, k, j)))
        elif form == "nn":
            if b_stacked:
                assert tn * N_CHIPS == N
                in_specs.append(spec((None, tk, tn), lambda j, i, k: (j, k, 0)))
            else:
                in_specs.append(spec((tk, tn), lambda j, i, k, off=off: (off + k, j)))
        elif form == "nt":
            if b_stacked:
                assert tk * N_CHIPS == K
                in_specs.append(spec((None, tn, tk), lambda j, i, k: (k, j, 0)))
            else:
                in_specs.append(spec((tn, tk), lambda j, i, k, off=off: (j, off + k)))
        else:
            in_specs.append(spec((tk, tn), lambda j, i, k: (k, j)))
    for kind, _ in extras:
        if kind == "tile":
            in_specs.append(spec((tm, tn), lambda j, i, k: (i, j)))
        else:
            in_specs.append(spec((1, tn), lambda j, i, k: (0, j)))
    if out_stacked:
        assert tn * N_CHIPS == N
        out_specs = [spec((None, tm, tn), lambda j, i, k: (j, i, 0)) for _ in out_dtypes]
        out_shape = [SDS((N_CHIPS, M, tn), d) for d in out_dtypes]
    else:
        rows, roff, prev = out_into if out_into else (M, 0, None)
        out_specs = [spec((tm, tn), lambda j, i, k: (roff + i, j)) for _ in out_dtypes]
        out_shape = [SDS((rows, N), d) for d in out_dtypes]
    dims = {"nn": (((1,), (0,)), ((), ())), "nt": (((1,), (1,)), ((), ())), "tn": (((0,), (0,)), ((), ()))}[form]
    n_a, n_ex, n_out = len(a_uniq), len(extras), len(out_dtypes)

    def body(*refs):
        a_vals = [refs[u][...].astype(BF16) for u in range(n_a)]
        b_refs = refs[n_a:n_a + npairs]
        e_refs = refs[n_a + npairs:n_a + npairs + n_ex]
        o_refs = refs[n_a + npairs + n_ex:n_a + npairs + n_ex + n_out]
        acc_refs = refs[n_a + npairs + n_ex + n_out:]

        tot = [None] * n_acc
        for p in range(npairs):
            d = lax.dot_general(a_vals[a_pos[p]], b_refs[p][...].astype(BF16), dims, preferred_element_type=F32)
            tot[acc_of[p]] = d if tot[acc_of[p]] is None else tot[acc_of[p]] + d

        def finish(accs):
            outs = epilogue(accs, [e[...] for e in e_refs])
            for o_ref, o in zip(o_refs, outs):
                o_ref[...] = o.astype(o_ref.dtype)

        if nk == 1:
            finish(tot)
        else:
            k = pl.program_id(2)

            @pl.when(k == 0)
            def _():
                for r, t in zip(acc_refs, tot):
                    r[...] = t

            @pl.when(k > 0)
            def _():
                for r, t in zip(acc_refs, tot):
                    r[...] += t

            @pl.when(k == nk - 1)
            def _():
                finish([r[...] for r in acc_refs])

    scratch = [pltpu.VMEM((tm, tn), F32) for _ in range(n_acc)] if nk > 1 else []
    return _call(body, name, grid=(nn, nm, nk) if outer == "j" else (nm, nn, nk), in_specs=in_specs,
                 out_specs=out_specs, out_shape=out_shape, scratch=scratch,
                 sem=("parallel", "parallel", "arbitrary"), comm=comm,
                 keep=[(out_into[2], 0)] if out_into and out_into[2] is not None else (),
                 args=[*a_uniq, *b_list, *[e for _, e in extras]])


def _rowwise(name, fn, rows, vecs, row_outs, acc_widths, tr, comm=None):
    S = rows[0].shape[0]
    assert S % tr == 0
    nr, nv, no, na = len(rows), len(vecs), len(row_outs), len(acc_widths)

    def body(*refs):
        r = [x[...] for x in refs[:nr]]
        v = [x[...] for x in refs[nr:nr + nv]]
        o_refs = refs[nr + nv:nr + nv + no]
        a_refs = refs[nr + nv + no:]
        outs, accs = fn(r, v)
        for o_ref, o in zip(o_refs, outs):
            o_ref[...] = o.astype(o_ref.dtype)
        first = pl.program_id(0) == 0

        @pl.when(first)
        def _():
            for a_ref, a in zip(a_refs, accs):
                a_ref[...] = a

        @pl.when(jnp.logical_not(first))
        def _():
            for a_ref, a in zip(a_refs, accs):
                a_ref[...] += a

    in_specs = [pl.BlockSpec((tr, x.shape[1]), lambda i: (i, 0)) for x in rows]
    in_specs += [pl.BlockSpec(x.shape, lambda i, nd=x.ndim: (0,) * nd) for x in vecs]
    out_specs = [pl.BlockSpec((tr, w), lambda i: (i, 0)) for w, _ in row_outs]
    out_specs += [pl.BlockSpec((1, w), lambda i: (0, 0)) for w in acc_widths]
    out_shape = [SDS((S, w), d) for w, d in row_outs] + [SDS((1, w), F32) for w in acc_widths]
    return _call(body, name, grid=(S // tr,), in_specs=in_specs, out_specs=out_specs, out_shape=out_shape,
                 sem=("arbitrary",), comm=comm, args=[*rows, *vecs])


def _colsum(a):
    return jnp.sum(a, axis=0, keepdims=True)


def _rms_stats(x):
    rstd = lax.rsqrt(jnp.mean(x * x, axis=1, keepdims=True) + EPS)
    return x * rstd, rstd


def _normmod_fwd(name, x, g, sc, sh, tr):
    def fn(r, v):
        n, _ = _rms_stats(r[0])
        return [(n * v[0]) * (1.0 + v[1]) + v[2]], []
    return _rowwise(name, fn, [x], [g, sc, sh], [(x.shape[1], BF16)], [], tr)[0]


def _gated(dx, nxt_out, gate):
    return gate * dx, _colsum(dx * nxt_out.astype(F32))


def _normmod_bwd(name, x, dh, dres, g, sc, tr, nxt=None, comm=None):
    def fn(r, v):
        x_, dh_, dres_ = r[:3]
        g_, sc_ = v[:2]
        n, rstd = _rms_stats(x_)
        hn = n * g_
        dhn = dh_ * (1.0 + sc_)
        dn = dhn * g_
        dx = dres_ + rstd * (dn - n * jnp.mean(dn * n, axis=1, keepdims=True))
        outs, accs = [dx], [_colsum(dh_), _colsum(dh_ * hn), _colsum(dhn * n)]
        if nxt:
            dog, dgate = _gated(dx, r[3], v[2])
            outs.append(dog)
            accs.append(dgate)
        return outs, accs
    D = x.shape[1]
    rows, vecs = [x, dh, dres] + ([nxt[0]] if nxt else []), [g, sc] + ([nxt[1]] if nxt else [])
    return _rowwise(name, fn, rows, vecs, [(D, F32)] + ([(D, BF16)] if nxt else []), [D] * (4 if nxt else 3), tr,
                    comm=comm)


def _loss_head(name, x, target, g, tr, nxt):
    D = x.shape[1]

    def fn(r, v):
        n, rstd = _rms_stats(r[0])
        err = n * v[0] - r[1]
        loss = 0.5 * jnp.sum(jnp.mean(err * err, axis=1, keepdims=True), axis=0, keepdims=True)
        dy = err * (1.0 / D)
        dn = dy * v[0]
        dx = rstd * (dn - n * jnp.mean(dn * n, axis=1, keepdims=True))
        dog, dgate = _gated(dx, r[2], v[1])
        return [dx, dog], [jnp.broadcast_to(loss, (1, LANE)), _colsum(dy * n), dgate]
    return _rowwise(name, fn, [x, target, nxt[0]], [g, nxt[1]], [(D, F32), (D, BF16)], [LANE, D, D], tr)


_INV_SQRT2 = 1.0 / math.sqrt(2.0)
_INV_SQRT2PI = 1.0 / math.sqrt(2.0 * math.pi)


def _gelu(a):
    return 0.5 * a * (1.0 + lax.erf(a * _INV_SQRT2))


def _gelu_grad(a):
    return 0.5 * (1.0 + lax.erf(a * _INV_SQRT2)) + a * jnp.exp(-0.5 * a * a) * _INV_SQRT2PI


def _ln_stats(v):
    mu = jnp.mean(v, axis=1, keepdims=True)
    vc = v - mu
    rstd = lax.rsqrt(jnp.mean(vc * vc, axis=1, keepdims=True) + EPS)
    return vc * rstd, rstd


def _chunk_mix(w_ref, vn, tr, G):
    rows = []
    for ch in range(tr // LANE):
        cols = []
        for g in range(G):
            blk = vn[ch * LANE:(ch + 1) * LANE, g * LANE:(g + 1) * LANE]
            cols.append(jnp.dot(w_ref[g], blk, preferred_element_type=F32))
        rows.append(jnp.concatenate(cols, axis=1))
    return jnp.concatenate(rows, axis=0)


def _sgu_mid_fwd(a, ln_g, ln_b, w_mask, bias_full, tr):
    S, D2 = a.shape
    D = D2 // 2
    G = D // LANE

    def body(a_ref, g_ref, b_ref, w_ref, bias_ref, y_ref):
        u = _gelu(a_ref[:, :D])
        v = _gelu(a_ref[:, D:])
        vhat, _ = _ln_stats(v)
        vn = (vhat * g_ref[...] + b_ref[...]).astype(BF16)
        sv = _chunk_mix(w_ref, vn, tr, G) + jnp.concatenate([bias_ref[...]] * (tr // LANE), axis=0)
        y_ref[...] = (u * sv).astype(BF16)

    return pl.pallas_call(
        body, name="sgu_mid_fwd", grid=(S // tr,),
        in_specs=[pl.BlockSpec((tr, D2), lambda i: (i, 0)), pl.BlockSpec((1, D), lambda i: (0, 0)),
                  pl.BlockSpec((1, D), lambda i: (0, 0)), pl.BlockSpec((G, LANE, LANE), lambda i: (0, 0, 0)),
                  pl.BlockSpec((LANE, D), lambda i: (0, 0))],
        out_specs=pl.BlockSpec((tr, D), lambda i: (i, 0)), out_shape=SDS((S, D), BF16),
        compiler_params=_params(("arbitrary",)))(a, ln_g, ln_b, w_mask, bias_full)


def _sgu_mid_bwd(a, dy, ln_g, ln_b, w_mask, bias_full, tr, comm=None):
    S, D2 = a.shape
    D = D2 // 2
    G = D // LANE
    nch = tr // LANE

    def body(a_ref, dy_ref, g_ref, b_ref, w_ref, bias_ref, da_ref, dw_ref, dbias_ref, dg_ref, db_ref, dbin_ref):
        au, av = a_ref[:, :D], a_ref[:, D:]
        u = _gelu(au)
        v = _gelu(av)
        vhat, rstd = _ln_stats(v)
        vn = (vhat * g_ref[...] + b_ref[...]).astype(BF16)
        sv = _chunk_mix(w_ref, vn, tr, G) + jnp.concatenate([bias_ref[...]] * nch, axis=0)
        dy_ = dy_ref[...].astype(F32)
        du = dy_ * sv
        dsv = dy_ * u
        dsv_b = dsv.astype(BF16)
        first = pl.program_id(0) == 0

        @pl.when(first)
        def _():
            dw_ref[...] = jnp.zeros_like(dw_ref)
            dbias_ref[...] = jnp.zeros_like(dbias_ref)
            dg_ref[...] = jnp.zeros_like(dg_ref)
            db_ref[...] = jnp.zeros_like(db_ref)
            dbin_ref[...] = jnp.zeros_like(dbin_ref)

        rows = []
        dbias = None
        for ch in range(nch):
            r0 = ch * LANE
            cols = []
            for g in range(G):
                c0 = g * LANE
                ds_blk = dsv_b[r0:r0 + LANE, c0:c0 + LANE]
                vn_blk = vn[r0:r0 + LANE, c0:c0 + LANE]
                cols.append(lax.dot_general(w_ref[g], ds_blk, (((0,), (0,)), ((), ())),
                                            preferred_element_type=F32))
                dw_ref[g] += lax.dot_general(ds_blk, vn_blk, (((1,), (1,)), ((), ())),
                                             preferred_element_type=F32)
            rows.append(jnp.concatenate(cols, axis=1))
            blk = dsv[r0:r0 + LANE, :]
            dbias = blk if dbias is None else dbias + blk
        dvn = jnp.concatenate(rows, axis=0)
        dbias_ref[...] += dbias
        dg_ref[...] += _colsum(dvn * vhat)
        db_ref[...] += _colsum(dvn)
        dvh = dvn * g_ref[...]
        dv = rstd * (dvh - jnp.mean(dvh, axis=1, keepdims=True)
                     - vhat * jnp.mean(dvh * vhat, axis=1, keepdims=True))
        da_u = du * _gelu_grad(au)
        da_v = dv * _gelu_grad(av)
        da_ref[:, :D] = da_u.astype(BF16)
        da_ref[:, D:] = da_v.astype(BF16)
        dbin_ref[:, :D] += _colsum(da_u)
        dbin_ref[:, D:] += _colsum(da_v)

    full = lambda shp: pl.BlockSpec(shp, lambda i, nd=len(shp): (0,) * nd)
    return _call(
        body, "sgu_mid_bwd", grid=(S // tr,),
        in_specs=[pl.BlockSpec((tr, D2), lambda i: (i, 0)), pl.BlockSpec((tr, D), lambda i: (i, 0)),
                  full((1, D)), full((1, D)), full((G, LANE, LANE)), full((LANE, D))],
        out_specs=[pl.BlockSpec((tr, D2), lambda i: (i, 0)), full((G, LANE, LANE)), full((LANE, D)),
                   full((1, D)), full((1, D)), full((1, D2))],
        out_shape=[SDS((S, D2), BF16), SDS((G, LANE, LANE), F32), SDS((LANE, D), F32),
                   SDS((1, D), F32), SDS((1, D), F32), SDS((1, D2), F32)],
        sem=("arbitrary",), comm=comm, args=[a, dy, ln_g, ln_b, w_mask, bias_full])


def _tri(lower):
    r = lax.broadcasted_iota(jnp.int32, (LANE, LANE), 0)
    c = lax.broadcasted_iota(jnp.int32, (LANE, LANE), 1)
    return jnp.where((c <= r) if lower else (c >= r), 1.0, 0.0).astype(F32)


def _fox_gate_fwd(fl, bf_pad):
    S = fl.shape[0]
    nblk = S // LANE

    def body(fl_ref, b_ref, f_ref):
        tri = _tri(True)

        def step(i, carry):
            r0 = pl.multiple_of(i * LANE, LANE)
            z = fl_ref[pl.ds(r0, LANE), :] + b_ref[...]
            logf = jnp.minimum(z, 0.0) - jnp.log1p(jnp.exp(-jnp.abs(z)))
            f = jnp.dot(tri, logf, preferred_element_type=F32, precision=lax.Precision.HIGHEST) + carry
            f_ref[pl.ds(r0, LANE), :] = f
            return f[LANE - 1:LANE, :]
        lax.fori_loop(0, nblk, step, jnp.zeros((1, LANE), F32))

    return pl.pallas_call(body, name="fox_gate_fwd", out_shape=SDS((S, LANE), F32),
                          compiler_params=_params())(fl, bf_pad)


def _fox_gate_bwd(fl, dF, bf_pad):
    S = fl.shape[0]
    nblk = S // LANE

    def body(fl_ref, df_ref, b_ref, dfl_ref, db_ref):
        tri = _tri(True)

        def prefix(i, carry):
            r0 = pl.multiple_of(i * LANE, LANE)
            p = jnp.dot(tri, df_ref[pl.ds(r0, LANE), :], preferred_element_type=F32,
                        precision=lax.Precision.HIGHEST) + carry
            dfl_ref[pl.ds(r0, LANE), :] = p
            return p[LANE - 1:LANE, :]
        total = lax.fori_loop(0, nblk, prefix, jnp.zeros((1, LANE), F32))
        db_ref[...] = jnp.zeros_like(db_ref)

        def finish(i, carry):
            r0 = pl.multiple_of(i * LANE, LANE)
            dlogf = total - dfl_ref[pl.ds(r0, LANE), :] + df_ref[pl.ds(r0, LANE), :]
            z = fl_ref[pl.ds(r0, LANE), :] + b_ref[...]
            dfl = dlogf / (1.0 + jnp.exp(z))
            dfl_ref[pl.ds(r0, LANE), :] = dfl
            db_ref[...] += _colsum(dfl)
            return carry
        lax.fori_loop(0, nblk, finish, 0)

    return pl.pallas_call(body, name="fox_gate_bwd", out_shape=[SDS((S, LANE), F32), SDS((1, LANE), F32)],
                          compiler_params=_params())(fl, dF, bf_pad)


def _lane_pick(blk, h):
    lane = lax.broadcasted_iota(jnp.int32, blk.shape, 1)
    return jnp.sum(jnp.where(lane == h, blk, 0.0), axis=1, keepdims=True)


_NEG = -1e30
_LOG2E = 1.0 / math.log(2.0)
_LN2 = math.log(2.0)
_SUB = 32


def _attn_fwd(qkv, F_sh, F_rows, H, tq, comm=None):
    S = qkv.shape[0]
    D = H * LANE
    nq = S // tq
    scale = 1.0 / math.sqrt(LANE)

    def body(q_ref, k_ref, v_ref, fsh_ref, frow_ref, o_ref, lse_ref):
        h, i = pl.program_id(0), pl.program_id(1)
        q = q_ref[...]
        fq = _lane_pick(fsh_ref[...], h) * _LOG2E

        def block(j, carry, diagonal):
            m, l, acc = carry
            r0 = pl.multiple_of(j * tq, tq)
            k = k_ref[pl.ds(r0, tq), :]
            v = v_ref[pl.ds(r0, tq), :]
            s = lax.dot_general(q, k, (((1,), (1,)), ((), ())), preferred_element_type=F32) * (scale * _LOG2E)
            s = s + (fq - frow_ref[j])
            if diagonal:
                row = lax.broadcasted_iota(jnp.int32, (tq, tq), 0)
                col = lax.broadcasted_iota(jnp.int32, (tq, tq), 1)
                s = jnp.where(col <= row, s, _NEG)
            m_new = jnp.maximum(m, jnp.max(s, axis=1, keepdims=True))
            alpha = jnp.exp2(m - m_new)
            p = jnp.exp2(s - m_new)
            l = alpha * l + jnp.sum(p, axis=1, keepdims=True)
            acc = alpha * acc + jnp.dot(p.astype(BF16), v, preferred_element_type=F32)
            return m_new, l, acc

        init = (jnp.full((tq, 1), _NEG, F32), jnp.zeros((tq, 1), F32), jnp.zeros((tq, LANE), F32))
        carry = lax.fori_loop(0, i, lambda j, cr: block(j, cr, False), init)
        m, l, acc = block(i, carry, True)
        o_ref[...] = (acc / l).astype(BF16)
        lse_ref[...] = (m + jnp.log2(l)) * _LN2

    return _call(
        body, "attn_fwd", grid=(H, nq),
        in_specs=[pl.BlockSpec((tq, LANE), lambda h, i: (i, h)),
                  pl.BlockSpec((S, LANE), lambda h, i: (0, H + h)),
                  pl.BlockSpec((S, LANE), lambda h, i: (0, 2 * H + h)),
                  pl.BlockSpec((tq, LANE), lambda h, i: (i, 0)),
                  pl.BlockSpec((None, nq, 1, tq), lambda h, i: (h, 0, 0, 0))],
        out_specs=[pl.BlockSpec((tq, LANE), lambda h, i: (i, h)),
                   pl.BlockSpec((None, tq, 1), lambda h, i: (h, i, 0))],
        out_shape=[SDS((S, D), BF16), SDS((H, S, 1), F32)],
        sem=("parallel", "arbitrary"), comm=comm, args=[qkv, qkv, qkv, F_sh, F_rows])


def _attn_bwd(qkv, do, F_sh, A_rows, delta_rows, H, tq, comm=None):
    S = qkv.shape[0]
    D = H * LANE
    nq = S // tq
    scale = 1.0 / math.sqrt(LANE)

    def body(q_ref, do_ref, k_ref, v_ref, fsh_ref, a_ref, dl_ref, dq_ref, dk_ref, dv_ref, df_ref, dfr_ref,
             dq_acc, dfr_acc, st_scr, dp_scr, pt_scr, ds_scr, fk_scr, df_scr, dk_scr, dv_scr):
        h, j = pl.program_id(0), pl.program_id(1)

        @pl.when(j == 0)
        def _():
            dq_acc[...] = jnp.zeros_like(dq_acc)
            dfr_acc[...] = jnp.zeros_like(dfr_acc)

        k = k_ref[...]
        v = v_ref[...]
        fk_scr[...] = _lane_pick(fsh_ref[...], h) * _LOG2E
        df_scr[...] = jnp.zeros((tq, 1), F32)
        dk_scr[...] = jnp.zeros((tq, LANE), F32)
        dv_scr[...] = jnp.zeros((tq, LANE), F32)

        def block(i, diagonal):
            r0 = pl.multiple_of(i * tq, tq)
            q = q_ref[pl.ds(r0, tq), :]
            do_ = do_ref[pl.ds(r0, tq), :]
            st_scr[...] = lax.dot_general(k, q, (((1,), (1,)), ((), ())), preferred_element_type=F32)
            dp_scr[...] = lax.dot_general(v, do_, (((1,), (1,)), ((), ())), preferred_element_type=F32)
            a_row, dl_row = a_ref[i], dl_ref[i]
            dfr = jnp.zeros((1, tq), F32)
            for r in range(tq // _SUB):
                rows = slice(r * _SUB, (r + 1) * _SUB)
                arg = st_scr[rows, :] * (scale * _LOG2E) + (a_row - fk_scr[rows, :])
                if diagonal:
                    row = lax.broadcasted_iota(jnp.int32, (_SUB, tq), 0) + r * _SUB
                    col = lax.broadcasted_iota(jnp.int32, (_SUB, tq), 1)
                    arg = jnp.where(row <= col, arg, _NEG)
                pt = jnp.exp2(arg)
                dst = pt * (dp_scr[rows, :] - dl_row)
                df_scr[rows, :] += jnp.sum(dst, axis=1, keepdims=True)
                dfr = dfr + jnp.sum(dst, axis=0, keepdims=True)
                pt_scr[rows, :] = pt.astype(BF16)
                ds_scr[rows, :] = dst.astype(BF16)
            dfr_acc[i] += dfr
            dsb = ds_scr[...]
            dv_scr[...] += jnp.dot(pt_scr[...], do_, preferred_element_type=F32)
            dk_scr[...] += jnp.dot(dsb, q, preferred_element_type=F32)
            dq_acc[pl.ds(r0, tq), :] += lax.dot_general(dsb, k, (((0,), (0,)), ((), ())),
                                                        preferred_element_type=F32)

        def full_block(i, carry):
            block(i, False)
            return carry

        block(j, True)
        lax.fori_loop(j + 1, nq, full_block, 0)
        dk_ref[...] = (dk_scr[...] * scale).astype(BF16)
        dv_ref[...] = dv_scr[...].astype(BF16)
        df_ref[...] = -df_scr[...]

        @pl.when(j == nq - 1)
        def _():
            dq_ref[...] = (dq_acc[...] * scale).astype(BF16)
            dfr_ref[...] = dfr_acc[...]

    return _call(
        body, "attn_bwd", grid=(H, nq),
        in_specs=[pl.BlockSpec((S, LANE), lambda h, j: (0, h)),
                  pl.BlockSpec((S, LANE), lambda h, j: (0, h)),
                  pl.BlockSpec((tq, LANE), lambda h, j: (j, H + h)),
                  pl.BlockSpec((tq, LANE), lambda h, j: (j, 2 * H + h)),
                  pl.BlockSpec((tq, LANE), lambda h, j: (j, 0)),
                  pl.BlockSpec((None, nq, 1, tq), lambda h, j: (h, 0, 0, 0)),
                  pl.BlockSpec((None, nq, 1, tq), lambda h, j: (h, 0, 0, 0))],
        out_specs=[pl.BlockSpec((S, LANE), lambda h, j: (0, h)),
                   pl.BlockSpec((tq, LANE), lambda h, j: (j, h)),
                   pl.BlockSpec((tq, LANE), lambda h, j: (j, h)),
                   pl.BlockSpec((None, tq, 1), lambda h, j: (h, j, 0)),
                   pl.BlockSpec((None, nq, 1, tq), lambda h, j: (h, 0, 0, 0))],
        out_shape=[SDS((S, D), BF16), SDS((S, D), BF16), SDS((S, D), BF16), SDS((H, S, 1), F32),
                   SDS((H, nq, 1, tq), F32)],
        scratch=[pltpu.VMEM((S, LANE), F32), pltpu.VMEM((nq, 1, tq), F32),
                 pltpu.VMEM((tq, tq), F32), pltpu.VMEM((tq, tq), F32), pltpu.VMEM((tq, tq), BF16),
                 pltpu.VMEM((tq, tq), BF16), pltpu.VMEM((tq, 1), F32), pltpu.VMEM((tq, 1), F32),
                 pltpu.VMEM((tq, LANE), F32), pltpu.VMEM((tq, LANE), F32)],
        sem=("parallel", "arbitrary"), comm=comm, args=[qkv, do, qkv, qkv, F_sh, A_rows, delta_rows])


def _head_dots(do, o, H, tr):
    def fn(r, v):
        prod = r[0].astype(F32) * r[1].astype(F32)
        lane = lax.broadcasted_iota(jnp.int32, (prod.shape[0], LANE), 1)
        out = jnp.zeros((prod.shape[0], LANE), F32)
        for h in range(H):
            s = jnp.sum(prod[:, h * LANE:(h + 1) * LANE], axis=1, keepdims=True)
            out = jnp.where(lane == h, s, out)
        return [out], []
    return _rowwise("attn_delta", fn, [do, o], [], [(LANE, F32)], [], tr)[0]


def _adamw(name, w, g, m, v, layer=None, prev=None):
    R, C = g.shape
    tr = _pick(R, max(8, (512 * 1024) // max(C, 1) // 8 * 8), 8)
    c1 = 1.0 - ADAM_B1 ** ADAM_STEP
    c2 = 1.0 - ADAM_B2 ** ADAM_STEP

    def body(w_ref, g_ref, m_ref, v_ref, go_ref, d_ref, mo_ref, vo_ref):
        g_ = g_ref[...]
        m_ = ADAM_B1 * m_ref[...] + (1.0 - ADAM_B1) * g_
        v_ = ADAM_B2 * v_ref[...] + (1.0 - ADAM_B2) * (g_ * g_)
        go_ref[...] = g_
        d_ref[...] = -ADAM_LR * ((m_ / c1) / (jnp.sqrt(v_ / c2) + ADAM_EPS) + ADAM_WD * w_ref[...])
        mo_ref[...] = m_
        vo_ref[...] = v_

    if w.ndim == 3 and w.shape[1] == 1 and layer is None:
        tc = _pick(C, 2 * LANE, LANE)
        gspec = pl.BlockSpec((R, tc), lambda i: (0, i))
        pspec = pl.BlockSpec((R, None, tc), lambda i: (0, 0, i))
        grid = (C // tc,)
    else:
        gspec = pl.BlockSpec((tr, C), lambda i: (i, 0))
        pspec = gspec if layer is None else pl.BlockSpec((None, tr, C), lambda i: (layer, i, 0))
        grid = (R // tr,)
    return _call(body, name, grid=grid, in_specs=[pspec, gspec, pspec, pspec], out_specs=[pspec] * 4,
                 out_shape=[SDS(w.shape, F32)] * 4, sem=("parallel",), args=[w, g, m, v],
                 keep=[(p, k) for k, p in enumerate(prev)] if prev else ())


def _sum_slots(name, a):
    n, R, C = a.shape
    tr = _pick(R, 256, 8)

    def body(a_ref, o_ref):
        acc = a_ref[0]
        for k in range(1, n):
            acc = acc + a_ref[k]
        o_ref[...] = acc

    return pl.pallas_call(body, name=name, grid=(R // tr,),
                          in_specs=[pl.BlockSpec((n, tr, C), lambda i: (0, i, 0))],
                          out_specs=pl.BlockSpec((tr, C), lambda i: (i, 0)), out_shape=SDS((R, C), F32),
                          compiler_params=_params(("parallel",)))(a)


def _silu_rows(name, a):
    def body(a_ref, o_ref):
        z = a_ref[...]
        o_ref[...] = (z / (1.0 + jnp.exp(-z))).astype(BF16)
    return pl.pallas_call(body, name=name, out_shape=SDS(a.shape, BF16), compiler_params=_params())(a)


def _place():
    return lax.axis_index("x"), lax.axis_index("y"), lax.axis_index("c")


def _rcopy(src, dst, ssem, rsem, dev):
    return pltpu.make_async_remote_copy(src_ref=src, dst_ref=dst, send_sem=ssem, recv_sem=rsem,
                                        device_id=dev, device_id_type=MESH)


def _other_chips(x, y):
    return [(1 - x, y), (x, 1 - y), (1 - x, 1 - y)]


def _job_gather_devices(buf8):
    def views(o):
        x, y, c = _place()
        slot = lambda px, py, pc: o.at[4 * px + 2 * py + pc]
        return c, (x, y, 1 - c), _other_chips(x, y), slot, slot(x, y, c)

    def first(o, ss, rs):
        c, sib, chips, slot, me = views(o)
        return [_rcopy(me, me, ss.at[0], rs.at[0], sib)] + [
            _rcopy(me, me, ss.at[1 + j], rs.at[1 + j], (cx, cy, c)) for j, (cx, cy) in enumerate(chips)]

    def start(i, o, ss, rs):
        for cp in first(o[0], ss, rs):
            cp.start()

    def wait(i, o, ss, rs):
        c, sib, chips, slot, me = views(o[0])
        x, y = sib[0], sib[1]
        passed = []
        for j, (cx, cy) in enumerate(chips):
            got = slot(cx, cy, c)
            _rcopy(me, got, ss.at[1 + j], rs.at[1 + j], (cx, cy, c)).wait_recv()
            passed.append(_rcopy(got, got, ss.at[4 + j], rs.at[4 + j], sib))
            passed[j].start()
        _rcopy(me, slot(x, y, 1 - c), ss.at[0], rs.at[0], sib).wait_recv()
        for j, (cx, cy) in enumerate(chips):
            _rcopy(me, slot(cx, cy, 1 - c), ss.at[4 + j], rs.at[4 + j], sib).wait_recv()
        for cp in first(o[0], ss, rs) + passed:
            cp.wait_send()

    return _Job([buf8], [SDS(buf8.shape, buf8.dtype)], {0: 0}, N_DEV - 1, start, wait)


def _gather_devices(name, buf, slot_idx, extra=None):
    R, C = buf.shape

    def place(s_ref, b_ref, o_ref):
        o_ref[...] = b_ref[...]

    buf8 = _call(place, name + "_place", grid=(1,), prefetch=slot_idx, args=[buf],
                 in_specs=[pl.BlockSpec((R, C), lambda i, s: (0, 0))],
                 out_specs=[pl.BlockSpec((None, R, C), lambda i, s: (s[0], 0, 0))],
                 out_shape=[SDS((N_DEV, R, C), buf.dtype)])[0]
    jobs, absorb_extra = extra if extra else ([], None)
    got = []

    def absorb(outs):
        got.append(outs[0])
        if absorb_extra:
            absorb_extra(outs[1:])

    _comm_call(name, ([_job_gather_devices(buf8)] + jobs, absorb))
    return got[0]


def _cast_slabs(name, ws, idx, comm=None):
    n = len(ws)
    steps = next(s for s in (8, 4, 2, 1)
                 if all((w.shape[2] % (LANE * s) if l is None else w.shape[1] % (16 * s)) == 0 for w, l in ws))

    def body(s_ref, *refs):
        for w_ref, o_ref in zip(refs[:n], refs[n:]):
            o_ref[...] = w_ref[...].astype(BF16)

    in_specs, out_specs, out_shape = [], [], []
    for w, l in ws:
        if l is None:
            R, _, C = w.shape
            in_specs.append(pl.BlockSpec((R, None, C // steps), lambda i, s: (0, 0, i)))
            out_specs.append(pl.BlockSpec((None, R, C // steps), lambda i, s: (s[1], 0, i)))
        else:
            _, R, C = w.shape
            in_specs.append(pl.BlockSpec((None, R // steps, C), lambda i, s, l=l: (l, i, 0)))
            out_specs.append(pl.BlockSpec((None, R // steps, C), lambda i, s: (s[1], i, 0)))
        out_shape.append(SDS((N_CHIPS, R, C), BF16))
    return _call(body, name, grid=(steps,), prefetch=idx, comm=comm, sem=("parallel",), args=[w for w, _ in ws],
                 in_specs=in_specs, out_specs=out_specs, out_shape=out_shape)


def _half(ref, lead, hf, cols):
    n = ref.shape[-1 if cols else -2] // 2
    assert 2 * n == ref.shape[-1 if cols else -2]
    cut = pl.ds(hf * n, n)
    return ref.at[(*lead, slice(None), cut) if cols else (*lead, cut, slice(None))]


def _job_gather_ici(buf, cols=False):
    def views(o):
        x, y, c = _place()
        return c, _other_chips(x, y), _half(o, (2 * x + y,), c, cols)

    def start(i, o, ss, rs):
        c, chips, mine = views(o[0])
        for j, (cx, cy) in enumerate(chips):
            _rcopy(mine, mine, ss.at[j], rs.at[j], (cx, cy, c)).start()

    def wait(i, o, ss, rs):
        c, chips, mine = views(o[0])
        for j, (cx, cy) in enumerate(chips):
            cp = _rcopy(mine, _half(o[0], (2 * cx + cy,), c, cols), ss.at[j], rs.at[j], (cx, cy, c))
            cp.wait_send()
            cp.wait_recv()

    return _Job([buf], [SDS(buf.shape, buf.dtype)], {0: 0}, 3, start, wait)


def _job_gather_pair(buf, cols=False):
    def views(o):
        x, y, c = _place()
        return c, (x, y, 1 - c), _other_chips(x, y)

    def start(i, o, ss, rs):
        c, sib, chips = views(o[0])
        for j, (cx, cy) in enumerate(chips):
            got = _half(o[0], (2 * cx + cy,), c, cols)
            _rcopy(got, got, ss.at[j], rs.at[j], sib).start()

    def wait(i, o, ss, rs):
        c, sib, chips = views(o[0])
        for j, (cx, cy) in enumerate(chips):
            cp = _rcopy(_half(o[0], (2 * cx + cy,), c, cols), _half(o[0], (2 * cx + cy,), 1 - c, cols),
                        ss.at[j], rs.at[j], sib)
            cp.wait_send()
            cp.wait_recv()

    return _Job([buf], [SDS(buf.shape, buf.dtype)], {0: 0}, 3, start, wait)


class _Gather:
    def __init__(self, bufs, by_cols=()):
        self.todo, self.half, self.done, self.by_cols = dict(bufs), {}, {}, set(by_cols)

    def comm(self, admit=()):
        second, first = list(self.half), list(admit)
        jobs = [_job_gather_pair(self.half[n], n in self.by_cols) for n in second]
        jobs += [_job_gather_ici(self.todo[n], n in self.by_cols) for n in first]

        def absorb(outs):
            for n, o in zip(second + first, outs):
                if n in self.half:
                    del self.half[n]
                    self.done[n] = o
                else:
                    del self.todo[n]
                    self.half[n] = o
        return (jobs, absorb) if jobs else None


def _half_shape(R, C, cols):
    return (R, C // 2) if cols else (R // 2, C)


def _job_rs_pair(g4, cols=False):
    _, R, C = g4.shape

    def desc(i, o, ss, rs):
        x, y, c = _place()
        return _rcopy(_half(i[0], (slice(None),), 1 - c, cols), o[0], ss.at[0], rs.at[0], (x, y, 1 - c))

    return _Job([g4], [SDS((N_CHIPS,) + _half_shape(R, C, cols), F32)], {}, 1,
                lambda i, o, ss, rs: desc(i, o, ss, rs).start(), lambda i, o, ss, rs: desc(i, o, ss, rs).wait())


def _job_rs_chips(p4):
    _, hR, C = p4.shape

    def descs(i, o, ss, rs):
        x, y, c = _place()
        return [_rcopy(i[0].at[2 * cx + cy], o[0].at[j], ss.at[j], rs.at[j], (cx, cy, c))
                for j, (cx, cy) in enumerate(_other_chips(x, y))]

    def start(i, o, ss, rs):
        for cp in descs(i, o, ss, rs):
            cp.start()

    def wait(i, o, ss, rs):
        for cp in descs(i, o, ss, rs):
            cp.wait_send()
            cp.wait_recv()

    return _Job([p4], [SDS((3, hR, C), p4.dtype)], {}, 3, start, wait)


def _job_rs_join(buf, cols=False):
    def desc(o, ss, rs, recv):
        x, y, c = _place()
        mine = _half(o[0], (), c, cols)
        return _rcopy(mine, _half(o[0], (), 1 - c, cols) if recv else mine, ss.at[0], rs.at[0], (x, y, 1 - c))

    def wait(i, o, ss, rs):
        cp = desc(o, ss, rs, True)
        cp.wait_send()
        cp.wait_recv()

    return _Job([buf], [SDS(buf.shape, buf.dtype)], {0: 0}, 1,
                lambda i, o, ss, rs: desc(o, ss, rs, False).start(), wait)


def _walk(R, C, cols):
    if cols:
        tc = _pick(C, 2 * LANE, LANE)
        return (R, tc), C // tc
    tr = _pick(R, max(16, (1 << 20) // C // 16 * 16), 16)
    return (tr, C), R // tr


def _rs_add_pair(name, g4, recv, idx, cols=False):
    _, R, C = g4.shape
    hshape = _half_shape(R, C, cols)
    blk, nb = _walk(*hshape, cols)
    at = (lambda k, i: (k, 0, i)) if cols else (lambda k, i: (k, i, 0))

    def body(s_ref, a_ref, b_ref, pb_ref, po_ref):
        s = a_ref[...] + b_ref[...]
        pb_ref[...] = s.astype(BF16)

        @pl.when(pl.program_id(1) == s_ref[1])
        def _():
            po_ref[...] = s

    gs = pltpu.PrefetchScalarGridSpec(
        num_scalar_prefetch=1, grid=(nb, N_CHIPS),
        in_specs=[pl.BlockSpec((None,) + blk, lambda i, k, s: at(k, s[0] * nb + i)),
                  pl.BlockSpec((None,) + blk, lambda i, k, s: at(k, i))],
        out_specs=[pl.BlockSpec((None,) + blk, lambda i, k, s: at(k, i)),
                   pl.BlockSpec(blk, lambda i, k, s: at(k, i)[1:])])
    return pl.pallas_call(body, name=name, grid_spec=gs,
                          out_shape=[SDS((N_CHIPS,) + hshape, BF16), SDS(hshape, F32)],
                          compiler_params=_params(("parallel", "arbitrary")))(idx, g4, recv)


def _rs_add_chips(name, own, recv3, idx, cols=False):
    hR, hC = own.shape
    blk, nb = _walk(hR, hC, cols)
    at = (lambda i: (0, i)) if cols else (lambda i: (i, 0))

    def body(s_ref, a_ref, b_ref, o_ref):
        o_ref[...] = ((a_ref[...] + b_ref[0].astype(F32)) + b_ref[1].astype(F32)) + b_ref[2].astype(F32)

    gs = pltpu.PrefetchScalarGridSpec(
        num_scalar_prefetch=1, grid=(nb,),
        in_specs=[pl.BlockSpec(blk, lambda i, s: at(i)), pl.BlockSpec((3,) + blk, lambda i, s: (0,) + at(i))],
        out_specs=pl.BlockSpec(blk, lambda i, s: at(s[0] * nb + i)))
    return pl.pallas_call(body, name=name, grid_spec=gs,
                          out_shape=SDS((hR, 2 * hC) if cols else (2 * hR, hC), F32),
                          compiler_params=_params(("parallel",)))(idx, own, recv3)


class _ReduceScatter:
    def __init__(self, idx):
        self.idx, self.items, self.done = idx, [], {}

    def push(self, tag, g4, cols=False):
        self.items.append([tag, 0, g4, cols])

    def comm(self, ici=1):
        cur, jobs = [], []
        for item in self.items:
            tag, stage, data, cols = item
            if stage == 1:
                if ici == 0:
                    continue
                ici -= 1
            cur.append(item)
            jobs.append(_job_rs_pair(data, cols) if stage == 0 else _job_rs_chips(data[0]) if stage == 1
                        else _job_rs_join(data, cols))

        def absorb(outs):
            for item, o in zip(cur, outs):
                tag, stage, data, cols = item
                if stage == 0:
                    item[1:3] = [1, _rs_add_pair(tag + "_rs_add2", data, o, self.idx, cols)]
                elif stage == 1:
                    item[1:3] = [2, _rs_add_chips(tag + "_rs_add4", data[1], o, self.idx, cols)]
                else:
                    self.items.remove(item)
                    self.done[tag] = o
        return (jobs, absorb) if jobs else None


def _resid_epilogue(accs, ex):
    return [ex[0] + ex[1] * accs[0], accs[0]]


def _swiglu_epilogue(accs, ex):
    gt, up = accs
    return [gt, up, gt / (1.0 + jnp.exp(-gt)) * up]


def _swiglu_bwd_epilogue(accs, ex):
    dact = accs[0]
    gt, up = ex[0].astype(F32), ex[1].astype(F32)
    sg = 1.0 / (1.0 + jnp.exp(-gt))
    return [dact * up * (sg * (1.0 + gt * (1.0 - sg))), dact * (gt * sg)]


def kernel(x, c, ada_w, ada_b, norm_mix_g, norm_ffn_g, a_w_in, a_b_in, a_ln_g, a_ln_b, a_w_s, a_b_s, a_w_out, b_w_in, b_b_f, b_w_out, ffn_w_gate, ffn_w_up, ffn_w_down, final_g, loss_target, m_ada_w, m_ada_b, m_norm_mix_g, m_norm_ffn_g, m_a_w_in, m_a_b_in, m_a_ln_g, m_a_ln_b, m_a_w_s, m_a_b_s, m_a_w_out, m_b_w_in, m_b_b_f, m_b_w_out, m_ffn_w_gate, m_ffn_w_up, m_ffn_w_down, m_final_g, v_ada_w, v_ada_b, v_norm_mix_g, v_norm_ffn_g, v_a_w_in, v_a_b_in, v_a_ln_g, v_a_ln_b, v_a_w_s, v_a_b_s, v_a_w_out, v_b_w_in, v_b_b_f, v_b_w_out, v_ffn_w_gate, v_ffn_w_up, v_ffn_w_down, v_final_g):
    S, D = x.shape[1], x.shape[2]
    H = D // LANE
    G = D // LANE
    FH = ffn_w_down.shape[1] * N_CHIPS
    depth = ada_w.shape[0]
    assert depth == 2 and a_w_in.shape[0] == 1 and b_w_in.shape[0] == 1
    mx, my, mc = _place()
    chip = 2 * mx + my
    dev = 4 * mx + 2 * my + mc
    x0 = x[0]
    target = loss_target[0]
    tr = _pick(S, 256, 8)
    tm = _pick(S, 512, 8)
    tq_f = _pick(S, 1024, LANE)
    tq = _pick(S, 512, LANE)
    nq = S // tq

    idx = jnp.stack([mc, chip]).astype(jnp.int32)
    t3 = lambda a: jnp.transpose(a, (2, 0, 1))
    shards = {"a_in": (a_w_in, 0), "a_out": (a_w_out, 0), "b_in": (t3(b_w_in), None), "b_out": (b_w_out, 0)}
    for l in range(depth):
        shards.update({f"wg{l}": (ffn_w_gate, l), f"wu{l}": (ffn_w_up, l), f"wd{l}": (ffn_w_down, l)})
    first = ["a_in", "a_out"]
    rest = [n for n in shards if n not in first]
    ag = _Gather(dict(zip(first, _cast_slabs("cast_first", [shards[n] for n in first], idx))), by_cols=["b_in"])
    ag.todo.update(zip(rest, _cast_slabs("cast_rest", [shards[n] for n in rest], idx, comm=ag.comm(first))))
    _comm_call("ag_first_pair", ag.comm(["wg0"]))
    wa_in4 = ag.done["a_in"]
    wa_out = ag.done["a_out"].reshape(D, D)

    dev_idx = dev.astype(jnp.int32).reshape(1)
    c_all = _gather_devices("ag_c", jnp.pad(c, ((0, 7), (0, 0))), dev_idx).reshape(N_DEV, 8, D)[:, 0, :]
    c_act = _silu_rows("c_silu", c_all)
    n_loc = ada_w.shape[2]
    mods = []
    for l in range(depth):
        b_loc = lax.dynamic_slice_in_dim(ada_b[l:l + 1], chip * n_loc, n_loc, axis=1)
        mods.append(_mm(f"ada_fwd{l}", "nn", [c_act], [ada_w], M=N_DEV, N=n_loc, K=D, tm=N_DEV, b_layer=l,
                        tn=_pick(n_loc, 1024, LANE), tk=D, extras=[("row", b_loc)],
                        epilogue=lambda accs, ex: [accs[0] + ex[0]])[0])
    mod_all = _gather_devices("ag_mod", jnp.concatenate(mods, axis=1), dev_idx)
    mod_mine = lax.dynamic_index_in_dim(mod_all[0::2], dev, axis=1, keepdims=False)
    mod_mine = mod_mine.reshape(N_CHIPS, depth, n_loc).transpose(1, 0, 2).reshape(depth, 1, N_MOD * D)
    mod = [[mod_mine[l, :, i * D:(i + 1) * D] for i in range(N_MOD)] for l in range(depth)]

    row = lambda a: a.reshape(1, -1)

    tril = jnp.tril(jnp.ones((LANE, LANE), dtype=bool))
    w_mask = jnp.where(tril[None], a_w_s[0], 0.0).astype(BF16)
    bias_full = jnp.repeat(a_b_s[0].T, LANE, axis=1)
    bf_pad = jnp.pad(b_b_f, ((0, 0), (0, LANE - H)))

    admit = {"sgu_in": ["wu0"], "mix_out0": ["wd0"], "ffn_in0": ["b_in", "b_out"], "ffn_out0": ["wg1"],
             "fox_qkv": ["wu1"], "attn_fwd": ["wd1"], "mix_out1": []}
    saved = []
    xs = x0
    for l in range(depth):
        sh1, sc1, g1, sh2, sc2, g2 = mod[l]
        st = {"x_in": xs}
        h1 = _normmod_fwd(f"normmod_mix{l}", xs, row(norm_mix_g[l]), sc1, sh1, tr)
        st["h1"] = h1
        if l == 0:
            a = _mm("sgu_in", "nn", [h1], [wa_in4], M=S, N=2 * D, K=D, tm=tm, tn=2 * D // N_CHIPS, tk=D,
                    b_stacked=True, extras=[("row", a_b_in)], comm=ag.comm(admit["sgu_in"]),
                    epilogue=lambda accs, ex: [accs[0] + ex[0]])[0]
            y = _sgu_mid_fwd(a, a_ln_g, a_ln_b, w_mask, bias_full, tr)
            st["a"], st["y"] = a, y
            w_o, mix_out = wa_out, y
        else:
            w_qkv = ag.done["b_in"].reshape(-1, D)
            w_f = jnp.pad(w_qkv[3 * D:], ((0, LANE - H), (0, 0)))
            qkv = _mm("fox_qkv", "nt", [h1], [w_qkv], M=S, N=3 * D, K=D, tm=tm, tn=_pick(3 * D, 1024, LANE),
                      tk=D, out_dtypes=(BF16,), comm=ag.comm(admit["fox_qkv"]))[0]
            fl = _mm("fox_f", "nt", [h1], [w_f], M=S, N=LANE, K=D, tm=tm, tn=LANE, tk=D)[0]
            F_sh = _fox_gate_fwd(fl, bf_pad)
            F_hs = F_sh[:, :H].T
            F_rows = F_hs.reshape(H, nq, 1, tq)
            o, lse = _attn_fwd(qkv, F_sh, F_hs.reshape(H, S // tq_f, 1, tq_f) * _LOG2E, H, tq_f,
                               comm=ag.comm(admit["attn_fwd"]))
            st.update(qkv=qkv, fl=fl, F_sh=F_sh, F_rows=F_rows, o=o, lse=lse, w_qkv=w_qkv, w_f=w_f)
            w_o, mix_out = ag.done["b_out"].reshape(D, D), o
        x1, out1 = _mm(f"mix_out{l}", "nn", [mix_out], [w_o], M=S, N=D, K=D, tm=tm, tn=_pick(D, 1024, LANE),
                       tk=D, extras=[("tile", xs), ("row", g1)], out_dtypes=(F32, BF16),
                       epilogue=_resid_epilogue, comm=ag.comm(admit[f"mix_out{l}"]))
        st["x_mid"], st["out1"] = x1, out1
        h2 = _normmod_fwd(f"normmod_ffn{l}", x1, row(norm_ffn_g[l]), sc2, sh2, tr)
        gt, up, act = _mm(f"ffn_in{l}", "nn", [h2, h2], [ag.done[f"wg{l}"], ag.done[f"wu{l}"]], M=S, N=FH, K=D,
                          tm=tm, tn=FH // N_CHIPS, tk=D, b_stacked=True, acc_of=[0, 1], n_acc=2,
                          out_dtypes=(BF16, BF16, BF16), epilogue=_swiglu_epilogue,
                          comm=ag.comm(admit.get(f"ffn_in{l}", ())))
        x2, out2 = _mm(f"ffn_out{l}", "nn", [act], [ag.done[f"wd{l}"].reshape(FH, D)], M=S, N=D, K=FH, tm=tm,
                       tn=_pick(D, 1024, LANE), tk=FH, extras=[("tile", x1), ("row", g2)],
                       out_dtypes=(F32, BF16), epilogue=_resid_epilogue,
                       comm=ag.comm(admit.get(f"ffn_out{l}", ())))
        st.update(h2=h2, gt=gt, up=up, act=act, out2=out2)
        saved.append(st)
        xs = x2
    assert not ag.todo and not ag.half
    wg4 = [ag.done[f"wg{l}"] for l in range(depth)]
    wu4 = [ag.done[f"wu{l}"] for l in range(depth)]
    wd = [ag.done[f"wd{l}"].reshape(FH, D) for l in range(depth)]
    wb_out = ag.done["b_out"].reshape(D, D)

    dx, dog, loss_vec, g_final, dg2 = _loss_head("loss_head", xs, target, row(final_g), tr,
                                                 nxt=(saved[-1]["out2"], mod[-1][5]))
    loss = lax.psum(loss_vec[0, 0], ("x", "y", "c"))

    rs = _ReduceScatter(idx)
    dmods = [None] * depth
    gmix = [None] * depth
    gffn = [None] * depth
    for l in reversed(range(depth)):
        sh1, sc1, g1, sh2, sc2, g2 = mod[l]
        st = saved[l]
        dgt, dup = _mm(f"ffn_dact{l}", "nt", [dog], [wd[l]], M=S, N=FH, K=D, tm=tm, tn=FH // N_CHIPS, tk=D,
                       extras=[("tile", st["gt"]), ("tile", st["up"])], out_dtypes=(BF16, BF16),
                       epilogue=_swiglu_bwd_epilogue, comm=rs.comm())
        rs.push(f"wd{l}", _mm(f"ffn_dwd{l}", "tn", [st["act"]], [dog], M=FH, N=D, K=S, tm=FH // N_CHIPS,
                              tn=_pick(D, 512, LANE), tk=S, outer="i",
                              comm=rs.comm())[0].reshape(N_CHIPS, FH // N_CHIPS, D))
        rs.push(f"wg{l}", _mm(f"ffn_dwg{l}", "tn", [st["h2"]], [dgt], M=D, N=FH, K=S, tm=_pick(D, 512, LANE),
                              tn=FH // N_CHIPS, tk=S, out_stacked=True, comm=rs.comm())[0])
        rs.push(f"wu{l}", _mm(f"ffn_dwu{l}", "tn", [st["h2"]], [dup], M=D, N=FH, K=S, tm=_pick(D, 512, LANE),
                              tn=FH // N_CHIPS, tk=S, out_stacked=True, comm=rs.comm())[0])
        dh2 = _mm(f"ffn_dh{l}", "nt", [dgt, dup], [wg4[l], wu4[l]], M=S, N=D, K=FH, tm=_pick(S, 1024, 8),
                  tn=_pick(D, 1024, LANE), tk=FH // N_CHIPS, b_stacked=True, comm=rs.comm())[0]
        dx, dog, dsh2, dsc2, gffn[l], dg1 = _normmod_bwd(f"normmod_ffn_bwd{l}", st["x_mid"], dh2, dx,
                                                         row(norm_ffn_g[l]), sc2, tr, nxt=(st["out1"], g1),
                                                         comm=rs.comm(ici=0))
        if l == 0:
            dy = _mm("sgu_dy", "nt", [dog], [wa_out], M=S, N=D, K=D, tm=tm, tn=_pick(D, 1024, LANE), tk=D,
                     out_dtypes=(BF16,))[0]
            rs.push("a_out", _mm("sgu_dwout", "tn", [st["y"]], [dog], M=D, N=D, K=S, tm=_pick(D, 512, LANE),
                                 tn=_pick(D, 1024, LANE),
                                 tk=S)[0].reshape(N_CHIPS, D // N_CHIPS, D))
            da, dws, dbias, g_ln_g, g_ln_b, g_b_in = _sgu_mid_bwd(st["a"], dy, a_ln_g, a_ln_b, w_mask, bias_full, tr,
                                                                  comm=rs.comm())
            rs.push("a_in", _mm("sgu_dwin", "tn", [st["h1"]], [da], M=D, N=2 * D, K=S, tm=_pick(D, 512, LANE),
                                tn=2 * D // N_CHIPS, tk=S, out_stacked=True, comm=rs.comm())[0])
            dh1 = _mm("sgu_dh", "nt", [da], [wa_in4], M=S, N=D, K=2 * D, tm=_pick(S, 1024, 8), tn=_pick(D, 1024, LANE),
                      tk=2 * D // N_CHIPS, b_stacked=True, comm=rs.comm())[0]
            g_w_s = jnp.where(tril[None], dws, 0.0)
            g_b_s = jnp.sum(dbias.reshape(LANE, G, LANE), axis=2).T
        else:
            do = _mm("fox_do", "nt", [dog], [wb_out], M=S, N=D, K=D, tm=tm, tn=_pick(D, 1024, LANE), tk=D,
                     out_dtypes=(BF16,))[0]
            rs.push("b_out", _mm("fox_dwout", "tn", [st["o"]], [dog], M=D, N=D, K=S, tm=_pick(D, 512, LANE),
                                 tn=_pick(D, 1024, LANE),
                                 tk=S)[0].reshape(N_CHIPS, D // N_CHIPS, D))
            delta = _head_dots(do, st["o"], H, tr)
            delta_rows = delta[:, :H].T.reshape(H, nq, 1, tq)
            A_rows = (st["F_rows"] - st["lse"].reshape(H, nq, 1, tq)) * _LOG2E
            w_qkv, w_f = st["w_qkv"], st["w_f"]
            dq, dk, dv, dF_k, dF_q = _attn_bwd(st["qkv"], do, st["F_sh"], A_rows, delta_rows, H, tq,
                                               comm=rs.comm(ici=2))
            dF_sh = jnp.pad((dF_k.reshape(H, S) + dF_q.reshape(H, S)).T, ((0, 0), (0, LANE - H)))
            dfl, db_f = _fox_gate_bwd(st["fl"], dF_sh, bf_pad)
            dfl_b = dfl.astype(BF16)
            dh_f = _mm("fox_dh_f", "nn", [dfl_b], [w_f], M=S, N=D, K=LANE, tm=tm, tn=_pick(D, 1024, LANE),
                       tk=LANE)[0]
            dh1 = _mm("fox_dh", "nn", [dq, dk, dv], [w_qkv, w_qkv, w_qkv], M=S, N=D, K=D, tm=_pick(S, 256, 8),
                      tn=_pick(D, 1024, LANE), tk=D, b_koffs=[0, 1, 2],
                      extras=[("tile", dh_f)], epilogue=lambda accs, ex: [accs[0] + ex[0]], comm=rs.comm())[0]
            tmw = _pick(D, 512, LANE)
            g_bT = None
            for p, (nm, d_) in enumerate((("q", dq), ("k", dk), ("v", dv))):
                g_bT = _mm(f"fox_dw{nm}", "tn", [d_], [st["h1"]], M=D, N=D, K=S, tm=tmw, tn=_pick(D, 1024, LANE),
                           tk=S, out_into=(3 * D + H, p * (D // tmw), g_bT), comm=rs.comm(ici=0))[0]
            dwf = _mm("fox_dwf", "tn", [dfl_b], [st["h1"]], M=LANE, N=D, K=S, tm=LANE, tn=_pick(D, 1024, LANE),
                      tk=S)[0]
            g_bT = lax.dynamic_update_slice(g_bT, dwf[:H], (3 * D, 0))
            rs.push("b_in", g_bT.reshape(N_CHIPS, -1, D), cols=True)
            g_b_f = db_f[:, :H]
        below = (saved[l - 1]["out2"], mod[l - 1][5]) if l else None
        res = _normmod_bwd(f"normmod_mix_bwd{l}", st["x_in"], dh1, dx, row(norm_mix_g[l]), sc1, tr, nxt=below,
                           comm=rs.comm(ici=1 - l))
        if l:
            dx, dog_below, dsh1, dsc1, gmix[l], dg2_below = res
        else:
            dx, dsh1, dsc1, gmix[l] = res
        dmods[l] = jnp.concatenate([dsh1, dsc1, dg1, dsh2, dsc2, dg2], axis=1)
        if l:
            dog, dg2 = dog_below, dg2_below
    grad_x = dx[None]

    small = [jnp.concatenate(dmods, axis=0), jnp.concatenate(gmix, axis=0), jnp.concatenate(gffn, axis=0),
             g_b_in, g_ln_g, g_ln_b, g_w_s[None], g_b_s[None], g_b_f, g_final[0]]
    small_w = [ada_b, norm_mix_g, norm_ffn_g, a_b_in, a_ln_g, a_ln_b, a_w_s, a_b_s, b_b_f, final_g]
    small_m = [m_ada_b, m_norm_mix_g, m_norm_ffn_g, m_a_b_in, m_a_ln_g, m_a_ln_b, m_a_w_s, m_a_b_s, m_b_b_f, m_final_g]
    small_v = [v_ada_b, v_norm_mix_g, v_norm_ffn_g, v_a_b_in, v_a_ln_g, v_a_ln_b, v_a_w_s, v_a_b_s, v_b_b_f, v_final_g]
    sizes = [w.size for w in small_w]
    total = sum(sizes)
    padded = -(-total // (8 * LANE)) * (8 * LANE)

    def pack(parts):
        flat = jnp.concatenate([p.reshape(-1) for p in parts])
        return jnp.pad(flat, (0, padded - total)).reshape(padded // LANE, LANE)

    def unpack(buf):
        flat = buf.reshape(-1)
        outs, off = [], 0
        for w, n in zip(small_w, sizes):
            outs.append(flat[off:off + n].reshape(w.shape))
            off += n
        return outs

    g_all = _gather_devices("ag_small_grads", pack(small), dev_idx, extra=rs.comm())
    g_small = _sum_slots("sum_small_grads", g_all)
    _, d_small, m_small, v_small = _adamw("adamw_small", pack(small_w), g_small, pack(small_m), pack(small_v))
    sg, sd, sm, sv_ = unpack(g_small), unpack(d_small), unpack(m_small), unpack(v_small)

    n_dm = depth * N_MOD * D
    dmod_all = g_all.reshape(N_DEV, -1)[:, :n_dm].reshape(N_DEV, depth, N_MOD * D)
    u_ada = None
    for l in range(depth):
        dm_loc = lax.dynamic_slice_in_dim(dmod_all[:, l, :], chip * n_loc, n_loc, axis=1).astype(BF16)
        g_ada = _mm(f"ada_dw{l}", "tn", [c_act], [dm_loc], M=D, N=n_loc, K=N_DEV, tm=_pick(D, 512, LANE),
                    tn=_pick(n_loc, 1024, LANE), tk=N_DEV)[0]
        u_ada = _adamw(f"adamw_ada_w{l}", ada_w, g_ada, m_ada_w, v_ada_w, layer=l, prev=u_ada)

    flushes = 0
    while rs.items:
        _comm_call(f"rs_flush{flushes}", rs.comm())
        flushes += 1
    groups = {"a_in": (a_w_in, m_a_w_in, v_a_w_in), "a_out": (a_w_out, m_a_w_out, v_a_w_out),
              "b_out": (b_w_out, m_b_w_out, v_b_w_out),
              "wg": (ffn_w_gate, m_ffn_w_gate, v_ffn_w_gate), "wu": (ffn_w_up, m_ffn_w_up, v_ffn_w_up),
              "wd": (ffn_w_down, m_ffn_w_down, v_ffn_w_down)}
    upd = {}
    for n, (w_, m_, v_) in groups.items():
        for l in range(w_.shape[0]):
            tag = n if n in rs.done else f"{n}{l}"
            upd[n] = _adamw("adamw_" + tag, w_, rs.done[tag], m_, v_, layer=l, prev=upd.get(n))
    u_a_in, u_a_out, u_b_out, u_wg, u_wu, u_wd = (upd[n] for n in groups)
    u_b_in = [jnp.transpose(t, (1, 2, 0)) for t in
              _adamw("adamw_b_in", t3(b_w_in), rs.done["b_in"], t3(m_b_w_in), t3(v_b_w_in))]

    def leaves(i, small_list):
        s = small_list
        return [u_ada[i], s[0], s[1], s[2], u_a_in[i], s[3], s[4], s[5], s[6], s[7], u_a_out[i],
                u_b_in[i], s[8], u_b_out[i], u_wg[i], u_wu[i], u_wd[i], s[9]]

    return (loss, grad_x, *leaves(0, sg), *leaves(1, sd), *leaves(2, sm), *leaves(3, sv_))
```

```python
import functools
import math

import jax
import jax.numpy as jnp
from jax import lax
from jax.experimental import pallas as pl
from jax.experimental.pallas import tpu as pltpu

F32, BF16 = jnp.float32, jnp.bfloat16
LANE = 128
N_CHIPS = 4
N_DEV = 8
N_MOD = 6
EPS = 1e-6
VMEM_LIMIT = 60 * 1024 * 1024
ADAM_LR, ADAM_B1, ADAM_B2, ADAM_EPS, ADAM_WD, ADAM_STEP = 0.001, 0.9, 0.999, 1e-08, 0.01, 10
MESH = pl.DeviceIdType.MESH
ANY = pl.BlockSpec(memory_space=pl.ANY)
SDS = jax.ShapeDtypeStruct


def _pick(dim, pref, align):
    t = min(dim, pref)
    t -= t % align
    while t >= align:
        if dim % t == 0:
            return t
        t -= align
    return dim


def _params(sem=None):
    return pltpu.CompilerParams(dimension_semantics=sem, vmem_limit_bytes=VMEM_LIMIT)


class _Job:
    def __init__(self, ins, outs, aliases, nsem, start, wait):
        self.ins, self.outs, self.aliases, self.nsem, self.start, self.wait = ins, outs, aliases, nsem, start, wait


def _call(body, name, *, grid, in_specs, out_specs, out_shape, args, scratch=(), sem=None, comm=None,
          prefetch=None, keep=()):
    jobs, absorb = comm if comm else ([], None)
    n_pre = 0 if prefetch is None else 1
    n_in, n_out, n_scr = len(args), len(out_shape), len(scratch)
    c_ins = [a for a, _ in keep] + [a for j in jobs for a in j.ins]
    c_outs = [s for j in jobs for s in j.outs]
    aliases = {n_pre + n_in + k: o for k, (_, o) in enumerate(keep)}
    c_scr = []
    i_off, o_off = len(keep), 0
    for j in jobs:
        for a, b in j.aliases.items():
            aliases[n_pre + n_in + i_off + a] = n_out + o_off + b
        i_off += len(j.ins)
        o_off += len(j.outs)
        c_scr += [pltpu.SemaphoreType.DMA((j.nsem,)), pltpu.SemaphoreType.DMA((j.nsem,))]

    def wrapped(*refs):
        bounds = [n_pre + n_in, len(c_ins), n_out, len(c_outs), n_scr]
        parts, p = [], 0
        for n in bounds:
            parts.append(refs[p:p + n])
            p += n
        main_in, cin, main_out, cout, main_scr = parts
        cin = cin[len(keep):]
        csem = refs[p:]

        def run(phase):
            a = b = 0
            for k, j in enumerate(jobs):
                fn = j.start if phase == 0 else j.wait
                fn(cin[a:a + len(j.ins)], cout[b:b + len(j.outs)], csem[2 * k], csem[2 * k + 1])
                a += len(j.ins)
                b += len(j.outs)

        if jobs and grid:
            first = functools.reduce(jnp.logical_and, [pl.program_id(d) == 0 for d in range(len(grid))])
            last = functools.reduce(jnp.logical_and, [pl.program_id(d) == grid[d] - 1 for d in range(len(grid))])
            pl.when(first)(lambda: run(0))
            body(*main_in, *main_out, *main_scr)
            pl.when(last)(lambda: run(1))
        elif jobs:
            run(0)
            body(*main_in, *main_out, *main_scr)
            run(1)
        else:
            body(*main_in, *main_out, *main_scr)

    specs = dict(grid=grid, in_specs=list(in_specs) + [ANY] * len(c_ins),
                 out_specs=list(out_specs) + [ANY] * len(c_outs), scratch_shapes=list(scratch) + c_scr)
    if n_pre:
        specs = dict(grid_spec=pltpu.PrefetchScalarGridSpec(num_scalar_prefetch=1, **specs))
    outs = pl.pallas_call(
        wrapped, name=name, out_shape=list(out_shape) + c_outs, input_output_aliases=aliases,
        compiler_params=_params(("arbitrary",) * len(grid) if jobs else sem), **specs,
    )(*([prefetch] if n_pre else []), *args, *c_ins)
    if jobs:
        absorb(list(outs[n_out:]))
    return list(outs[:n_out])


def _comm_call(name, comm):
    _call(lambda: None, name, grid=(), in_specs=[], out_specs=[], out_shape=[], args=[], comm=comm)


def _mm(name, form, a_list, b_list, *, M, N, K, tm, tn, tk, b_stacked=False, out_stacked=False,
        b_koffs=None, acc_of=None, n_acc=1, extras=(), out_dtypes=(F32,), epilogue=None, comm=None,
        outer="j", b_layer=None, out_into=None):
    assert M % tm == 0 and N % tn == 0 and K % tk == 0, (name, M, N, K, tm, tn, tk)
    nm, nn, nk = M // tm, N // tn, K // tk
    npairs = len(a_list)
    acc_of = acc_of or [0] * npairs
    b_koffs = b_koffs or [0] * npairs
    if epilogue is None:
        epilogue = lambda accs, ex: [accs[0]]

    def spec(shape, fn, **kw):
        return pl.BlockSpec(shape, fn if outer == "j" else (lambda i, j, k: fn(j, i, k)), **kw)

    slabs = form == "nt" and b_stacked and tk == K

    a_uniq = [a for p, a in enumerate(a_list) if all(a is not b for b in a_list[:p])]
    a_pos = [next(u for u, b in enumerate(a_uniq) if b is a) for a in a_list]
    in_specs = []
    for _ in a_uniq:
        if form == "tn":
            in_specs.append(spec((tk, tm), lambda j, i, k: (k, i)))
        else:
            in_specs.append(spec((tm, tk), lambda j, i, k: (i, k)))
    for off in b_koffs:
        if b_layer is not None:
            assert form == "nn" and not b_stacked
            in_specs.append(spec((None, tk, tn), lambda j, i, k: (b_layer, k, j)))
        elif form == "nn":
            if b_stacked:
                assert tn * N_CHIPS == N
                in_specs.append(spec((None, tk, tn), lambda j, i, k: (j, k, 0)))
            else:
                in_specs.append(spec((tk, tn), lambda j, i, k, off=off: (off + k, j)))
        elif form == "nt":
            if slabs:
                in_specs.append(spec((N_CHIPS, tn, K // N_CHIPS), lambda j, i, k: (0, j, 0),
                                     pipeline_mode=pl.Buffered(1)))
            elif b_stacked:
                assert tk * N_CHIPS == K
                in_specs.append(spec((None, tn, tk), lambda j, i, k: (k, j, 0)))
            else:
                in_specs.append(spec((tn, tk), lambda j, i, k, off=off: (j, off + k)))
        else:
            in_specs.append(spec((tk, tn), lambda j, i, k: (k, j)))
    for kind, _ in extras:
        if kind == "tile":
            in_specs.append(spec((tm, tn), lambda j, i, k: (i, j)))
        else:
            in_specs.append(spec((1, tn), lambda j, i, k: (0, j)))
    if out_stacked:
        assert tn * N_CHIPS == N
        out_specs = [spec((None, tm, tn), lambda j, i, k: (j, i, 0)) for _ in out_dtypes]
        out_shape = [SDS((N_CHIPS, M, tn), d) for d in out_dtypes]
    else:
        rows, roff, prev = out_into if out_into else (M, 0, None)
        out_specs = [spec((tm, tn), lambda j, i, k: (roff + i, j)) for _ in out_dtypes]
        out_shape = [SDS((rows, N), d) for d in out_dtypes]
    dims = {"nn": (((1,), (0,)), ((), ())), "nt": (((1,), (1,)), ((), ())), "tn": (((0,), (0,)), ((), ()))}[form]
    n_a, n_ex, n_out = len(a_uniq), len(extras), len(out_dtypes)

    def body(*refs):
        a_vals = [refs[u][...].astype(BF16) for u in range(n_a)]
        b_refs = refs[n_a:n_a + npairs]
        e_refs = refs[n_a + npairs:n_a + npairs + n_ex]
        o_refs = refs[n_a + npairs + n_ex:n_a + npairs + n_ex + n_out]
        acc_refs = refs[n_a + npairs + n_ex + n_out:]

        tot = [None] * n_acc
        for p in range(npairs):
            if slabs:
                n = K // N_CHIPS
                d = sum(lax.dot_general(a_vals[a_pos[p]][:, s * n:(s + 1) * n], b_refs[p][s], dims,
                                        preferred_element_type=F32) for s in range(N_CHIPS))
            else:
                d = lax.dot_general(a_vals[a_pos[p]], b_refs[p][...].astype(BF16), dims, preferred_element_type=F32)
            tot[acc_of[p]] = d if tot[acc_of[p]] is None else tot[acc_of[p]] + d

        def finish(accs):
            outs = epilogue(accs, [e[...] for e in e_refs])
            for o_ref, o in zip(o_refs, outs):
                o_ref[...] = o.astype(o_ref.dtype)

        if nk == 1:
            finish(tot)
        else:
            k = pl.program_id(2)

            @pl.when(k == 0)
            def _():
                for r, t in zip(acc_refs, tot):
                    r[...] = t

            @pl.when(k > 0)
            def _():
                for r, t in zip(acc_refs, tot):
                    r[...] += t

            @pl.when(k == nk - 1)
            def _():
                finish([r[...] for r in acc_refs])

    scratch = [pltpu.VMEM((tm, tn), F32) for _ in range(n_acc)] if nk > 1 else []
    return _call(body, name, grid=(nn, nm, nk) if outer == "j" else (nm, nn, nk), in_specs=in_specs,
                 out_specs=out_specs, out_shape=out_shape, scratch=scratch,
                 sem=("parallel", "parallel", "arbitrary"), comm=comm,
                 keep=[(out_into[2], 0)] if out_into and out_into[2] is not None else (),
                 args=[*a_uniq, *b_list, *[e for _, e in extras]])


def _rowwise(name, fn, rows, vecs, row_outs, acc_widths, tr, comm=None):
    S = rows[0].shape[0]
    assert S % tr == 0
    nr, nv, no, na = len(rows), len(vecs), len(row_outs), len(acc_widths)

    def body(*refs):
        r = [x[...] for x in refs[:nr]]
        v = [x[...] for x in refs[nr:nr + nv]]
        o_refs = refs[nr + nv:nr + nv + no]
        a_refs = refs[nr + nv + no:]
        outs, accs = fn(r, v)
        for o_ref, o in zip(o_refs, outs):
            o_ref[...] = o.astype(o_ref.dtype)
        first = pl.program_id(0) == 0

        @pl.when(first)
        def _():
            for a_ref, a in zip(a_refs, accs):
                a_ref[...] = a

        @pl.when(jnp.logical_not(first))
        def _():
            for a_ref, a in zip(a_refs, accs):
                a_ref[...] += a

    in_specs = [pl.BlockSpec((tr, x.shape[1]), lambda i: (i, 0)) for x in rows]
    in_specs += [pl.BlockSpec(x.shape, lambda i, nd=x.ndim: (0,) * nd) for x in vecs]
    out_specs = [pl.BlockSpec((tr, w), lambda i: (i, 0)) for w, _ in row_outs]
    out_specs += [pl.BlockSpec((1, w), lambda i: (0, 0)) for w in acc_widths]
    out_shape = [SDS((S, w), d) for w, d in row_outs] + [SDS((1, w), F32) for w in acc_widths]
    return _call(body, name, grid=(S // tr,), in_specs=in_specs, out_specs=out_specs, out_shape=out_shape,
                 sem=("arbitrary",), comm=comm, args=[*rows, *vecs])


def _colsum(a):
    return jnp.sum(a, axis=0, keepdims=True)


def _rms_stats(x):
    rstd = lax.rsqrt(jnp.mean(x * x, axis=1, keepdims=True) + EPS)
    return x * rstd, rstd


def _normmod_fwd(name, x, g, sc, sh, tr):
    def fn(r, v):
        n, _ = _rms_stats(r[0])
        return [(n * v[0]) * (1.0 + v[1]) + v[2]], []
    return _rowwise(name, fn, [x], [g, sc, sh], [(x.shape[1], BF16)], [], tr)[0]


def _gated(dx, nxt_out, gate):
    return gate * dx, _colsum(dx * nxt_out.astype(F32))


def _normmod_bwd(name, x, dh, dres, g, sc, tr, nxt=None, comm=None):
    def fn(r, v):
        x_, dh_, dres_ = r[:3]
        g_, sc_ = v[:2]
        n, rstd = _rms_stats(x_)
        hn = n * g_
        dhn = dh_ * (1.0 + sc_)
        dn = dhn * g_
        dx = dres_ + rstd * (dn - n * jnp.mean(dn * n, axis=1, keepdims=True))
        outs, accs = [dx], [_colsum(dh_), _colsum(dh_ * hn), _colsum(dhn * n)]
        if nxt:
            dog, dgate = _gated(dx, r[3], v[2])
            outs.append(dog)
            accs.append(dgate)
        return outs, accs
    D = x.shape[1]
    rows, vecs = [x, dh, dres] + ([nxt[0]] if nxt else []), [g, sc] + ([nxt[1]] if nxt else [])
    return _rowwise(name, fn, rows, vecs, [(D, F32)] + ([(D, BF16)] if nxt else []), [D] * (4 if nxt else 3), tr,
                    comm=comm)


def _loss_head(name, x, target, g, tr, nxt):
    D = x.shape[1]

    def fn(r, v):
        n, rstd = _rms_stats(r[0])
        err = n * v[0] - r[1]
        loss = 0.5 * jnp.sum(jnp.mean(err * err, axis=1, keepdims=True), axis=0, keepdims=True)
        dy = err * (1.0 / D)
        dn = dy * v[0]
        dx = rstd * (dn - n * jnp.mean(dn * n, axis=1, keepdims=True))
        dog, dgate = _gated(dx, r[2], v[1])
        return [dx, dog], [jnp.broadcast_to(loss, (1, LANE)), _colsum(dy * n), dgate]
    return _rowwise(name, fn, [x, target, nxt[0]], [g, nxt[1]], [(D, F32), (D, BF16)], [LANE, D, D], tr)


_INV_SQRT2 = 1.0 / math.sqrt(2.0)
_INV_SQRT2PI = 1.0 / math.sqrt(2.0 * math.pi)


def _gelu(a):
    return 0.5 * a * (1.0 + lax.erf(a * _INV_SQRT2))


def _gelu_grad(a):
    return 0.5 * (1.0 + lax.erf(a * _INV_SQRT2)) + a * jnp.exp(-0.5 * a * a) * _INV_SQRT2PI


def _ln_stats(v):
    mu = jnp.mean(v, axis=1, keepdims=True)
    vc = v - mu
    rstd = lax.rsqrt(jnp.mean(vc * vc, axis=1, keepdims=True) + EPS)
    return vc * rstd, rstd


def _chunk_mix(w_ref, vn, tr, G):
    rows = []
    for ch in range(tr // LANE):
        cols = []
        for g in range(G):
            blk = vn[ch * LANE:(ch + 1) * LANE, g * LANE:(g + 1) * LANE]
            cols.append(jnp.dot(w_ref[g], blk, preferred_element_type=F32))
        rows.append(jnp.concatenate(cols, axis=1))
    return jnp.concatenate(rows, axis=0)


def _sgu_mid_fwd(a, ln_g, ln_b, w_mask, bias_full, tr):
    S, D2 = a.shape
    D = D2 // 2
    G = D // LANE

    def body(a_ref, g_ref, b_ref, w_ref, bias_ref, y_ref):
        u = _gelu(a_ref[:, :D])
        v = _gelu(a_ref[:, D:])
        vhat, _ = _ln_stats(v)
        vn = (vhat * g_ref[...] + b_ref[...]).astype(BF16)
        sv = _chunk_mix(w_ref, vn, tr, G) + jnp.concatenate([bias_ref[...]] * (tr // LANE), axis=0)
        y_ref[...] = (u * sv).astype(BF16)

    return pl.pallas_call(
        body, name="sgu_mid_fwd", grid=(S // tr,),
        in_specs=[pl.BlockSpec((tr, D2), lambda i: (i, 0)), pl.BlockSpec((1, D), lambda i: (0, 0)),
                  pl.BlockSpec((1, D), lambda i: (0, 0)), pl.BlockSpec((G, LANE, LANE), lambda i: (0, 0, 0)),
                  pl.BlockSpec((LANE, D), lambda i: (0, 0))],
        out_specs=pl.BlockSpec((tr, D), lambda i: (i, 0)), out_shape=SDS((S, D), BF16),
        compiler_params=_params(("arbitrary",)))(a, ln_g, ln_b, w_mask, bias_full)


def _sgu_mid_bwd(a, dy, ln_g, ln_b, w_mask, bias_full, tr, comm=None):
    S, D2 = a.shape
    D = D2 // 2
    G = D // LANE
    nch = tr // LANE

    def body(a_ref, dy_ref, g_ref, b_ref, w_ref, bias_ref, da_ref, dw_ref, dbias_ref, dg_ref, db_ref, dbin_ref):
        au, av = a_ref[:, :D], a_ref[:, D:]
        u = _gelu(au)
        v = _gelu(av)
        vhat, rstd = _ln_stats(v)
        vn = (vhat * g_ref[...] + b_ref[...]).astype(BF16)
        sv = _chunk_mix(w_ref, vn, tr, G) + jnp.concatenate([bias_ref[...]] * nch, axis=0)
        dy_ = dy_ref[...].astype(F32)
        du = dy_ * sv
        dsv = dy_ * u
        dsv_b = dsv.astype(BF16)
        first = pl.program_id(0) == 0

        @pl.when(first)
        def _():
            dw_ref[...] = jnp.zeros_like(dw_ref)
            dbias_ref[...] = jnp.zeros_like(dbias_ref)
            dg_ref[...] = jnp.zeros_like(dg_ref)
            db_ref[...] = jnp.zeros_like(db_ref)
            dbin_ref[...] = jnp.zeros_like(dbin_ref)

        rows = []
        dbias = None
        for ch in range(nch):
            r0 = ch * LANE
            cols = []
            for g in range(G):
                c0 = g * LANE
                ds_blk = dsv_b[r0:r0 + LANE, c0:c0 + LANE]
                vn_blk = vn[r0:r0 + LANE, c0:c0 + LANE]
                cols.append(lax.dot_general(w_ref[g], ds_blk, (((0,), (0,)), ((), ())),
                                            preferred_element_type=F32))
                dw_ref[g] += lax.dot_general(ds_blk, vn_blk, (((1,), (1,)), ((), ())),
                                             preferred_element_type=F32)
            rows.append(jnp.concatenate(cols, axis=1))
            blk = dsv[r0:r0 + LANE, :]
            dbias = blk if dbias is None else dbias + blk
        dvn = jnp.concatenate(rows, axis=0)
        dbias_ref[...] += dbias
        dg_ref[...] += _colsum(dvn * vhat)
        db_ref[...] += _colsum(dvn)
        dvh = dvn * g_ref[...]
        dv = rstd * (dvh - jnp.mean(dvh, axis=1, keepdims=True)
                     - vhat * jnp.mean(dvh * vhat, axis=1, keepdims=True))
        da_u = du * _gelu_grad(au)
        da_v = dv * _gelu_grad(av)
        da_ref[:, :D] = da_u.astype(BF16)
        da_ref[:, D:] = da_v.astype(BF16)
        dbin_ref[:, :D] += _colsum(da_u)
        dbin_ref[:, D:] += _colsum(da_v)

    full = lambda shp: pl.BlockSpec(shp, lambda i, nd=len(shp): (0,) * nd)
    return _call(
        body, "sgu_mid_bwd", grid=(S // tr,),
        in_specs=[pl.BlockSpec((tr, D2), lambda i: (i, 0)), pl.BlockSpec((tr, D), lambda i: (i, 0)),
                  full((1, D)), full((1, D)), full((G, LANE, LANE)), full((LANE, D))],
        out_specs=[pl.BlockSpec((tr, D2), lambda i: (i, 0)), full((G, LANE, LANE)), full((LANE, D)),
                   full((1, D)), full((1, D)), full((1, D2))],
        out_shape=[SDS((S, D2), BF16), SDS((G, LANE, LANE), F32), SDS((LANE, D), F32),
                   SDS((1, D), F32), SDS((1, D), F32), SDS((1, D2), F32)],
        sem=("arbitrary",), comm=comm, args=[a, dy, ln_g, ln_b, w_mask, bias_full])


def _tri(lower):
    r = lax.broadcasted_iota(jnp.int32, (LANE, LANE), 0)
    c = lax.broadcasted_iota(jnp.int32, (LANE, LANE), 1)
    return jnp.where((c <= r) if lower else (c >= r), 1.0, 0.0).astype(F32)


def _fox_gate_fwd(fl, bf_pad):
    S = fl.shape[0]
    nblk = S // LANE

    def body(fl_ref, b_ref, f_ref):
        tri = _tri(True)

        def step(i, carry):
            r0 = pl.multiple_of(i * LANE, LANE)
            z = fl_ref[pl.ds(r0, LANE), :] + b_ref[...]
            logf = jnp.minimum(z, 0.0) - jnp.log1p(jnp.exp(-jnp.abs(z)))
            f = jnp.dot(tri, logf, preferred_element_type=F32, precision=lax.Precision.HIGHEST) + carry
            f_ref[pl.ds(r0, LANE), :] = f
            return f[LANE - 1:LANE, :]
        lax.fori_loop(0, nblk, step, jnp.zeros((1, LANE), F32))

    return pl.pallas_call(body, name="fox_gate_fwd", out_shape=SDS((S, LANE), F32),
                          compiler_params=_params())(fl, bf_pad)


def _fox_gate_bwd(fl, dF, bf_pad):
    S = fl.shape[0]
    nblk = S // LANE

    def body(fl_ref, df_ref, b_ref, dfl_ref, db_ref):
        tri = _tri(True)

        def prefix(i, carry):
            r0 = pl.multiple_of(i * LANE, LANE)
            p = jnp.dot(tri, df_ref[pl.ds(r0, LANE), :], preferred_element_type=F32,
                        precision=lax.Precision.HIGHEST) + carry
            dfl_ref[pl.ds(r0, LANE), :] = p
            return p[LANE - 1:LANE, :]
        total = lax.fori_loop(0, nblk, prefix, jnp.zeros((1, LANE), F32))
        db_ref[...] = jnp.zeros_like(db_ref)

        def finish(i, carry):
            r0 = pl.multiple_of(i * LANE, LANE)
            dlogf = total - dfl_ref[pl.ds(r0, LANE), :] + df_ref[pl.ds(r0, LANE), :]
            z = fl_ref[pl.ds(r0, LANE), :] + b_ref[...]
            dfl = dlogf / (1.0 + jnp.exp(z))
            dfl_ref[pl.ds(r0, LANE), :] = dfl
            db_ref[...] += _colsum(dfl)
            return carry
        lax.fori_loop(0, nblk, finish, 0)

    return pl.pallas_call(body, name="fox_gate_bwd", out_shape=[SDS((S, LANE), F32), SDS((1, LANE), F32)],
                          compiler_params=_params())(fl, dF, bf_pad)


def _lane_pick(blk, h):
    lane = lax.broadcasted_iota(jnp.int32, blk.shape, 1)
    return jnp.sum(jnp.where(lane == h, blk, 0.0), axis=1, keepdims=True)


_NEG = -1e30
_LOG2E = 1.0 / math.log(2.0)
_LN2 = math.log(2.0)
_SUB = 32


def _attn_fwd(qkv, F_sh, F_rows, H, tq, comm=None):
    S = qkv.shape[0]
    D = H * LANE
    nq = S // tq
    scale = 1.0 / math.sqrt(LANE)

    def body(q_ref, k_ref, v_ref, fsh_ref, frow_ref, o_ref, lse_ref):
        h, i = pl.program_id(0), pl.program_id(1)
        q = q_ref[...]
        fq = _lane_pick(fsh_ref[...], h) * _LOG2E

        def block(j, carry, diagonal):
            m, l, acc = carry
            r0 = pl.multiple_of(j * tq, tq)
            k = k_ref[pl.ds(r0, tq), :]
            v = v_ref[pl.ds(r0, tq), :]
            s = lax.dot_general(q, k, (((1,), (1,)), ((), ())), preferred_element_type=F32) * (scale * _LOG2E)
            s = s + (fq - frow_ref[j])
            if diagonal:
                row = lax.broadcasted_iota(jnp.int32, (tq, tq), 0)
                col = lax.broadcasted_iota(jnp.int32, (tq, tq), 1)
                s = jnp.where(col <= row, s, _NEG)
            m_new = jnp.maximum(m, jnp.max(s, axis=1, keepdims=True))
            alpha = jnp.exp2(m - m_new)
            p = jnp.exp2(s - m_new)
            l = alpha * l + jnp.sum(p, axis=1, keepdims=True)
            acc = alpha * acc + jnp.dot(p.astype(BF16), v, preferred_element_type=F32)
            return m_new, l, acc

        init = (jnp.full((tq, 1), _NEG, F32), jnp.zeros((tq, 1), F32), jnp.zeros((tq, LANE), F32))
        carry = lax.fori_loop(0, i, lambda j, cr: block(j, cr, False), init)
        m, l, acc = block(i, carry, True)
        o_ref[...] = (acc / l).astype(BF16)
        lse_ref[...] = (m + jnp.log2(l)) * _LN2

    return _call(
        body, "attn_fwd", grid=(H, nq),
        in_specs=[pl.BlockSpec((tq, LANE), lambda h, i: (i, h)),
                  pl.BlockSpec((S, LANE), lambda h, i: (0, H + h)),
                  pl.BlockSpec((S, LANE), lambda h, i: (0, 2 * H + h)),
                  pl.BlockSpec((tq, LANE), lambda h, i: (i, 0)),
                  pl.BlockSpec((None, nq, 1, tq), lambda h, i: (h, 0, 0, 0))],
        out_specs=[pl.BlockSpec((tq, LANE), lambda h, i: (i, h)),
                   pl.BlockSpec((None, tq, 1), lambda h, i: (h, i, 0))],
        out_shape=[SDS((S, D), BF16), SDS((H, S, 1), F32)],
        sem=("parallel", "arbitrary"), comm=comm, args=[qkv, qkv, qkv, F_sh, F_rows])


def _attn_bwd(qkv, do, F_sh, A_rows, delta_rows, H, tq, comm=None):
    S = qkv.shape[0]
    D = H * LANE
    nq = S // tq
    scale = 1.0 / math.sqrt(LANE)

    def body(q_ref, do_ref, k_ref, v_ref, fsh_ref, a_ref, dl_ref, dq_ref, dk_ref, dv_ref, df_ref, dfr_ref,
             dq_acc, dfr_acc, st_scr, dp_scr, pt_scr, ds_scr, fk_scr, df_scr, dk_scr, dv_scr):
        h, j = pl.program_id(0), pl.program_id(1)

        @pl.when(j == 0)
        def _():
            dq_acc[...] = jnp.zeros_like(dq_acc)
            dfr_acc[...] = jnp.zeros_like(dfr_acc)

        k = k_ref[...]
        v = v_ref[...]
        fk_scr[...] = _lane_pick(fsh_ref[...], h) * _LOG2E
        df_scr[...] = jnp.zeros((tq, 1), F32)
        dk_scr[...] = jnp.zeros((tq, LANE), F32)
        dv_scr[...] = jnp.zeros((tq, LANE), F32)

        def block(i, diagonal):
            r0 = pl.multiple_of(i * tq, tq)
            q = q_ref[pl.ds(r0, tq), :]
            do_ = do_ref[pl.ds(r0, tq), :]
            st_scr[...] = lax.dot_general(k, q, (((1,), (1,)), ((), ())), preferred_element_type=F32)
            dp_scr[...] = lax.dot_general(v, do_, (((1,), (1,)), ((), ())), preferred_element_type=F32)
            a_row, dl_row = a_ref[i], dl_ref[i]
            dfr = jnp.zeros((1, tq), F32)
            for r in range(tq // _SUB):
                rows = slice(r * _SUB, (r + 1) * _SUB)
                arg = st_scr[rows, :] * (scale * _LOG2E) + (a_row - fk_scr[rows, :])
                if diagonal:
                    row = lax.broadcasted_iota(jnp.int32, (_SUB, tq), 0) + r * _SUB
                    col = lax.broadcasted_iota(jnp.int32, (_SUB, tq), 1)
                    arg = jnp.where(row <= col, arg, _NEG)
                pt = jnp.exp2(arg)
                dst = pt * (dp_scr[rows, :] - dl_row)
                df_scr[rows, :] += jnp.sum(dst, axis=1, keepdims=True)
                dfr = dfr + jnp.sum(dst, axis=0, keepdims=True)
                pt_scr[rows, :] = pt.astype(BF16)
                ds_scr[rows, :] = dst.astype(BF16)
            dfr_acc[i] += dfr
            dsb = ds_scr[...]
            dv_scr[...] += jnp.dot(pt_scr[...], do_, preferred_element_type=F32)
            dk_scr[...] += jnp.dot(dsb, q, preferred_element_type=F32)
            dq_acc[pl.ds(r0, tq), :] += lax.dot_general(dsb, k, (((0,), (0,)), ((), ())),
                                                        preferred_element_type=F32)

        def full_block(i, carry):
            block(i, False)
            return carry

        block(j, True)
        lax.fori_loop(j + 1, nq, full_block, 0)
        dk_ref[...] = (dk_scr[...] * scale).astype(BF16)
        dv_ref[...] = dv_scr[...].astype(BF16)
        df_ref[...] = -df_scr[...]

        @pl.when(j == nq - 1)
        def _():
            dq_ref[...] = (dq_acc[...] * scale).astype(BF16)
            dfr_ref[...] = dfr_acc[...]

    return _call(
        body, "attn_bwd", grid=(H, nq),
        in_specs=[pl.BlockSpec((S, LANE), lambda h, j: (0, h)),
                  pl.BlockSpec((S, LANE), lambda h, j: (0, h)),
                  pl.BlockSpec((tq, LANE), lambda h, j: (j, H + h)),
                  pl.BlockSpec((tq, LANE), lambda h, j: (j, 2 * H + h)),
                  pl.BlockSpec((tq, LANE), lambda h, j: (j, 0)),
                  pl.BlockSpec((None, nq, 1, tq), lambda h, j: (h, 0, 0, 0)),
                  pl.BlockSpec((None, nq, 1, tq), lambda h, j: (h, 0, 0, 0))],
        out_specs=[pl.BlockSpec((S, LANE), lambda h, j: (0, h)),
                   pl.BlockSpec((tq, LANE), lambda h, j: (j, h)),
                   pl.BlockSpec((tq, LANE), lambda h, j: (j, h)),
                   pl.BlockSpec((None, tq, 1), lambda h, j: (h, j, 0)),
                   pl.BlockSpec((None, nq, 1, tq), lambda h, j: (h, 0, 0, 0))],
        out_shape=[SDS((S, D), BF16), SDS((S, D), BF16), SDS((S, D), BF16), SDS((H, S, 1), F32),
                   SDS((H, nq, 1, tq), F32)],
        scratch=[pltpu.VMEM((S, LANE), F32), pltpu.VMEM((nq, 1, tq), F32),
                 pltpu.VMEM((tq, tq), F32), pltpu.VMEM((tq, tq), F32), pltpu.VMEM((tq, tq), BF16),
                 pltpu.VMEM((tq, tq), BF16), pltpu.VMEM((tq, 1), F32), pltpu.VMEM((tq, 1), F32),
                 pltpu.VMEM((tq, LANE), F32), pltpu.VMEM((tq, LANE), F32)],
        sem=("parallel", "arbitrary"), comm=comm, args=[qkv, do, qkv, qkv, F_sh, A_rows, delta_rows])


def _head_dots(do, o, H, tr):
    def fn(r, v):
        prod = r[0].astype(F32) * r[1].astype(F32)
        lane = lax.broadcasted_iota(jnp.int32, (prod.shape[0], LANE), 1)
        out = jnp.zeros((prod.shape[0], LANE), F32)
        for h in range(H):
            s = jnp.sum(prod[:, h * LANE:(h + 1) * LANE], axis=1, keepdims=True)
            out = jnp.where(lane == h, s, out)
        return [out], []
    return _rowwise("attn_delta", fn, [do, o], [], [(LANE, F32)], [], tr)[0]


def _adamw(name, w, g, m, v, layer=None, prev=None):
    R, C = g.shape
    tr = _pick(R, max(8, (512 * 1024) // max(C, 1) // 8 * 8), 8)
    c1 = 1.0 - ADAM_B1 ** ADAM_STEP
    c2 = 1.0 - ADAM_B2 ** ADAM_STEP

    def body(w_ref, g_ref, m_ref, v_ref, go_ref, d_ref, mo_ref, vo_ref):
        g_ = g_ref[...]
        m_ = ADAM_B1 * m_ref[...] + (1.0 - ADAM_B1) * g_
        v_ = ADAM_B2 * v_ref[...] + (1.0 - ADAM_B2) * (g_ * g_)
        go_ref[...] = g_
        d_ref[...] = -ADAM_LR * ((m_ / c1) / (jnp.sqrt(v_ / c2) + ADAM_EPS) + ADAM_WD * w_ref[...])
        mo_ref[...] = m_
        vo_ref[...] = v_

    if w.ndim == 3 and w.shape[1] == 1 and layer is None:
        tc = _pick(C, 2 * LANE, LANE)
        gspec = pl.BlockSpec((R, tc), lambda i: (0, i))
        pspec = pl.BlockSpec((R, None, tc), lambda i: (0, 0, i))
        grid = (C // tc,)
    else:
        gspec = pl.BlockSpec((tr, C), lambda i: (i, 0))
        pspec = gspec if layer is None else pl.BlockSpec((None, tr, C), lambda i: (layer, i, 0))
        grid = (R // tr,)
    return _call(body, name, grid=grid, in_specs=[pspec, gspec, pspec, pspec], out_specs=[pspec] * 4,
                 out_shape=[SDS(w.shape, F32)] * 4, sem=("parallel",), args=[w, g, m, v],
                 keep=[(p, k) for k, p in enumerate(prev)] if prev else ())


def _sum_slots(name, a):
    n, R, C = a.shape
    tr = _pick(R, 256, 8)

    def body(a_ref, o_ref):
        acc = a_ref[0]
        for k in range(1, n):
            acc = acc + a_ref[k]
        o_ref[...] = acc

    return pl.pallas_call(body, name=name, grid=(R // tr,),
                          in_specs=[pl.BlockSpec((n, tr, C), lambda i: (0, i, 0))],
                          out_specs=pl.BlockSpec((tr, C), lambda i: (i, 0)), out_shape=SDS((R, C), F32),
                          compiler_params=_params(("parallel",)))(a)


def _silu_rows(name, a):
    def body(a_ref, o_ref):
        z = a_ref[...]
        o_ref[...] = (z / (1.0 + jnp.exp(-z))).astype(BF16)
    return pl.pallas_call(body, name=name, out_shape=SDS(a.shape, BF16), compiler_params=_params())(a)


def _place():
    return lax.axis_index("x"), lax.axis_index("y"), lax.axis_index("c")


def _rcopy(src, dst, ssem, rsem, dev):
    return pltpu.make_async_remote_copy(src_ref=src, dst_ref=dst, send_sem=ssem, recv_sem=rsem,
                                        device_id=dev, device_id_type=MESH)


def _other_chips(x, y):
    return [(1 - x, y), (x, 1 - y), (1 - x, 1 - y)]


def _job_gather_devices(buf8):
    def views(o):
        x, y, c = _place()
        slot = lambda px, py, pc: o.at[4 * px + 2 * py + pc]
        return c, (x, y, 1 - c), _other_chips(x, y), slot, slot(x, y, c)

    def first(o, ss, rs):
        c, sib, chips, slot, me = views(o)
        return [_rcopy(me, me, ss.at[0], rs.at[0], sib)] + [
            _rcopy(me, me, ss.at[1 + j], rs.at[1 + j], (cx, cy, c)) for j, (cx, cy) in enumerate(chips)]

    def start(i, o, ss, rs):
        for cp in first(o[0], ss, rs):
            cp.start()

    def wait(i, o, ss, rs):
        c, sib, chips, slot, me = views(o[0])
        x, y = sib[0], sib[1]
        passed = []
        for j, (cx, cy) in enumerate(chips):
            got = slot(cx, cy, c)
            _rcopy(me, got, ss.at[1 + j], rs.at[1 + j], (cx, cy, c)).wait_recv()
            passed.append(_rcopy(got, got, ss.at[4 + j], rs.at[4 + j], sib))
            passed[j].start()
        _rcopy(me, slot(x, y, 1 - c), ss.at[0], rs.at[0], sib).wait_recv()
        for j, (cx, cy) in enumerate(chips):
            _rcopy(me, slot(cx, cy, 1 - c), ss.at[4 + j], rs.at[4 + j], sib).wait_recv()
        for cp in first(o[0], ss, rs) + passed:
            cp.wait_send()

    return _Job([buf8], [SDS(buf8.shape, buf8.dtype)], {0: 0}, N_DEV - 1, start, wait)


def _gather_devices(name, buf, slot_idx, extra=None):
    R, C = buf.shape

    def place(s_ref, b_ref, o_ref):
        o_ref[...] = b_ref[...]

    buf8 = _call(place, name + "_place", grid=(1,), prefetch=slot_idx, args=[buf],
                 in_specs=[pl.BlockSpec((R, C), lambda i, s: (0, 0))],
                 out_specs=[pl.BlockSpec((None, R, C), lambda i, s: (s[0], 0, 0))],
                 out_shape=[SDS((N_DEV, R, C), buf.dtype)])[0]
    jobs, absorb_extra = extra if extra else ([], None)
    got = []

    def absorb(outs):
        got.append(outs[0])
        if absorb_extra:
            absorb_extra(outs[1:])

    _comm_call(name, ([_job_gather_devices(buf8)] + jobs, absorb))
    return got[0]


def _cast_slabs(name, ws, idx, comm=None):
    n = len(ws)
    steps = next(s for s in (8, 4, 2, 1)
                 if all((w.shape[2] % (LANE * s) if l is None else w.shape[1] % (16 * s)) == 0 for w, l in ws))

    def body(s_ref, *refs):
        for w_ref, o_ref in zip(refs[:n], refs[n:]):
            o_ref[...] = w_ref[...].astype(BF16)

    in_specs, out_specs, out_shape = [], [], []
    for w, l in ws:
        if l is None:
            R, _, C = w.shape
            in_specs.append(pl.BlockSpec((R, None, C // steps), lambda i, s: (0, 0, i)))
            out_specs.append(pl.BlockSpec((None, R, C // steps), lambda i, s: (s[1], 0, i)))
        else:
            _, R, C = w.shape
            in_specs.append(pl.BlockSpec((None, R // steps, C), lambda i, s, l=l: (l, i, 0)))
            out_specs.append(pl.BlockSpec((None, R // steps, C), lambda i, s: (s[1], i, 0)))
        out_shape.append(SDS((N_CHIPS, R, C), BF16))
    return _call(body, name, grid=(steps,), prefetch=idx, comm=comm, sem=("parallel",), args=[w for w, _ in ws],
                 in_specs=in_specs, out_specs=out_specs, out_shape=out_shape)


def _half(ref, lead, hf, cols):
    n = ref.shape[-1 if cols else -2] // 2
    assert 2 * n == ref.shape[-1 if cols else -2]
    cut = pl.ds(hf * n, n)
    return ref.at[(*lead, slice(None), cut) if cols else (*lead, cut, slice(None))]


def _job_gather_ici(buf, cols=False):
    def views(o):
        x, y, c = _place()
        return c, _other_chips(x, y), _half(o, (2 * x + y,), c, cols)

    def start(i, o, ss, rs):
        c, chips, mine = views(o[0])
        for j, (cx, cy) in enumerate(chips):
            _rcopy(mine, mine, ss.at[j], rs.at[j], (cx, cy, c)).start()

    def wait(i, o, ss, rs):
        c, chips, mine = views(o[0])
        for j, (cx, cy) in enumerate(chips):
            cp = _rcopy(mine, _half(o[0], (2 * cx + cy,), c, cols), ss.at[j], rs.at[j], (cx, cy, c))
            cp.wait_send()
            cp.wait_recv()

    return _Job([buf], [SDS(buf.shape, buf.dtype)], {0: 0}, 3, start, wait)


def _job_gather_pair(buf, cols=False):
    def views(o):
        x, y, c = _place()
        return c, (x, y, 1 - c), _other_chips(x, y)

    def start(i, o, ss, rs):
        c, sib, chips = views(o[0])
        for j, (cx, cy) in enumerate(chips):
            got = _half(o[0], (2 * cx + cy,), c, cols)
            _rcopy(got, got, ss.at[j], rs.at[j], sib).start()

    def wait(i, o, ss, rs):
        c, sib, chips = views(o[0])
        for j, (cx, cy) in enumerate(chips):
            cp = _rcopy(_half(o[0], (2 * cx + cy,), c, cols), _half(o[0], (2 * cx + cy,), 1 - c, cols),
                        ss.at[j], rs.at[j], sib)
            cp.wait_send()
            cp.wait_recv()

    return _Job([buf], [SDS(buf.shape, buf.dtype)], {0: 0}, 3, start, wait)


class _Gather:
    def __init__(self, bufs, by_cols=()):
        self.todo, self.half, self.done, self.by_cols = dict(bufs), {}, {}, set(by_cols)

    def comm(self, admit=()):
        second, first = list(self.half), list(admit)
        jobs = [_job_gather_pair(self.half[n], n in self.by_cols) for n in second]
        jobs += [_job_gather_ici(self.todo[n], n in self.by_cols) for n in first]

        def absorb(outs):
            for n, o in zip(second + first, outs):
                if n in self.half:
                    del self.half[n]
                    self.done[n] = o
                else:
                    del self.todo[n]
                    self.half[n] = o
        return (jobs, absorb) if jobs else None


def _half_shape(R, C, cols):
    return (R, C // 2) if cols else (R // 2, C)


def _job_rs_pair(g4, cols=False):
    _, R, C = g4.shape

    def desc(i, o, ss, rs):
        x, y, c = _place()
        return _rcopy(_half(i[0], (slice(None),), 1 - c, cols), o[0], ss.at[0], rs.at[0], (x, y, 1 - c))

    return _Job([g4], [SDS((N_CHIPS,) + _half_shape(R, C, cols), F32)], {}, 1,
                lambda i, o, ss, rs: desc(i, o, ss, rs).start(), lambda i, o, ss, rs: desc(i, o, ss, rs).wait())


def _job_rs_chips(p4):
    _, hR, C = p4.shape

    def descs(i, o, ss, rs):
        x, y, c = _place()
        return [_rcopy(i[0].at[2 * cx + cy], o[0].at[j], ss.at[j], rs.at[j], (cx, cy, c))
                for j, (cx, cy) in enumerate(_other_chips(x, y))]

    def start(i, o, ss, rs):
        for cp in descs(i, o, ss, rs):
            cp.start()

    def wait(i, o, ss, rs):
        for cp in descs(i, o, ss, rs):
            cp.wait_send()
            cp.wait_recv()

    return _Job([p4], [SDS((3, hR, C), p4.dtype)], {}, 3, start, wait)


def _job_rs_join(buf, cols=False):
    def desc(o, ss, rs, recv):
        x, y, c = _place()
        mine = _half(o[0], (), c, cols)
        return _rcopy(mine, _half(o[0], (), 1 - c, cols) if recv else mine, ss.at[0], rs.at[0], (x, y, 1 - c))

    def wait(i, o, ss, rs):
        cp = desc(o, ss, rs, True)
        cp.wait_send()
        cp.wait_recv()

    return _Job([buf], [SDS(buf.shape, buf.dtype)], {0: 0}, 1,
                lambda i, o, ss, rs: desc(o, ss, rs, False).start(), wait)


def _walk(R, C, cols):
    if cols:
        tc = _pick(C, 2 * LANE, LANE)
        return (R, tc), C // tc
    tr = _pick(R, max(16, (1 << 20) // C // 16 * 16), 16)
    return (tr, C), R // tr


def _rs_add_pair(name, g4, recv, idx, cols=False):
    _, R, C = g4.shape
    hshape = _half_shape(R, C, cols)
    blk, nb = _walk(*hshape, cols)
    at = (lambda k, i: (k, 0, i)) if cols else (lambda k, i: (k, i, 0))

    def body(s_ref, a_ref, b_ref, pb_ref, po_ref):
        s = a_ref[...] + b_ref[...]
        pb_ref[...] = s.astype(BF16)

        @pl.when(pl.program_id(1) == s_ref[1])
        def _():
            po_ref[...] = s

    gs = pltpu.PrefetchScalarGridSpec(
        num_scalar_prefetch=1, grid=(nb, N_CHIPS),
        in_specs=[pl.BlockSpec((None,) + blk, lambda i, k, s: at(k, s[0] * nb + i)),
                  pl.BlockSpec((None,) + blk, lambda i, k, s: at(k, i))],
        out_specs=[pl.BlockSpec((None,) + blk, lambda i, k, s: at(k, i)),
                   pl.BlockSpec(blk, lambda i, k, s: at(k, i)[1:])])
    return pl.pallas_call(body, name=name, grid_spec=gs,
                          out_shape=[SDS((N_CHIPS,) + hshape, BF16), SDS(hshape, F32)],
                          compiler_params=_params(("parallel", "arbitrary")))(idx, g4, recv)


def _rs_add_chips(name, own, recv3, idx, cols=False):
    hR, hC = own.shape
    blk, nb = _walk(hR, hC, cols)
    at = (lambda i: (0, i)) if cols else (lambda i: (i, 0))

    def body(s_ref, a_ref, b_ref, o_ref):
        o_ref[...] = ((a_ref[...] + b_ref[0].astype(F32)) + b_ref[1].astype(F32)) + b_ref[2].astype(F32)

    gs = pltpu.PrefetchScalarGridSpec(
        num_scalar_prefetch=1, grid=(nb,),
        in_specs=[pl.BlockSpec(blk, lambda i, s: at(i)), pl.BlockSpec((3,) + blk, lambda i, s: (0,) + at(i))],
        out_specs=pl.BlockSpec(blk, lambda i, s: at(s[0] * nb + i)))
    return pl.pallas_call(body, name=name, grid_spec=gs,
                          out_shape=SDS((hR, 2 * hC) if cols else (2 * hR, hC), F32),
                          compiler_params=_params(("parallel",)))(idx, own, recv3)


class _ReduceScatter:
    def __init__(self, idx):
        self.idx, self.items, self.done = idx, [], {}

    def push(self, tag, g4, cols=False):
        self.items.append([tag, 0, g4, cols])

    def comm(self, ici=1):
        cur, jobs = [], []
        for item in self.items:
            tag, stage, data, cols = item
            if stage == 1:
                if ici == 0:
                    continue
                ici -= 1
            cur.append(item)
            jobs.append(_job_rs_pair(data, cols) if stage == 0 else _job_rs_chips(data[0]) if stage == 1
                        else _job_rs_join(data, cols))

        def absorb(outs):
            for item, o in zip(cur, outs):
                tag, stage, data, cols = item
                if stage == 0:
                    item[1:3] = [1, _rs_add_pair(tag + "_rs_add2", data, o, self.idx, cols)]
                elif stage == 1:
                    item[1:3] = [2, _rs_add_chips(tag + "_rs_add4", data[1], o, self.idx, cols)]
                else:
                    self.items.remove(item)
                    self.done[tag] = o
        return (jobs, absorb) if jobs else None


def _resid_epilogue(accs, ex):
    return [ex[0] + ex[1] * accs[0], accs[0]]


def _sigmoid(x):
    return 0.5 * jnp.tanh(0.5 * x) + 0.5


def _swiglu_epilogue(accs, ex):
    gt, up = accs
    return [gt, up, gt * _sigmoid(gt) * up]


def _swiglu_bwd_epilogue(accs, ex):
    dact = accs[0]
    gt, up = ex[0].astype(F32), ex[1].astype(F32)
    sg = _sigmoid(gt)
    silu = gt * sg
    return [dact * up * (sg + silu * (1.0 - sg)), dact * silu]


def kernel(x, c, ada_w, ada_b, norm_mix_g, norm_ffn_g, a_w_in, a_b_in, a_ln_g, a_ln_b, a_w_s, a_b_s, a_w_out, b_w_in, b_b_f, b_w_out, ffn_w_gate, ffn_w_up, ffn_w_down, final_g, loss_target, m_ada_w, m_ada_b, m_norm_mix_g, m_norm_ffn_g, m_a_w_in, m_a_b_in, m_a_ln_g, m_a_ln_b, m_a_w_s, m_a_b_s, m_a_w_out, m_b_w_in, m_b_b_f, m_b_w_out, m_ffn_w_gate, m_ffn_w_up, m_ffn_w_down, m_final_g, v_ada_w, v_ada_b, v_norm_mix_g, v_norm_ffn_g, v_a_w_in, v_a_b_in, v_a_ln_g, v_a_ln_b, v_a_w_s, v_a_b_s, v_a_w_out, v_b_w_in, v_b_b_f, v_b_w_out, v_ffn_w_gate, v_ffn_w_up, v_ffn_w_down, v_final_g):
    S, D = x.shape[1], x.shape[2]
    H = D // LANE
    G = D // LANE
    FH = ffn_w_down.shape[1] * N_CHIPS
    depth = ada_w.shape[0]
    assert depth == 2 and a_w_in.shape[0] == 1 and b_w_in.shape[0] == 1
    mx, my, mc = _place()
    chip = 2 * mx + my
    dev = 4 * mx + 2 * my + mc
    x0 = x[0]
    target = loss_target[0]
    tr = _pick(S, 256, 8)
    tm = _pick(S, 512, 8)
    tq_f = _pick(S, 1024, LANE)
    tq = _pick(S, 512, LANE)
    nq = S // tq

    idx = jnp.stack([mc, chip]).astype(jnp.int32)
    t3 = lambda a: jnp.transpose(a, (2, 0, 1))
    shards = {"a_in": (a_w_in, 0), "a_out": (a_w_out, 0), "b_in": (t3(b_w_in), None), "b_out": (b_w_out, 0)}
    for l in range(depth):
        shards.update({f"wg{l}": (ffn_w_gate, l), f"wu{l}": (ffn_w_up, l), f"wd{l}": (ffn_w_down, l)})
    first = ["a_in", "a_out"]
    rest = [n for n in shards if n not in first]
    ag = _Gather(dict(zip(first, _cast_slabs("cast_first", [shards[n] for n in first], idx))), by_cols=["b_in"])
    ag.todo.update(zip(rest, _cast_slabs("cast_rest", [shards[n] for n in rest], idx, comm=ag.comm(first))))
    _comm_call("ag_first_pair", ag.comm(["wg0"]))
    wa_in4 = ag.done["a_in"]
    wa_out = ag.done["a_out"].reshape(D, D)

    dev_idx = dev.astype(jnp.int32).reshape(1)
    c_all = _gather_devices("ag_c", jnp.pad(c, ((0, 7), (0, 0))), dev_idx).reshape(N_DEV, 8, D)[:, 0, :]
    c_act = _silu_rows("c_silu", c_all)
    n_loc = ada_w.shape[2]
    mods = []
    for l in range(depth):
        b_loc = lax.dynamic_slice_in_dim(ada_b[l:l + 1], chip * n_loc, n_loc, axis=1)
        mods.append(_mm(f"ada_fwd{l}", "nn", [c_act], [ada_w], M=N_DEV, N=n_loc, K=D, tm=N_DEV, b_layer=l,
                        tn=_pick(n_loc, 1024, LANE), tk=D, extras=[("row", b_loc)],
                        epilogue=lambda accs, ex: [accs[0] + ex[0]])[0])
    mod_all = _gather_devices("ag_mod", jnp.concatenate(mods, axis=1), dev_idx)
    mod_mine = lax.dynamic_index_in_dim(mod_all[0::2], dev, axis=1, keepdims=False)
    mod_mine = mod_mine.reshape(N_CHIPS, depth, n_loc).transpose(1, 0, 2).reshape(depth, 1, N_MOD * D)
    mod = [[mod_mine[l, :, i * D:(i + 1) * D] for i in range(N_MOD)] for l in range(depth)]

    row = lambda a: a.reshape(1, -1)

    tril = jnp.tril(jnp.ones((LANE, LANE), dtype=bool))
    w_mask = jnp.where(tril[None], a_w_s[0], 0.0).astype(BF16)
    bias_full = jnp.repeat(a_b_s[0].T, LANE, axis=1)
    bf_pad = jnp.pad(b_b_f, ((0, 0), (0, LANE - H)))

    admit = {"sgu_in": ["wu0"], "mix_out0": ["wd0"], "ffn_in0": ["b_in", "b_out"], "ffn_out0": ["wg1"],
             "fox_qkv": ["wu1"], "attn_fwd": ["wd1"], "mix_out1": []}
    saved = []
    xs = x0
    for l in range(depth):
        sh1, sc1, g1, sh2, sc2, g2 = mod[l]
        st = {"x_in": xs}
        h1 = _normmod_fwd(f"normmod_mix{l}", xs, row(norm_mix_g[l]), sc1, sh1, tr)
        st["h1"] = h1
        if l == 0:
            a = _mm("sgu_in", "nn", [h1], [wa_in4], M=S, N=2 * D, K=D, tm=tm, tn=2 * D // N_CHIPS, tk=D,
                    b_stacked=True, extras=[("row", a_b_in)], comm=ag.comm(admit["sgu_in"]),
                    epilogue=lambda accs, ex: [accs[0] + ex[0]])[0]
            y = _sgu_mid_fwd(a, a_ln_g, a_ln_b, w_mask, bias_full, tr)
            st["a"], st["y"] = a, y
            w_o, mix_out = wa_out, y
        else:
            w_qkv = ag.done["b_in"].reshape(-1, D)
            w_f = jnp.pad(w_qkv[3 * D:], ((0, LANE - H), (0, 0)))
            qkv = _mm("fox_qkv", "nt", [h1], [w_qkv], M=S, N=3 * D, K=D, tm=tm, tn=_pick(3 * D, 1024, LANE),
                      tk=D, out_dtypes=(BF16,), comm=ag.comm(admit["fox_qkv"]))[0]
            fl = _mm("fox_f", "nt", [h1], [w_f], M=S, N=LANE, K=D, tm=tm, tn=LANE, tk=D)[0]
            F_sh = _fox_gate_fwd(fl, bf_pad)
            F_hs = F_sh[:, :H].T
            F_rows = F_hs.reshape(H, nq, 1, tq)
            o, lse = _attn_fwd(qkv, F_sh, F_hs.reshape(H, S // tq_f, 1, tq_f) * _LOG2E, H, tq_f,
                               comm=ag.comm(admit["attn_fwd"]))
            st.update(qkv=qkv, fl=fl, F_sh=F_sh, F_rows=F_rows, o=o, lse=lse, w_qkv=w_qkv, w_f=w_f)
            w_o, mix_out = ag.done["b_out"].reshape(D, D), o
        x1, out1 = _mm(f"mix_out{l}", "nn", [mix_out], [w_o], M=S, N=D, K=D, tm=tm, tn=_pick(D, 1024, LANE),
                       tk=D, extras=[("tile", xs), ("row", g1)], out_dtypes=(F32, BF16),
                       epilogue=_resid_epilogue, comm=ag.comm(admit[f"mix_out{l}"]))
        st["x_mid"], st["out1"] = x1, out1
        h2 = _normmod_fwd(f"normmod_ffn{l}", x1, row(norm_ffn_g[l]), sc2, sh2, tr)
        gt, up, act = _mm(f"ffn_in{l}", "nn", [h2, h2], [ag.done[f"wg{l}"], ag.done[f"wu{l}"]], M=S, N=FH, K=D,
                          tm=tm, tn=FH // N_CHIPS, tk=D, b_stacked=True, acc_of=[0, 1], n_acc=2,
                          out_dtypes=(BF16, BF16, BF16), epilogue=_swiglu_epilogue,
                          comm=ag.comm(admit.get(f"ffn_in{l}", ())))
        x2, out2 = _mm(f"ffn_out{l}", "nn", [act], [ag.done[f"wd{l}"].reshape(FH, D)], M=S, N=D, K=FH, tm=tm,
                       tn=_pick(D, 1024, LANE), tk=FH, extras=[("tile", x1), ("row", g2)],
                       out_dtypes=(F32, BF16), epilogue=_resid_epilogue,
                       comm=ag.comm(admit.get(f"ffn_out{l}", ())))
        st.update(h2=h2, gt=gt, up=up, act=act, out2=out2)
        saved.append(st)
        xs = x2
    assert not ag.todo and not ag.half
    wg4 = [ag.done[f"wg{l}"] for l in range(depth)]
    wu4 = [ag.done[f"wu{l}"] for l in range(depth)]
    wd = [ag.done[f"wd{l}"].reshape(FH, D) for l in range(depth)]
    wb_out = ag.done["b_out"].reshape(D, D)

    dx, dog, loss_vec, g_final, dg2 = _loss_head("loss_head", xs, target, row(final_g), tr,
                                                 nxt=(saved[-1]["out2"], mod[-1][5]))
    loss = lax.psum(loss_vec[0, 0], ("x", "y", "c"))

    rs = _ReduceScatter(idx)
    dmods = [None] * depth
    gmix = [None] * depth
    gffn = [None] * depth
    for l in reversed(range(depth)):
        sh1, sc1, g1, sh2, sc2, g2 = mod[l]
        st = saved[l]
        dgt, dup = _mm(f"ffn_dact{l}", "nt", [dog], [wd[l]], M=S, N=FH, K=D, tm=tm, tn=FH // N_CHIPS, tk=D,
                       extras=[("tile", st["gt"]), ("tile", st["up"])], out_dtypes=(BF16, BF16),
                       epilogue=_swiglu_bwd_epilogue, comm=rs.comm())
        rs.push(f"wd{l}", _mm(f"ffn_dwd{l}", "tn", [st["act"]], [dog], M=FH, N=D, K=S, tm=FH // N_CHIPS,
                              tn=_pick(D, 512, LANE), tk=S, outer="i",
                              comm=rs.comm())[0].reshape(N_CHIPS, FH // N_CHIPS, D))
        rs.push(f"wg{l}", _mm(f"ffn_dwg{l}", "tn", [st["h2"]], [dgt], M=D, N=FH, K=S, tm=_pick(D, 512, LANE),
                              tn=FH // N_CHIPS, tk=S, out_stacked=True, comm=rs.comm())[0])
        rs.push(f"wu{l}", _mm(f"ffn_dwu{l}", "tn", [st["h2"]], [dup], M=D, N=FH, K=S, tm=_pick(D, 512, LANE),
                              tn=FH // N_CHIPS, tk=S, out_stacked=True, comm=rs.comm())[0])
        dh2 = _mm(f"ffn_dh{l}", "nt", [dgt, dup], [wg4[l], wu4[l]], M=S, N=D, K=FH, tm=_pick(S, 512, 8),
                  tn=_pick(D, 1024, LANE), tk=FH, b_stacked=True, comm=rs.comm())[0]
        dx, dog, dsh2, dsc2, gffn[l], dg1 = _normmod_bwd(f"normmod_ffn_bwd{l}", st["x_mid"], dh2, dx,
                                                         row(norm_ffn_g[l]), sc2, tr, nxt=(st["out1"], g1),
                                                         comm=rs.comm(ici=0))
        if l == 0:
            dy = _mm("sgu_dy", "nt", [dog], [wa_out], M=S, N=D, K=D, tm=tm, tn=_pick(D, 1024, LANE), tk=D,
                     out_dtypes=(BF16,))[0]
            rs.push("a_out", _mm("sgu_dwout", "tn", [st["y"]], [dog], M=D, N=D, K=S, tm=_pick(D, 512, LANE),
                                 tn=_pick(D, 1024, LANE),
                                 tk=S)[0].reshape(N_CHIPS, D // N_CHIPS, D))
            da, dws, dbias, g_ln_g, g_ln_b, g_b_in = _sgu_mid_bwd(st["a"], dy, a_ln_g, a_ln_b, w_mask, bias_full, tr,
                                                                  comm=rs.comm())
            rs.push("a_in", _mm("sgu_dwin", "tn", [st["h1"]], [da], M=D, N=2 * D, K=S, tm=_pick(D, 512, LANE),
                                tn=2 * D // N_CHIPS, tk=S, out_stacked=True, comm=rs.comm())[0])
            dh1 = _mm("sgu_dh", "nt", [da], [wa_in4], M=S, N=D, K=2 * D, tm=_pick(S, 512, 8), tn=_pick(D, 1024, LANE),
                      tk=2 * D, b_stacked=True, comm=rs.comm())[0]
            g_w_s = jnp.where(tril[None], dws, 0.0)
            g_b_s = jnp.sum(dbias.reshape(LANE, G, LANE), axis=2).T
        else:
            do = _mm("fox_do", "nt", [dog], [wb_out], M=S, N=D, K=D, tm=tm, tn=_pick(D, 1024, LANE), tk=D,
                     out_dtypes=(BF16,))[0]
            rs.push("b_out", _mm("fox_dwout", "tn", [st["o"]], [dog], M=D, N=D, K=S, tm=_pick(D, 512, LANE),
                                 tn=_pick(D, 1024, LANE),
                                 tk=S)[0].reshape(N_CHIPS, D // N_CHIPS, D))
            delta = _head_dots(do, st["o"], H, tr)
            delta_rows = delta[:, :H].T.reshape(H, nq, 1, tq)
            A_rows = (st["F_rows"] - st["lse"].reshape(H, nq, 1, tq)) * _LOG2E
            w_qkv, w_f = st["w_qkv"], st["w_f"]
            dq, dk, dv, dF_k, dF_q = _attn_bwd(st["qkv"], do, st["F_sh"], A_rows, delta_rows, H, tq,
                                               comm=rs.comm(ici=2))
            dF_sh = jnp.pad((dF_k.reshape(H, S) + dF_q.reshape(H, S)).T, ((0, 0), (0, LANE - H)))
            dfl, db_f = _fox_gate_bwd(st["fl"], dF_sh, bf_pad)
            dfl_b = dfl.astype(BF16)
            dh_f = _mm("fox_dh_f", "nn", [dfl_b], [w_f], M=S, N=D, K=LANE, tm=tm, tn=_pick(D, 1024, LANE),
                       tk=LANE)[0]
            dh1 = _mm("fox_dh", "nn", [dq, dk, dv], [w_qkv, w_qkv, w_qkv], M=S, N=D, K=D, tm=_pick(S, 256, 8),
                      tn=_pick(D, 1024, LANE), tk=D, b_koffs=[0, 1, 2],
                      extras=[("tile", dh_f)], epilogue=lambda accs, ex: [accs[0] + ex[0]], comm=rs.comm())[0]
            tmw = _pick(D, 512, LANE)
            g_bT = None
            for p, (nm, d_) in enumerate((("q", dq), ("k", dk), ("v", dv))):
                g_bT = _mm(f"fox_dw{nm}", "tn", [d_], [st["h1"]], M=D, N=D, K=S, tm=tmw, tn=_pick(D, 1024, LANE),
                           tk=S, out_into=(3 * D + H, p * (D // tmw), g_bT), comm=rs.comm(ici=0))[0]
            dwf = _mm("fox_dwf", "tn", [dfl_b], [st["h1"]], M=LANE, N=D, K=S, tm=LANE, tn=_pick(D, 1024, LANE),
                      tk=S)[0]
            g_bT = lax.dynamic_update_slice(g_bT, dwf[:H], (3 * D, 0))
            rs.push("b_in", g_bT.reshape(N_CHIPS, -1, D), cols=True)
            g_b_f = db_f[:, :H]
        below = (saved[l - 1]["out2"], mod[l - 1][5]) if l else None
        res = _normmod_bwd(f"normmod_mix_bwd{l}", st["x_in"], dh1, dx, row(norm_mix_g[l]), sc1, tr, nxt=below,
                           comm=rs.comm(ici=1 - l))
        if l:
            dx, dog_below, dsh1, dsc1, gmix[l], dg2_below = res
        else:
            dx, dsh1, dsc1, gmix[l] = res
        dmods[l] = jnp.concatenate([dsh1, dsc1, dg1, dsh2, dsc2, dg2], axis=1)
        if l:
            dog, dg2 = dog_below, dg2_below
    grad_x = dx[None]

    small = [jnp.concatenate(dmods, axis=0), jnp.concatenate(gmix, axis=0), jnp.concatenate(gffn, axis=0),
             g_b_in, g_ln_g, g_ln_b, g_w_s[None], g_b_s[None], g_b_f, g_final[0]]
    small_w = [ada_b, norm_mix_g, norm_ffn_g, a_b_in, a_ln_g, a_ln_b, a_w_s, a_b_s, b_b_f, final_g]
    small_m = [m_ada_b, m_norm_mix_g, m_norm_ffn_g, m_a_b_in, m_a_ln_g, m_a_ln_b, m_a_w_s, m_a_b_s, m_b_b_f, m_final_g]
    small_v = [v_ada_b, v_norm_mix_g, v_norm_ffn_g, v_a_b_in, v_a_ln_g, v_a_ln_b, v_a_w_s, v_a_b_s, v_b_b_f, v_final_g]
    sizes = [w.size for w in small_w]
    total = sum(sizes)
    padded = -(-total // (8 * LANE)) * (8 * LANE)

    def pack(parts):
        flat = jnp.concatenate([p.reshape(-1) for p in parts])
        return jnp.pad(flat, (0, padded - total)).reshape(padded // LANE, LANE)

    def unpack(buf):
        flat = buf.reshape(-1)
        outs, off = [], 0
        for w, n in zip(small_w, sizes):
            outs.append(flat[off:off + n].reshape(w.shape))
            off += n
        return outs

    g_all = _gather_devices("ag_small_grads", pack(small), dev_idx, extra=rs.comm())
    g_small = _sum_slots("sum_small_grads", g_all)
    _, d_small, m_small, v_small = _adamw("adamw_small", pack(small_w), g_small, pack(small_m), pack(small_v))
    sg, sd, sm, sv_ = unpack(g_small), unpack(d_small), unpack(m_small), unpack(v_small)

    n_dm = depth * N_MOD * D
    dmod_all = g_all.reshape(N_DEV, -1)[:, :n_dm].reshape(N_DEV, depth, N_MOD * D)
    u_ada = None
    for l in range(depth):
        dm_loc = lax.dynamic_slice_in_dim(dmod_all[:, l, :], chip * n_loc, n_loc, axis=1).astype(BF16)
        g_ada = _mm(f"ada_dw{l}", "tn", [c_act], [dm_loc], M=D, N=n_loc, K=N_DEV, tm=_pick(D, 512, LANE),
                    tn=_pick(n_loc, 1024, LANE), tk=N_DEV)[0]
        u_ada = _adamw(f"adamw_ada_w{l}", ada_w, g_ada, m_ada_w, v_ada_w, layer=l, prev=u_ada)

    flushes = 0
    while rs.items:
        _comm_call(f"rs_flush{flushes}", rs.comm())
        flushes += 1
    groups = {"a_in": (a_w_in, m_a_w_in, v_a_w_in), "a_out": (a_w_out, m_a_w_out, v_a_w_out),
              "b_out": (b_w_out, m_b_w_out, v_b_w_out),
              "wg": (ffn_w_gate, m_ffn_w_gate, v_ffn_w_gate), "wu": (ffn_w_up, m_ffn_w_up, v_ffn_w_up),
              "wd": (ffn_w_down, m_ffn_w_down, v_ffn_w_down)}
    upd = {}
    for n, (w_, m_, v_) in groups.items():
        for l in range(w_.shape[0]):
            tag = n if n in rs.done else f"{n}{l}"
            upd[n] = _adamw("adamw_" + tag, w_, rs.done[tag], m_, v_, layer=l, prev=upd.get(n))
    u_a_in, u_a_out, u_b_out, u_wg, u_wu, u_wd = (upd[n] for n in groups)
    u_b_in = [jnp.transpose(t, (1, 2, 0)) for t in
              _adamw("adamw_b_in", t3(b_w_in), rs.done["b_in"], t3(m_b_w_in), t3(v_b_w_in))]

    def leaves(i, small_list):
        s = small_list
        return [u_ada[i], s[0], s[1], s[2], u_a_in[i], s[3], s[4], s[5], s[6], s[7], u_a_out[i],
                u_b_in[i], s[8], u_b_out[i], u_wg[i], u_wu[i], u_wd[i], s[9]]

    return (loss, grad_x, *leaves(0, sg), *leaves(1, sd), *leaves(2, sm), *leaves(3, sv_))
```

```python
import functools
import math

import jax
import jax.numpy as jnp
from jax import lax
from jax.experimental import pallas as pl
from jax.experimental.pallas import tpu as pltpu

F32, BF16 = jnp.float32, jnp.bfloat16
LANE = 128
N_CHIPS = 4
N_DEV = 8
N_MOD = 6
EPS = 1e-6
VMEM_LIMIT = 60 * 1024 * 1024
ADAM_LR, ADAM_B1, ADAM_B2, ADAM_EPS, ADAM_WD, ADAM_STEP = 0.001, 0.9, 0.999, 1e-08, 0.01, 10
MESH = pl.DeviceIdType.MESH
ANY = pl.BlockSpec(memory_space=pl.ANY)
SDS = jax.ShapeDtypeStruct


def _pick(dim, pref, align):
    t = min(dim, pref)
    t -= t % align
    while t >= align:
        if dim % t == 0:
            return t
        t -= align
    return dim


def _params(sem=None):
    return pltpu.CompilerParams(dimension_semantics=sem, vmem_limit_bytes=VMEM_LIMIT)


class _Job:
    def __init__(self, ins, outs, aliases, nsem, start, wait):
        self.ins, self.outs, self.aliases, self.nsem, self.start, self.wait = ins, outs, aliases, nsem, start, wait


def _call(body, name, *, grid, in_specs, out_specs, out_shape, args, scratch=(), sem=None, comm=None,
          prefetch=None, keep=()):
    jobs, absorb = comm if comm else ([], None)
    n_pre = 0 if prefetch is None else 1
    n_in, n_out, n_scr = len(args), len(out_shape), len(scratch)
    c_ins = [a for a, _ in keep] + [a for j in jobs for a in j.ins]
    c_outs = [s for j in jobs for s in j.outs]
    aliases = {n_pre + n_in + k: o for k, (_, o) in enumerate(keep)}
    c_scr = []
    i_off, o_off = len(keep), 0
    for j in jobs:
        for a, b in j.aliases.items():
            aliases[n_pre + n_in + i_off + a] = n_out + o_off + b
        i_off += len(j.ins)
        o_off += len(j.outs)
        c_scr += [pltpu.SemaphoreType.DMA((j.nsem,)), pltpu.SemaphoreType.DMA((j.nsem,))]

    def wrapped(*refs):
        bounds = [n_pre + n_in, len(c_ins), n_out, len(c_outs), n_scr]
        parts, p = [], 0
        for n in bounds:
            parts.append(refs[p:p + n])
            p += n
        main_in, cin, main_out, cout, main_scr = parts
        cin = cin[len(keep):]
        csem = refs[p:]

        def run(phase):
            a = b = 0
            for k, j in enumerate(jobs):
                fn = j.start if phase == 0 else j.wait
                fn(cin[a:a + len(j.ins)], cout[b:b + len(j.outs)], csem[2 * k], csem[2 * k + 1])
                a += len(j.ins)
                b += len(j.outs)

        if jobs and grid:
            first = functools.reduce(jnp.logical_and, [pl.program_id(d) == 0 for d in range(len(grid))])
            last = functools.reduce(jnp.logical_and, [pl.program_id(d) == grid[d] - 1 for d in range(len(grid))])
            pl.when(first)(lambda: run(0))
            body(*main_in, *main_out, *main_scr)
            pl.when(last)(lambda: run(1))
        elif jobs:
            run(0)
            body(*main_in, *main_out, *main_scr)
            run(1)
        else:
            body(*main_in, *main_out, *main_scr)

    specs = dict(grid=grid, in_specs=list(in_specs) + [ANY] * len(c_ins),
                 out_specs=list(out_specs) + [ANY] * len(c_outs), scratch_shapes=list(scratch) + c_scr)
    if n_pre:
        specs = dict(grid_spec=pltpu.PrefetchScalarGridSpec(num_scalar_prefetch=1, **specs))
    outs = pl.pallas_call(
        wrapped, name=name, out_shape=list(out_shape) + c_outs, input_output_aliases=aliases,
        compiler_params=_params(("arbitrary",) * len(grid) if jobs else sem), **specs,
    )(*([prefetch] if n_pre else []), *args, *c_ins)
    if jobs:
        absorb(list(outs[n_out:]))
    return list(outs[:n_out])


def _comm_call(name, comm):
    _call(lambda: None, name, grid=(), in_specs=[], out_specs=[], out_shape=[], args=[], comm=comm)


def _mm(name, form, a_list, b_list, *, M, N, K, tm, tn, tk, b_stacked=False, out_stacked=False,
        b_koffs=None, acc_of=None, n_acc=1, extras=(), out_dtypes=(F32,), epilogue=None, comm=None,
        outer="j", b_layer=None, out_into=None):
    assert M % tm == 0 and N % tn == 0 and K % tk == 0, (name, M, N, K, tm, tn, tk)
    nm, nn, nk = M // tm, N // tn, K // tk
    npairs = len(a_list)
    acc_of = acc_of or [0] * npairs
    b_koffs = b_koffs or [0] * npairs
    if epilogue is None:
        epilogue = lambda accs, ex: [accs[0]]

    def spec(shape, fn, **kw):
        return pl.BlockSpec(shape, fn if outer == "j" else (lambda i, j, k: fn(j, i, k)), **kw)

    slabs = form == "nt" and b_stacked and tk == K

    a_uniq = [a for p, a in enumerate(a_list) if all(a is not b for b in a_list[:p])]
    a_pos = [next(u for u, b in enumerate(a_uniq) if b is a) for a in a_list]
    in_specs = []
    for _ in a_uniq:
        if form == "tn":
            in_specs.append(spec((tk, tm), lambda j, i, k: (k, i)))
        else:
            in_specs.append(spec((tm, tk), lambda j, i, k: (i, k)))
    for off in b_koffs:
        if b_layer is not None:
            assert form == "nn" and not b_stacked
            in_specs.append(spec((None, tk, tn), lambda j, i, k: (b_layer, k, j)))
        elif form == "nn":
            if b_stacked:
                assert tn * N_CHIPS == N
                in_specs.append(spec((None, tk, tn), lambda j, i, k: (j, k, 0)))
            else:
                in_specs.append(spec((tk, tn), lambda j, i, k, off=off: (off + k, j)))
        elif form == "nt":
            if slabs:
                in_specs.append(spec((N_CHIPS, tn, K // N_CHIPS), lambda j, i, k: (0, j, 0),
                                     pipeline_mode=pl.Buffered(1)))
            elif b_stacked:
                assert tk * N_CHIPS == K
                in_specs.append(spec((None, tn, tk), lambda j, i, k: (k, j, 0)))
            else:
                in_specs.append(spec((tn, tk), lambda j, i, k, off=off: (j, off + k)))
        else:
            in_specs.append(spec((tk, tn), lambda j, i, k: (k, j)))
    for kind, _ in extras:
        if kind == "tile":
            in_specs.append(spec((tm, tn), lambda j, i, k: (i, j)))
        else:
            in_specs.append(spec((1, tn), lambda j, i, k: (0, j)))
    if out_stacked:
        assert tn * N_CHIPS == N
        out_specs = [spec((None, tm, tn), lambda j, i, k: (j, i, 0)) for _ in out_dtypes]
        out_shape = [SDS((N_CHIPS, M, tn), d) for d in out_dtypes]
    else:
        rows, roff, prev = out_into if out_into else (M, 0, None)
        out_specs = [spec((tm, tn), lambda j, i, k: (roff + i, j)) for _ in out_dtypes]
        out_shape = [SDS((rows, N), d) for d in out_dtypes]
    dims = {"nn": (((1,), (0,)), ((), ())), "nt": (((1,), (1,)), ((), ())), "tn": (((0,), (0,)), ((), ()))}[form]
    n_a, n_ex, n_out = len(a_uniq), len(extras), len(out_dtypes)

    def body(*refs):
        a_vals = [refs[u][...].astype(BF16) for u in range(n_a)]
        b_refs = refs[n_a:n_a + npairs]
        e_refs = refs[n_a + npairs:n_a + npairs + n_ex]
        o_refs = refs[n_a + npairs + n_ex:n_a + npairs + n_ex + n_out]
        acc_refs = refs[n_a + npairs + n_ex + n_out:]

        tot = [None] * n_acc
        for p in range(npairs):
            if slabs:
                n = K // N_CHIPS
                d = sum(lax.dot_general(a_vals[a_pos[p]][:, s * n:(s + 1) * n], b_refs[p][s], dims,
                                        preferred_element_type=F32) for s in range(N_CHIPS))
            else:
                d = lax.dot_general(a_vals[a_pos[p]], b_refs[p][...].astype(BF16), dims, preferred_element_type=F32)
            tot[acc_of[p]] = d if tot[acc_of[p]] is None else tot[acc_of[p]] + d

        def finish(accs):
            outs = epilogue(accs, [e[...] for e in e_refs])
            for o_ref, o in zip(o_refs, outs):
                o_ref[...] = o.astype(o_ref.dtype)

        if nk == 1:
            finish(tot)
        else:
            k = pl.program_id(2)

            @pl.when(k == 0)
            def _():
                for r, t in zip(acc_refs, tot):
                    r[...] = t

            @pl.when(k > 0)
            def _():
                for r, t in zip(acc_refs, tot):
                    r[...] += t

            @pl.when(k == nk - 1)
            def _():
                finish([r[...] for r in acc_refs])

    scratch = [pltpu.VMEM((tm, tn), F32) for _ in range(n_acc)] if nk > 1 else []
    return _call(body, name, grid=(nn, nm, nk) if outer == "j" else (nm, nn, nk), in_specs=in_specs,
                 out_specs=out_specs, out_shape=out_shape, scratch=scratch,
                 sem=("parallel", "parallel", "arbitrary"), comm=comm,
                 keep=[(out_into[2], 0)] if out_into and out_into[2] is not None else (),
                 args=[*a_uniq, *b_list, *[e for _, e in extras]])


def _rowwise(name, fn, rows, vecs, row_outs, acc_widths, tr, comm=None):
    S = rows[0].shape[0]
    assert S % tr == 0
    nr, nv, no, na = len(rows), len(vecs), len(row_outs), len(acc_widths)

    def body(*refs):
        r = [x[...] for x in refs[:nr]]
        v = [x[...] for x in refs[nr:nr + nv]]
        o_refs = refs[nr + nv:nr + nv + no]
        a_refs = refs[nr + nv + no:]
        outs, accs = fn(r, v)
        for o_ref, o in zip(o_refs, outs):
            o_ref[...] = o.astype(o_ref.dtype)
        first = pl.program_id(0) == 0

        @pl.when(first)
        def _():
            for a_ref, a in zip(a_refs, accs):
                a_ref[...] = a

        @pl.when(jnp.logical_not(first))
        def _():
            for a_ref, a in zip(a_refs, accs):
                a_ref[...] += a

    in_specs = [pl.BlockSpec((tr, x.shape[1]), lambda i: (i, 0)) for x in rows]
    in_specs += [pl.BlockSpec(x.shape, lambda i, nd=x.ndim: (0,) * nd) for x in vecs]
    out_specs = [pl.BlockSpec((tr, w), lambda i: (i, 0)) for w, _ in row_outs]
    out_specs += [pl.BlockSpec((1, w), lambda i: (0, 0)) for w in acc_widths]
    out_shape = [SDS((S, w), d) for w, d in row_outs] + [SDS((1, w), F32) for w in acc_widths]
    return _call(body, name, grid=(S // tr,), in_specs=in_specs, out_specs=out_specs, out_shape=out_shape,
                 sem=("arbitrary",), comm=comm, args=[*rows, *vecs])


def _colsum(a):
    return jnp.sum(a, axis=0, keepdims=True)


def _rms_stats(x):
    rstd = lax.rsqrt(jnp.mean(x * x, axis=1, keepdims=True) + EPS)
    return x * rstd, rstd


def _normmod_fwd(name, x, g, sc, sh, tr):
    def fn(r, v):
        n, _ = _rms_stats(r[0])
        return [(n * v[0]) * (1.0 + v[1]) + v[2]], []
    return _rowwise(name, fn, [x], [g, sc, sh], [(x.shape[1], BF16)], [], tr)[0]


def _gated(dx, nxt_out, gate):
    return gate * dx, _colsum(dx * nxt_out.astype(F32))


def _normmod_bwd(name, x, dh, dres, g, sc, tr, nxt=None, comm=None):
    def fn(r, v):
        x_, dh_, dres_ = r[:3]
        g_, sc_ = v[:2]
        n, rstd = _rms_stats(x_)
        hn = n * g_
        dhn = dh_ * (1.0 + sc_)
        dn = dhn * g_
        dx = dres_ + rstd * (dn - n * jnp.mean(dn * n, axis=1, keepdims=True))
        outs, accs = [dx], [_colsum(dh_), _colsum(dh_ * hn), _colsum(dhn * n)]
        if nxt:
            dog, dgate = _gated(dx, r[3], v[2])
            outs.append(dog)
            accs.append(dgate)
        return outs, accs
    D = x.shape[1]
    rows, vecs = [x, dh, dres] + ([nxt[0]] if nxt else []), [g, sc] + ([nxt[1]] if nxt else [])
    return _rowwise(name, fn, rows, vecs, [(D, F32)] + ([(D, BF16)] if nxt else []), [D] * (4 if nxt else 3), tr,
                    comm=comm)


def _loss_head(name, x, target, g, tr, nxt):
    D = x.shape[1]

    def fn(r, v):
        n, rstd = _rms_stats(r[0])
        err = n * v[0] - r[1]
        loss = 0.5 * jnp.sum(jnp.mean(err * err, axis=1, keepdims=True), axis=0, keepdims=True)
        dy = err * (1.0 / D)
        dn = dy * v[0]
        dx = rstd * (dn - n * jnp.mean(dn * n, axis=1, keepdims=True))
        dog, dgate = _gated(dx, r[2], v[1])
        return [dx, dog], [jnp.broadcast_to(loss, (1, LANE)), _colsum(dy * n), dgate]
    return _rowwise(name, fn, [x, target, nxt[0]], [g, nxt[1]], [(D, F32), (D, BF16)], [LANE, D, D], tr)


_INV_SQRT2 = 1.0 / math.sqrt(2.0)
_INV_SQRT2PI = 1.0 / math.sqrt(2.0 * math.pi)


def _gelu(a):
    return 0.5 * a * (1.0 + lax.erf(a * _INV_SQRT2))


def _gelu_grad(a):
    return 0.5 * (1.0 + lax.erf(a * _INV_SQRT2)) + a * jnp.exp(-0.5 * a * a) * _INV_SQRT2PI


def _ln_stats(v):
    mu = jnp.mean(v, axis=1, keepdims=True)
    vc = v - mu
    rstd = lax.rsqrt(jnp.mean(vc * vc, axis=1, keepdims=True) + EPS)
    return vc * rstd, rstd


def _chunk_mix(w_ref, vn, tr, G):
    rows = []
    for ch in range(tr // LANE):
        cols = []
        for g in range(G):
            blk = vn[ch * LANE:(ch + 1) * LANE, g * LANE:(g + 1) * LANE]
            cols.append(jnp.dot(w_ref[g], blk, preferred_element_type=F32))
        rows.append(jnp.concatenate(cols, axis=1))
    return jnp.concatenate(rows, axis=0)


def _sgu_mid_fwd(a, ln_g, ln_b, w_mask, bias_full, tr):
    S, D2 = a.shape
    D = D2 // 2
    G = D // LANE

    def body(a_ref, g_ref, b_ref, w_ref, bias_ref, y_ref):
        u = _gelu(a_ref[:, :D])
        v = _gelu(a_ref[:, D:])
        vhat, _ = _ln_stats(v)
        vn = (vhat * g_ref[...] + b_ref[...]).astype(BF16)
        sv = _chunk_mix(w_ref, vn, tr, G) + jnp.concatenate([bias_ref[...]] * (tr // LANE), axis=0)
        y_ref[...] = (u * sv).astype(BF16)

    return pl.pallas_call(
        body, name="sgu_mid_fwd", grid=(S // tr,),
        in_specs=[pl.BlockSpec((tr, D2), lambda i: (i, 0)), pl.BlockSpec((1, D), lambda i: (0, 0)),
                  pl.BlockSpec((1, D), lambda i: (0, 0)), pl.BlockSpec((G, LANE, LANE), lambda i: (0, 0, 0)),
                  pl.BlockSpec((LANE, D), lambda i: (0, 0))],
        out_specs=pl.BlockSpec((tr, D), lambda i: (i, 0)), out_shape=SDS((S, D), BF16),
        compiler_params=_params(("arbitrary",)))(a, ln_g, ln_b, w_mask, bias_full)


def _sgu_mid_bwd(a, dy, ln_g, ln_b, w_mask, bias_full, tr, comm=None):
    S, D2 = a.shape
    D = D2 // 2
    G = D // LANE
    nch = tr // LANE

    def body(a_ref, dy_ref, g_ref, b_ref, w_ref, bias_ref, da_ref, dw_ref, dbias_ref, dg_ref, db_ref, dbin_ref):
        au, av = a_ref[:, :D], a_ref[:, D:]
        u = _gelu(au)
        v = _gelu(av)
        vhat, rstd = _ln_stats(v)
        vn = (vhat * g_ref[...] + b_ref[...]).astype(BF16)
        sv = _chunk_mix(w_ref, vn, tr, G) + jnp.concatenate([bias_ref[...]] * nch, axis=0)
        dy_ = dy_ref[...].astype(F32)
        du = dy_ * sv
        dsv = dy_ * u
        dsv_b = dsv.astype(BF16)
        first = pl.program_id(0) == 0

        @pl.when(first)
        def _():
            dw_ref[...] = jnp.zeros_like(dw_ref)
            dbias_ref[...] = jnp.zeros_like(dbias_ref)
            dg_ref[...] = jnp.zeros_like(dg_ref)
            db_ref[...] = jnp.zeros_like(db_ref)
            dbin_ref[...] = jnp.zeros_like(dbin_ref)

        rows = []
        dbias = None
        for ch in range(nch):
            r0 = ch * LANE
            cols = []
            for g in range(G):
                c0 = g * LANE
                ds_blk = dsv_b[r0:r0 + LANE, c0:c0 + LANE]
                vn_blk = vn[r0:r0 + LANE, c0:c0 + LANE]
                cols.append(lax.dot_general(w_ref[g], ds_blk, (((0,), (0,)), ((), ())),
                                            preferred_element_type=F32))
                dw_ref[g] += lax.dot_general(ds_blk, vn_blk, (((1,), (1,)), ((), ())),
                                             preferred_element_type=F32)
            rows.append(jnp.concatenate(cols, axis=1))
            blk = dsv[r0:r0 + LANE, :]
            dbias = blk if dbias is None else dbias + blk
        dvn = jnp.concatenate(rows, axis=0)
        dbias_ref[...] += dbias
        dg_ref[...] += _colsum(dvn * vhat)
        db_ref[...] += _colsum(dvn)
        dvh = dvn * g_ref[...]
        dv = rstd * (dvh - jnp.mean(dvh, axis=1, keepdims=True)
                     - vhat * jnp.mean(dvh * vhat, axis=1, keepdims=True))
        da_u = du * _gelu_grad(au)
        da_v = dv * _gelu_grad(av)
        da_ref[:, :D] = da_u.astype(BF16)
        da_ref[:, D:] = da_v.astype(BF16)
        dbin_ref[:, :D] += _colsum(da_u)
        dbin_ref[:, D:] += _colsum(da_v)

    full = lambda shp: pl.BlockSpec(shp, lambda i, nd=len(shp): (0,) * nd)
    return _call(
        body, "sgu_mid_bwd", grid=(S // tr,),
        in_specs=[pl.BlockSpec((tr, D2), lambda i: (i, 0)), pl.BlockSpec((tr, D), lambda i: (i, 0)),
                  full((1, D)), full((1, D)), full((G, LANE, LANE)), full((LANE, D))],
        out_specs=[pl.BlockSpec((tr, D2), lambda i: (i, 0)), full((G, LANE, LANE)), full((LANE, D)),
                   full((1, D)), full((1, D)), full((1, D2))],
        out_shape=[SDS((S, D2), BF16), SDS((G, LANE, LANE), F32), SDS((LANE, D), F32),
                   SDS((1, D), F32), SDS((1, D), F32), SDS((1, D2), F32)],
        sem=("arbitrary",), comm=comm, args=[a, dy, ln_g, ln_b, w_mask, bias_full])


def _tri(lower):
    r = lax.broadcasted_iota(jnp.int32, (LANE, LANE), 0)
    c = lax.broadcasted_iota(jnp.int32, (LANE, LANE), 1)
    return jnp.where((c <= r) if lower else (c >= r), 1.0, 0.0).astype(F32)


def _fox_gate_fwd(fl, bf_pad):
    S = fl.shape[0]
    nblk = S // LANE

    def body(fl_ref, b_ref, f_ref):
        tri = _tri(True)

        def step(i, carry):
            r0 = pl.multiple_of(i * LANE, LANE)
            z = fl_ref[pl.ds(r0, LANE), :] + b_ref[...]
            logf = jnp.minimum(z, 0.0) - jnp.log1p(jnp.exp(-jnp.abs(z)))
            f = jnp.dot(tri, logf, preferred_element_type=F32, precision=lax.Precision.HIGHEST) + carry
            f_ref[pl.ds(r0, LANE), :] = f
            return f[LANE - 1:LANE, :]
        lax.fori_loop(0, nblk, step, jnp.zeros((1, LANE), F32))

    return pl.pallas_call(body, name="fox_gate_fwd", out_shape=SDS((S, LANE), F32),
                          compiler_params=_params())(fl, bf_pad)


def _fox_gate_bwd(fl, dF, bf_pad):
    S = fl.shape[0]
    nblk = S // LANE

    def body(fl_ref, df_ref, b_ref, dfl_ref, db_ref):
        tri = _tri(True)

        def prefix(i, carry):
            r0 = pl.multiple_of(i * LANE, LANE)
            p = jnp.dot(tri, df_ref[pl.ds(r0, LANE), :], preferred_element_type=F32,
                        precision=lax.Precision.HIGHEST) + carry
            dfl_ref[pl.ds(r0, LANE), :] = p
            return p[LANE - 1:LANE, :]
        total = lax.fori_loop(0, nblk, prefix, jnp.zeros((1, LANE), F32))
        db_ref[...] = jnp.zeros_like(db_ref)

        def finish(i, carry):
            r0 = pl.multiple_of(i * LANE, LANE)
            dlogf = total - dfl_ref[pl.ds(r0, LANE), :] + df_ref[pl.ds(r0, LANE), :]
            z = fl_ref[pl.ds(r0, LANE), :] + b_ref[...]
            dfl = dlogf / (1.0 + jnp.exp(z))
            dfl_ref[pl.ds(r0, LANE), :] = dfl
            db_ref[...] += _colsum(dfl)
            return carry
        lax.fori_loop(0, nblk, finish, 0)

    return pl.pallas_call(body, name="fox_gate_bwd", out_shape=[SDS((S, LANE), F32), SDS((1, LANE), F32)],
                          compiler_params=_params())(fl, dF, bf_pad)


def _lane_pick(blk, h):
    lane = lax.broadcasted_iota(jnp.int32, blk.shape, 1)
    return jnp.sum(jnp.where(lane == h, blk, 0.0), axis=1, keepdims=True)


_NEG = -1e30
_LOG2E = 1.0 / math.log(2.0)
_LN2 = math.log(2.0)
_SUB = 32


def _attn_fwd(qkv, F_sh, F_rows, H, tq, comm=None):
    S = qkv.shape[0]
    D = H * LANE
    nq = S // tq
    scale = 1.0 / math.sqrt(LANE)

    def body(q_ref, k_ref, v_ref, fsh_ref, frow_ref, o_ref, lse_ref):
        h, i = pl.program_id(0), pl.program_id(1)
        q = q_ref[...]
        fq = _lane_pick(fsh_ref[...], h) * _LOG2E

        def block(j, carry, diagonal):
            m, l, acc = carry
            r0 = pl.multiple_of(j * tq, tq)
            k = k_ref[pl.ds(r0, tq), :]
            v = v_ref[pl.ds(r0, tq), :]
            s = lax.dot_general(q, k, (((1,), (1,)), ((), ())), preferred_element_type=F32)
            s = s + (fq - frow_ref[j])
            if diagonal:
                row = lax.broadcasted_iota(jnp.int32, (tq, tq), 0)
                col = lax.broadcasted_iota(jnp.int32, (tq, tq), 1)
                s = jnp.where(col <= row, s, _NEG)
            m_new = jnp.maximum(m, jnp.max(s, axis=1, keepdims=True))
            alpha = jnp.exp2(m - m_new)
            p = jnp.exp2(s - m_new)
            l = alpha * l + jnp.sum(p, axis=1, keepdims=True)
            acc = alpha * acc + jnp.dot(p.astype(BF16), v, preferred_element_type=F32)
            return m_new, l, acc

        init = (jnp.full((tq, 1), _NEG, F32), jnp.zeros((tq, 1), F32), jnp.zeros((tq, LANE), F32))
        carry = lax.fori_loop(0, i, lambda j, cr: block(j, cr, False), init)
        m, l, acc = block(i, carry, True)
        o_ref[...] = (acc / l).astype(BF16)
        lse_ref[...] = (m + jnp.log2(l)) * _LN2

    return _call(
        body, "attn_fwd", grid=(H, nq),
        in_specs=[pl.BlockSpec((tq, LANE), lambda h, i: (i, h)),
                  pl.BlockSpec((S, LANE), lambda h, i: (0, H + h)),
                  pl.BlockSpec((S, LANE), lambda h, i: (0, 2 * H + h)),
                  pl.BlockSpec((tq, LANE), lambda h, i: (i, 0)),
                  pl.BlockSpec((None, nq, 1, tq), lambda h, i: (h, 0, 0, 0))],
        out_specs=[pl.BlockSpec((tq, LANE), lambda h, i: (i, h)),
                   pl.BlockSpec((None, tq, 1), lambda h, i: (h, i, 0))],
        out_shape=[SDS((S, D), BF16), SDS((H, S, 1), F32)],
        sem=("parallel", "arbitrary"), comm=comm, args=[qkv, qkv, qkv, F_sh, F_rows])


def _attn_bwd(qkv, do, F_sh, A_rows, delta_rows, H, tq, comm=None):
    S = qkv.shape[0]
    D = H * LANE
    nq = S // tq
    scale = 1.0 / math.sqrt(LANE)

    def body(q_ref, do_ref, k_ref, v_ref, fsh_ref, a_ref, dl_ref, dq_ref, dk_ref, dv_ref, df_ref, dfr_ref,
             dq_acc, dfr_acc, st_scr, dp_scr, pt_scr, ds_scr, fk_scr, df_scr, dk_scr, dv_scr):
        h, j = pl.program_id(0), pl.program_id(1)

        @pl.when(j == 0)
        def _():
            dq_acc[...] = jnp.zeros_like(dq_acc)
            dfr_acc[...] = jnp.zeros_like(dfr_acc)

        k = k_ref[...]
        v = v_ref[...]
        fk_scr[...] = _lane_pick(fsh_ref[...], h) * _LOG2E
        df_scr[...] = jnp.zeros((tq, 1), F32)
        dk_scr[...] = jnp.zeros((tq, LANE), F32)
        dv_scr[...] = jnp.zeros((tq, LANE), F32)

        def block(i, diagonal):
            r0 = pl.multiple_of(i * tq, tq)
            q = q_ref[pl.ds(r0, tq), :]
            do_ = do_ref[pl.ds(r0, tq), :]
            st_scr[...] = lax.dot_general(k, q, (((1,), (1,)), ((), ())), preferred_element_type=F32)
            dp_scr[...] = lax.dot_general(v, do_, (((1,), (1,)), ((), ())), preferred_element_type=F32)
            a_row, dl_row = a_ref[i], dl_ref[i]
            dfr = jnp.zeros((1, tq), F32)
            for r in range(tq // _SUB):
                rows = slice(r * _SUB, (r + 1) * _SUB)
                arg = st_scr[rows, :] + (a_row - fk_scr[rows, :])
                if diagonal:
                    row = lax.broadcasted_iota(jnp.int32, (_SUB, tq), 0) + r * _SUB
                    col = lax.broadcasted_iota(jnp.int32, (_SUB, tq), 1)
                    arg = jnp.where(row <= col, arg, _NEG)
                pt = jnp.exp2(arg)
                dst = pt * (dp_scr[rows, :] - dl_row)
                df_scr[rows, :] += jnp.sum(dst, axis=1, keepdims=True)
                dfr = dfr + jnp.sum(dst, axis=0, keepdims=True)
                pt_scr[rows, :] = pt.astype(BF16)
                ds_scr[rows, :] = dst.astype(BF16)
            dfr_acc[i] += dfr
            dsb = ds_scr[...]
            dv_scr[...] += jnp.dot(pt_scr[...], do_, preferred_element_type=F32)
            dk_scr[...] += jnp.dot(dsb, q, preferred_element_type=F32)
            dq_acc[pl.ds(r0, tq), :] += lax.dot_general(dsb, k, (((0,), (0,)), ((), ())),
                                                        preferred_element_type=F32)

        def full_block(i, carry):
            block(i, False)
            return carry

        block(j, True)
        lax.fori_loop(j + 1, nq, full_block, 0)
        dk_ref[...] = (dk_scr[...] * _LN2).astype(BF16)
        dv_ref[...] = dv_scr[...].astype(BF16)
        df_ref[...] = -df_scr[...]

        @pl.when(j == nq - 1)
        def _():
            dq_ref[...] = (dq_acc[...] * scale).astype(BF16)
            dfr_ref[...] = dfr_acc[...]

    return _call(
        body, "attn_bwd", grid=(H, nq),
        in_specs=[pl.BlockSpec((S, LANE), lambda h, j: (0, h)),
                  pl.BlockSpec((S, LANE), lambda h, j: (0, h)),
                  pl.BlockSpec((tq, LANE), lambda h, j: (j, H + h)),
                  pl.BlockSpec((tq, LANE), lambda h, j: (j, 2 * H + h)),
                  pl.BlockSpec((tq, LANE), lambda h, j: (j, 0)),
                  pl.BlockSpec((None, nq, 1, tq), lambda h, j: (h, 0, 0, 0)),
                  pl.BlockSpec((None, nq, 1, tq), lambda h, j: (h, 0, 0, 0))],
        out_specs=[pl.BlockSpec((S, LANE), lambda h, j: (0, h)),
                   pl.BlockSpec((tq, LANE), lambda h, j: (j, h)),
                   pl.BlockSpec((tq, LANE), lambda h, j: (j, h)),
                   pl.BlockSpec((None, tq, 1), lambda h, j: (h, j, 0)),
                   pl.BlockSpec((None, nq, 1, tq), lambda h, j: (h, 0, 0, 0))],
        out_shape=[SDS((S, D), BF16), SDS((S, D), BF16), SDS((S, D), BF16), SDS((H, S, 1), F32),
                   SDS((H, nq, 1, tq), F32)],
        scratch=[pltpu.VMEM((S, LANE), F32), pltpu.VMEM((nq, 1, tq), F32),
                 pltpu.VMEM((tq, tq), F32), pltpu.VMEM((tq, tq), F32), pltpu.VMEM((tq, tq), BF16),
                 pltpu.VMEM((tq, tq), BF16), pltpu.VMEM((tq, 1), F32), pltpu.VMEM((tq, 1), F32),
                 pltpu.VMEM((tq, LANE), F32), pltpu.VMEM((tq, LANE), F32)],
        sem=("parallel", "arbitrary"), comm=comm, args=[qkv, do, qkv, qkv, F_sh, A_rows, delta_rows])


def _head_dots(do, o, H, tr):
    def fn(r, v):
        prod = r[0].astype(F32) * r[1].astype(F32)
        lane = lax.broadcasted_iota(jnp.int32, (prod.shape[0], LANE), 1)
        out = jnp.zeros((prod.shape[0], LANE), F32)
        for h in range(H):
            s = jnp.sum(prod[:, h * LANE:(h + 1) * LANE], axis=1, keepdims=True)
            out = jnp.where(lane == h, s, out)
        return [out], []
    return _rowwise("attn_delta", fn, [do, o], [], [(LANE, F32)], [], tr)[0]


def _adamw(name, w, g, m, v, layer=None, prev=None):
    R, C = g.shape
    tr = _pick(R, max(8, (512 * 1024) // max(C, 1) // 8 * 8), 8)
    c1 = 1.0 - ADAM_B1 ** ADAM_STEP
    c2 = 1.0 - ADAM_B2 ** ADAM_STEP

    def body(w_ref, g_ref, m_ref, v_ref, go_ref, d_ref, mo_ref, vo_ref):
        g_ = g_ref[...]
        m_ = ADAM_B1 * m_ref[...] + (1.0 - ADAM_B1) * g_
        v_ = ADAM_B2 * v_ref[...] + (1.0 - ADAM_B2) * (g_ * g_)
        go_ref[...] = g_
        d_ref[...] = -ADAM_LR * ((m_ / c1) / (jnp.sqrt(v_ / c2) + ADAM_EPS) + ADAM_WD * w_ref[...])
        mo_ref[...] = m_
        vo_ref[...] = v_

    if w.ndim == 3 and w.shape[1] == 1 and layer is None:
        tc = _pick(C, 2 * LANE, LANE)
        gspec = pl.BlockSpec((R, tc), lambda i: (0, i))
        pspec = pl.BlockSpec((R, None, tc), lambda i: (0, 0, i))
        grid = (C // tc,)
    else:
        gspec = pl.BlockSpec((tr, C), lambda i: (i, 0))
        pspec = gspec if layer is None else pl.BlockSpec((None, tr, C), lambda i: (layer, i, 0))
        grid = (R // tr,)
    return _call(body, name, grid=grid, in_specs=[pspec, gspec, pspec, pspec], out_specs=[pspec] * 4,
                 out_shape=[SDS(w.shape, F32)] * 4, sem=("parallel",), args=[w, g, m, v],
                 keep=[(p, k) for k, p in enumerate(prev)] if prev else ())


def _sum_slots(name, a):
    n, R, C = a.shape
    tr = _pick(R, 256, 8)

    def body(a_ref, o_ref):
        acc = a_ref[0]
        for k in range(1, n):
            acc = acc + a_ref[k]
        o_ref[...] = acc

    return pl.pallas_call(body, name=name, grid=(R // tr,),
                          in_specs=[pl.BlockSpec((n, tr, C), lambda i: (0, i, 0))],
                          out_specs=pl.BlockSpec((tr, C), lambda i: (i, 0)), out_shape=SDS((R, C), F32),
                          compiler_params=_params(("parallel",)))(a)


def _silu_rows(name, a):
    def body(a_ref, o_ref):
        z = a_ref[...]
        o_ref[...] = (z / (1.0 + jnp.exp(-z))).astype(BF16)
    return pl.pallas_call(body, name=name, out_shape=SDS(a.shape, BF16), compiler_params=_params())(a)


def _place():
    return lax.axis_index("x"), lax.axis_index("y"), lax.axis_index("c")


def _rcopy(src, dst, ssem, rsem, dev):
    return pltpu.make_async_remote_copy(src_ref=src, dst_ref=dst, send_sem=ssem, recv_sem=rsem,
                                        device_id=dev, device_id_type=MESH)


def _other_chips(x, y):
    return [(1 - x, y), (x, 1 - y), (1 - x, 1 - y)]


def _job_gather_devices(buf8):
    def views(o):
        x, y, c = _place()
        slot = lambda px, py, pc: o.at[4 * px + 2 * py + pc]
        return c, (x, y, 1 - c), _other_chips(x, y), slot, slot(x, y, c)

    def first(o, ss, rs):
        c, sib, chips, slot, me = views(o)
        return [_rcopy(me, me, ss.at[0], rs.at[0], sib)] + [
            _rcopy(me, me, ss.at[1 + j], rs.at[1 + j], (cx, cy, c)) for j, (cx, cy) in enumerate(chips)]

    def start(i, o, ss, rs):
        for cp in first(o[0], ss, rs):
            cp.start()

    def wait(i, o, ss, rs):
        c, sib, chips, slot, me = views(o[0])
        x, y = sib[0], sib[1]
        passed = []
        for j, (cx, cy) in enumerate(chips):
            got = slot(cx, cy, c)
            _rcopy(me, got, ss.at[1 + j], rs.at[1 + j], (cx, cy, c)).wait_recv()
            passed.append(_rcopy(got, got, ss.at[4 + j], rs.at[4 + j], sib))
            passed[j].start()
        _rcopy(me, slot(x, y, 1 - c), ss.at[0], rs.at[0], sib).wait_recv()
        for j, (cx, cy) in enumerate(chips):
            _rcopy(me, slot(cx, cy, 1 - c), ss.at[4 + j], rs.at[4 + j], sib).wait_recv()
        for cp in first(o[0], ss, rs) + passed:
            cp.wait_send()

    return _Job([buf8], [SDS(buf8.shape, buf8.dtype)], {0: 0}, N_DEV - 1, start, wait)


def _gather_devices(name, buf, slot_idx, extra=None):
    R, C = buf.shape

    def place(s_ref, b_ref, o_ref):
        o_ref[...] = b_ref[...]

    buf8 = _call(place, name + "_place", grid=(1,), prefetch=slot_idx, args=[buf],
                 in_specs=[pl.BlockSpec((R, C), lambda i, s: (0, 0))],
                 out_specs=[pl.BlockSpec((None, R, C), lambda i, s: (s[0], 0, 0))],
                 out_shape=[SDS((N_DEV, R, C), buf.dtype)])[0]
    jobs, absorb_extra = extra if extra else ([], None)
    got = []

    def absorb(outs):
        got.append(outs[0])
        if absorb_extra:
            absorb_extra(outs[1:])

    _comm_call(name, ([_job_gather_devices(buf8)] + jobs, absorb))
    return got[0]


def _cast_slabs(name, ws, idx, comm=None):
    n = len(ws)
    steps = next(s for s in (8, 4, 2, 1)
                 if all((w.shape[2] % (LANE * s) if l is None else w.shape[1] % (16 * s)) == 0 for w, l in ws))

    def body(s_ref, *refs):
        for w_ref, o_ref in zip(refs[:n], refs[n:]):
            o_ref[...] = w_ref[...].astype(BF16)

    in_specs, out_specs, out_shape = [], [], []
    for w, l in ws:
        if l is None:
            R, _, C = w.shape
            in_specs.append(pl.BlockSpec((R, None, C // steps), lambda i, s: (0, 0, i)))
            out_specs.append(pl.BlockSpec((None, R, C // steps), lambda i, s: (s[1], 0, i)))
        else:
            _, R, C = w.shape
            in_specs.append(pl.BlockSpec((None, R // steps, C), lambda i, s, l=l: (l, i, 0)))
            out_specs.append(pl.BlockSpec((None, R // steps, C), lambda i, s: (s[1], i, 0)))
        out_shape.append(SDS((N_CHIPS, R, C), BF16))
    return _call(body, name, grid=(steps,), prefetch=idx, comm=comm, sem=("parallel",), args=[w for w, _ in ws],
                 in_specs=in_specs, out_specs=out_specs, out_shape=out_shape)


def _half(ref, lead, hf, cols):
    n = ref.shape[-1 if cols else -2] // 2
    assert 2 * n == ref.shape[-1 if cols else -2]
    cut = pl.ds(hf * n, n)
    return ref.at[(*lead, slice(None), cut) if cols else (*lead, cut, slice(None))]


def _job_gather_ici(buf, cols=False):
    def views(o):
        x, y, c = _place()
        return c, _other_chips(x, y), _half(o, (2 * x + y,), c, cols)

    def start(i, o, ss, rs):
        c, chips, mine = views(o[0])
        for j, (cx, cy) in enumerate(chips):
            _rcopy(mine, mine, ss.at[j], rs.at[j], (cx, cy, c)).start()

    def wait(i, o, ss, rs):
        c, chips, mine = views(o[0])
        for j, (cx, cy) in enumerate(chips):
            cp = _rcopy(mine, _half(o[0], (2 * cx + cy,), c, cols), ss.at[j], rs.at[j], (cx, cy, c))
            cp.wait_send()
            cp.wait_recv()

    return _Job([buf], [SDS(buf.shape, buf.dtype)], {0: 0}, 3, start, wait)


def _job_gather_pair(buf, cols=False):
    def views(o):
        x, y, c = _place()
        return c, (x, y, 1 - c), _other_chips(x, y)

    def start(i, o, ss, rs):
        c, sib, chips = views(o[0])
        for j, (cx, cy) in enumerate(chips):
            got = _half(o[0], (2 * cx + cy,), c, cols)
            _rcopy(got, got, ss.at[j], rs.at[j], sib).start()

    def wait(i, o, ss, rs):
        c, sib, chips = views(o[0])
        for j, (cx, cy) in enumerate(chips):
            cp = _rcopy(_half(o[0], (2 * cx + cy,), c, cols), _half(o[0], (2 * cx + cy,), 1 - c, cols),
                        ss.at[j], rs.at[j], sib)
            cp.wait_send()
            cp.wait_recv()

    return _Job([buf], [SDS(buf.shape, buf.dtype)], {0: 0}, 3, start, wait)


class _Gather:
    def __init__(self, bufs, by_cols=()):
        self.todo, self.half, self.done, self.by_cols = dict(bufs), {}, {}, set(by_cols)

    def comm(self, admit=()):
        second, first = list(self.half), list(admit)
        jobs = [_job_gather_pair(self.half[n], n in self.by_cols) for n in second]
        jobs += [_job_gather_ici(self.todo[n], n in self.by_cols) for n in first]

        def absorb(outs):
            for n, o in zip(second + first, outs):
                if n in self.half:
                    del self.half[n]
                    self.done[n] = o
                else:
                    del self.todo[n]
                    self.half[n] = o
        return (jobs, absorb) if jobs else None


def _half_shape(R, C, cols):
    return (R, C // 2) if cols else (R // 2, C)


def _job_rs_pair(g4, cols=False):
    _, R, C = g4.shape

    def desc(i, o, ss, rs):
        x, y, c = _place()
        return _rcopy(_half(i[0], (slice(None),), 1 - c, cols), o[0], ss.at[0], rs.at[0], (x, y, 1 - c))

    return _Job([g4], [SDS((N_CHIPS,) + _half_shape(R, C, cols), F32)], {}, 1,
                lambda i, o, ss, rs: desc(i, o, ss, rs).start(), lambda i, o, ss, rs: desc(i, o, ss, rs).wait())


def _job_rs_chips(p4):
    _, hR, C = p4.shape

    def descs(i, o, ss, rs):
        x, y, c = _place()
        return [_rcopy(i[0].at[2 * cx + cy], o[0].at[j], ss.at[j], rs.at[j], (cx, cy, c))
                for j, (cx, cy) in enumerate(_other_chips(x, y))]

    def start(i, o, ss, rs):
        for cp in descs(i, o, ss, rs):
            cp.start()

    def wait(i, o, ss, rs):
        for cp in descs(i, o, ss, rs):
            cp.wait_send()
            cp.wait_recv()

    return _Job([p4], [SDS((3, hR, C), p4.dtype)], {}, 3, start, wait)


def _job_rs_join(buf, cols=False):
    def desc(o, ss, rs, recv):
        x, y, c = _place()
        mine = _half(o[0], (), c, cols)
        return _rcopy(mine, _half(o[0], (), 1 - c, cols) if recv else mine, ss.at[0], rs.at[0], (x, y, 1 - c))

    def wait(i, o, ss, rs):
        cp = desc(o, ss, rs, True)
        cp.wait_send()
        cp.wait_recv()

    return _Job([buf], [SDS(buf.shape, buf.dtype)], {0: 0}, 1,
                lambda i, o, ss, rs: desc(o, ss, rs, False).start(), wait)


def _walk(R, C, cols):
    if cols:
        tc = _pick(C, 2 * LANE, LANE)
        return (R, tc), C // tc
    tr = _pick(R, max(16, (1 << 20) // C // 16 * 16), 16)
    return (tr, C), R // tr


def _rs_add_pair(name, g4, recv, idx, cols=False):
    _, R, C = g4.shape
    hshape = _half_shape(R, C, cols)
    blk, nb = _walk(*hshape, cols)
    at = (lambda k, i: (k, 0, i)) if cols else (lambda k, i: (k, i, 0))

    def body(s_ref, a_ref, b_ref, pb_ref, po_ref):
        s = a_ref[...] + b_ref[...]
        pb_ref[...] = s.astype(BF16)

        @pl.when(pl.program_id(1) == s_ref[1])
        def _():
            po_ref[...] = s

    gs = pltpu.PrefetchScalarGridSpec(
        num_scalar_prefetch=1, grid=(nb, N_CHIPS),
        in_specs=[pl.BlockSpec((None,) + blk, lambda i, k, s: at(k, s[0] * nb + i)),
                  pl.BlockSpec((None,) + blk, lambda i, k, s: at(k, i))],
        out_specs=[pl.BlockSpec((None,) + blk, lambda i, k, s: at(k, i)),
                   pl.BlockSpec(blk, lambda i, k, s: at(k, i)[1:])])
    return pl.pallas_call(body, name=name, grid_spec=gs,
                          out_shape=[SDS((N_CHIPS,) + hshape, BF16), SDS(hshape, F32)],
                          compiler_params=_params(("parallel", "arbitrary")))(idx, g4, recv)


def _rs_add_chips(name, own, recv3, idx, cols=False):
    hR, hC = own.shape
    blk, nb = _walk(hR, hC, cols)
    at = (lambda i: (0, i)) if cols else (lambda i: (i, 0))

    def body(s_ref, a_ref, b_ref, o_ref):
        o_ref[...] = ((a_ref[...] + b_ref[0].astype(F32)) + b_ref[1].astype(F32)) + b_ref[2].astype(F32)

    gs = pltpu.PrefetchScalarGridSpec(
        num_scalar_prefetch=1, grid=(nb,),
        in_specs=[pl.BlockSpec(blk, lambda i, s: at(i)), pl.BlockSpec((3,) + blk, lambda i, s: (0,) + at(i))],
        out_specs=pl.BlockSpec(blk, lambda i, s: at(s[0] * nb + i)))
    return pl.pallas_call(body, name=name, grid_spec=gs,
                          out_shape=SDS((hR, 2 * hC) if cols else (2 * hR, hC), F32),
                          compiler_params=_params(("parallel",)))(idx, own, recv3)


class _ReduceScatter:
    def __init__(self, idx):
        self.idx, self.items, self.done = idx, [], {}

    def push(self, tag, g4, cols=False):
        self.items.append([tag, 0, g4, cols])

    def comm(self, ici=1):
        cur, jobs = [], []
        for item in self.items:
            tag, stage, data, cols = item
            if stage == 1:
                if ici == 0:
                    continue
                ici -= 1
            cur.append(item)
            jobs.append(_job_rs_pair(data, cols) if stage == 0 else _job_rs_chips(data[0]) if stage == 1
                        else _job_rs_join(data, cols))

        def absorb(outs):
            for item, o in zip(cur, outs):
                tag, stage, data, cols = item
                if stage == 0:
                    item[1:3] = [1, _rs_add_pair(tag + "_rs_add2", data, o, self.idx, cols)]
                elif stage == 1:
                    item[1:3] = [2, _rs_add_chips(tag + "_rs_add4", data[1], o, self.idx, cols)]
                else:
                    self.items.remove(item)
                    self.done[tag] = o
        return (jobs, absorb) if jobs else None


def _resid_epilogue(accs, ex):
    return [ex[0] + ex[1] * accs[0], accs[0]]


def _sigmoid(x):
    return 0.5 * jnp.tanh(0.5 * x) + 0.5


def _swiglu_epilogue(accs, ex):
    gt, up = accs
    return [gt, up, gt * _sigmoid(gt) * up]


def _swiglu_bwd_epilogue(accs, ex):
    dact = accs[0]
    gt, up = ex[0].astype(F32), ex[1].astype(F32)
    sg = _sigmoid(gt)
    silu = gt * sg
    return [dact * up * (sg + silu * (1.0 - sg)), dact * silu]


def kernel(x, c, ada_w, ada_b, norm_mix_g, norm_ffn_g, a_w_in, a_b_in, a_ln_g, a_ln_b, a_w_s, a_b_s, a_w_out, b_w_in, b_b_f, b_w_out, ffn_w_gate, ffn_w_up, ffn_w_down, final_g, loss_target, m_ada_w, m_ada_b, m_norm_mix_g, m_norm_ffn_g, m_a_w_in, m_a_b_in, m_a_ln_g, m_a_ln_b, m_a_w_s, m_a_b_s, m_a_w_out, m_b_w_in, m_b_b_f, m_b_w_out, m_ffn_w_gate, m_ffn_w_up, m_ffn_w_down, m_final_g, v_ada_w, v_ada_b, v_norm_mix_g, v_norm_ffn_g, v_a_w_in, v_a_b_in, v_a_ln_g, v_a_ln_b, v_a_w_s, v_a_b_s, v_a_w_out, v_b_w_in, v_b_b_f, v_b_w_out, v_ffn_w_gate, v_ffn_w_up, v_ffn_w_down, v_final_g):
    S, D = x.shape[1], x.shape[2]
    H = D // LANE
    G = D // LANE
    FH = ffn_w_down.shape[1] * N_CHIPS
    depth = ada_w.shape[0]
    assert depth == 2 and a_w_in.shape[0] == 1 and b_w_in.shape[0] == 1
    mx, my, mc = _place()
    chip = 2 * mx + my
    dev = 4 * mx + 2 * my + mc
    x0 = x[0]
    target = loss_target[0]
    tr = _pick(S, 256, 8)
    tm = _pick(S, 512, 8)
    tq_f = _pick(S, 1024, LANE)
    tq = _pick(S, 512, LANE)
    nq = S // tq

    idx = jnp.stack([mc, chip]).astype(jnp.int32)
    t3 = lambda a: jnp.transpose(a, (2, 0, 1))
    shards = {"a_in": (a_w_in, 0), "a_out": (a_w_out, 0), "b_in": (t3(b_w_in), None), "b_out": (b_w_out, 0)}
    for l in range(depth):
        shards.update({f"wg{l}": (ffn_w_gate, l), f"wu{l}": (ffn_w_up, l), f"wd{l}": (ffn_w_down, l)})
    first = ["a_in", "a_out"]
    rest = [n for n in shards if n not in first]
    ag = _Gather(dict(zip(first, _cast_slabs("cast_first", [shards[n] for n in first], idx))), by_cols=["b_in"])
    ag.todo.update(zip(rest, _cast_slabs("cast_rest", [shards[n] for n in rest], idx, comm=ag.comm(first))))
    _comm_call("ag_first_pair", ag.comm(["wg0"]))
    wa_in4 = ag.done["a_in"]
    wa_out = ag.done["a_out"].reshape(D, D)

    dev_idx = dev.astype(jnp.int32).reshape(1)
    c_all = _gather_devices("ag_c", jnp.pad(c, ((0, 7), (0, 0))), dev_idx).reshape(N_DEV, 8, D)[:, 0, :]
    c_act = _silu_rows("c_silu", c_all)
    n_loc = ada_w.shape[2]
    mods = []
    for l in range(depth):
        b_loc = lax.dynamic_slice_in_dim(ada_b[l:l + 1], chip * n_loc, n_loc, axis=1)
        mods.append(_mm(f"ada_fwd{l}", "nn", [c_act], [ada_w], M=N_DEV, N=n_loc, K=D, tm=N_DEV, b_layer=l,
                        tn=_pick(n_loc, 1024, LANE), tk=D, extras=[("row", b_loc)],
                        epilogue=lambda accs, ex: [accs[0] + ex[0]])[0])
    mod_all = _gather_devices("ag_mod", jnp.concatenate(mods, axis=1), dev_idx)
    mod_mine = lax.dynamic_index_in_dim(mod_all[0::2], dev, axis=1, keepdims=False)
    mod_mine = mod_mine.reshape(N_CHIPS, depth, n_loc).transpose(1, 0, 2).reshape(depth, 1, N_MOD * D)
    mod = [[mod_mine[l, :, i * D:(i + 1) * D] for i in range(N_MOD)] for l in range(depth)]

    row = lambda a: a.reshape(1, -1)

    tril = jnp.tril(jnp.ones((LANE, LANE), dtype=bool))
    w_mask = jnp.where(tril[None], a_w_s[0], 0.0).astype(BF16)
    bias_full = jnp.repeat(a_b_s[0].T, LANE, axis=1)
    bf_pad = jnp.pad(b_b_f, ((0, 0), (0, LANE - H)))

    admit = {"sgu_in": ["wu0"], "mix_out0": ["wd0"], "ffn_in0": ["b_in", "b_out"], "ffn_out0": [],
             "fox_qkv": ["wg1"], "attn_fwd": ["wu1", "wd1"], "mix_out1": []}
    saved = []
    xs = x0
    for l in range(depth):
        sh1, sc1, g1, sh2, sc2, g2 = mod[l]
        st = {"x_in": xs}
        h1 = _normmod_fwd(f"normmod_mix{l}", xs, row(norm_mix_g[l]), sc1, sh1, tr)
        st["h1"] = h1
        if l == 0:
            a = _mm("sgu_in", "nn", [h1], [wa_in4], M=S, N=2 * D, K=D, tm=tm, tn=2 * D // N_CHIPS, tk=D,
                    b_stacked=True, extras=[("row", a_b_in)], comm=ag.comm(admit["sgu_in"]),
                    epilogue=lambda accs, ex: [accs[0] + ex[0]])[0]
            y = _sgu_mid_fwd(a, a_ln_g, a_ln_b, w_mask, bias_full, tr)
            st["a"], st["y"] = a, y
            w_o, mix_out = wa_out, y
        else:
            w_qkv = ag.done["b_in"].reshape(-1, D)
            w_f = jnp.pad(w_qkv[3 * D:], ((0, LANE - H), (0, 0)))
            q_scale = jnp.concatenate([jnp.full((1, D), _LOG2E / math.sqrt(LANE), F32), jnp.ones((1, 2 * D), F32)],
                                      axis=1)
            qkv = _mm("fox_qkv", "nt", [h1], [w_qkv], M=S, N=3 * D, K=D, tm=tm, tn=_pick(3 * D, 1024, LANE),
                      tk=D, out_dtypes=(BF16,), extras=[("row", q_scale)],
                      epilogue=lambda accs, ex: [accs[0] * ex[0]], comm=ag.comm(admit["fox_qkv"]))[0]
            fl = _mm("fox_f", "nt", [h1], [w_f], M=S, N=LANE, K=D, tm=tm, tn=LANE, tk=D)[0]
            F_sh = _fox_gate_fwd(fl, bf_pad)
            F_hs = F_sh[:, :H].T
            F_rows = F_hs.reshape(H, nq, 1, tq)
            o, lse = _attn_fwd(qkv, F_sh, F_hs.reshape(H, S // tq_f, 1, tq_f) * _LOG2E, H, tq_f,
                               comm=ag.comm(admit["attn_fwd"]))
            st.update(qkv=qkv, fl=fl, F_sh=F_sh, F_rows=F_rows, o=o, lse=lse, w_qkv=w_qkv, w_f=w_f)
            w_o, mix_out = ag.done["b_out"].reshape(D, D), o
        x1, out1 = _mm(f"mix_out{l}", "nn", [mix_out], [w_o], M=S, N=D, K=D, tm=tm, tn=_pick(D, 1024, LANE),
                       tk=D, extras=[("tile", xs), ("row", g1)], out_dtypes=(F32, BF16),
                       epilogue=_resid_epilogue, comm=ag.comm(admit[f"mix_out{l}"]))
        st["x_mid"], st["out1"] = x1, out1
        h2 = _normmod_fwd(f"normmod_ffn{l}", x1, row(norm_ffn_g[l]), sc2, sh2, tr)
        gt, up, act = _mm(f"ffn_in{l}", "nn", [h2, h2], [ag.done[f"wg{l}"], ag.done[f"wu{l}"]], M=S, N=FH, K=D,
                          tm=tm, tn=FH // N_CHIPS, tk=D, b_stacked=True, acc_of=[0, 1], n_acc=2,
                          out_dtypes=(BF16, BF16, BF16), epilogue=_swiglu_epilogue,
                          comm=ag.comm(admit.get(f"ffn_in{l}", ())))
        x2, out2 = _mm(f"ffn_out{l}", "nn", [act], [ag.done[f"wd{l}"].reshape(FH, D)], M=S, N=D, K=FH, tm=tm,
                       tn=_pick(D, 1024, LANE), tk=FH, extras=[("tile", x1), ("row", g2)],
                       out_dtypes=(F32, BF16), epilogue=_resid_epilogue,
                       comm=ag.comm(admit.get(f"ffn_out{l}", ())))
        st.update(h2=h2, gt=gt, up=up, act=act, out2=out2)
        saved.append(st)
        xs = x2
    assert not ag.todo and not ag.half
    wg4 = [ag.done[f"wg{l}"] for l in range(depth)]
    wu4 = [ag.done[f"wu{l}"] for l in range(depth)]
    wd = [ag.done[f"wd{l}"].reshape(FH, D) for l in range(depth)]
    wb_out = ag.done["b_out"].reshape(D, D)

    dx, dog, loss_vec, g_final, dg2 = _loss_head("loss_head", xs, target, row(final_g), tr,
                                                 nxt=(saved[-1]["out2"], mod[-1][5]))
    loss = lax.psum(loss_vec[0, 0], ("x", "y", "c"))

    rs = _ReduceScatter(idx)
    dmods = [None] * depth
    gmix = [None] * depth
    gffn = [None] * depth
    for l in reversed(range(depth)):
        sh1, sc1, g1, sh2, sc2, g2 = mod[l]
        st = saved[l]
        dgt, dup = _mm(f"ffn_dact{l}", "nt", [dog], [wd[l]], M=S, N=FH, K=D, tm=tm, tn=FH // N_CHIPS, tk=D,
                       extras=[("tile", st["gt"]), ("tile", st["up"])], out_dtypes=(BF16, BF16),
                       epilogue=_swiglu_bwd_epilogue, comm=rs.comm())
        rs.push(f"wd{l}", _mm(f"ffn_dwd{l}", "tn", [st["act"]], [dog], M=FH, N=D, K=S, tm=FH // N_CHIPS,
                              tn=_pick(D, 512, LANE), tk=S, outer="i",
                              comm=rs.comm())[0].reshape(N_CHIPS, FH // N_CHIPS, D))
        rs.push(f"wg{l}", _mm(f"ffn_dwg{l}", "tn", [st["h2"]], [dgt], M=D, N=FH, K=S, tm=_pick(D, 512, LANE),
                              tn=FH // N_CHIPS, tk=S, out_stacked=True, comm=rs.comm())[0])
        rs.push(f"wu{l}", _mm(f"ffn_dwu{l}", "tn", [st["h2"]], [dup], M=D, N=FH, K=S, tm=_pick(D, 512, LANE),
                              tn=FH // N_CHIPS, tk=S, out_stacked=True, comm=rs.comm())[0])
        dh2 = _mm(f"ffn_dh{l}", "nt", [dgt, dup], [wg4[l], wu4[l]], M=S, N=D, K=FH, tm=_pick(S, 512, 8),
                  tn=_pick(D, 1024, LANE), tk=FH, b_stacked=True, comm=rs.comm())[0]
        dx, dog, dsh2, dsc2, gffn[l], dg1 = _normmod_bwd(f"normmod_ffn_bwd{l}", st["x_mid"], dh2, dx,
                                                         row(norm_ffn_g[l]), sc2, tr, nxt=(st["out1"], g1),
                                                         comm=rs.comm(ici=0))
        if l == 0:
            dy = _mm("sgu_dy", "nt", [dog], [wa_out], M=S, N=D, K=D, tm=tm, tn=_pick(D, 1024, LANE), tk=D,
                     out_dtypes=(BF16,))[0]
            rs.push("a_out", _mm("sgu_dwout", "tn", [st["y"]], [dog], M=D, N=D, K=S, tm=_pick(D, 512, LANE),
                                 tn=_pick(D, 1024, LANE),
                                 tk=S)[0].reshape(N_CHIPS, D // N_CHIPS, D))
            da, dws, dbias, g_ln_g, g_ln_b, g_b_in = _sgu_mid_bwd(st["a"], dy, a_ln_g, a_ln_b, w_mask, bias_full, tr,
                                                                  comm=rs.comm())
            rs.push("a_in", _mm("sgu_dwin", "tn", [st["h1"]], [da], M=D, N=2 * D, K=S, tm=_pick(D, 512, LANE),
                                tn=2 * D // N_CHIPS, tk=S, out_stacked=True, comm=rs.comm())[0])
            dh1 = _mm("sgu_dh", "nt", [da], [wa_in4], M=S, N=D, K=2 * D, tm=_pick(S, 512, 8), tn=_pick(D, 1024, LANE),
                      tk=2 * D, b_stacked=True, comm=rs.comm())[0]
            g_w_s = jnp.where(tril[None], dws, 0.0)
            g_b_s = jnp.sum(dbias.reshape(LANE, G, LANE), axis=2).T
        else:
            do = _mm("fox_do", "nt", [dog], [wb_out], M=S, N=D, K=D, tm=tm, tn=_pick(D, 1024, LANE), tk=D,
                     out_dtypes=(BF16,))[0]
            rs.push("b_out", _mm("fox_dwout", "tn", [st["o"]], [dog], M=D, N=D, K=S, tm=_pick(D, 512, LANE),
                                 tn=_pick(D, 1024, LANE),
                                 tk=S)[0].reshape(N_CHIPS, D // N_CHIPS, D))
            delta = _head_dots(do, st["o"], H, tr)
            delta_rows = delta[:, :H].T.reshape(H, nq, 1, tq)
            A_rows = (st["F_rows"] - st["lse"].reshape(H, nq, 1, tq)) * _LOG2E
            w_qkv, w_f = st["w_qkv"], st["w_f"]
            dq, dk, dv, dF_k, dF_q = _attn_bwd(st["qkv"], do, st["F_sh"], A_rows, delta_rows, H, tq,
                                               comm=rs.comm(ici=2))
            dF_sh = jnp.pad((dF_k.reshape(H, S) + dF_q.reshape(H, S)).T, ((0, 0), (0, LANE - H)))
            dfl, db_f = _fox_gate_bwd(st["fl"], dF_sh, bf_pad)
            dfl_b = dfl.astype(BF16)
            dh_f = _mm("fox_dh_f", "nn", [dfl_b], [w_f], M=S, N=D, K=LANE, tm=tm, tn=_pick(D, 1024, LANE),
                       tk=LANE)[0]
            dh1 = _mm("fox_dh", "nn", [dq, dk, dv], [w_qkv, w_qkv, w_qkv], M=S, N=D, K=D, tm=_pick(S, 256, 8),
                      tn=_pick(D, 1024, LANE), tk=D, b_koffs=[0, 1, 2],
                      extras=[("tile", dh_f)], epilogue=lambda accs, ex: [accs[0] + ex[0]], comm=rs.comm())[0]
            tmw = _pick(D, 512, LANE)
            g_bT = None
            for p, (nm, d_) in enumerate((("q", dq), ("k", dk), ("v", dv))):
                g_bT = _mm(f"fox_dw{nm}", "tn", [d_], [st["h1"]], M=D, N=D, K=S, tm=tmw, tn=_pick(D, 1024, LANE),
                           tk=S, out_into=(3 * D + H, p * (D // tmw), g_bT), comm=rs.comm(ici=0))[0]
            dwf = _mm("fox_dwf", "tn", [dfl_b], [st["h1"]], M=LANE, N=D, K=S, tm=LANE, tn=_pick(D, 1024, LANE),
                      tk=S)[0]
            g_bT = lax.dynamic_update_slice(g_bT, dwf[:H], (3 * D, 0))
            rs.push("b_in", g_bT.reshape(N_CHIPS, -1, D), cols=True)
            g_b_f = db_f[:, :H]
        below = (saved[l - 1]["out2"], mod[l - 1][5]) if l else None
        res = _normmod_bwd(f"normmod_mix_bwd{l}", st["x_in"], dh1, dx, row(norm_mix_g[l]), sc1, tr, nxt=below,
                           comm=rs.comm(ici=1 - l))
        if l:
            dx, dog_below, dsh1, dsc1, gmix[l], dg2_below = res
        else:
            dx, dsh1, dsc1, gmix[l] = res
        dmods[l] = jnp.concatenate([dsh1, dsc1, dg1, dsh2, dsc2, dg2], axis=1)
        if l:
            dog, dg2 = dog_below, dg2_below
    grad_x = dx[None]

    small = [jnp.concatenate(dmods, axis=0), jnp.concatenate(gmix, axis=0), jnp.concatenate(gffn, axis=0),
             g_b_in, g_ln_g, g_ln_b, g_w_s[None], g_b_s[None], g_b_f, g_final[0]]
    small_w = [ada_b, norm_mix_g, norm_ffn_g, a_b_in, a_ln_g, a_ln_b, a_w_s, a_b_s, b_b_f, final_g]
    small_m = [m_ada_b, m_norm_mix_g, m_norm_ffn_g, m_a_b_in, m_a_ln_g, m_a_ln_b, m_a_w_s, m_a_b_s, m_b_b_f, m_final_g]
    small_v = [v_ada_b, v_norm_mix_g, v_norm_ffn_g, v_a_b_in, v_a_ln_g, v_a_ln_b, v_a_w_s, v_a_b_s, v_b_b_f, v_final_g]
    sizes = [w.size for w in small_w]
    total = sum(sizes)
    padded = -(-total // (8 * LANE)) * (8 * LANE)

    def pack(parts):
        flat = jnp.concatenate([p.reshape(-1) for p in parts])
        return jnp.pad(flat, (0, padded - total)).reshape(padded // LANE, LANE)

    def unpack(buf):
        flat = buf.reshape(-1)
        outs, off = [], 0
        for w, n in zip(small_w, sizes):
            outs.append(flat[off:off + n].reshape(w.shape))
            off += n
        return outs

    g_all = _gather_devices("ag_small_grads", pack(small), dev_idx, extra=rs.comm())
    g_small = _sum_slots("sum_small_grads", g_all)
    _, d_small, m_small, v_small = _adamw("adamw_small", pack(small_w), g_small, pack(small_m), pack(small_v))
    sg, sd, sm, sv_ = unpack(g_small), unpack(d_small), unpack(m_small), unpack(v_small)

    n_dm = depth * N_MOD * D
    dmod_all = g_all.reshape(N_DEV, -1)[:, :n_dm].reshape(N_DEV, depth, N_MOD * D)
    u_ada = None
    for l in range(depth):
        dm_loc = lax.dynamic_slice_in_dim(dmod_all[:, l, :], chip * n_loc, n_loc, axis=1).astype(BF16)
        g_ada = _mm(f"ada_dw{l}", "tn", [c_act], [dm_loc], M=D, N=n_loc, K=N_DEV, tm=_pick(D, 512, LANE),
                    tn=_pick(n_loc, 1024, LANE), tk=N_DEV)[0]
        u_ada = _adamw(f"adamw_ada_w{l}", ada_w, g_ada, m_ada_w, v_ada_w, layer=l, prev=u_ada)

    flushes = 0
    while rs.items:
        _comm_call(f"rs_flush{flushes}", rs.comm())
        flushes += 1
    groups = {"a_in": (a_w_in, m_a_w_in, v_a_w_in), "a_out": (a_w_out, m_a_w_out, v_a_w_out),
              "b_out": (b_w_out, m_b_w_out, v_b_w_out),
              "wg": (ffn_w_gate, m_ffn_w_gate, v_ffn_w_gate), "wu": (ffn_w_up, m_ffn_w_up, v_ffn_w_up),
              "wd": (ffn_w_down, m_ffn_w_down, v_ffn_w_down)}
    upd = {}
    for n, (w_, m_, v_) in groups.items():
        for l in range(w_.shape[0]):
            tag = n if n in rs.done else f"{n}{l}"
            upd[n] = _adamw("adamw_" + tag, w_, rs.done[tag], m_, v_, layer=l, prev=upd.get(n))
    u_a_in, u_a_out, u_b_out, u_wg, u_wu, u_wd = (upd[n] for n in groups)
    u_b_in = [jnp.transpose(t, (1, 2, 0)) for t in
              _adamw("adamw_b_in", t3(b_w_in), rs.done["b_in"], t3(m_b_w_in), t3(v_b_w_in))]

    def leaves(i, small_list):
        s = small_list
        return [u_ada[i], s[0], s[1], s[2], u_a_in[i], s[3], s[4], s[5], s[6], s[7], u_a_out[i],
                u_b_in[i], s[8], u_b_out[i], u_wg[i], u_wu[i], u_wd[i], s[9]]

    return (loss, grad_x, *leaves(0, sg), *leaves(1, sd), *leaves(2, sm), *leaves(3, sv_))
```

```python
import functools
import math

import jax
import jax.numpy as jnp
from jax import lax
from jax.experimental import pallas as pl
from jax.experimental.pallas import tpu as pltpu

F32, BF16 = jnp.float32, jnp.bfloat16
LANE = 128
N_CHIPS = 4
N_DEV = 8
N_MOD = 6
EPS = 1e-6
VMEM_LIMIT = 60 * 1024 * 1024
ADAM_LR, ADAM_B1, ADAM_B2, ADAM_EPS, ADAM_WD, ADAM_STEP = 0.001, 0.9, 0.999, 1e-08, 0.01, 10
MESH = pl.DeviceIdType.MESH
ANY = pl.BlockSpec(memory_space=pl.ANY)
SDS = jax.ShapeDtypeStruct


def _pick(dim, pref, align):
    t = min(dim, pref)
    t -= t % align
    while t >= align:
        if dim % t == 0:
            return t
        t -= align
    return dim


def _params(sem=None):
    return pltpu.CompilerParams(dimension_semantics=sem, vmem_limit_bytes=VMEM_LIMIT)


class _Job:
    def __init__(self, ins, outs, aliases, nsem, start, wait):
        self.ins, self.outs, self.aliases, self.nsem, self.start, self.wait = ins, outs, aliases, nsem, start, wait


def _call(body, name, *, grid, in_specs, out_specs, out_shape, args, scratch=(), sem=None, comm=None,
          prefetch=None, keep=()):
    jobs, absorb = comm if comm else ([], None)
    n_pre = 0 if prefetch is None else 1
    n_in, n_out, n_scr = len(args), len(out_shape), len(scratch)
    c_ins = [a for a, _ in keep] + [a for j in jobs for a in j.ins]
    c_outs = [s for j in jobs for s in j.outs]
    aliases = {n_pre + n_in + k: o for k, (_, o) in enumerate(keep)}
    c_scr = []
    i_off, o_off = len(keep), 0
    for j in jobs:
        for a, b in j.aliases.items():
            aliases[n_pre + n_in + i_off + a] = n_out + o_off + b
        i_off += len(j.ins)
        o_off += len(j.outs)
        c_scr += [pltpu.SemaphoreType.DMA((j.nsem,)), pltpu.SemaphoreType.DMA((j.nsem,))]

    def wrapped(*refs):
        bounds = [n_pre + n_in, len(c_ins), n_out, len(c_outs), n_scr]
        parts, p = [], 0
        for n in bounds:
            parts.append(refs[p:p + n])
            p += n
        main_in, cin, main_out, cout, main_scr = parts
        cin = cin[len(keep):]
        csem = refs[p:]

        def run(phase):
            a = b = 0
            for k, j in enumerate(jobs):
                fn = j.start if phase == 0 else j.wait
                fn(cin[a:a + len(j.ins)], cout[b:b + len(j.outs)], csem[2 * k], csem[2 * k + 1])
                a += len(j.ins)
                b += len(j.outs)

        if jobs and grid:
            first = functools.reduce(jnp.logical_and, [pl.program_id(d) == 0 for d in range(len(grid))])
            last = functools.reduce(jnp.logical_and, [pl.program_id(d) == grid[d] - 1 for d in range(len(grid))])
            pl.when(first)(lambda: run(0))
            body(*main_in, *main_out, *main_scr)
            pl.when(last)(lambda: run(1))
        elif jobs:
            run(0)
            body(*main_in, *main_out, *main_scr)
            run(1)
        else:
            body(*main_in, *main_out, *main_scr)

    specs = dict(grid=grid, in_specs=list(in_specs) + [ANY] * len(c_ins),
                 out_specs=list(out_specs) + [ANY] * len(c_outs), scratch_shapes=list(scratch) + c_scr)
    if n_pre:
        specs = dict(grid_spec=pltpu.PrefetchScalarGridSpec(num_scalar_prefetch=1, **specs))
    outs = pl.pallas_call(
        wrapped, name=name, out_shape=list(out_shape) + c_outs, input_output_aliases=aliases,
        compiler_params=_params(("arbitrary",) * len(grid) if jobs else sem), **specs,
    )(*([prefetch] if n_pre else []), *args, *c_ins)
    if jobs:
        absorb(list(outs[n_out:]))
    return list(outs[:n_out])


def _comm_call(name, comm):
    _call(lambda: None, name, grid=(), in_specs=[], out_specs=[], out_shape=[], args=[], comm=comm)


def _mm(name, form, a_list, b_list, *, M, N, K, tm, tn, tk, b_stacked=False, out_stacked=False,
        b_koffs=None, acc_of=None, n_acc=1, extras=(), out_dtypes=(F32,), epilogue=None, comm=None,
        outer="j", b_layer=None, out_into=None):
    assert M % tm == 0 and N % tn == 0 and K % tk == 0, (name, M, N, K, tm, tn, tk)
    nm, nn, nk = M // tm, N // tn, K // tk
    npairs = len(a_list)
    acc_of = acc_of or [0] * npairs
    b_koffs = b_koffs or [0] * npairs
    if epilogue is None:
        epilogue = lambda accs, ex: [accs[0]]

    def spec(shape, fn, **kw):
        return pl.BlockSpec(shape, fn if outer == "j" else (lambda i, j, k: fn(j, i, k)), **kw)

    slabs = form == "nt" and b_stacked and tk == K

    a_uniq = [a for p, a in enumerate(a_list) if all(a is not b for b in a_list[:p])]
    a_pos = [next(u for u, b in enumerate(a_uniq) if b is a) for a in a_list]
    in_specs = []
    for _ in a_uniq:
        if form == "tn":
            in_specs.append(spec((tk, tm), lambda j, i, k: (k, i)))
        else:
            in_specs.append(spec((tm, tk), lambda j, i, k: (i, k)))
    for off in b_koffs:
        if b_layer is not None:
            assert form == "nn" and not b_stacked
            in_specs.append(spec((None, tk, tn), lambda j, i, k: (b_layer, k, j)))
        elif form == "nn":
            if b_stacked:
                assert tn * N_CHIPS == N
                in_specs.append(spec((None, tk, tn), lambda j, i, k: (j, k, 0)))
            else:
                in_specs.append(spec((tk, tn), lambda j, i, k, off=off: (off + k, j)))
        elif form == "nt":
            if slabs:
                in_specs.append(spec((N_CHIPS, tn, K // N_CHIPS), lambda j, i, k: (0, j, 0),
                                     pipeline_mode=pl.Buffered(1)))
            elif b_stacked:
                assert tk * N_CHIPS == K
                in_specs.append(spec((None, tn, tk), lambda j, i, k: (k, j, 0)))
            else:
                in_specs.append(spec((tn, tk), lambda j, i, k, off=off: (j, off + k)))
        else:
            in_specs.append(spec((tk, tn), lambda j, i, k: (k, j)))
    for kind, _ in extras:
        if kind == "tile":
            in_specs.append(spec((tm, tn), lambda j, i, k: (i, j)))
        else:
            in_specs.append(spec((1, tn), lambda j, i, k: (0, j)))
    if out_stacked:
        assert tn * N_CHIPS == N
        out_specs = [spec((None, tm, tn), lambda j, i, k: (j, i, 0)) for _ in out_dtypes]
        out_shape = [SDS((N_CHIPS, M, tn), d) for d in out_dtypes]
    else:
        rows, roff, prev = out_into if out_into else (M, 0, None)
        out_specs = [spec((tm, tn), lambda j, i, k: (roff + i, j)) for _ in out_dtypes]
        out_shape = [SDS((rows, N), d) for d in out_dtypes]
    dims = {"nn": (((1,), (0,)), ((), ())), "nt": (((1,), (1,)), ((), ())), "tn": (((0,), (0,)), ((), ()))}[form]
    n_a, n_ex, n_out = len(a_uniq), len(extras), len(out_dtypes)

    def body(*refs):
        a_vals = [refs[u][...].astype(BF16) for u in range(n_a)]
        b_refs = refs[n_a:n_a + npairs]
        e_refs = refs[n_a + npairs:n_a + npairs + n_ex]
        o_refs = refs[n_a + npairs + n_ex:n_a + npairs + n_ex + n_out]
        acc_refs = refs[n_a + npairs + n_ex + n_out:]

        tot = [None] * n_acc
        for p in range(npairs):
            if slabs:
                n = K // N_CHIPS
                d = sum(lax.dot_general(a_vals[a_pos[p]][:, s * n:(s + 1) * n], b_refs[p][s], dims,
                                        preferred_element_type=F32) for s in range(N_CHIPS))
            else:
                d = lax.dot_general(a_vals[a_pos[p]], b_refs[p][...].astype(BF16), dims, preferred_element_type=F32)
            tot[acc_of[p]] = d if tot[acc_of[p]] is None else tot[acc_of[p]] + d

        def finish(accs):
            outs = epilogue(accs, [e[...] for e in e_refs])
            for o_ref, o in zip(o_refs, outs):
                o_ref[...] = o.astype(o_ref.dtype)

        if nk == 1:
            finish(tot)
        else:
            k = pl.program_id(2)

            @pl.when(k == 0)
            def _():
                for r, t in zip(acc_refs, tot):
                    r[...] = t

            @pl.when(k > 0)
            def _():
                for r, t in zip(acc_refs, tot):
                    r[...] += t

            @pl.when(k == nk - 1)
            def _():
                finish([r[...] for r in acc_refs])

    scratch = [pltpu.VMEM((tm, tn), F32) for _ in range(n_acc)] if nk > 1 else []
    return _call(body, name, grid=(nn, nm, nk) if outer == "j" else (nm, nn, nk), in_specs=in_specs,
                 out_specs=out_specs, out_shape=out_shape, scratch=scratch,
                 sem=("parallel", "parallel", "arbitrary"), comm=comm,
                 keep=[(out_into[2], 0)] if out_into and out_into[2] is not None else (),
                 args=[*a_uniq, *b_list, *[e for _, e in extras]])


def _rowwise(name, fn, rows, vecs, row_outs, acc_widths, tr, comm=None):
    S = rows[0].shape[0]
    assert S % tr == 0
    nr, nv, no, na = len(rows), len(vecs), len(row_outs), len(acc_widths)

    def body(*refs):
        r = [x[...] for x in refs[:nr]]
        v = [x[...] for x in refs[nr:nr + nv]]
        o_refs = refs[nr + nv:nr + nv + no]
        a_refs = refs[nr + nv + no:]
        outs, accs = fn(r, v)
        for o_ref, o in zip(o_refs, outs):
            o_ref[...] = o.astype(o_ref.dtype)
        first = pl.program_id(0) == 0

        @pl.when(first)
        def _():
            for a_ref, a in zip(a_refs, accs):
                a_ref[...] = a

        @pl.when(jnp.logical_not(first))
        def _():
            for a_ref, a in zip(a_refs, accs):
                a_ref[...] += a

    in_specs = [pl.BlockSpec((tr, x.shape[1]), lambda i: (i, 0)) for x in rows]
    in_specs += [pl.BlockSpec(x.shape, lambda i, nd=x.ndim: (0,) * nd) for x in vecs]
    out_specs = [pl.BlockSpec((tr, w), lambda i: (i, 0)) for w, _ in row_outs]
    out_specs += [pl.BlockSpec((1, w), lambda i: (0, 0)) for w in acc_widths]
    out_shape = [SDS((S, w), d) for w, d in row_outs] + [SDS((1, w), F32) for w in acc_widths]
    return _call(body, name, grid=(S // tr,), in_specs=in_specs, out_specs=out_specs, out_shape=out_shape,
                 sem=("arbitrary",), comm=comm, args=[*rows, *vecs])


def _colsum(a):
    return jnp.sum(a, axis=0, keepdims=True)


def _rms_stats(x):
    rstd = lax.rsqrt(jnp.mean(x * x, axis=1, keepdims=True) + EPS)
    return x * rstd, rstd


def _normmod_fwd(name, x, g, sc, sh, tr):
    def fn(r, v):
        n, _ = _rms_stats(r[0])
        return [(n * v[0]) * (1.0 + v[1]) + v[2]], []
    return _rowwise(name, fn, [x], [g, sc, sh], [(x.shape[1], BF16)], [], tr)[0]


def _gated(dx, nxt_out, gate):
    return gate * dx, _colsum(dx * nxt_out.astype(F32))


def _normmod_bwd(name, x, dh, dres, g, sc, tr, nxt=None, comm=None):
    def fn(r, v):
        x_, dh_, dres_ = r[:3]
        g_, sc_ = v[:2]
        n, rstd = _rms_stats(x_)
        hn = n * g_
        dhn = dh_ * (1.0 + sc_)
        dn = dhn * g_
        dx = dres_ + rstd * (dn - n * jnp.mean(dn * n, axis=1, keepdims=True))
        outs, accs = [dx], [_colsum(dh_), _colsum(dh_ * hn), _colsum(dhn * n)]
        if nxt:
            dog, dgate = _gated(dx, r[3], v[2])
            outs.append(dog)
            accs.append(dgate)
        return outs, accs
    D = x.shape[1]
    rows, vecs = [x, dh, dres] + ([nxt[0]] if nxt else []), [g, sc] + ([nxt[1]] if nxt else [])
    return _rowwise(name, fn, rows, vecs, [(D, F32)] + ([(D, BF16)] if nxt else []), [D] * (4 if nxt else 3), tr,
                    comm=comm)


def _loss_head(name, x, target, g, tr, nxt):
    D = x.shape[1]

    def fn(r, v):
        n, rstd = _rms_stats(r[0])
        err = n * v[0] - r[1]
        loss = 0.5 * jnp.sum(jnp.mean(err * err, axis=1, keepdims=True), axis=0, keepdims=True)
        dy = err * (1.0 / D)
        dn = dy * v[0]
        dx = rstd * (dn - n * jnp.mean(dn * n, axis=1, keepdims=True))
        dog, dgate = _gated(dx, r[2], v[1])
        return [dx, dog], [jnp.broadcast_to(loss, (1, LANE)), _colsum(dy * n), dgate]
    return _rowwise(name, fn, [x, target, nxt[0]], [g, nxt[1]], [(D, F32), (D, BF16)], [LANE, D, D], tr)


_INV_SQRT2 = 1.0 / math.sqrt(2.0)
_INV_SQRT2PI = 1.0 / math.sqrt(2.0 * math.pi)


def _gelu(a):
    return 0.5 * a * (1.0 + lax.erf(a * _INV_SQRT2))


def _gelu_grad(a):
    return 0.5 * (1.0 + lax.erf(a * _INV_SQRT2)) + a * jnp.exp(-0.5 * a * a) * _INV_SQRT2PI


def _ln_stats(v):
    mu = jnp.mean(v, axis=1, keepdims=True)
    vc = v - mu
    rstd = lax.rsqrt(jnp.mean(vc * vc, axis=1, keepdims=True) + EPS)
    return vc * rstd, rstd


def _chunk_mix(w_ref, vn, tr, G):
    rows = []
    for ch in range(tr // LANE):
        cols = []
        for g in range(G):
            blk = vn[ch * LANE:(ch + 1) * LANE, g * LANE:(g + 1) * LANE]
            cols.append(jnp.dot(w_ref[g], blk, preferred_element_type=F32))
        rows.append(jnp.concatenate(cols, axis=1))
    return jnp.concatenate(rows, axis=0)


def _sgu_mid_fwd(a, ln_g, ln_b, w_mask, bias_full, tr):
    S, D2 = a.shape
    D = D2 // 2
    G = D // LANE

    def body(a_ref, g_ref, b_ref, w_ref, bias_ref, y_ref):
        u = _gelu(a_ref[:, :D])
        v = _gelu(a_ref[:, D:])
        vhat, _ = _ln_stats(v)
        vn = (vhat * g_ref[...] + b_ref[...]).astype(BF16)
        sv = _chunk_mix(w_ref, vn, tr, G) + jnp.concatenate([bias_ref[...]] * (tr // LANE), axis=0)
        y_ref[...] = (u * sv).astype(BF16)

    return pl.pallas_call(
        body, name="sgu_mid_fwd", grid=(S // tr,),
        in_specs=[pl.BlockSpec((tr, D2), lambda i: (i, 0)), pl.BlockSpec((1, D), lambda i: (0, 0)),
                  pl.BlockSpec((1, D), lambda i: (0, 0)), pl.BlockSpec((G, LANE, LANE), lambda i: (0, 0, 0)),
                  pl.BlockSpec((LANE, D), lambda i: (0, 0))],
        out_specs=pl.BlockSpec((tr, D), lambda i: (i, 0)), out_shape=SDS((S, D), BF16),
        compiler_params=_params(("arbitrary",)))(a, ln_g, ln_b, w_mask, bias_full)


def _sgu_mid_bwd(a, dy, ln_g, ln_b, w_mask, bias_full, tr, comm=None):
    S, D2 = a.shape
    D = D2 // 2
    G = D // LANE
    nch = tr // LANE

    def body(a_ref, dy_ref, g_ref, b_ref, w_ref, bias_ref, da_ref, dw_ref, dbias_ref, dg_ref, db_ref, dbin_ref):
        au, av = a_ref[:, :D], a_ref[:, D:]
        u = _gelu(au)
        v = _gelu(av)
        vhat, rstd = _ln_stats(v)
        vn = (vhat * g_ref[...] + b_ref[...]).astype(BF16)
        sv = _chunk_mix(w_ref, vn, tr, G) + jnp.concatenate([bias_ref[...]] * nch, axis=0)
        dy_ = dy_ref[...].astype(F32)
        du = dy_ * sv
        dsv = dy_ * u
        dsv_b = dsv.astype(BF16)
        first = pl.program_id(0) == 0

        @pl.when(first)
        def _():
            dw_ref[...] = jnp.zeros_like(dw_ref)
            dbias_ref[...] = jnp.zeros_like(dbias_ref)
            dg_ref[...] = jnp.zeros_like(dg_ref)
            db_ref[...] = jnp.zeros_like(db_ref)
            dbin_ref[...] = jnp.zeros_like(dbin_ref)

        rows = []
        dbias = None
        for ch in range(nch):
            r0 = ch * LANE
            cols = []
            for g in range(G):
                c0 = g * LANE
                ds_blk = dsv_b[r0:r0 + LANE, c0:c0 + LANE]
                vn_blk = vn[r0:r0 + LANE, c0:c0 + LANE]
                cols.append(lax.dot_general(w_ref[g], ds_blk, (((0,), (0,)), ((), ())),
                                            preferred_element_type=F32))
                dw_ref[g] += lax.dot_general(ds_blk, vn_blk, (((1,), (1,)), ((), ())),
                                             preferred_element_type=F32)
            rows.append(jnp.concatenate(cols, axis=1))
            blk = dsv[r0:r0 + LANE, :]
            dbias = blk if dbias is None else dbias + blk
        dvn = jnp.concatenate(rows, axis=0)
        dbias_ref[...] += dbias
        dg_ref[...] += _colsum(dvn * vhat)
        db_ref[...] += _colsum(dvn)
        dvh = dvn * g_ref[...]
        dv = rstd * (dvh - jnp.mean(dvh, axis=1, keepdims=True)
                     - vhat * jnp.mean(dvh * vhat, axis=1, keepdims=True))
        da_u = du * _gelu_grad(au)
        da_v = dv * _gelu_grad(av)
        da_ref[:, :D] = da_u.astype(BF16)
        da_ref[:, D:] = da_v.astype(BF16)
        dbin_ref[:, :D] += _colsum(da_u)
        dbin_ref[:, D:] += _colsum(da_v)

    full = lambda shp: pl.BlockSpec(shp, lambda i, nd=len(shp): (0,) * nd)
    return _call(
        body, "sgu_mid_bwd", grid=(S // tr,),
        in_specs=[pl.BlockSpec((tr, D2), lambda i: (i, 0)), pl.BlockSpec((tr, D), lambda i: (i, 0)),
                  full((1, D)), full((1, D)), full((G, LANE, LANE)), full((LANE, D))],
        out_specs=[pl.BlockSpec((tr, D2), lambda i: (i, 0)), full((G, LANE, LANE)), full((LANE, D)),
                   full((1, D)), full((1, D)), full((1, D2))],
        out_shape=[SDS((S, D2), BF16), SDS((G, LANE, LANE), F32), SDS((LANE, D), F32),
                   SDS((1, D), F32), SDS((1, D), F32), SDS((1, D2), F32)],
        sem=("arbitrary",), comm=comm, args=[a, dy, ln_g, ln_b, w_mask, bias_full])


def _tri(lower):
    r = lax.broadcasted_iota(jnp.int32, (LANE, LANE), 0)
    c = lax.broadcasted_iota(jnp.int32, (LANE, LANE), 1)
    return jnp.where((c <= r) if lower else (c >= r), 1.0, 0.0).astype(F32)


def _fox_gate_fwd(fl, bf_pad):
    S = fl.shape[0]
    nblk = S // LANE

    def body(fl_ref, b_ref, f_ref):
        tri = _tri(True)

        def step(i, carry):
            r0 = pl.multiple_of(i * LANE, LANE)
            z = fl_ref[pl.ds(r0, LANE), :] + b_ref[...]
            logf = jnp.minimum(z, 0.0) - jnp.log1p(jnp.exp(-jnp.abs(z)))
            f = jnp.dot(tri, logf, preferred_element_type=F32, precision=lax.Precision.HIGHEST) + carry
            f_ref[pl.ds(r0, LANE), :] = f
            return f[LANE - 1:LANE, :]
        lax.fori_loop(0, nblk, step, jnp.zeros((1, LANE), F32))

    return pl.pallas_call(body, name="fox_gate_fwd", out_shape=SDS((S, LANE), F32),
                          compiler_params=_params())(fl, bf_pad)


def _fox_gate_bwd(fl, dF, bf_pad):
    S = fl.shape[0]
    nblk = S // LANE

    def body(fl_ref, df_ref, b_ref, dfl_ref, db_ref):
        tri = _tri(True)

        def prefix(i, carry):
            r0 = pl.multiple_of(i * LANE, LANE)
            p = jnp.dot(tri, df_ref[pl.ds(r0, LANE), :], preferred_element_type=F32,
                        precision=lax.Precision.HIGHEST) + carry
            dfl_ref[pl.ds(r0, LANE), :] = p
            return p[LANE - 1:LANE, :]
        total = lax.fori_loop(0, nblk, prefix, jnp.zeros((1, LANE), F32))
        db_ref[...] = jnp.zeros_like(db_ref)

        def finish(i, carry):
            r0 = pl.multiple_of(i * LANE, LANE)
            dlogf = total - dfl_ref[pl.ds(r0, LANE), :] + df_ref[pl.ds(r0, LANE), :]
            z = fl_ref[pl.ds(r0, LANE), :] + b_ref[...]
            dfl = dlogf / (1.0 + jnp.exp(z))
            dfl_ref[pl.ds(r0, LANE), :] = dfl
            db_ref[...] += _colsum(dfl)
            return carry
        lax.fori_loop(0, nblk, finish, 0)

    return pl.pallas_call(body, name="fox_gate_bwd", out_shape=[SDS((S, LANE), F32), SDS((1, LANE), F32)],
                          compiler_params=_params())(fl, dF, bf_pad)


def _lane_pick(blk, h):
    lane = lax.broadcasted_iota(jnp.int32, blk.shape, 1)
    return jnp.sum(jnp.where(lane == h, blk, 0.0), axis=1, keepdims=True)


_NEG = -1e30
_LOG2E = 1.0 / math.log(2.0)
_LN2 = math.log(2.0)
_SUB = 64


def _attn_fwd(qkv, F_sh, F_rows, H, tq, comm=None):
    S = qkv.shape[0]
    D = H * LANE
    nq = S // tq
    scale = 1.0 / math.sqrt(LANE)

    def body(q_ref, k_ref, v_ref, fsh_ref, frow_ref, o_ref, lse_ref):
        h, i = pl.program_id(0), pl.program_id(1)
        q = q_ref[...]
        fq = _lane_pick(fsh_ref[...], h) * _LOG2E

        def block(j, carry, diagonal):
            m, l, acc = carry
            r0 = pl.multiple_of(j * tq, tq)
            k = k_ref[pl.ds(r0, tq), :]
            v = v_ref[pl.ds(r0, tq), :]
            s = lax.dot_general(q, k, (((1,), (1,)), ((), ())), preferred_element_type=F32)
            s = s + (fq - frow_ref[j])
            if diagonal:
                row = lax.broadcasted_iota(jnp.int32, (tq, tq), 0)
                col = lax.broadcasted_iota(jnp.int32, (tq, tq), 1)
                s = jnp.where(col <= row, s, _NEG)
            m_new = jnp.maximum(m, jnp.max(s, axis=1, keepdims=True))
            alpha = jnp.exp2(m - m_new)
            p = jnp.exp2(s - m_new)
            l = alpha * l + jnp.sum(p, axis=1, keepdims=True)
            acc = alpha * acc + jnp.dot(p.astype(BF16), v, preferred_element_type=F32)
            return m_new, l, acc

        init = (jnp.full((tq, 1), _NEG, F32), jnp.zeros((tq, 1), F32), jnp.zeros((tq, LANE), F32))
        carry = lax.fori_loop(0, i, lambda j, cr: block(j, cr, False), init)
        m, l, acc = block(i, carry, True)
        o_ref[...] = (acc / l).astype(BF16)
        lse_ref[...] = (m + jnp.log2(l)) * _LN2

    return _call(
        body, "attn_fwd", grid=(H, nq),
        in_specs=[pl.BlockSpec((tq, LANE), lambda h, i: (i, h)),
                  pl.BlockSpec((S, LANE), lambda h, i: (0, H + h)),
                  pl.BlockSpec((S, LANE), lambda h, i: (0, 2 * H + h)),
                  pl.BlockSpec((tq, LANE), lambda h, i: (i, 0)),
                  pl.BlockSpec((None, nq, 1, tq), lambda h, i: (h, 0, 0, 0))],
        out_specs=[pl.BlockSpec((tq, LANE), lambda h, i: (i, h)),
                   pl.BlockSpec((None, tq, 1), lambda h, i: (h, i, 0))],
        out_shape=[SDS((S, D), BF16), SDS((H, S, 1), F32)],
        sem=("parallel", "arbitrary"), comm=comm, args=[qkv, qkv, qkv, F_sh, F_rows])


def _attn_bwd(qkv, do, F_sh, A_rows, delta_rows, H, tq, comm=None):
    S = qkv.shape[0]
    D = H * LANE
    nq = S // tq
    scale = 1.0 / math.sqrt(LANE)

    def body(q_ref, do_ref, k_ref, v_ref, fsh_ref, a_ref, dl_ref, dq_ref, dk_ref, dv_ref, df_ref, dfr_ref,
             dq_acc, dfr_acc, st_scr, dp_scr, pt_scr, ds_scr, fk_scr, df_scr, dk_scr, dv_scr):
        h, j = pl.program_id(0), pl.program_id(1)

        @pl.when(j == 0)
        def _():
            dq_acc[...] = jnp.zeros_like(dq_acc)
            dfr_acc[...] = jnp.zeros_like(dfr_acc)

        k = k_ref[...]
        v = v_ref[...]
        fk_scr[...] = _lane_pick(fsh_ref[...], h) * _LOG2E
        df_scr[...] = jnp.zeros((tq, 1), F32)
        dk_scr[...] = jnp.zeros((tq, LANE), F32)
        dv_scr[...] = jnp.zeros((tq, LANE), F32)

        def block(i, diagonal):
            r0 = pl.multiple_of(i * tq, tq)
            q = q_ref[pl.ds(r0, tq), :]
            do_ = do_ref[pl.ds(r0, tq), :]
            st_scr[...] = lax.dot_general(k, q, (((1,), (1,)), ((), ())), preferred_element_type=F32)
            dp_scr[...] = lax.dot_general(v, do_, (((1,), (1,)), ((), ())), preferred_element_type=F32)
            a_row, dl_row = a_ref[i], dl_ref[i]
            dfr = jnp.zeros((1, tq), F32)
            for r in range(tq // _SUB):
                rows = slice(r * _SUB, (r + 1) * _SUB)
                arg = st_scr[rows, :] + (a_row - fk_scr[rows, :])
                if diagonal:
                    row = lax.broadcasted_iota(jnp.int32, (_SUB, tq), 0) + r * _SUB
                    col = lax.broadcasted_iota(jnp.int32, (_SUB, tq), 1)
                    arg = jnp.where(row <= col, arg, _NEG)
                pt = jnp.exp2(arg)
                dst = pt * (dp_scr[rows, :] - dl_row)
                df_scr[rows, :] += jnp.sum(dst, axis=1, keepdims=True)
                dfr = dfr + jnp.sum(dst, axis=0, keepdims=True)
                pt_scr[rows, :] = pt.astype(BF16)
                ds_scr[rows, :] = dst.astype(BF16)
            dfr_acc[i] += dfr
            dsb = ds_scr[...]
            dv_scr[...] += jnp.dot(pt_scr[...], do_, preferred_element_type=F32)
            dk_scr[...] += jnp.dot(dsb, q, preferred_element_type=F32)
            dq_acc[pl.ds(r0, tq), :] += lax.dot_general(dsb, k, (((0,), (0,)), ((), ())),
                                                        preferred_element_type=F32)

        def full_block(i, carry):
            block(i, False)
            return carry

        block(j, True)
        lax.fori_loop(j + 1, nq, full_block, 0)
        dk_ref[...] = (dk_scr[...] * _LN2).astype(BF16)
        dv_ref[...] = dv_scr[...].astype(BF16)
        df_ref[...] = -df_scr[...]

        @pl.when(j == nq - 1)
        def _():
            dq_ref[...] = (dq_acc[...] * scale).astype(BF16)
            dfr_ref[...] = dfr_acc[...]

    return _call(
        body, "attn_bwd", grid=(H, nq),
        in_specs=[pl.BlockSpec((S, LANE), lambda h, j: (0, h)),
                  pl.BlockSpec((S, LANE), lambda h, j: (0, h)),
                  pl.BlockSpec((tq, LANE), lambda h, j: (j, H + h)),
                  pl.BlockSpec((tq, LANE), lambda h, j: (j, 2 * H + h)),
                  pl.BlockSpec((tq, LANE), lambda h, j: (j, 0)),
                  pl.BlockSpec((None, nq, 1, tq), lambda h, j: (h, 0, 0, 0)),
                  pl.BlockSpec((None, nq, 1, tq), lambda h, j: (h, 0, 0, 0))],
        out_specs=[pl.BlockSpec((S, LANE), lambda h, j: (0, h)),
                   pl.BlockSpec((tq, LANE), lambda h, j: (j, h)),
                   pl.BlockSpec((tq, LANE), lambda h, j: (j, h)),
                   pl.BlockSpec((None, tq, 1), lambda h, j: (h, j, 0)),
                   pl.BlockSpec((None, nq, 1, tq), lambda h, j: (h, 0, 0, 0))],
        out_shape=[SDS((S, D), BF16), SDS((S, D), BF16), SDS((S, D), BF16), SDS((H, S, 1), F32),
                   SDS((H, nq, 1, tq), F32)],
        scratch=[pltpu.VMEM((S, LANE), F32), pltpu.VMEM((nq, 1, tq), F32),
                 pltpu.VMEM((tq, tq), F32), pltpu.VMEM((tq, tq), F32), pltpu.VMEM((tq, tq), BF16),
                 pltpu.VMEM((tq, tq), BF16), pltpu.VMEM((tq, 1), F32), pltpu.VMEM((tq, 1), F32),
                 pltpu.VMEM((tq, LANE), F32), pltpu.VMEM((tq, LANE), F32)],
        sem=("parallel", "arbitrary"), comm=comm, args=[qkv, do, qkv, qkv, F_sh, A_rows, delta_rows])


def _head_dots(do, o, H, tr):
    def fn(r, v):
        prod = r[0].astype(F32) * r[1].astype(F32)
        lane = lax.broadcasted_iota(jnp.int32, (prod.shape[0], LANE), 1)
        out = jnp.zeros((prod.shape[0], LANE), F32)
        for h in range(H):
            s = jnp.sum(prod[:, h * LANE:(h + 1) * LANE], axis=1, keepdims=True)
            out = jnp.where(lane == h, s, out)
        return [out], []
    return _rowwise("attn_delta", fn, [do, o], [], [(LANE, F32)], [], tr)[0]


def _adamw(name, w, g, m, v, layer=None, prev=None):
    R, C = g.shape
    tr = _pick(R, max(8, (512 * 1024) // max(C, 1) // 8 * 8), 8)
    c1 = 1.0 - ADAM_B1 ** ADAM_STEP
    c2 = 1.0 - ADAM_B2 ** ADAM_STEP

    def body(w_ref, g_ref, m_ref, v_ref, go_ref, d_ref, mo_ref, vo_ref):
        g_ = g_ref[...]
        m_ = ADAM_B1 * m_ref[...] + (1.0 - ADAM_B1) * g_
        v_ = ADAM_B2 * v_ref[...] + (1.0 - ADAM_B2) * (g_ * g_)
        go_ref[...] = g_
        d_ref[...] = -ADAM_LR * ((m_ / c1) / (jnp.sqrt(v_ / c2) + ADAM_EPS) + ADAM_WD * w_ref[...])
        mo_ref[...] = m_
        vo_ref[...] = v_

    if w.ndim == 3 and w.shape[1] == 1 and layer is None:
        tc = _pick(C, 2 * LANE, LANE)
        gspec = pl.BlockSpec((R, tc), lambda i: (0, i))
        pspec = pl.BlockSpec((R, None, tc), lambda i: (0, 0, i))
        grid = (C // tc,)
    else:
        gspec = pl.BlockSpec((tr, C), lambda i: (i, 0))
        pspec = gspec if layer is None else pl.BlockSpec((None, tr, C), lambda i: (layer, i, 0))
        grid = (R // tr,)
    return _call(body, name, grid=grid, in_specs=[pspec, gspec, pspec, pspec], out_specs=[pspec] * 4,
                 out_shape=[SDS(w.shape, F32)] * 4, sem=("parallel",), args=[w, g, m, v],
                 keep=[(p, k) for k, p in enumerate(prev)] if prev else ())


def _sum_slots(name, a):
    n, R, C = a.shape
    tr = _pick(R, 256, 8)

    def body(a_ref, o_ref):
        acc = a_ref[0]
        for k in range(1, n):
            acc = acc + a_ref[k]
        o_ref[...] = acc

    return pl.pallas_call(body, name=name, grid=(R // tr,),
                          in_specs=[pl.BlockSpec((n, tr, C), lambda i: (0, i, 0))],
                          out_specs=pl.BlockSpec((tr, C), lambda i: (i, 0)), out_shape=SDS((R, C), F32),
                          compiler_params=_params(("parallel",)))(a)


def _silu_rows(name, a):
    def body(a_ref, o_ref):
        z = a_ref[...]
        o_ref[...] = (z / (1.0 + jnp.exp(-z))).astype(BF16)
    return pl.pallas_call(body, name=name, out_shape=SDS(a.shape, BF16), compiler_params=_params())(a)


def _place():
    return lax.axis_index("x"), lax.axis_index("y"), lax.axis_index("c")


def _rcopy(src, dst, ssem, rsem, dev):
    return pltpu.make_async_remote_copy(src_ref=src, dst_ref=dst, send_sem=ssem, recv_sem=rsem,
                                        device_id=dev, device_id_type=MESH)


def _other_chips(x, y):
    return [(1 - x, y), (x, 1 - y), (1 - x, 1 - y)]


def _job_gather_devices(buf8):
    def views(o):
        x, y, c = _place()
        slot = lambda px, py, pc: o.at[4 * px + 2 * py + pc]
        return c, (x, y, 1 - c), _other_chips(x, y), slot, slot(x, y, c)

    def first(o, ss, rs):
        c, sib, chips, slot, me = views(o)
        return [_rcopy(me, me, ss.at[0], rs.at[0], sib)] + [
            _rcopy(me, me, ss.at[1 + j], rs.at[1 + j], (cx, cy, c)) for j, (cx, cy) in enumerate(chips)]

    def start(i, o, ss, rs):
        for cp in first(o[0], ss, rs):
            cp.start()

    def wait(i, o, ss, rs):
        c, sib, chips, slot, me = views(o[0])
        x, y = sib[0], sib[1]
        passed = []
        for j, (cx, cy) in enumerate(chips):
            got = slot(cx, cy, c)
            _rcopy(me, got, ss.at[1 + j], rs.at[1 + j], (cx, cy, c)).wait_recv()
            passed.append(_rcopy(got, got, ss.at[4 + j], rs.at[4 + j], sib))
            passed[j].start()
        _rcopy(me, slot(x, y, 1 - c), ss.at[0], rs.at[0], sib).wait_recv()
        for j, (cx, cy) in enumerate(chips):
            _rcopy(me, slot(cx, cy, 1 - c), ss.at[4 + j], rs.at[4 + j], sib).wait_recv()
        for cp in first(o[0], ss, rs) + passed:
            cp.wait_send()

    return _Job([buf8], [SDS(buf8.shape, buf8.dtype)], {0: 0}, N_DEV - 1, start, wait)


def _gather_devices(name, buf, slot_idx, extra=None):
    R, C = buf.shape

    def place(s_ref, b_ref, o_ref):
        o_ref[...] = b_ref[...]

    buf8 = _call(place, name + "_place", grid=(1,), prefetch=slot_idx, args=[buf],
                 in_specs=[pl.BlockSpec((R, C), lambda i, s: (0, 0))],
                 out_specs=[pl.BlockSpec((None, R, C), lambda i, s: (s[0], 0, 0))],
                 out_shape=[SDS((N_DEV, R, C), buf.dtype)])[0]
    jobs, absorb_extra = extra if extra else ([], None)
    got = []

    def absorb(outs):
        got.append(outs[0])
        if absorb_extra:
            absorb_extra(outs[1:])

    _comm_call(name, ([_job_gather_devices(buf8)] + jobs, absorb))
    return got[0]


def _cast_slabs(name, ws, idx, comm=None):
    n = len(ws)
    steps = next(s for s in (8, 4, 2, 1)
                 if all((w.shape[2] % (LANE * s) if l is None else w.shape[1] % (16 * s)) == 0 for w, l in ws))

    def body(s_ref, *refs):
        for w_ref, o_ref in zip(refs[:n], refs[n:]):
            o_ref[...] = w_ref[...].astype(BF16)

    in_specs, out_specs, out_shape = [], [], []
    for w, l in ws:
        if l is None:
            R, _, C = w.shape
            in_specs.append(pl.BlockSpec((R, None, C // steps), lambda i, s: (0, 0, i)))
            out_specs.append(pl.BlockSpec((None, R, C // steps), lambda i, s: (s[1], 0, i)))
        else:
            _, R, C = w.shape
            in_specs.append(pl.BlockSpec((None, R // steps, C), lambda i, s, l=l: (l, i, 0)))
            out_specs.append(pl.BlockSpec((None, R // steps, C), lambda i, s: (s[1], i, 0)))
        out_shape.append(SDS((N_CHIPS, R, C), BF16))
    return _call(body, name, grid=(steps,), prefetch=idx, comm=comm, sem=("parallel",), args=[w for w, _ in ws],
                 in_specs=in_specs, out_specs=out_specs, out_shape=out_shape)


def _half(ref, lead, hf, cols):
    n = ref.shape[-1 if cols else -2] // 2
    assert 2 * n == ref.shape[-1 if cols else -2]
    cut = pl.ds(hf * n, n)
    return ref.at[(*lead, slice(None), cut) if cols else (*lead, cut, slice(None))]


def _job_gather_ici(buf, cols=False):
    def views(o):
        x, y, c = _place()
        return c, _other_chips(x, y), _half(o, (2 * x + y,), c, cols)

    def start(i, o, ss, rs):
        c, chips, mine = views(o[0])
        for j, (cx, cy) in enumerate(chips):
            _rcopy(mine, mine, ss.at[j], rs.at[j], (cx, cy, c)).start()

    def wait(i, o, ss, rs):
        c, chips, mine = views(o[0])
        for j, (cx, cy) in enumerate(chips):
            cp = _rcopy(mine, _half(o[0], (2 * cx + cy,), c, cols), ss.at[j], rs.at[j], (cx, cy, c))
            cp.wait_send()
            cp.wait_recv()

    return _Job([buf], [SDS(buf.shape, buf.dtype)], {0: 0}, 3, start, wait)


def _job_gather_pair(buf, cols=False):
    def views(o):
        x, y, c = _place()
        return c, (x, y, 1 - c), _other_chips(x, y)

    def start(i, o, ss, rs):
        c, sib, chips = views(o[0])
        for j, (cx, cy) in enumerate(chips):
            got = _half(o[0], (2 * cx + cy,), c, cols)
            _rcopy(got, got, ss.at[j], rs.at[j], sib).start()

    def wait(i, o, ss, rs):
        c, sib, chips = views(o[0])
        for j, (cx, cy) in enumerate(chips):
            cp = _rcopy(_half(o[0], (2 * cx + cy,), c, cols), _half(o[0], (2 * cx + cy,), 1 - c, cols),
                        ss.at[j], rs.at[j], sib)
            cp.wait_send()
            cp.wait_recv()

    return _Job([buf], [SDS(buf.shape, buf.dtype)], {0: 0}, 3, start, wait)


class _Gather:
    def __init__(self, bufs, by_cols=()):
        self.todo, self.half, self.done, self.by_cols = dict(bufs), {}, {}, set(by_cols)

    def comm(self, admit=()):
        second, first = list(self.half), list(admit)
        jobs = [_job_gather_pair(self.half[n], n in self.by_cols) for n in second]
        jobs += [_job_gather_ici(self.todo[n], n in self.by_cols) for n in first]

        def absorb(outs):
            for n, o in zip(second + first, outs):
                if n in self.half:
                    del self.half[n]
                    self.done[n] = o
                else:
                    del self.todo[n]
                    self.half[n] = o
        return (jobs, absorb) if jobs else None


def _half_shape(R, C, cols):
    return (R, C // 2) if cols else (R // 2, C)


def _job_rs_pair(g4, cols=False):
    _, R, C = g4.shape

    def desc(i, o, ss, rs):
        x, y, c = _place()
        return _rcopy(_half(i[0], (slice(None),), 1 - c, cols), o[0], ss.at[0], rs.at[0], (x, y, 1 - c))

    return _Job([g4], [SDS((N_CHIPS,) + _half_shape(R, C, cols), F32)], {}, 1,
                lambda i, o, ss, rs: desc(i, o, ss, rs).start(), lambda i, o, ss, rs: desc(i, o, ss, rs).wait())


def _job_rs_chips(p4):
    _, hR, C = p4.shape

    def descs(i, o, ss, rs):
        x, y, c = _place()
        return [_rcopy(i[0].at[2 * cx + cy], o[0].at[j], ss.at[j], rs.at[j], (cx, cy, c))
                for j, (cx, cy) in enumerate(_other_chips(x, y))]

    def start(i, o, ss, rs):
        for cp in descs(i, o, ss, rs):
            cp.start()

    def wait(i, o, ss, rs):
        for cp in descs(i, o, ss, rs):
            cp.wait_send()
            cp.wait_recv()

    return _Job([p4], [SDS((3, hR, C), p4.dtype)], {}, 3, start, wait)


def _job_rs_join(buf, cols=False):
    def desc(o, ss, rs, recv):
        x, y, c = _place()
        mine = _half(o[0], (), c, cols)
        return _rcopy(mine, _half(o[0], (), 1 - c, cols) if recv else mine, ss.at[0], rs.at[0], (x, y, 1 - c))

    def wait(i, o, ss, rs):
        cp = desc(o, ss, rs, True)
        cp.wait_send()
        cp.wait_recv()

    return _Job([buf], [SDS(buf.shape, buf.dtype)], {0: 0}, 1,
                lambda i, o, ss, rs: desc(o, ss, rs, False).start(), wait)


def _walk(R, C, cols):
    if cols:
        tc = _pick(C, 2 * LANE, LANE)
        return (R, tc), C // tc
    tr = _pick(R, max(16, (1 << 20) // C // 16 * 16), 16)
    return (tr, C), R // tr


def _rs_add_pair(name, g4, recv, idx, cols=False):
    _, R, C = g4.shape
    hshape = _half_shape(R, C, cols)
    blk, nb = _walk(*hshape, cols)
    at = (lambda k, i: (k, 0, i)) if cols else (lambda k, i: (k, i, 0))

    def body(s_ref, a_ref, b_ref, pb_ref, po_ref):
        s = a_ref[...] + b_ref[...]
        pb_ref[...] = s.astype(BF16)

        @pl.when(pl.program_id(1) == s_ref[1])
        def _():
            po_ref[...] = s

    gs = pltpu.PrefetchScalarGridSpec(
        num_scalar_prefetch=1, grid=(nb, N_CHIPS),
        in_specs=[pl.BlockSpec((None,) + blk, lambda i, k, s: at(k, s[0] * nb + i)),
                  pl.BlockSpec((None,) + blk, lambda i, k, s: at(k, i))],
        out_specs=[pl.BlockSpec((None,) + blk, lambda i, k, s: at(k, i)),
                   pl.BlockSpec(blk, lambda i, k, s: at(k, i)[1:])])
    return pl.pallas_call(body, name=name, grid_spec=gs,
                          out_shape=[SDS((N_CHIPS,) + hshape, BF16), SDS(hshape, F32)],
                          compiler_params=_params(("parallel", "arbitrary")))(idx, g4, recv)


def _rs_add_chips(name, own, recv3, idx, cols=False):
    hR, hC = own.shape
    blk, nb = _walk(hR, hC, cols)
    at = (lambda i: (0, i)) if cols else (lambda i: (i, 0))

    def body(s_ref, a_ref, b_ref, o_ref):
        o_ref[...] = ((a_ref[...] + b_ref[0].astype(F32)) + b_ref[1].astype(F32)) + b_ref[2].astype(F32)

    gs = pltpu.PrefetchScalarGridSpec(
        num_scalar_prefetch=1, grid=(nb,),
        in_specs=[pl.BlockSpec(blk, lambda i, s: at(i)), pl.BlockSpec((3,) + blk, lambda i, s: (0,) + at(i))],
        out_specs=pl.BlockSpec(blk, lambda i, s: at(s[0] * nb + i)))
    return pl.pallas_call(body, name=name, grid_spec=gs,
                          out_shape=SDS((hR, 2 * hC) if cols else (2 * hR, hC), F32),
                          compiler_params=_params(("parallel",)))(idx, own, recv3)


class _ReduceScatter:
    def __init__(self, idx):
        self.idx, self.items, self.done = idx, [], {}

    def push(self, tag, g4, cols=False):
        self.items.append([tag, 0, g4, cols])

    def comm(self, ici=1):
        cur, jobs = [], []
        for item in self.items:
            tag, stage, data, cols = item
            if stage == 1:
                if ici == 0:
                    continue
                ici -= 1
            cur.append(item)
            jobs.append(_job_rs_pair(data, cols) if stage == 0 else _job_rs_chips(data[0]) if stage == 1
                        else _job_rs_join(data, cols))

        def absorb(outs):
            for item, o in zip(cur, outs):
                tag, stage, data, cols = item
                if stage == 0:
                    item[1:3] = [1, _rs_add_pair(tag + "_rs_add2", data, o, self.idx, cols)]
                elif stage == 1:
                    item[1:3] = [2, _rs_add_chips(tag + "_rs_add4", data[1], o, self.idx, cols)]
                else:
                    self.items.remove(item)
                    self.done[tag] = o
        return (jobs, absorb) if jobs else None


def _resid_epilogue(accs, ex):
    return [ex[0] + ex[1] * accs[0], accs[0]]


def _sigmoid(x):
    return 0.5 * jnp.tanh(0.5 * x) + 0.5


def _swiglu_epilogue(accs, ex):
    gt, up = accs
    return [gt, up, gt * _sigmoid(gt) * up]


def _swiglu_bwd_epilogue(accs, ex):
    dact = accs[0]
    gt, up = ex[0].astype(F32), ex[1].astype(F32)
    sg = _sigmoid(gt)
    silu = gt * sg
    return [dact * up * (sg + silu * (1.0 - sg)), dact * silu]


def kernel(x, c, ada_w, ada_b, norm_mix_g, norm_ffn_g, a_w_in, a_b_in, a_ln_g, a_ln_b, a_w_s, a_b_s, a_w_out, b_w_in, b_b_f, b_w_out, ffn_w_gate, ffn_w_up, ffn_w_down, final_g, loss_target, m_ada_w, m_ada_b, m_norm_mix_g, m_norm_ffn_g, m_a_w_in, m_a_b_in, m_a_ln_g, m_a_ln_b, m_a_w_s, m_a_b_s, m_a_w_out, m_b_w_in, m_b_b_f, m_b_w_out, m_ffn_w_gate, m_ffn_w_up, m_ffn_w_down, m_final_g, v_ada_w, v_ada_b, v_norm_mix_g, v_norm_ffn_g, v_a_w_in, v_a_b_in, v_a_ln_g, v_a_ln_b, v_a_w_s, v_a_b_s, v_a_w_out, v_b_w_in, v_b_b_f, v_b_w_out, v_ffn_w_gate, v_ffn_w_up, v_ffn_w_down, v_final_g):
    S, D = x.shape[1], x.shape[2]
    H = D // LANE
    G = D // LANE
    FH = ffn_w_down.shape[1] * N_CHIPS
    depth = ada_w.shape[0]
    assert depth == 2 and a_w_in.shape[0] == 1 and b_w_in.shape[0] == 1
    mx, my, mc = _place()
    chip = 2 * mx + my
    dev = 4 * mx + 2 * my + mc
    x0 = x[0]
    target = loss_target[0]
    tr = _pick(S, 256, 8)
    tm = _pick(S, 512, 8)
    tq_f = _pick(S, 1024, LANE)
    tq = _pick(S, 512, LANE)
    nq = S // tq

    idx = jnp.stack([mc, chip]).astype(jnp.int32)
    t3 = lambda a: jnp.transpose(a, (2, 0, 1))
    shards = {"a_in": (a_w_in, 0), "a_out": (a_w_out, 0), "b_in": (t3(b_w_in), None), "b_out": (b_w_out, 0)}
    for l in range(depth):
        shards.update({f"wg{l}": (ffn_w_gate, l), f"wu{l}": (ffn_w_up, l), f"wd{l}": (ffn_w_down, l)})
    first = ["a_in", "a_out"]
    rest = [n for n in shards if n not in first]
    ag = _Gather(dict(zip(first, _cast_slabs("cast_first", [shards[n] for n in first], idx))), by_cols=["b_in"])
    ag.todo.update(zip(rest, _cast_slabs("cast_rest", [shards[n] for n in rest], idx, comm=ag.comm(first))))
    _comm_call("ag_first_pair", ag.comm(["wg0"]))
    wa_in4 = ag.done["a_in"]
    wa_out = ag.done["a_out"].reshape(D, D)

    dev_idx = dev.astype(jnp.int32).reshape(1)
    c_all = _gather_devices("ag_c", jnp.pad(c, ((0, 7), (0, 0))), dev_idx).reshape(N_DEV, 8, D)[:, 0, :]
    c_act = _silu_rows("c_silu", c_all)
    n_loc = ada_w.shape[2]
    mods = []
    for l in range(depth):
        b_loc = lax.dynamic_slice_in_dim(ada_b[l:l + 1], chip * n_loc, n_loc, axis=1)
        mods.append(_mm(f"ada_fwd{l}", "nn", [c_act], [ada_w], M=N_DEV, N=n_loc, K=D, tm=N_DEV, b_layer=l,
                        tn=_pick(n_loc, 1024, LANE), tk=D, extras=[("row", b_loc)],
                        epilogue=lambda accs, ex: [accs[0] + ex[0]])[0])
    mod_all = _gather_devices("ag_mod", jnp.concatenate(mods, axis=1), dev_idx)
    mod_mine = lax.dynamic_index_in_dim(mod_all[0::2], dev, axis=1, keepdims=False)
    mod_mine = mod_mine.reshape(N_CHIPS, depth, n_loc).transpose(1, 0, 2).reshape(depth, 1, N_MOD * D)
    mod = [[mod_mine[l, :, i * D:(i + 1) * D] for i in range(N_MOD)] for l in range(depth)]

    row = lambda a: a.reshape(1, -1)

    tril = jnp.tril(jnp.ones((LANE, LANE), dtype=bool))
    w_mask = jnp.where(tril[None], a_w_s[0], 0.0).astype(BF16)
    bias_full = jnp.repeat(a_b_s[0].T, LANE, axis=1)
    bf_pad = jnp.pad(b_b_f, ((0, 0), (0, LANE - H)))

    admit = {"sgu_in": ["wu0"], "mix_out0": ["wd0"], "ffn_in0": ["b_in", "b_out"], "ffn_out0": [],
             "fox_qkv": ["wg1"], "attn_fwd": ["wu1", "wd1"], "mix_out1": []}
    saved = []
    xs = x0
    for l in range(depth):
        sh1, sc1, g1, sh2, sc2, g2 = mod[l]
        st = {"x_in": xs}
        h1 = _normmod_fwd(f"normmod_mix{l}", xs, row(norm_mix_g[l]), sc1, sh1, tr)
        st["h1"] = h1
        if l == 0:
            a = _mm("sgu_in", "nn", [h1], [wa_in4], M=S, N=2 * D, K=D, tm=tm, tn=2 * D // N_CHIPS, tk=D,
                    b_stacked=True, extras=[("row", a_b_in)], comm=ag.comm(admit["sgu_in"]),
                    epilogue=lambda accs, ex: [accs[0] + ex[0]])[0]
            y = _sgu_mid_fwd(a, a_ln_g, a_ln_b, w_mask, bias_full, tr)
            st["a"], st["y"] = a, y
            w_o, mix_out = wa_out, y
        else:
            w_qkv = ag.done["b_in"].reshape(-1, D)
            w_f = jnp.pad(w_qkv[3 * D:], ((0, LANE - H), (0, 0)))
            q_scale = jnp.concatenate([jnp.full((1, D), _LOG2E / math.sqrt(LANE), F32), jnp.ones((1, 2 * D), F32)],
                                      axis=1)
            qkv = _mm("fox_qkv", "nt", [h1], [w_qkv], M=S, N=3 * D, K=D, tm=tm, tn=_pick(3 * D, 1024, LANE),
                      tk=D, out_dtypes=(BF16,), extras=[("row", q_scale)],
                      epilogue=lambda accs, ex: [accs[0] * ex[0]], comm=ag.comm(admit["fox_qkv"]))[0]
            fl = _mm("fox_f", "nt", [h1], [w_f], M=S, N=LANE, K=D, tm=tm, tn=LANE, tk=D)[0]
            F_sh = _fox_gate_fwd(fl, bf_pad)
            F_hs = F_sh[:, :H].T
            F_rows = F_hs.reshape(H, nq, 1, tq)
            o, lse = _attn_fwd(qkv, F_sh, F_hs.reshape(H, S // tq_f, 1, tq_f) * _LOG2E, H, tq_f,
                               comm=ag.comm(admit["attn_fwd"]))
            st.update(qkv=qkv, fl=fl, F_sh=F_sh, F_rows=F_rows, o=o, lse=lse, w_qkv=w_qkv, w_f=w_f)
            w_o, mix_out = ag.done["b_out"].reshape(D, D), o
        x1, out1 = _mm(f"mix_out{l}", "nn", [mix_out], [w_o], M=S, N=D, K=D, tm=tm, tn=_pick(D, 1024, LANE),
                       tk=D, extras=[("tile", xs), ("row", g1)], out_dtypes=(F32, BF16),
                       epilogue=_resid_epilogue, comm=ag.comm(admit[f"mix_out{l}"]))
        st["x_mid"], st["out1"] = x1, out1
        h2 = _normmod_fwd(f"normmod_ffn{l}", x1, row(norm_ffn_g[l]), sc2, sh2, tr)
        gt, up, act = _mm(f"ffn_in{l}", "nn", [h2, h2], [ag.done[f"wg{l}"], ag.done[f"wu{l}"]], M=S, N=FH, K=D,
                          tm=tm, tn=FH // N_CHIPS, tk=D, b_stacked=True, acc_of=[0, 1], n_acc=2,
                          out_dtypes=(BF16, BF16, BF16), epilogue=_swiglu_epilogue,
                          comm=ag.comm(admit.get(f"ffn_in{l}", ())))
        x2, out2 = _mm(f"ffn_out{l}", "nn", [act], [ag.done[f"wd{l}"].reshape(FH, D)], M=S, N=D, K=FH, tm=tm,
                       tn=_pick(D, 1024, LANE), tk=FH, extras=[("tile", x1), ("row", g2)],
                       out_dtypes=(F32, BF16), epilogue=_resid_epilogue,
                       comm=ag.comm(admit.get(f"ffn_out{l}", ())))
        st.update(h2=h2, gt=gt, up=up, act=act, out2=out2)
        saved.append(st)
        xs = x2
    assert not ag.todo and not ag.half
    wg4 = [ag.done[f"wg{l}"] for l in range(depth)]
    wu4 = [ag.done[f"wu{l}"] for l in range(depth)]
    wd = [ag.done[f"wd{l}"].reshape(FH, D) for l in range(depth)]
    wb_out = ag.done["b_out"].reshape(D, D)

    dx, dog, loss_vec, g_final, dg2 = _loss_head("loss_head", xs, target, row(final_g), tr,
                                                 nxt=(saved[-1]["out2"], mod[-1][5]))
    loss = lax.psum(loss_vec[0, 0], ("x", "y", "c"))

    rs = _ReduceScatter(idx)
    dmods = [None] * depth
    gmix = [None] * depth
    gffn = [None] * depth
    for l in reversed(range(depth)):
        sh1, sc1, g1, sh2, sc2, g2 = mod[l]
        st = saved[l]
        dgt, dup = _mm(f"ffn_dact{l}", "nt", [dog], [wd[l]], M=S, N=FH, K=D, tm=tm, tn=FH // N_CHIPS, tk=D,
                       extras=[("tile", st["gt"]), ("tile", st["up"])], out_dtypes=(BF16, BF16),
                       epilogue=_swiglu_bwd_epilogue, comm=rs.comm())
        rs.push(f"wd{l}", _mm(f"ffn_dwd{l}", "tn", [st["act"]], [dog], M=FH, N=D, K=S, tm=FH // N_CHIPS,
                              tn=_pick(D, 512, LANE), tk=S, outer="i",
                              comm=rs.comm())[0].reshape(N_CHIPS, FH // N_CHIPS, D))
        rs.push(f"wg{l}", _mm(f"ffn_dwg{l}", "tn", [st["h2"]], [dgt], M=D, N=FH, K=S, tm=_pick(D, 512, LANE),
                              tn=FH // N_CHIPS, tk=S, out_stacked=True, comm=rs.comm())[0])
        rs.push(f"wu{l}", _mm(f"ffn_dwu{l}", "tn", [st["h2"]], [dup], M=D, N=FH, K=S, tm=_pick(D, 512, LANE),
                              tn=FH // N_CHIPS, tk=S, out_stacked=True, comm=rs.comm())[0])
        dh2 = _mm(f"ffn_dh{l}", "nt", [dgt, dup], [wg4[l], wu4[l]], M=S, N=D, K=FH, tm=_pick(S, 512, 8),
                  tn=_pick(D, 1024, LANE), tk=FH, b_stacked=True, comm=rs.comm())[0]
        dx, dog, dsh2, dsc2, gffn[l], dg1 = _normmod_bwd(f"normmod_ffn_bwd{l}", st["x_mid"], dh2, dx,
                                                         row(norm_ffn_g[l]), sc2, tr, nxt=(st["out1"], g1),
                                                         comm=rs.comm(ici=0))
        if l == 0:
            dy = _mm("sgu_dy", "nt", [dog], [wa_out], M=S, N=D, K=D, tm=tm, tn=_pick(D, 1024, LANE), tk=D,
                     out_dtypes=(BF16,))[0]
            rs.push("a_out", _mm("sgu_dwout", "tn", [st["y"]], [dog], M=D, N=D, K=S, tm=_pick(D, 512, LANE),
                                 tn=_pick(D, 1024, LANE),
                                 tk=S)[0].reshape(N_CHIPS, D // N_CHIPS, D))
            da, dws, dbias, g_ln_g, g_ln_b, g_b_in = _sgu_mid_bwd(st["a"], dy, a_ln_g, a_ln_b, w_mask, bias_full, tr,
                                                                  comm=rs.comm())
            rs.push("a_in", _mm("sgu_dwin", "tn", [st["h1"]], [da], M=D, N=2 * D, K=S, tm=_pick(D, 512, LANE),
                                tn=2 * D // N_CHIPS, tk=S, out_stacked=True, comm=rs.comm())[0])
            dh1 = _mm("sgu_dh", "nt", [da], [wa_in4], M=S, N=D, K=2 * D, tm=_pick(S, 512, 8), tn=_pick(D, 1024, LANE),
                      tk=2 * D, b_stacked=True, comm=rs.comm())[0]
            g_w_s = jnp.where(tril[None], dws, 0.0)
            g_b_s = jnp.sum(dbias.reshape(LANE, G, LANE), axis=2).T
        else:
            do = _mm("fox_do", "nt", [dog], [wb_out], M=S, N=D, K=D, tm=tm, tn=_pick(D, 1024, LANE), tk=D,
                     out_dtypes=(BF16,))[0]
            rs.push("b_out", _mm("fox_dwout", "tn", [st["o"]], [dog], M=D, N=D, K=S, tm=_pick(D, 512, LANE),
                                 tn=_pick(D, 1024, LANE),
                                 tk=S)[0].reshape(N_CHIPS, D // N_CHIPS, D))
            delta = _head_dots(do, st["o"], H, tr)
            delta_rows = delta[:, :H].T.reshape(H, nq, 1, tq)
            A_rows = (st["F_rows"] - st["lse"].reshape(H, nq, 1, tq)) * _LOG2E
            w_qkv, w_f = st["w_qkv"], st["w_f"]
            dq, dk, dv, dF_k, dF_q = _attn_bwd(st["qkv"], do, st["F_sh"], A_rows, delta_rows, H, tq,
                                               comm=rs.comm(ici=2))
            dF_sh = jnp.pad((dF_k.reshape(H, S) + dF_q.reshape(H, S)).T, ((0, 0), (0, LANE - H)))
            dfl, db_f = _fox_gate_bwd(st["fl"], dF_sh, bf_pad)
            dfl_b = dfl.astype(BF16)
            dh_f = _mm("fox_dh_f", "nn", [dfl_b], [w_f], M=S, N=D, K=LANE, tm=tm, tn=_pick(D, 1024, LANE),
                       tk=LANE)[0]
            dh1 = _mm("fox_dh", "nn", [dq, dk, dv], [w_qkv, w_qkv, w_qkv], M=S, N=D, K=D, tm=_pick(S, 256, 8),
                      tn=_pick(D, 1024, LANE), tk=D, b_koffs=[0, 1, 2],
                      extras=[("tile", dh_f)], epilogue=lambda accs, ex: [accs[0] + ex[0]], comm=rs.comm())[0]
            tmw = _pick(D, 512, LANE)
            g_bT = None
            for p, (nm, d_) in enumerate((("q", dq), ("k", dk), ("v", dv))):
                g_bT = _mm(f"fox_dw{nm}", "tn", [d_], [st["h1"]], M=D, N=D, K=S, tm=tmw, tn=_pick(D, 1024, LANE),
                           tk=S, out_into=(3 * D + H, p * (D // tmw), g_bT), comm=rs.comm(ici=0))[0]
            dwf = _mm("fox_dwf", "tn", [dfl_b], [st["h1"]], M=LANE, N=D, K=S, tm=LANE, tn=_pick(D, 1024, LANE),
                      tk=S)[0]
            g_bT = lax.dynamic_update_slice(g_bT, dwf[:H], (3 * D, 0))
            rs.push("b_in", g_bT.reshape(N_CHIPS, -1, D), cols=True)
            g_b_f = db_f[:, :H]
        below = (saved[l - 1]["out2"], mod[l - 1][5]) if l else None
        res = _normmod_bwd(f"normmod_mix_bwd{l}", st["x_in"], dh1, dx, row(norm_mix_g[l]), sc1, tr, nxt=below,
                           comm=rs.comm(ici=1 - l))
        if l:
            dx, dog_below, dsh1, dsc1, gmix[l], dg2_below = res
        else:
            dx, dsh1, dsc1, gmix[l] = res
        dmods[l] = jnp.concatenate([dsh1, dsc1, dg1, dsh2, dsc2, dg2], axis=1)
        if l:
            dog, dg2 = dog_below, dg2_below
    grad_x = dx[None]

    small = [jnp.concatenate(dmods, axis=0), jnp.concatenate(gmix, axis=0), jnp.concatenate(gffn, axis=0),
             g_b_in, g_ln_g, g_ln_b, g_w_s[None], g_b_s[None], g_b_f, g_final[0]]
    small_w = [ada_b, norm_mix_g, norm_ffn_g, a_b_in, a_ln_g, a_ln_b, a_w_s, a_b_s, b_b_f, final_g]
    small_m = [m_ada_b, m_norm_mix_g, m_norm_ffn_g, m_a_b_in, m_a_ln_g, m_a_ln_b, m_a_w_s, m_a_b_s, m_b_b_f, m_final_g]
    small_v = [v_ada_b, v_norm_mix_g, v_norm_ffn_g, v_a_b_in, v_a_ln_g, v_a_ln_b, v_a_w_s, v_a_b_s, v_b_b_f, v_final_g]
    sizes = [w.size for w in small_w]
    total = sum(sizes)
    padded = -(-total // (8 * LANE)) * (8 * LANE)

    def pack(parts):
        flat = jnp.concatenate([p.reshape(-1) for p in parts])
        return jnp.pad(flat, (0, padded - total)).reshape(padded // LANE, LANE)

    def unpack(buf):
        flat = buf.reshape(-1)
        outs, off = [], 0
        for w, n in zip(small_w, sizes):
            outs.append(flat[off:off + n].reshape(w.shape))
            off += n
        return outs

    g_all = _gather_devices("ag_small_grads", pack(small), dev_idx, extra=rs.comm())
    g_small = _sum_slots("sum_small_grads", g_all)
    _, d_small, m_small, v_small = _adamw("adamw_small", pack(small_w), g_small, pack(small_m), pack(small_v))
    sg, sd, sm, sv_ = unpack(g_small), unpack(d_small), unpack(m_small), unpack(v_small)

    n_dm = depth * N_MOD * D
    dmod_all = g_all.reshape(N_DEV, -1)[:, :n_dm].reshape(N_DEV, depth, N_MOD * D)
    u_ada = None
    for l in range(depth):
        dm_loc = lax.dynamic_slice_in_dim(dmod_all[:, l, :], chip * n_loc, n_loc, axis=1).astype(BF16)
        g_ada = _mm(f"ada_dw{l}", "tn", [c_act], [dm_loc], M=D, N=n_loc, K=N_DEV, tm=_pick(D, 512, LANE),
                    tn=_pick(n_loc, 1024, LANE), tk=N_DEV)[0]
        u_ada = _adamw(f"adamw_ada_w{l}", ada_w, g_ada, m_ada_w, v_ada_w, layer=l, prev=u_ada)

    flushes = 0
    while rs.items:
        _comm_call(f"rs_flush{flushes}", rs.comm())
        flushes += 1
    groups = {"a_in": (a_w_in, m_a_w_in, v_a_w_in), "a_out": (a_w_out, m_a_w_out, v_a_w_out),
              "b_out": (b_w_out, m_b_w_out, v_b_w_out),
              "wg": (ffn_w_gate, m_ffn_w_gate, v_ffn_w_gate), "wu": (ffn_w_up, m_ffn_w_up, v_ffn_w_up),
              "wd": (ffn_w_down, m_ffn_w_down, v_ffn_w_down)}
    upd = {}
    for n, (w_, m_, v_) in groups.items():
        for l in range(w_.shape[0]):
            tag = n if n in rs.done else f"{n}{l}"
            upd[n] = _adamw("adamw_" + tag, w_, rs.done[tag], m_, v_, layer=l, prev=upd.get(n))
    u_a_in, u_a_out, u_b_out, u_wg, u_wu, u_wd = (upd[n] for n in groups)
    u_b_in = [jnp.transpose(t, (1, 2, 0)) for t in
              _adamw("adamw_b_in", t3(b_w_in), rs.done["b_in"], t3(m_b_w_in), t3(v_b_w_in))]

    def leaves(i, small_list):
        s = small_list
        return [u_ada[i], s[0], s[1], s[2], u_a_in[i], s[3], s[4], s[5], s[6], s[7], u_a_out[i],
                u_b_in[i], s[8], u_b_out[i], u_wg[i], u_wu[i], u_wd[i], s[9]]

    return (loss, grad_x, *leaves(0, sg), *leaves(1, sd), *leaves(2, sm), *leaves(3, sv_))
```

```python
import functools
import math

import jax
import jax.numpy as jnp
from jax import lax
from jax.experimental import pallas as pl
from jax.experimental.pallas import tpu as pltpu

F32, BF16 = jnp.float32, jnp.bfloat16
LANE = 128
N_CHIPS = 4
N_DEV = 8
N_MOD = 6
EPS = 1e-6
VMEM_LIMIT = 60 * 1024 * 1024
ADAM_LR, ADAM_B1, ADAM_B2, ADAM_EPS, ADAM_WD, ADAM_STEP = 0.001, 0.9, 0.999, 1e-08, 0.01, 10
MESH = pl.DeviceIdType.MESH
ANY = pl.BlockSpec(memory_space=pl.ANY)
SDS = jax.ShapeDtypeStruct


def _pick(dim, pref, align):
    t = min(dim, pref)
    t -= t % align
    while t >= align:
        if dim % t == 0:
            return t
        t -= align
    return dim


def _params(sem=None):
    return pltpu.CompilerParams(dimension_semantics=sem, vmem_limit_bytes=VMEM_LIMIT)


class _Job:
    def __init__(self, ins, outs, aliases, nsem, start, wait):
        self.ins, self.outs, self.aliases, self.nsem, self.start, self.wait = ins, outs, aliases, nsem, start, wait


def _call(body, name, *, grid, in_specs, out_specs, out_shape, args, scratch=(), sem=None, comm=None,
          prefetch=None, keep=()):
    jobs, absorb = comm if comm else ([], None)
    n_pre = 0 if prefetch is None else 1
    n_in, n_out, n_scr = len(args), len(out_shape), len(scratch)
    c_ins = [a for a, _ in keep] + [a for j in jobs for a in j.ins]
    c_outs = [s for j in jobs for s in j.outs]
    aliases = {n_pre + n_in + k: o for k, (_, o) in enumerate(keep)}
    c_scr = []
    i_off, o_off = len(keep), 0
    for j in jobs:
        for a, b in j.aliases.items():
            aliases[n_pre + n_in + i_off + a] = n_out + o_off + b
        i_off += len(j.ins)
        o_off += len(j.outs)
        c_scr += [pltpu.SemaphoreType.DMA((j.nsem,)), pltpu.SemaphoreType.DMA((j.nsem,))]

    def wrapped(*refs):
        bounds = [n_pre + n_in, len(c_ins), n_out, len(c_outs), n_scr]
        parts, p = [], 0
        for n in bounds:
            parts.append(refs[p:p + n])
            p += n
        main_in, cin, main_out, cout, main_scr = parts
        cin = cin[len(keep):]
        csem = refs[p:]

        def run(phase):
            a = b = 0
            for k, j in enumerate(jobs):
                fn = j.start if phase == 0 else j.wait
                fn(cin[a:a + len(j.ins)], cout[b:b + len(j.outs)], csem[2 * k], csem[2 * k + 1])
                a += len(j.ins)
                b += len(j.outs)

        if jobs and grid:
            first = functools.reduce(jnp.logical_and, [pl.program_id(d) == 0 for d in range(len(grid))])
            last = functools.reduce(jnp.logical_and, [pl.program_id(d) == grid[d] - 1 for d in range(len(grid))])
            pl.when(first)(lambda: run(0))
            body(*main_in, *main_out, *main_scr)
            pl.when(last)(lambda: run(1))
        elif jobs:
            run(0)
            body(*main_in, *main_out, *main_scr)
            run(1)
        else:
            body(*main_in, *main_out, *main_scr)

    specs = dict(grid=grid, in_specs=list(in_specs) + [ANY] * len(c_ins),
                 out_specs=list(out_specs) + [ANY] * len(c_outs), scratch_shapes=list(scratch) + c_scr)
    if n_pre:
        specs = dict(grid_spec=pltpu.PrefetchScalarGridSpec(num_scalar_prefetch=1, **specs))
    outs = pl.pallas_call(
        wrapped, name=name, out_shape=list(out_shape) + c_outs, input_output_aliases=aliases,
        compiler_params=_params(("arbitrary",) * len(grid) if jobs else sem), **specs,
    )(*([prefetch] if n_pre else []), *args, *c_ins)
    if jobs:
        absorb(list(outs[n_out:]))
    return list(outs[:n_out])


def _comm_call(name, comm):
    _call(lambda: None, name, grid=(), in_specs=[], out_specs=[], out_shape=[], args=[], comm=comm)


def _mm(name, form, a_list, b_list, *, M, N, K, tm, tn, tk, b_stacked=False, out_stacked=False,
        b_koffs=None, acc_of=None, n_acc=1, extras=(), out_dtypes=(F32,), epilogue=None, comm=None,
        outer="j", b_layer=None, out_into=None):
    assert M % tm == 0 and N % tn == 0 and K % tk == 0, (name, M, N, K, tm, tn, tk)
    nm, nn, nk = M // tm, N // tn, K // tk
    npairs = len(a_list)
    acc_of = acc_of or [0] * npairs
    b_koffs = b_koffs or [0] * npairs
    if epilogue is None:
        epilogue = lambda accs, ex: [accs[0]]

    def spec(shape, fn, **kw):
        return pl.BlockSpec(shape, fn if outer == "j" else (lambda i, j, k: fn(j, i, k)), **kw)

    slabs = form == "nt" and b_stacked and tk == K

    a_uniq = [a for p, a in enumerate(a_list) if all(a is not b for b in a_list[:p])]
    a_pos = [next(u for u, b in enumerate(a_uniq) if b is a) for a in a_list]
    in_specs = []
    for _ in a_uniq:
        if form == "tn":
            in_specs.append(spec((tk, tm), lambda j, i, k: (k, i)))
        else:
            in_specs.append(spec((tm, tk), lambda j, i, k: (i, k)))
    for off in b_koffs:
        if b_layer is not None:
            assert form == "nn" and not b_stacked
            in_specs.append(spec((None, tk, tn), lambda j, i, k: (b_layer, k, j)))
        elif form == "nn":
            if b_stacked:
                assert tn * N_CHIPS == N
                in_specs.append(spec((None, tk, tn), lambda j, i, k: (j, k, 0)))
            else:
                in_specs.append(spec((tk, tn), lambda j, i, k, off=off: (off + k, j)))
        elif form == "nt":
            if slabs:
                in_specs.append(spec((N_CHIPS, tn, K // N_CHIPS), lambda j, i, k: (0, j, 0),
                                     pipeline_mode=pl.Buffered(1)))
            elif b_stacked:
                assert tk * N_CHIPS == K
                in_specs.append(spec((None, tn, tk), lambda j, i, k: (k, j, 0)))
            else:
                in_specs.append(spec((tn, tk), lambda j, i, k, off=off: (j, off + k)))
        else:
            in_specs.append(spec((tk, tn), lambda j, i, k: (k, j)))
    for kind, _ in extras:
        if kind == "tile":
            in_specs.append(spec((tm, tn), lambda j, i, k: (i, j)))
        else:
            in_specs.append(spec((1, tn), lambda j, i, k: (0, j)))
    if out_stacked:
        assert tn * N_CHIPS == N
        out_specs = [spec((None, tm, tn), lambda j, i, k: (j, i, 0)) for _ in out_dtypes]
        out_shape = [SDS((N_CHIPS, M, tn), d) for d in out_dtypes]
    else:
        rows, roff, prev = out_into if out_into else (M, 0, None)
        out_specs = [spec((tm, tn), lambda j, i, k: (roff + i, j)) for _ in out_dtypes]
        out_shape = [SDS((rows, N), d) for d in out_dtypes]
    dims = {"nn": (((1,), (0,)), ((), ())), "nt": (((1,), (1,)), ((), ())), "tn": (((0,), (0,)), ((), ()))}[form]
    n_a, n_ex, n_out = len(a_uniq), len(extras), len(out_dtypes)

    def body(*refs):
        a_vals = [refs[u][...].astype(BF16) for u in range(n_a)]
        b_refs = refs[n_a:n_a + npairs]
        e_refs = refs[n_a + npairs:n_a + npairs + n_ex]
        o_refs = refs[n_a + npairs + n_ex:n_a + npairs + n_ex + n_out]
        acc_refs = refs[n_a + npairs + n_ex + n_out:]

        tot = [None] * n_acc
        for p in range(npairs):
            if slabs:
                n = K // N_CHIPS
                d = sum(lax.dot_general(a_vals[a_pos[p]][:, s * n:(s + 1) * n], b_refs[p][s], dims,
                                        preferred_element_type=F32) for s in range(N_CHIPS))
            else:
                d = lax.dot_general(a_vals[a_pos[p]], b_refs[p][...].astype(BF16), dims, preferred_element_type=F32)
            tot[acc_of[p]] = d if tot[acc_of[p]] is None else tot[acc_of[p]] + d

        def finish(accs):
            outs = epilogue(accs, [e[...] for e in e_refs])
            for o_ref, o in zip(o_refs, outs):
                o_ref[...] = o.astype(o_ref.dtype)

        if nk == 1:
            finish(tot)
        else:
            k = pl.program_id(2)

            @pl.when(k == 0)
            def _():
                for r, t in zip(acc_refs, tot):
                    r[...] = t

            @pl.when(k > 0)
            def _():
                for r, t in zip(acc_refs, tot):
                    r[...] += t

            @pl.when(k == nk - 1)
            def _():
                finish([r[...] for r in acc_refs])

    scratch = [pltpu.VMEM((tm, tn), F32) for _ in range(n_acc)] if nk > 1 else []
    return _call(body, name, grid=(nn, nm, nk) if outer == "j" else (nm, nn, nk), in_specs=in_specs,
                 out_specs=out_specs, out_shape=out_shape, scratch=scratch,
                 sem=("parallel", "parallel", "arbitrary"), comm=comm,
                 keep=[(out_into[2], 0)] if out_into and out_into[2] is not None else (),
                 args=[*a_uniq, *b_list, *[e for _, e in extras]])


def _rowwise(name, fn, rows, vecs, row_outs, acc_widths, tr, comm=None):
    S = rows[0].shape[0]
    assert S % tr == 0
    nr, nv, no, na = len(rows), len(vecs), len(row_outs), len(acc_widths)

    def body(*refs):
        r = [x[...] for x in refs[:nr]]
        v = [x[...] for x in refs[nr:nr + nv]]
        o_refs = refs[nr + nv:nr + nv + no]
        a_refs = refs[nr + nv + no:]
        outs, accs = fn(r, v)
        for o_ref, o in zip(o_refs, outs):
            o_ref[...] = o.astype(o_ref.dtype)
        first = pl.program_id(0) == 0

        @pl.when(first)
        def _():
            for a_ref, a in zip(a_refs, accs):
                a_ref[...] = a

        @pl.when(jnp.logical_not(first))
        def _():
            for a_ref, a in zip(a_refs, accs):
                a_ref[...] += a

    in_specs = [pl.BlockSpec((tr, x.shape[1]), lambda i: (i, 0)) for x in rows]
    in_specs += [pl.BlockSpec(x.shape, lambda i, nd=x.ndim: (0,) * nd) for x in vecs]
    out_specs = [pl.BlockSpec((tr, w), lambda i: (i, 0)) for w, _ in row_outs]
    out_specs += [pl.BlockSpec((1, w), lambda i: (0, 0)) for w in acc_widths]
    out_shape = [SDS((S, w), d) for w, d in row_outs] + [SDS((1, w), F32) for w in acc_widths]
    return _call(body, name, grid=(S // tr,), in_specs=in_specs, out_specs=out_specs, out_shape=out_shape,
                 sem=("arbitrary",), comm=comm, args=[*rows, *vecs])


def _colsum(a):
    return jnp.sum(a, axis=0, keepdims=True)


def _rms_stats(x):
    rstd = lax.rsqrt(jnp.mean(x * x, axis=1, keepdims=True) + EPS)
    return x * rstd, rstd


def _normmod_fwd(name, x, g, sc, sh, tr):
    def fn(r, v):
        n, _ = _rms_stats(r[0])
        return [(n * v[0]) * (1.0 + v[1]) + v[2]], []
    return _rowwise(name, fn, [x], [g, sc, sh], [(x.shape[1], BF16)], [], tr)[0]


def _gated(dx, nxt_out, gate):
    return gate * dx, _colsum(dx * nxt_out.astype(F32))


def _normmod_bwd(name, x, dh, dres, g, sc, tr, nxt=None, comm=None):
    def fn(r, v):
        x_, dh_, dres_ = r[0], r[1].astype(F32), r[2]
        g_, sc_ = v[:2]
        n, rstd = _rms_stats(x_)
        hn = n * g_
        dhn = dh_ * (1.0 + sc_)
        dn = dhn * g_
        dx = dres_ + rstd * (dn - n * jnp.mean(dn * n, axis=1, keepdims=True))
        outs, accs = [dx], [_colsum(dh_), _colsum(dh_ * hn), _colsum(dhn * n)]
        if nxt:
            dog, dgate = _gated(dx, r[3], v[2])
            outs.append(dog)
            accs.append(dgate)
        return outs, accs
    D = x.shape[1]
    rows, vecs = [x, dh, dres] + ([nxt[0]] if nxt else []), [g, sc] + ([nxt[1]] if nxt else [])
    return _rowwise(name, fn, rows, vecs, [(D, F32)] + ([(D, BF16)] if nxt else []), [D] * (4 if nxt else 3), tr,
                    comm=comm)


def _loss_head(name, x, target, g, tr, nxt):
    D = x.shape[1]

    def fn(r, v):
        n, rstd = _rms_stats(r[0])
        err = n * v[0] - r[1]
        loss = 0.5 * jnp.sum(jnp.mean(err * err, axis=1, keepdims=True), axis=0, keepdims=True)
        dy = err * (1.0 / D)
        dn = dy * v[0]
        dx = rstd * (dn - n * jnp.mean(dn * n, axis=1, keepdims=True))
        dog, dgate = _gated(dx, r[2], v[1])
        return [dx, dog], [jnp.broadcast_to(loss, (1, LANE)), _colsum(dy * n), dgate]
    return _rowwise(name, fn, [x, target, nxt[0]], [g, nxt[1]], [(D, F32), (D, BF16)], [LANE, D, D], tr)


_INV_SQRT2 = 1.0 / math.sqrt(2.0)
_INV_SQRT2PI = 1.0 / math.sqrt(2.0 * math.pi)


def _gelu(a):
    return 0.5 * a * (1.0 + lax.erf(a * _INV_SQRT2))


def _gelu_grad(a):
    return 0.5 * (1.0 + lax.erf(a * _INV_SQRT2)) + a * jnp.exp(-0.5 * a * a) * _INV_SQRT2PI


def _ln_stats(v):
    mu = jnp.mean(v, axis=1, keepdims=True)
    vc = v - mu
    rstd = lax.rsqrt(jnp.mean(vc * vc, axis=1, keepdims=True) + EPS)
    return vc * rstd, rstd


def _chunk_mix(w_ref, vn, tr, G):
    rows = []
    for ch in range(tr // LANE):
        cols = []
        for g in range(G):
            blk = vn[ch * LANE:(ch + 1) * LANE, g * LANE:(g + 1) * LANE]
            cols.append(jnp.dot(w_ref[g], blk, preferred_element_type=F32))
        rows.append(jnp.concatenate(cols, axis=1))
    return jnp.concatenate(rows, axis=0)


def _sgu_mid_fwd(a, ln_g, ln_b, w_mask, bias_full, tr):
    S, D2 = a.shape
    D = D2 // 2
    G = D // LANE

    def body(a_ref, g_ref, b_ref, w_ref, bias_ref, y_ref):
        u = _gelu(a_ref[:, :D])
        v = _gelu(a_ref[:, D:])
        vhat, _ = _ln_stats(v)
        vn = (vhat * g_ref[...] + b_ref[...]).astype(BF16)
        sv = _chunk_mix(w_ref, vn, tr, G) + jnp.concatenate([bias_ref[...]] * (tr // LANE), axis=0)
        y_ref[...] = (u * sv).astype(BF16)

    return pl.pallas_call(
        body, name="sgu_mid_fwd", grid=(S // tr,),
        in_specs=[pl.BlockSpec((tr, D2), lambda i: (i, 0)), pl.BlockSpec((1, D), lambda i: (0, 0)),
                  pl.BlockSpec((1, D), lambda i: (0, 0)), pl.BlockSpec((G, LANE, LANE), lambda i: (0, 0, 0)),
                  pl.BlockSpec((LANE, D), lambda i: (0, 0))],
        out_specs=pl.BlockSpec((tr, D), lambda i: (i, 0)), out_shape=SDS((S, D), BF16),
        compiler_params=_params(("arbitrary",)))(a, ln_g, ln_b, w_mask, bias_full)


def _sgu_mid_bwd(a, dy, ln_g, ln_b, w_mask, bias_full, tr, comm=None):
    S, D2 = a.shape
    D = D2 // 2
    G = D // LANE
    nch = tr // LANE

    def body(a_ref, dy_ref, g_ref, b_ref, w_ref, bias_ref, da_ref, dw_ref, dbias_ref, dg_ref, db_ref, dbin_ref):
        au, av = a_ref[:, :D], a_ref[:, D:]
        u = _gelu(au)
        v = _gelu(av)
        vhat, rstd = _ln_stats(v)
        vn = (vhat * g_ref[...] + b_ref[...]).astype(BF16)
        sv = _chunk_mix(w_ref, vn, tr, G) + jnp.concatenate([bias_ref[...]] * nch, axis=0)
        dy_ = dy_ref[...].astype(F32)
        du = dy_ * sv
        dsv = dy_ * u
        dsv_b = dsv.astype(BF16)
        first = pl.program_id(0) == 0

        @pl.when(first)
        def _():
            dw_ref[...] = jnp.zeros_like(dw_ref)
            dbias_ref[...] = jnp.zeros_like(dbias_ref)
            dg_ref[...] = jnp.zeros_like(dg_ref)
            db_ref[...] = jnp.zeros_like(db_ref)
            dbin_ref[...] = jnp.zeros_like(dbin_ref)

        rows = []
        dbias = None
        for ch in range(nch):
            r0 = ch * LANE
            cols = []
            for g in range(G):
                c0 = g * LANE
                ds_blk = dsv_b[r0:r0 + LANE, c0:c0 + LANE]
                vn_blk = vn[r0:r0 + LANE, c0:c0 + LANE]
                cols.append(lax.dot_general(w_ref[g], ds_blk, (((0,), (0,)), ((), ())),
                                            preferred_element_type=F32))
                dw_ref[g] += lax.dot_general(ds_blk, vn_blk, (((1,), (1,)), ((), ())),
                                             preferred_element_type=F32)
            rows.append(jnp.concatenate(cols, axis=1))
            blk = dsv[r0:r0 + LANE, :]
            dbias = blk if dbias is None else dbias + blk
        dvn = jnp.concatenate(rows, axis=0)
        dbias_ref[...] += dbias
        dg_ref[...] += _colsum(dvn * vhat)
        db_ref[...] += _colsum(dvn)
        dvh = dvn * g_ref[...]
        dv = rstd * (dvh - jnp.mean(dvh, axis=1, keepdims=True)
                     - vhat * jnp.mean(dvh * vhat, axis=1, keepdims=True))
        da_u = du * _gelu_grad(au)
        da_v = dv * _gelu_grad(av)
        da_ref[:, :D] = da_u.astype(BF16)
        da_ref[:, D:] = da_v.astype(BF16)
        dbin_ref[:, :D] += _colsum(da_u)
        dbin_ref[:, D:] += _colsum(da_v)

    full = lambda shp: pl.BlockSpec(shp, lambda i, nd=len(shp): (0,) * nd)
    return _call(
        body, "sgu_mid_bwd", grid=(S // tr,),
        in_specs=[pl.BlockSpec((tr, D2), lambda i: (i, 0)), pl.BlockSpec((tr, D), lambda i: (i, 0)),
                  full((1, D)), full((1, D)), full((G, LANE, LANE)), full((LANE, D))],
        out_specs=[pl.BlockSpec((tr, D2), lambda i: (i, 0)), full((G, LANE, LANE)), full((LANE, D)),
                   full((1, D)), full((1, D)), full((1, D2))],
        out_shape=[SDS((S, D2), BF16), SDS((G, LANE, LANE), F32), SDS((LANE, D), F32),
                   SDS((1, D), F32), SDS((1, D), F32), SDS((1, D2), F32)],
        sem=("arbitrary",), comm=comm, args=[a, dy, ln_g, ln_b, w_mask, bias_full])


def _tri(lower):
    r = lax.broadcasted_iota(jnp.int32, (LANE, LANE), 0)
    c = lax.broadcasted_iota(jnp.int32, (LANE, LANE), 1)
    return jnp.where((c <= r) if lower else (c >= r), 1.0, 0.0).astype(F32)


def _fox_gate_fwd(fl, bf_pad):
    S = fl.shape[0]
    nblk = S // LANE

    def body(fl_ref, b_ref, f_ref):
        tri = _tri(True)

        def step(i, carry):
            r0 = pl.multiple_of(i * LANE, LANE)
            z = fl_ref[pl.ds(r0, LANE), :] + b_ref[...]
            logf = jnp.minimum(z, 0.0) - jnp.log1p(jnp.exp(-jnp.abs(z)))
            f = jnp.dot(tri, logf, preferred_element_type=F32, precision=lax.Precision.HIGHEST) + carry
            f_ref[pl.ds(r0, LANE), :] = f
            return f[LANE - 1:LANE, :]
        lax.fori_loop(0, nblk, step, jnp.zeros((1, LANE), F32))

    return pl.pallas_call(body, name="fox_gate_fwd", out_shape=SDS((S, LANE), F32),
                          compiler_params=_params())(fl, bf_pad)


def _fox_gate_bwd(fl, dF, bf_pad):
    S = fl.shape[0]
    nblk = S // LANE

    def body(fl_ref, df_ref, b_ref, dfl_ref, db_ref):
        tri = _tri(True)

        def prefix(i, carry):
            r0 = pl.multiple_of(i * LANE, LANE)
            p = jnp.dot(tri, df_ref[pl.ds(r0, LANE), :], preferred_element_type=F32,
                        precision=lax.Precision.HIGHEST) + carry
            dfl_ref[pl.ds(r0, LANE), :] = p
            return p[LANE - 1:LANE, :]
        total = lax.fori_loop(0, nblk, prefix, jnp.zeros((1, LANE), F32))
        db_ref[...] = jnp.zeros_like(db_ref)

        def finish(i, carry):
            r0 = pl.multiple_of(i * LANE, LANE)
            dlogf = total - dfl_ref[pl.ds(r0, LANE), :] + df_ref[pl.ds(r0, LANE), :]
            z = fl_ref[pl.ds(r0, LANE), :] + b_ref[...]
            dfl = dlogf / (1.0 + jnp.exp(z))
            dfl_ref[pl.ds(r0, LANE), :] = dfl
            db_ref[...] += _colsum(dfl)
            return carry
        lax.fori_loop(0, nblk, finish, 0)

    return pl.pallas_call(body, name="fox_gate_bwd", out_shape=[SDS((S, LANE), F32), SDS((1, LANE), F32)],
                          compiler_params=_params())(fl, dF, bf_pad)


def _lane_pick(blk, h):
    lane = lax.broadcasted_iota(jnp.int32, blk.shape, 1)
    return jnp.sum(jnp.where(lane == h, blk, 0.0), axis=1, keepdims=True)


_NEG = -1e30
_LOG2E = 1.0 / math.log(2.0)
_LN2 = math.log(2.0)
_SUB = 64


def _attn_fwd(qkv, F_sh, F_rows, H, tq, comm=None):
    S = qkv.shape[0]
    D = H * LANE
    nq = S // tq
    scale = 1.0 / math.sqrt(LANE)

    def body(q_ref, k_ref, v_ref, fsh_ref, frow_ref, o_ref, lse_ref):
        h, i = pl.program_id(0), pl.program_id(1)
        q = q_ref[...]
        fq = _lane_pick(fsh_ref[...], h) * _LOG2E

        def block(j, carry, diagonal):
            m, l, acc = carry
            r0 = pl.multiple_of(j * tq, tq)
            k = k_ref[pl.ds(r0, tq), :]
            v = v_ref[pl.ds(r0, tq), :]
            s = lax.dot_general(q, k, (((1,), (1,)), ((), ())), preferred_element_type=F32)
            s = s + (fq - frow_ref[j])
            if diagonal:
                row = lax.broadcasted_iota(jnp.int32, (tq, tq), 0)
                col = lax.broadcasted_iota(jnp.int32, (tq, tq), 1)
                s = jnp.where(col <= row, s, _NEG)
            m_new = jnp.maximum(m, jnp.max(s, axis=1, keepdims=True))
            alpha = jnp.exp2(m - m_new)
            p = jnp.exp2(s - m_new)
            l = alpha * l + jnp.sum(p, axis=1, keepdims=True)
            acc = alpha * acc + jnp.dot(p.astype(BF16), v, preferred_element_type=F32)
            return m_new, l, acc

        init = (jnp.full((tq, 1), _NEG, F32), jnp.zeros((tq, 1), F32), jnp.zeros((tq, LANE), F32))
        carry = lax.fori_loop(0, i, lambda j, cr: block(j, cr, False), init)
        m, l, acc = block(i, carry, True)
        o_ref[...] = (acc / l).astype(BF16)
        lse_ref[...] = (m + jnp.log2(l)) * _LN2

    return _call(
        body, "attn_fwd", grid=(H, nq),
        in_specs=[pl.BlockSpec((tq, LANE), lambda h, i: (i, h)),
                  pl.BlockSpec((S, LANE), lambda h, i: (0, H + h)),
                  pl.BlockSpec((S, LANE), lambda h, i: (0, 2 * H + h)),
                  pl.BlockSpec((tq, LANE), lambda h, i: (i, 0)),
                  pl.BlockSpec((None, nq, 1, tq), lambda h, i: (h, 0, 0, 0))],
        out_specs=[pl.BlockSpec((tq, LANE), lambda h, i: (i, h)),
                   pl.BlockSpec((None, tq, 1), lambda h, i: (h, i, 0))],
        out_shape=[SDS((S, D), BF16), SDS((H, S, 1), F32)],
        sem=("parallel", "arbitrary"), comm=comm, args=[qkv, qkv, qkv, F_sh, F_rows])


def _attn_bwd(qkv, do, F_sh, A_rows, delta_rows, H, tq, comm=None):
    S = qkv.shape[0]
    D = H * LANE
    nq = S // tq
    scale = 1.0 / math.sqrt(LANE)

    def body(q_ref, do_ref, k_ref, v_ref, fsh_ref, a_ref, dl_ref, dq_ref, dk_ref, dv_ref, df_ref, dfr_ref,
             dq_acc, dfr_acc, st_scr, dp_scr, pt_scr, ds_scr, fk_scr, df_scr, dk_scr, dv_scr):
        h, j = pl.program_id(0), pl.program_id(1)

        @pl.when(j == 0)
        def _():
            dq_acc[...] = jnp.zeros_like(dq_acc)
            dfr_acc[...] = jnp.zeros_like(dfr_acc)

        k = k_ref[...]
        v = v_ref[...]
        fk_scr[...] = _lane_pick(fsh_ref[...], h) * _LOG2E
        df_scr[...] = jnp.zeros((tq, 1), F32)
        dk_scr[...] = jnp.zeros((tq, LANE), F32)
        dv_scr[...] = jnp.zeros((tq, LANE), F32)

        def block(i, diagonal):
            r0 = pl.multiple_of(i * tq, tq)
            q = q_ref[pl.ds(r0, tq), :]
            do_ = do_ref[pl.ds(r0, tq), :]
            st_scr[...] = lax.dot_general(k, q, (((1,), (1,)), ((), ())), preferred_element_type=F32)
            dp_scr[...] = lax.dot_general(v, do_, (((1,), (1,)), ((), ())), preferred_element_type=F32)
            a_row, dl_row = a_ref[i], dl_ref[i]
            dfr = jnp.zeros((1, tq), F32)
            for r in range(tq // _SUB):
                rows = slice(r * _SUB, (r + 1) * _SUB)
                arg = st_scr[rows, :] + (a_row - fk_scr[rows, :])
                if diagonal:
                    row = lax.broadcasted_iota(jnp.int32, (_SUB, tq), 0) + r * _SUB
                    col = lax.broadcasted_iota(jnp.int32, (_SUB, tq), 1)
                    arg = jnp.where(row <= col, arg, _NEG)
                pt = jnp.exp2(arg)
                dst = pt * (dp_scr[rows, :] - dl_row)
                df_scr[rows, :] += jnp.sum(dst, axis=1, keepdims=True)
                dfr = dfr + jnp.sum(dst, axis=0, keepdims=True)
                pt_scr[rows, :] = pt.astype(BF16)
                ds_scr[rows, :] = dst.astype(BF16)
            dfr_acc[i] += dfr
            dsb = ds_scr[...]
            dv_scr[...] += jnp.dot(pt_scr[...], do_, preferred_element_type=F32)
            dk_scr[...] += jnp.dot(dsb, q, preferred_element_type=F32)
            dq_acc[pl.ds(r0, tq), :] += lax.dot_general(dsb, k, (((0,), (0,)), ((), ())),
                                                        preferred_element_type=F32)

        def full_block(i, carry):
            block(i, False)
            return carry

        block(j, True)
        lax.fori_loop(j + 1, nq, full_block, 0)
        dk_ref[...] = (dk_scr[...] * _LN2).astype(BF16)
        dv_ref[...] = dv_scr[...].astype(BF16)
        df_ref[...] = -df_scr[...]

        @pl.when(j == nq - 1)
        def _():
            dq_ref[...] = (dq_acc[...] * scale).astype(BF16)
            dfr_ref[...] = dfr_acc[...]

    return _call(
        body, "attn_bwd", grid=(H, nq),
        in_specs=[pl.BlockSpec((S, LANE), lambda h, j: (0, h)),
                  pl.BlockSpec((S, LANE), lambda h, j: (0, h)),
                  pl.BlockSpec((tq, LANE), lambda h, j: (j, H + h)),
                  pl.BlockSpec((tq, LANE), lambda h, j: (j, 2 * H + h)),
                  pl.BlockSpec((tq, LANE), lambda h, j: (j, 0)),
                  pl.BlockSpec((None, nq, 1, tq), lambda h, j: (h, 0, 0, 0)),
                  pl.BlockSpec((None, nq, 1, tq), lambda h, j: (h, 0, 0, 0))],
        out_specs=[pl.BlockSpec((S, LANE), lambda h, j: (0, h)),
                   pl.BlockSpec((tq, LANE), lambda h, j: (j, h)),
                   pl.BlockSpec((tq, LANE), lambda h, j: (j, h)),
                   pl.BlockSpec((None, tq, 1), lambda h, j: (h, j, 0)),
                   pl.BlockSpec((None, nq, 1, tq), lambda h, j: (h, 0, 0, 0))],
        out_shape=[SDS((S, D), BF16), SDS((S, D), BF16), SDS((S, D), BF16), SDS((H, S, 1), F32),
                   SDS((H, nq, 1, tq), F32)],
        scratch=[pltpu.VMEM((S, LANE), F32), pltpu.VMEM((nq, 1, tq), F32),
                 pltpu.VMEM((tq, tq), F32), pltpu.VMEM((tq, tq), F32), pltpu.VMEM((tq, tq), BF16),
                 pltpu.VMEM((tq, tq), BF16), pltpu.VMEM((tq, 1), F32), pltpu.VMEM((tq, 1), F32),
                 pltpu.VMEM((tq, LANE), F32), pltpu.VMEM((tq, LANE), F32)],
        sem=("parallel", "arbitrary"), comm=comm, args=[qkv, do, qkv, qkv, F_sh, A_rows, delta_rows])


def _head_dots(do, o, H, tr):
    def fn(r, v):
        prod = r[0].astype(F32) * r[1].astype(F32)
        lane = lax.broadcasted_iota(jnp.int32, (prod.shape[0], LANE), 1)
        out = jnp.zeros((prod.shape[0], LANE), F32)
        for h in range(H):
            s = jnp.sum(prod[:, h * LANE:(h + 1) * LANE], axis=1, keepdims=True)
            out = jnp.where(lane == h, s, out)
        return [out], []
    return _rowwise("attn_delta", fn, [do, o], [], [(LANE, F32)], [], tr)[0]


def _adamw(name, w, g, m, v, layer=None, prev=None):
    R, C = g.shape
    tr = _pick(R, max(8, (512 * 1024) // max(C, 1) // 8 * 8), 8)
    c1 = 1.0 - ADAM_B1 ** ADAM_STEP
    c2 = 1.0 - ADAM_B2 ** ADAM_STEP

    def body(w_ref, g_ref, m_ref, v_ref, go_ref, d_ref, mo_ref, vo_ref):
        g_ = g_ref[...]
        m_ = ADAM_B1 * m_ref[...] + (1.0 - ADAM_B1) * g_
        v_ = ADAM_B2 * v_ref[...] + (1.0 - ADAM_B2) * (g_ * g_)
        go_ref[...] = g_
        d_ref[...] = -ADAM_LR * ((m_ / c1) / (jnp.sqrt(v_ / c2) + ADAM_EPS) + ADAM_WD * w_ref[...])
        mo_ref[...] = m_
        vo_ref[...] = v_

    if w.ndim == 3 and w.shape[1] == 1 and layer is None:
        tc = _pick(C, 2 * LANE, LANE)
        gspec = pl.BlockSpec((R, tc), lambda i: (0, i))
        pspec = pl.BlockSpec((R, None, tc), lambda i: (0, 0, i))
        grid = (C // tc,)
    else:
        gspec = pl.BlockSpec((tr, C), lambda i: (i, 0))
        pspec = gspec if layer is None else pl.BlockSpec((None, tr, C), lambda i: (layer, i, 0))
        grid = (R // tr,)
    return _call(body, name, grid=grid, in_specs=[pspec, gspec, pspec, pspec], out_specs=[pspec] * 4,
                 out_shape=[SDS(w.shape, F32)] * 4, sem=("parallel",), args=[w, g, m, v],
                 keep=[(p, k) for k, p in enumerate(prev)] if prev else ())


def _sum_slots(name, a):
    n, R, C = a.shape
    tr = _pick(R, 256, 8)

    def body(a_ref, o_ref):
        acc = a_ref[0]
        for k in range(1, n):
            acc = acc + a_ref[k]
        o_ref[...] = acc

    return pl.pallas_call(body, name=name, grid=(R // tr,),
                          in_specs=[pl.BlockSpec((n, tr, C), lambda i: (0, i, 0))],
                          out_specs=pl.BlockSpec((tr, C), lambda i: (i, 0)), out_shape=SDS((R, C), F32),
                          compiler_params=_params(("parallel",)))(a)


def _silu_rows(name, a):
    def body(a_ref, o_ref):
        z = a_ref[...]
        o_ref[...] = (z / (1.0 + jnp.exp(-z))).astype(BF16)
    return pl.pallas_call(body, name=name, out_shape=SDS(a.shape, BF16), compiler_params=_params())(a)


def _place():
    return lax.axis_index("x"), lax.axis_index("y"), lax.axis_index("c")


def _rcopy(src, dst, ssem, rsem, dev):
    return pltpu.make_async_remote_copy(src_ref=src, dst_ref=dst, send_sem=ssem, recv_sem=rsem,
                                        device_id=dev, device_id_type=MESH)


def _other_chips(x, y):
    return [(1 - x, y), (x, 1 - y), (1 - x, 1 - y)]


def _job_gather_devices(buf8):
    def views(o):
        x, y, c = _place()
        slot = lambda px, py, pc: o.at[4 * px + 2 * py + pc]
        return c, (x, y, 1 - c), _other_chips(x, y), slot, slot(x, y, c)

    def first(o, ss, rs):
        c, sib, chips, slot, me = views(o)
        return [_rcopy(me, me, ss.at[0], rs.at[0], sib)] + [
            _rcopy(me, me, ss.at[1 + j], rs.at[1 + j], (cx, cy, c)) for j, (cx, cy) in enumerate(chips)]

    def start(i, o, ss, rs):
        for cp in first(o[0], ss, rs):
            cp.start()

    def wait(i, o, ss, rs):
        c, sib, chips, slot, me = views(o[0])
        x, y = sib[0], sib[1]
        passed = []
        for j, (cx, cy) in enumerate(chips):
            got = slot(cx, cy, c)
            _rcopy(me, got, ss.at[1 + j], rs.at[1 + j], (cx, cy, c)).wait_recv()
            passed.append(_rcopy(got, got, ss.at[4 + j], rs.at[4 + j], sib))
            passed[j].start()
        _rcopy(me, slot(x, y, 1 - c), ss.at[0], rs.at[0], sib).wait_recv()
        for j, (cx, cy) in enumerate(chips):
            _rcopy(me, slot(cx, cy, 1 - c), ss.at[4 + j], rs.at[4 + j], sib).wait_recv()
        for cp in first(o[0], ss, rs) + passed:
            cp.wait_send()

    return _Job([buf8], [SDS(buf8.shape, buf8.dtype)], {0: 0}, N_DEV - 1, start, wait)


def _gather_devices(name, buf, slot_idx, extra=None):
    R, C = buf.shape

    def place(s_ref, b_ref, o_ref):
        o_ref[...] = b_ref[...]

    buf8 = _call(place, name + "_place", grid=(1,), prefetch=slot_idx, args=[buf],
                 in_specs=[pl.BlockSpec((R, C), lambda i, s: (0, 0))],
                 out_specs=[pl.BlockSpec((None, R, C), lambda i, s: (s[0], 0, 0))],
                 out_shape=[SDS((N_DEV, R, C), buf.dtype)])[0]
    jobs, absorb_extra = extra if extra else ([], None)
    got = []

    def absorb(outs):
        got.append(outs[0])
        if absorb_extra:
            absorb_extra(outs[1:])

    _comm_call(name, ([_job_gather_devices(buf8)] + jobs, absorb))
    return got[0]


def _cast_slabs(name, ws, idx, comm=None):
    n = len(ws)
    steps = next(s for s in (8, 4, 2, 1)
                 if all((w.shape[2] % (LANE * s) if l is None else w.shape[1] % (16 * s)) == 0 for w, l in ws))

    def body(s_ref, *refs):
        for w_ref, o_ref in zip(refs[:n], refs[n:]):
            o_ref[...] = w_ref[...].astype(BF16)

    in_specs, out_specs, out_shape = [], [], []
    for w, l in ws:
        if l is None:
            R, _, C = w.shape
            in_specs.append(pl.BlockSpec((R, None, C // steps), lambda i, s: (0, 0, i)))
            out_specs.append(pl.BlockSpec((None, R, C // steps), lambda i, s: (s[1], 0, i)))
        else:
            _, R, C = w.shape
            in_specs.append(pl.BlockSpec((None, R // steps, C), lambda i, s, l=l: (l, i, 0)))
            out_specs.append(pl.BlockSpec((None, R // steps, C), lambda i, s: (s[1], i, 0)))
        out_shape.append(SDS((N_CHIPS, R, C), BF16))
    return _call(body, name, grid=(steps,), prefetch=idx, comm=comm, sem=("parallel",), args=[w for w, _ in ws],
                 in_specs=in_specs, out_specs=out_specs, out_shape=out_shape)


def _half(ref, lead, hf, cols):
    n = ref.shape[-1 if cols else -2] // 2
    assert 2 * n == ref.shape[-1 if cols else -2]
    cut = pl.ds(hf * n, n)
    return ref.at[(*lead, slice(None), cut) if cols else (*lead, cut, slice(None))]


def _job_gather_ici(buf, cols=False):
    def views(o):
        x, y, c = _place()
        return c, _other_chips(x, y), _half(o, (2 * x + y,), c, cols)

    def start(i, o, ss, rs):
        c, chips, mine = views(o[0])
        for j, (cx, cy) in enumerate(chips):
            _rcopy(mine, mine, ss.at[j], rs.at[j], (cx, cy, c)).start()

    def wait(i, o, ss, rs):
        c, chips, mine = views(o[0])
        for j, (cx, cy) in enumerate(chips):
            cp = _rcopy(mine, _half(o[0], (2 * cx + cy,), c, cols), ss.at[j], rs.at[j], (cx, cy, c))
            cp.wait_send()
            cp.wait_recv()

    return _Job([buf], [SDS(buf.shape, buf.dtype)], {0: 0}, 3, start, wait)


def _job_gather_pair(buf, cols=False):
    def views(o):
        x, y, c = _place()
        return c, (x, y, 1 - c), _other_chips(x, y)

    def start(i, o, ss, rs):
        c, sib, chips = views(o[0])
        for j, (cx, cy) in enumerate(chips):
            got = _half(o[0], (2 * cx + cy,), c, cols)
            _rcopy(got, got, ss.at[j], rs.at[j], sib).start()

    def wait(i, o, ss, rs):
        c, sib, chips = views(o[0])
        for j, (cx, cy) in enumerate(chips):
            cp = _rcopy(_half(o[0], (2 * cx + cy,), c, cols), _half(o[0], (2 * cx + cy,), 1 - c, cols),
                        ss.at[j], rs.at[j], sib)
            cp.wait_send()
            cp.wait_recv()

    return _Job([buf], [SDS(buf.shape, buf.dtype)], {0: 0}, 3, start, wait)


class _Gather:
    def __init__(self, bufs, by_cols=()):
        self.todo, self.half, self.done, self.by_cols = dict(bufs), {}, {}, set(by_cols)

    def comm(self, admit=()):
        second, first = list(self.half), list(admit)
        jobs = [_job_gather_pair(self.half[n], n in self.by_cols) for n in second]
        jobs += [_job_gather_ici(self.todo[n], n in self.by_cols) for n in first]

        def absorb(outs):
            for n, o in zip(second + first, outs):
                if n in self.half:
                    del self.half[n]
                    self.done[n] = o
                else:
                    del self.todo[n]
                    self.half[n] = o
        return (jobs, absorb) if jobs else None


def _half_shape(R, C, cols):
    return (R, C // 2) if cols else (R // 2, C)


def _job_rs_pair(g4, cols=False):
    _, R, C = g4.shape

    def desc(i, o, ss, rs):
        x, y, c = _place()
        return _rcopy(_half(i[0], (slice(None),), 1 - c, cols), o[0], ss.at[0], rs.at[0], (x, y, 1 - c))

    return _Job([g4], [SDS((N_CHIPS,) + _half_shape(R, C, cols), F32)], {}, 1,
                lambda i, o, ss, rs: desc(i, o, ss, rs).start(), lambda i, o, ss, rs: desc(i, o, ss, rs).wait())


def _job_rs_chips(p4):
    _, hR, C = p4.shape

    def descs(i, o, ss, rs):
        x, y, c = _place()
        return [_rcopy(i[0].at[2 * cx + cy], o[0].at[j], ss.at[j], rs.at[j], (cx, cy, c))
                for j, (cx, cy) in enumerate(_other_chips(x, y))]

    def start(i, o, ss, rs):
        for cp in descs(i, o, ss, rs):
            cp.start()

    def wait(i, o, ss, rs):
        for cp in descs(i, o, ss, rs):
            cp.wait_send()
            cp.wait_recv()

    return _Job([p4], [SDS((3, hR, C), p4.dtype)], {}, 3, start, wait)


def _job_rs_join(buf, cols=False):
    def desc(o, ss, rs, recv):
        x, y, c = _place()
        mine = _half(o[0], (), c, cols)
        return _rcopy(mine, _half(o[0], (), 1 - c, cols) if recv else mine, ss.at[0], rs.at[0], (x, y, 1 - c))

    def wait(i, o, ss, rs):
        cp = desc(o, ss, rs, True)
        cp.wait_send()
        cp.wait_recv()

    return _Job([buf], [SDS(buf.shape, buf.dtype)], {0: 0}, 1,
                lambda i, o, ss, rs: desc(o, ss, rs, False).start(), wait)


def _walk(R, C, cols):
    if cols:
        tc = _pick(C, 2 * LANE, LANE)
        return (R, tc), C // tc
    tr = _pick(R, max(16, (1 << 20) // C // 16 * 16), 16)
    return (tr, C), R // tr


def _rs_add_pair(name, g4, recv, idx, cols=False):
    _, R, C = g4.shape
    hshape = _half_shape(R, C, cols)
    blk, nb = _walk(*hshape, cols)
    at = (lambda k, i: (k, 0, i)) if cols else (lambda k, i: (k, i, 0))

    def body(s_ref, a_ref, b_ref, pb_ref, po_ref):
        s = a_ref[...] + b_ref[...]
        pb_ref[...] = s.astype(BF16)

        @pl.when(pl.program_id(1) == s_ref[1])
        def _():
            po_ref[...] = s

    gs = pltpu.PrefetchScalarGridSpec(
        num_scalar_prefetch=1, grid=(nb, N_CHIPS),
        in_specs=[pl.BlockSpec((None,) + blk, lambda i, k, s: at(k, s[0] * nb + i)),
                  pl.BlockSpec((None,) + blk, lambda i, k, s: at(k, i))],
        out_specs=[pl.BlockSpec((None,) + blk, lambda i, k, s: at(k, i)),
                   pl.BlockSpec(blk, lambda i, k, s: at(k, i)[1:])])
    return pl.pallas_call(body, name=name, grid_spec=gs,
                          out_shape=[SDS((N_CHIPS,) + hshape, BF16), SDS(hshape, F32)],
                          compiler_params=_params(("parallel", "arbitrary")))(idx, g4, recv)


def _rs_add_chips(name, own, recv3, idx, cols=False):
    hR, hC = own.shape
    blk, nb = _walk(hR, hC, cols)
    at = (lambda i: (0, i)) if cols else (lambda i: (i, 0))

    def body(s_ref, a_ref, b_ref, o_ref):
        o_ref[...] = ((a_ref[...] + b_ref[0].astype(F32)) + b_ref[1].astype(F32)) + b_ref[2].astype(F32)

    gs = pltpu.PrefetchScalarGridSpec(
        num_scalar_prefetch=1, grid=(nb,),
        in_specs=[pl.BlockSpec(blk, lambda i, s: at(i)), pl.BlockSpec((3,) + blk, lambda i, s: (0,) + at(i))],
        out_specs=pl.BlockSpec(blk, lambda i, s: at(s[0] * nb + i)))
    return pl.pallas_call(body, name=name, grid_spec=gs,
                          out_shape=SDS((hR, 2 * hC) if cols else (2 * hR, hC), F32),
                          compiler_params=_params(("parallel",)))(idx, own, recv3)


class _ReduceScatter:
    def __init__(self, idx):
        self.idx, self.items, self.done = idx, [], {}

    def push(self, tag, g4, cols=False):
        self.items.append([tag, 0, g4, cols])

    def comm(self, ici=1):
        cur, jobs = [], []
        for item in self.items:
            tag, stage, data, cols = item
            if stage == 1:
                if ici == 0:
                    continue
                ici -= 1
            cur.append(item)
            jobs.append(_job_rs_pair(data, cols) if stage == 0 else _job_rs_chips(data[0]) if stage == 1
                        else _job_rs_join(data, cols))

        def absorb(outs):
            for item, o in zip(cur, outs):
                tag, stage, data, cols = item
                if stage == 0:
                    item[1:3] = [1, _rs_add_pair(tag + "_rs_add2", data, o, self.idx, cols)]
                elif stage == 1:
                    item[1:3] = [2, _rs_add_chips(tag + "_rs_add4", data[1], o, self.idx, cols)]
                else:
                    self.items.remove(item)
                    self.done[tag] = o
        return (jobs, absorb) if jobs else None


def _resid_epilogue(accs, ex):
    return [ex[0] + ex[1] * accs[0], accs[0]]


def _sigmoid(x):
    return 0.5 * jnp.tanh(0.5 * x) + 0.5


def _swiglu_epilogue(accs, ex):
    gt, up = accs
    return [gt, up, gt * _sigmoid(gt) * up]


def _swiglu_bwd_epilogue(accs, ex):
    dact = accs[0]
    gt, up = ex[0].astype(F32), ex[1].astype(F32)
    sg = _sigmoid(gt)
    silu = gt * sg
    return [dact * up * (sg + silu * (1.0 - sg)), dact * silu]


def kernel(x, c, ada_w, ada_b, norm_mix_g, norm_ffn_g, a_w_in, a_b_in, a_ln_g, a_ln_b, a_w_s, a_b_s, a_w_out, b_w_in, b_b_f, b_w_out, ffn_w_gate, ffn_w_up, ffn_w_down, final_g, loss_target, m_ada_w, m_ada_b, m_norm_mix_g, m_norm_ffn_g, m_a_w_in, m_a_b_in, m_a_ln_g, m_a_ln_b, m_a_w_s, m_a_b_s, m_a_w_out, m_b_w_in, m_b_b_f, m_b_w_out, m_ffn_w_gate, m_ffn_w_up, m_ffn_w_down, m_final_g, v_ada_w, v_ada_b, v_norm_mix_g, v_norm_ffn_g, v_a_w_in, v_a_b_in, v_a_ln_g, v_a_ln_b, v_a_w_s, v_a_b_s, v_a_w_out, v_b_w_in, v_b_b_f, v_b_w_out, v_ffn_w_gate, v_ffn_w_up, v_ffn_w_down, v_final_g):
    S, D = x.shape[1], x.shape[2]
    H = D // LANE
    G = D // LANE
    FH = ffn_w_down.shape[1] * N_CHIPS
    depth = ada_w.shape[0]
    assert depth == 2 and a_w_in.shape[0] == 1 and b_w_in.shape[0] == 1
    mx, my, mc = _place()
    chip = 2 * mx + my
    dev = 4 * mx + 2 * my + mc
    x0 = x[0]
    target = loss_target[0]
    tr = _pick(S, 256, 8)
    tm = _pick(S, 512, 8)
    tq_f = _pick(S, 1024, LANE)
    tq = _pick(S, 512, LANE)
    nq = S // tq

    idx = jnp.stack([mc, chip]).astype(jnp.int32)
    t3 = lambda a: jnp.transpose(a, (2, 0, 1))
    shards = {"a_in": (a_w_in, 0), "a_out": (a_w_out, 0), "b_in": (t3(b_w_in), None), "b_out": (b_w_out, 0)}
    for l in range(depth):
        shards.update({f"wg{l}": (ffn_w_gate, l), f"wu{l}": (ffn_w_up, l), f"wd{l}": (ffn_w_down, l)})
    first = ["a_in", "a_out"]
    rest = [n for n in shards if n not in first]
    ag = _Gather(dict(zip(first, _cast_slabs("cast_first", [shards[n] for n in first], idx))), by_cols=["b_in"])
    ag.todo.update(zip(rest, _cast_slabs("cast_rest", [shards[n] for n in rest], idx, comm=ag.comm(first))))
    _comm_call("ag_first_pair", ag.comm(["wg0"]))
    wa_in4 = ag.done["a_in"]
    wa_out = ag.done["a_out"].reshape(D, D)

    dev_idx = dev.astype(jnp.int32).reshape(1)
    c_all = _gather_devices("ag_c", jnp.pad(c, ((0, 7), (0, 0))), dev_idx).reshape(N_DEV, 8, D)[:, 0, :]
    c_act = _silu_rows("c_silu", c_all)
    n_loc = ada_w.shape[2]
    mods = []
    for l in range(depth):
        b_loc = lax.dynamic_slice_in_dim(ada_b[l:l + 1], chip * n_loc, n_loc, axis=1)
        mods.append(_mm(f"ada_fwd{l}", "nn", [c_act], [ada_w], M=N_DEV, N=n_loc, K=D, tm=N_DEV, b_layer=l,
                        tn=_pick(n_loc, 1024, LANE), tk=D, extras=[("row", b_loc)],
                        epilogue=lambda accs, ex: [accs[0] + ex[0]])[0])
    mod_all = _gather_devices("ag_mod", jnp.concatenate(mods, axis=1), dev_idx)
    mod_mine = lax.dynamic_index_in_dim(mod_all[0::2], dev, axis=1, keepdims=False)
    mod_mine = mod_mine.reshape(N_CHIPS, depth, n_loc).transpose(1, 0, 2).reshape(depth, 1, N_MOD * D)
    mod = [[mod_mine[l, :, i * D:(i + 1) * D] for i in range(N_MOD)] for l in range(depth)]

    row = lambda a: a.reshape(1, -1)

    tril = jnp.tril(jnp.ones((LANE, LANE), dtype=bool))
    w_mask = jnp.where(tril[None], a_w_s[0], 0.0).astype(BF16)
    bias_full = jnp.repeat(a_b_s[0].T, LANE, axis=1)
    bf_pad = jnp.pad(b_b_f, ((0, 0), (0, LANE - H)))

    admit = {"sgu_in": ["wu0"], "mix_out0": ["wd0"], "ffn_in0": ["b_in", "b_out"], "ffn_out0": [],
             "fox_qkv": ["wg1"], "attn_fwd": ["wu1", "wd1"], "mix_out1": []}
    saved = []
    xs = x0
    for l in range(depth):
        sh1, sc1, g1, sh2, sc2, g2 = mod[l]
        st = {"x_in": xs}
        h1 = _normmod_fwd(f"normmod_mix{l}", xs, row(norm_mix_g[l]), sc1, sh1, tr)
        st["h1"] = h1
        if l == 0:
            a = _mm("sgu_in", "nn", [h1], [wa_in4], M=S, N=2 * D, K=D, tm=tm, tn=2 * D // N_CHIPS, tk=D,
                    b_stacked=True, extras=[("row", a_b_in)], comm=ag.comm(admit["sgu_in"]),
                    epilogue=lambda accs, ex: [accs[0] + ex[0]])[0]
            y = _sgu_mid_fwd(a, a_ln_g, a_ln_b, w_mask, bias_full, tr)
            st["a"], st["y"] = a, y
            w_o, mix_out = wa_out, y
        else:
            w_qkv = ag.done["b_in"].reshape(-1, D)
            w_f = jnp.pad(w_qkv[3 * D:], ((0, LANE - H), (0, 0)))
            q_scale = jnp.concatenate([jnp.full((1, D), _LOG2E / math.sqrt(LANE), F32), jnp.ones((1, 2 * D), F32)],
                                      axis=1)
            qkv = _mm("fox_qkv", "nt", [h1], [w_qkv], M=S, N=3 * D, K=D, tm=tm, tn=_pick(3 * D, 1024, LANE),
                      tk=D, out_dtypes=(BF16,), extras=[("row", q_scale)],
                      epilogue=lambda accs, ex: [accs[0] * ex[0]], comm=ag.comm(admit["fox_qkv"]))[0]
            fl = _mm("fox_f", "nt", [h1], [w_f], M=S, N=LANE, K=D, tm=tm, tn=LANE, tk=D)[0]
            F_sh = _fox_gate_fwd(fl, bf_pad)
            F_hs = F_sh[:, :H].T
            F_rows = F_hs.reshape(H, nq, 1, tq)
            o, lse = _attn_fwd(qkv, F_sh, F_hs.reshape(H, S // tq_f, 1, tq_f) * _LOG2E, H, tq_f,
                               comm=ag.comm(admit["attn_fwd"]))
            st.update(qkv=qkv, fl=fl, F_sh=F_sh, F_rows=F_rows, o=o, lse=lse, w_qkv=w_qkv, w_f=w_f)
            w_o, mix_out = ag.done["b_out"].reshape(D, D), o
        x1, out1 = _mm(f"mix_out{l}", "nn", [mix_out], [w_o], M=S, N=D, K=D, tm=tm, tn=_pick(D, 1024, LANE),
                       tk=D, extras=[("tile", xs), ("row", g1)], out_dtypes=(F32, BF16),
                       epilogue=_resid_epilogue, comm=ag.comm(admit[f"mix_out{l}"]))
        st["x_mid"], st["out1"] = x1, out1
        h2 = _normmod_fwd(f"normmod_ffn{l}", x1, row(norm_ffn_g[l]), sc2, sh2, tr)
        gt, up, act = _mm(f"ffn_in{l}", "nn", [h2, h2], [ag.done[f"wg{l}"], ag.done[f"wu{l}"]], M=S, N=FH, K=D,
                          tm=tm, tn=FH // N_CHIPS, tk=D, b_stacked=True, acc_of=[0, 1], n_acc=2,
                          out_dtypes=(BF16, BF16, BF16), epilogue=_swiglu_epilogue,
                          comm=ag.comm(admit.get(f"ffn_in{l}", ())))
        x2, out2 = _mm(f"ffn_out{l}", "nn", [act], [ag.done[f"wd{l}"].reshape(FH, D)], M=S, N=D, K=FH, tm=tm,
                       tn=_pick(D, 1024, LANE), tk=FH, extras=[("tile", x1), ("row", g2)],
                       out_dtypes=(F32, BF16), epilogue=_resid_epilogue,
                       comm=ag.comm(admit.get(f"ffn_out{l}", ())))
        st.update(h2=h2, gt=gt, up=up, act=act, out2=out2)
        saved.append(st)
        xs = x2
    assert not ag.todo and not ag.half
    wg4 = [ag.done[f"wg{l}"] for l in range(depth)]
    wu4 = [ag.done[f"wu{l}"] for l in range(depth)]
    wd = [ag.done[f"wd{l}"].reshape(FH, D) for l in range(depth)]
    wb_out = ag.done["b_out"].reshape(D, D)

    dx, dog, loss_vec, g_final, dg2 = _loss_head("loss_head", xs, target, row(final_g), tr,
                                                 nxt=(saved[-1]["out2"], mod[-1][5]))
    loss = lax.psum(loss_vec[0, 0], ("x", "y", "c"))

    rs = _ReduceScatter(idx)
    dmods = [None] * depth
    gmix = [None] * depth
    gffn = [None] * depth
    for l in reversed(range(depth)):
        sh1, sc1, g1, sh2, sc2, g2 = mod[l]
        st = saved[l]
        dgt, dup = _mm(f"ffn_dact{l}", "nt", [dog], [wd[l]], M=S, N=FH, K=D, tm=tm, tn=FH // N_CHIPS, tk=D,
                       extras=[("tile", st["gt"]), ("tile", st["up"])], out_dtypes=(BF16, BF16),
                       epilogue=_swiglu_bwd_epilogue, comm=rs.comm())
        rs.push(f"wd{l}", _mm(f"ffn_dwd{l}", "tn", [st["act"]], [dog], M=FH, N=D, K=S, tm=FH // N_CHIPS,
                              tn=_pick(D, 512, LANE), tk=S, outer="i",
                              comm=rs.comm())[0].reshape(N_CHIPS, FH // N_CHIPS, D))
        rs.push(f"wg{l}", _mm(f"ffn_dwg{l}", "tn", [st["h2"]], [dgt], M=D, N=FH, K=S, tm=_pick(D, 512, LANE),
                              tn=FH // N_CHIPS, tk=S, out_stacked=True, comm=rs.comm())[0])
        rs.push(f"wu{l}", _mm(f"ffn_dwu{l}", "tn", [st["h2"]], [dup], M=D, N=FH, K=S, tm=_pick(D, 512, LANE),
                              tn=FH // N_CHIPS, tk=S, out_stacked=True, comm=rs.comm())[0])
        dh2 = _mm(f"ffn_dh{l}", "nt", [dgt, dup], [wg4[l], wu4[l]], M=S, N=D, K=FH, tm=_pick(S, 512, 8),
                  tn=_pick(D, 1024, LANE), tk=FH, b_stacked=True, out_dtypes=(BF16,), comm=rs.comm())[0]
        dx, dog, dsh2, dsc2, gffn[l], dg1 = _normmod_bwd(f"normmod_ffn_bwd{l}", st["x_mid"], dh2, dx,
                                                         row(norm_ffn_g[l]), sc2, tr, nxt=(st["out1"], g1),
                                                         comm=rs.comm(ici=0))
        if l == 0:
            dy = _mm("sgu_dy", "nt", [dog], [wa_out], M=S, N=D, K=D, tm=tm, tn=_pick(D, 1024, LANE), tk=D,
                     out_dtypes=(BF16,))[0]
            rs.push("a_out", _mm("sgu_dwout", "tn", [st["y"]], [dog], M=D, N=D, K=S, tm=_pick(D, 512, LANE),
                                 tn=_pick(D, 1024, LANE),
                                 tk=S)[0].reshape(N_CHIPS, D // N_CHIPS, D))
            da, dws, dbias, g_ln_g, g_ln_b, g_b_in = _sgu_mid_bwd(st["a"], dy, a_ln_g, a_ln_b, w_mask, bias_full, tr,
                                                                  comm=rs.comm())
            rs.push("a_in", _mm("sgu_dwin", "tn", [st["h1"]], [da], M=D, N=2 * D, K=S, tm=_pick(D, 512, LANE),
                                tn=2 * D // N_CHIPS, tk=S, out_stacked=True, comm=rs.comm())[0])
            dh1 = _mm("sgu_dh", "nt", [da], [wa_in4], M=S, N=D, K=2 * D, tm=_pick(S, 512, 8), tn=_pick(D, 1024, LANE),
                      tk=2 * D, b_stacked=True, out_dtypes=(BF16,), comm=rs.comm())[0]
            g_w_s = jnp.where(tril[None], dws, 0.0)
            g_b_s = jnp.sum(dbias.reshape(LANE, G, LANE), axis=2).T
        else:
            do = _mm("fox_do", "nt", [dog], [wb_out], M=S, N=D, K=D, tm=tm, tn=_pick(D, 1024, LANE), tk=D,
                     out_dtypes=(BF16,))[0]
            rs.push("b_out", _mm("fox_dwout", "tn", [st["o"]], [dog], M=D, N=D, K=S, tm=_pick(D, 512, LANE),
                                 tn=_pick(D, 1024, LANE),
                                 tk=S)[0].reshape(N_CHIPS, D // N_CHIPS, D))
            delta = _head_dots(do, st["o"], H, tr)
            delta_rows = delta[:, :H].T.reshape(H, nq, 1, tq)
            A_rows = (st["F_rows"] - st["lse"].reshape(H, nq, 1, tq)) * _LOG2E
            w_qkv, w_f = st["w_qkv"], st["w_f"]
            dq, dk, dv, dF_k, dF_q = _attn_bwd(st["qkv"], do, st["F_sh"], A_rows, delta_rows, H, tq,
                                               comm=rs.comm(ici=2))
            dF_sh = jnp.pad((dF_k.reshape(H, S) + dF_q.reshape(H, S)).T, ((0, 0), (0, LANE - H)))
            dfl, db_f = _fox_gate_bwd(st["fl"], dF_sh, bf_pad)
            dfl_b = dfl.astype(BF16)
            dh_f = _mm("fox_dh_f", "nn", [dfl_b], [w_f], M=S, N=D, K=LANE, tm=tm, tn=_pick(D, 1024, LANE),
                       tk=LANE)[0]
            dh1 = _mm("fox_dh", "nn", [dq, dk, dv], [w_qkv, w_qkv, w_qkv], M=S, N=D, K=D, tm=_pick(S, 256, 8),
                      tn=_pick(D, 1024, LANE), tk=D, b_koffs=[0, 1, 2],
                      extras=[("tile", dh_f)], epilogue=lambda accs, ex: [accs[0] + ex[0]], out_dtypes=(BF16,),
                      comm=rs.comm())[0]
            tmw = _pick(D, 512, LANE)
            g_bT = None
            for p, (nm, d_) in enumerate((("q", dq), ("k", dk), ("v", dv))):
                g_bT = _mm(f"fox_dw{nm}", "tn", [d_], [st["h1"]], M=D, N=D, K=S, tm=tmw, tn=_pick(D, 1024, LANE),
                           tk=S, out_into=(3 * D + H, p * (D // tmw), g_bT), comm=rs.comm(ici=0))[0]
            dwf = _mm("fox_dwf", "tn", [dfl_b], [st["h1"]], M=LANE, N=D, K=S, tm=LANE, tn=_pick(D, 1024, LANE),
                      tk=S)[0]
            g_bT = lax.dynamic_update_slice(g_bT, dwf[:H], (3 * D, 0))
            rs.push("b_in", g_bT.reshape(N_CHIPS, -1, D), cols=True)
            g_b_f = db_f[:, :H]
        below = (saved[l - 1]["out2"], mod[l - 1][5]) if l else None
        res = _normmod_bwd(f"normmod_mix_bwd{l}", st["x_in"], dh1, dx, row(norm_mix_g[l]), sc1, tr, nxt=below,
                           comm=rs.comm(ici=1 - l))
        if l:
            dx, dog_below, dsh1, dsc1, gmix[l], dg2_below = res
        else:
            dx, dsh1, dsc1, gmix[l] = res
        dmods[l] = jnp.concatenate([dsh1, dsc1, dg1, dsh2, dsc2, dg2], axis=1)
        if l:
            dog, dg2 = dog_below, dg2_below
    grad_x = dx[None]

    small = [jnp.concatenate(dmods, axis=0), jnp.concatenate(gmix, axis=0), jnp.concatenate(gffn, axis=0),
             g_b_in, g_ln_g, g_ln_b, g_w_s[None], g_b_s[None], g_b_f, g_final[0]]
    small_w = [ada_b, norm_mix_g, norm_ffn_g, a_b_in, a_ln_g, a_ln_b, a_w_s, a_b_s, b_b_f, final_g]
    small_m = [m_ada_b, m_norm_mix_g, m_norm_ffn_g, m_a_b_in, m_a_ln_g, m_a_ln_b, m_a_w_s, m_a_b_s, m_b_b_f, m_final_g]
    small_v = [v_ada_b, v_norm_mix_g, v_norm_ffn_g, v_a_b_in, v_a_ln_g, v_a_ln_b, v_a_w_s, v_a_b_s, v_b_b_f, v_final_g]
    sizes = [w.size for w in small_w]
    total = sum(sizes)
    padded = -(-total // (8 * LANE)) * (8 * LANE)

    def pack(parts):
        flat = jnp.concatenate([p.reshape(-1) for p in parts])
        return jnp.pad(flat, (0, padded - total)).reshape(padded // LANE, LANE)

    def unpack(buf):
        flat = buf.reshape(-1)
        outs, off = [], 0
        for w, n in zip(small_w, sizes):
            outs.append(flat[off:off + n].reshape(w.shape))
            off += n
        return outs

    g_all = _gather_devices("ag_small_grads", pack(small), dev_idx, extra=rs.comm())
    g_small = _sum_slots("sum_small_grads", g_all)
    _, d_small, m_small, v_small = _adamw("adamw_small", pack(small_w), g_small, pack(small_m), pack(small_v))
    sg, sd, sm, sv_ = unpack(g_small), unpack(d_small), unpack(m_small), unpack(v_small)

    n_dm = depth * N_MOD * D
    dmod_all = g_all.reshape(N_DEV, -1)[:, :n_dm].reshape(N_DEV, depth, N_MOD * D)
    u_ada = None
    for l in range(depth):
        dm_loc = lax.dynamic_slice_in_dim(dmod_all[:, l, :], chip * n_loc, n_loc, axis=1).astype(BF16)
        g_ada = _mm(f"ada_dw{l}", "tn", [c_act], [dm_loc], M=D, N=n_loc, K=N_DEV, tm=_pick(D, 512, LANE),
                    tn=_pick(n_loc, 1024, LANE), tk=N_DEV)[0]
        u_ada = _adamw(f"adamw_ada_w{l}", ada_w, g_ada, m_ada_w, v_ada_w, layer=l, prev=u_ada)

    flushes = 0
    while rs.items:
        _comm_call(f"rs_flush{flushes}", rs.comm())
        flushes += 1
    groups = {"a_in": (a_w_in, m_a_w_in, v_a_w_in), "a_out": (a_w_out, m_a_w_out, v_a_w_out),
              "b_out": (b_w_out, m_b_w_out, v_b_w_out),
              "wg": (ffn_w_gate, m_ffn_w_gate, v_ffn_w_gate), "wu": (ffn_w_up, m_ffn_w_up, v_ffn_w_up),
              "wd": (ffn_w_down, m_ffn_w_down, v_ffn_w_down)}
    upd = {}
    for n, (w_, m_, v_) in groups.items():
        for l in range(w_.shape[0]):
            tag = n if n in rs.done else f"{n}{l}"
            upd[n] = _adamw("adamw_" + tag, w_, rs.done[tag], m_, v_, layer=l, prev=upd.get(n))
    u_a_in, u_a_out, u_b_out, u_wg, u_wu, u_wd = (upd[n] for n in groups)
    u_b_in = [jnp.transpose(t, (1, 2, 0)) for t in
              _adamw("adamw_b_in", t3(b_w_in), rs.done["b_in"], t3(m_b_w_in), t3(v_b_w_in))]

    def leaves(i, small_list):
        s = small_list
        return [u_ada[i], s[0], s[1], s[2], u_a_in[i], s[3], s[4], s[5], s[6], s[7], u_a_out[i],
                u_b_in[i], s[8], u_b_out[i], u_wg[i], u_wu[i], u_wd[i], s[9]]

    return (loss, grad_x, *leaves(0, sg), *leaves(1, sd), *leaves(2, sm), *leaves(3, sv_))
```

```python
import functools
import math

import jax
import jax.numpy as jnp
from jax import lax
from jax.experimental import pallas as pl
from jax.experimental.pallas import tpu as pltpu

F32, BF16 = jnp.float32, jnp.bfloat16
LANE = 128
N_CHIPS = 4
N_DEV = 8
N_MOD = 6
EPS = 1e-6
VMEM_LIMIT = 60 * 1024 * 1024
ADAM_LR, ADAM_B1, ADAM_B2, ADAM_EPS, ADAM_WD, ADAM_STEP = 0.001, 0.9, 0.999, 1e-08, 0.01, 10
MESH = pl.DeviceIdType.MESH
ANY = pl.BlockSpec(memory_space=pl.ANY)
SDS = jax.ShapeDtypeStruct


def _pick(dim, pref, align):
    t = min(dim, pref)
    t -= t % align
    while t >= align:
        if dim % t == 0:
            return t
        t -= align
    return dim


def _params(sem=None):
    return pltpu.CompilerParams(dimension_semantics=sem, vmem_limit_bytes=VMEM_LIMIT)


class _Job:
    def __init__(self, ins, outs, aliases, nsem, start, wait):
        self.ins, self.outs, self.aliases, self.nsem, self.start, self.wait = ins, outs, aliases, nsem, start, wait


def _call(body, name, *, grid, in_specs, out_specs, out_shape, args, scratch=(), sem=None, comm=None,
          prefetch=None, keep=()):
    jobs, absorb = comm if comm else ([], None)
    n_pre = 0 if prefetch is None else 1
    n_in, n_out, n_scr = len(args), len(out_shape), len(scratch)
    c_ins = [a for a, _ in keep] + [a for j in jobs for a in j.ins]
    c_outs = [s for j in jobs for s in j.outs]
    aliases = {n_pre + n_in + k: o for k, (_, o) in enumerate(keep)}
    c_scr = []
    i_off, o_off = len(keep), 0
    for j in jobs:
        for a, b in j.aliases.items():
            aliases[n_pre + n_in + i_off + a] = n_out + o_off + b
        i_off += len(j.ins)
        o_off += len(j.outs)
        c_scr += [pltpu.SemaphoreType.DMA((j.nsem,)), pltpu.SemaphoreType.DMA((j.nsem,))]

    def wrapped(*refs):
        bounds = [n_pre + n_in, len(c_ins), n_out, len(c_outs), n_scr]
        parts, p = [], 0
        for n in bounds:
            parts.append(refs[p:p + n])
            p += n
        main_in, cin, main_out, cout, main_scr = parts
        cin = cin[len(keep):]
        csem = refs[p:]

        def run(phase):
            a = b = 0
            for k, j in enumerate(jobs):
                fn = j.start if phase == 0 else j.wait
                fn(cin[a:a + len(j.ins)], cout[b:b + len(j.outs)], csem[2 * k], csem[2 * k + 1])
                a += len(j.ins)
                b += len(j.outs)

        if jobs and grid:
            first = functools.reduce(jnp.logical_and, [pl.program_id(d) == 0 for d in range(len(grid))])
            last = functools.reduce(jnp.logical_and, [pl.program_id(d) == grid[d] - 1 for d in range(len(grid))])
            pl.when(first)(lambda: run(0))
            body(*main_in, *main_out, *main_scr)
            pl.when(last)(lambda: run(1))
        elif jobs:
            run(0)
            body(*main_in, *main_out, *main_scr)
            run(1)
        else:
            body(*main_in, *main_out, *main_scr)

    specs = dict(grid=grid, in_specs=list(in_specs) + [ANY] * len(c_ins),
                 out_specs=list(out_specs) + [ANY] * len(c_outs), scratch_shapes=list(scratch) + c_scr)
    if n_pre:
        specs = dict(grid_spec=pltpu.PrefetchScalarGridSpec(num_scalar_prefetch=1, **specs))
    outs = pl.pallas_call(
        wrapped, name=name, out_shape=list(out_shape) + c_outs, input_output_aliases=aliases,
        compiler_params=_params(("arbitrary",) * len(grid) if jobs else sem), **specs,
    )(*([prefetch] if n_pre else []), *args, *c_ins)
    if jobs:
        absorb(list(outs[n_out:]))
    return list(outs[:n_out])


def _comm_call(name, comm):
    _call(lambda: None, name, grid=(), in_specs=[], out_specs=[], out_shape=[], args=[], comm=comm)


def _mm(name, form, a_list, b_list, *, M, N, K, tm, tn, tk, b_stacked=False, out_stacked=False,
        b_koffs=None, acc_of=None, n_acc=1, extras=(), out_dtypes=(F32,), epilogue=None, comm=None,
        outer="j", b_layer=None, out_into=None):
    assert M % tm == 0 and N % tn == 0 and K % tk == 0, (name, M, N, K, tm, tn, tk)
    nm, nn, nk = M // tm, N // tn, K // tk
    npairs = len(a_list)
    acc_of = acc_of or [0] * npairs
    b_koffs = b_koffs or [0] * npairs
    if epilogue is None:
        epilogue = lambda accs, ex: [accs[0]]

    def spec(shape, fn, **kw):
        return pl.BlockSpec(shape, fn if outer == "j" else (lambda i, j, k: fn(j, i, k)), **kw)

    slabs = form == "nt" and b_stacked and tk == K

    a_uniq = [a for p, a in enumerate(a_list) if all(a is not b for b in a_list[:p])]
    a_pos = [next(u for u, b in enumerate(a_uniq) if b is a) for a in a_list]
    in_specs = []
    for _ in a_uniq:
        if form == "tn":
            in_specs.append(spec((tk, tm), lambda j, i, k: (k, i)))
        else:
            in_specs.append(spec((tm, tk), lambda j, i, k: (i, k)))
    for off in b_koffs:
        if b_layer is not None:
            assert form == "nn" and not b_stacked
            in_specs.append(spec((None, tk, tn), lambda j, i, k: (b_layer, k, j)))
        elif form == "nn":
            if b_stacked:
                assert tn * N_CHIPS == N
                in_specs.append(spec((None, tk, tn), lambda j, i, k: (j, k, 0)))
            else:
                in_specs.append(spec((tk, tn), lambda j, i, k, off=off: (off + k, j)))
        elif form == "nt":
            if slabs:
                in_specs.append(spec((N_CHIPS, tn, K // N_CHIPS), lambda j, i, k: (0, j, 0),
                                     pipeline_mode=pl.Buffered(1)))
            elif b_stacked:
                assert tk * N_CHIPS == K
                in_specs.append(spec((None, tn, tk), lambda j, i, k: (k, j, 0)))
            else:
                in_specs.append(spec((tn, tk), lambda j, i, k, off=off: (j, off + k)))
        else:
            in_specs.append(spec((tk, tn), lambda j, i, k: (k, j)))
    for kind, _ in extras:
        if kind == "tile":
            in_specs.append(spec((tm, tn), lambda j, i, k: (i, j)))
        else:
            in_specs.append(spec((1, tn), lambda j, i, k: (0, j)))
    if out_stacked:
        assert tn * N_CHIPS == N
        out_specs = [spec((None, tm, tn), lambda j, i, k: (j, i, 0)) for _ in out_dtypes]
        out_shape = [SDS((N_CHIPS, M, tn), d) for d in out_dtypes]
    else:
        rows, roff, prev = out_into if out_into else (M, 0, None)
        out_specs = [spec((tm, tn), lambda j, i, k: (roff + i, j)) for _ in out_dtypes]
        out_shape = [SDS((rows, N), d) for d in out_dtypes]
    dims = {"nn": (((1,), (0,)), ((), ())), "nt": (((1,), (1,)), ((), ())), "tn": (((0,), (0,)), ((), ()))}[form]
    n_a, n_ex, n_out = len(a_uniq), len(extras), len(out_dtypes)

    def body(*refs):
        a_vals = [refs[u][...].astype(BF16) for u in range(n_a)]
        b_refs = refs[n_a:n_a + npairs]
        e_refs = refs[n_a + npairs:n_a + npairs + n_ex]
        o_refs = refs[n_a + npairs + n_ex:n_a + npairs + n_ex + n_out]
        acc_refs = refs[n_a + npairs + n_ex + n_out:]

        tot = [None] * n_acc
        for p in range(npairs):
            if slabs:
                n = K // N_CHIPS
                d = sum(lax.dot_general(a_vals[a_pos[p]][:, s * n:(s + 1) * n], b_refs[p][s], dims,
                                        preferred_element_type=F32) for s in range(N_CHIPS))
            else:
                d = lax.dot_general(a_vals[a_pos[p]], b_refs[p][...].astype(BF16), dims, preferred_element_type=F32)
            tot[acc_of[p]] = d if tot[acc_of[p]] is None else tot[acc_of[p]] + d

        def finish(accs):
            outs = epilogue(accs, [e[...] for e in e_refs])
            for o_ref, o in zip(o_refs, outs):
                o_ref[...] = o.astype(o_ref.dtype)

        if nk == 1:
            finish(tot)
        else:
            k = pl.program_id(2)

            @pl.when(k == 0)
            def _():
                for r, t in zip(acc_refs, tot):
                    r[...] = t

            @pl.when(k > 0)
            def _():
                for r, t in zip(acc_refs, tot):
                    r[...] += t

            @pl.when(k == nk - 1)
            def _():
                finish([r[...] for r in acc_refs])

    scratch = [pltpu.VMEM((tm, tn), F32) for _ in range(n_acc)] if nk > 1 else []
    return _call(body, name, grid=(nn, nm, nk) if outer == "j" else (nm, nn, nk), in_specs=in_specs,
                 out_specs=out_specs, out_shape=out_shape, scratch=scratch,
                 sem=("parallel", "parallel", "arbitrary"), comm=comm,
                 keep=[(out_into[2], 0)] if out_into and out_into[2] is not None else (),
                 args=[*a_uniq, *b_list, *[e for _, e in extras]])


def _rowwise(name, fn, rows, vecs, row_outs, acc_widths, tr, comm=None):
    S = rows[0].shape[0]
    assert S % tr == 0
    nr, nv, no, na = len(rows), len(vecs), len(row_outs), len(acc_widths)

    def body(*refs):
        r = [x[...] for x in refs[:nr]]
        v = [x[...] for x in refs[nr:nr + nv]]
        o_refs = refs[nr + nv:nr + nv + no]
        a_refs = refs[nr + nv + no:]
        outs, accs = fn(r, v)
        for o_ref, o in zip(o_refs, outs):
            o_ref[...] = o.astype(o_ref.dtype)
        first = pl.program_id(0) == 0

        @pl.when(first)
        def _():
            for a_ref, a in zip(a_refs, accs):
                a_ref[...] = a

        @pl.when(jnp.logical_not(first))
        def _():
            for a_ref, a in zip(a_refs, accs):
                a_ref[...] += a

    in_specs = [pl.BlockSpec((tr, x.shape[1]), lambda i: (i, 0)) for x in rows]
    in_specs += [pl.BlockSpec(x.shape, lambda i, nd=x.ndim: (0,) * nd) for x in vecs]
    out_specs = [pl.BlockSpec((tr, w), lambda i: (i, 0)) for w, _ in row_outs]
    out_specs += [pl.BlockSpec((1, w), lambda i: (0, 0)) for w in acc_widths]
    out_shape = [SDS((S, w), d) for w, d in row_outs] + [SDS((1, w), F32) for w in acc_widths]
    return _call(body, name, grid=(S // tr,), in_specs=in_specs, out_specs=out_specs, out_shape=out_shape,
                 sem=("arbitrary",), comm=comm, args=[*rows, *vecs])


def _colsum(a):
    return jnp.sum(a, axis=0, keepdims=True)


def _rms_stats(x):
    rstd = lax.rsqrt(jnp.mean(x * x, axis=1, keepdims=True) + EPS)
    return x * rstd, rstd


def _normmod_fwd(name, x, g, sc, sh, tr):
    def fn(r, v):
        n, _ = _rms_stats(r[0])
        return [(n * v[0]) * (1.0 + v[1]) + v[2]], []
    return _rowwise(name, fn, [x], [g, sc, sh], [(x.shape[1], BF16)], [], tr)[0]


def _gated(dx, nxt_out, gate):
    return gate * dx, _colsum(dx * nxt_out.astype(F32))


def _normmod_bwd(name, x, dh, dres, g, sc, tr, nxt=None, comm=None):
    def fn(r, v):
        x_, dh_, dres_ = r[0], r[1].astype(F32), r[2]
        g_, sc_ = v[:2]
        n, rstd = _rms_stats(x_)
        hn = n * g_
        dhn = dh_ * (1.0 + sc_)
        dn = dhn * g_
        dx = dres_ + rstd * (dn - n * jnp.mean(dn * n, axis=1, keepdims=True))
        outs, accs = [dx], [_colsum(dh_), _colsum(dh_ * hn), _colsum(dhn * n)]
        if nxt:
            dog, dgate = _gated(dx, r[3], v[2])
            outs.append(dog)
            accs.append(dgate)
        return outs, accs
    D = x.shape[1]
    rows, vecs = [x, dh, dres] + ([nxt[0]] if nxt else []), [g, sc] + ([nxt[1]] if nxt else [])
    return _rowwise(name, fn, rows, vecs, [(D, F32)] + ([(D, BF16)] if nxt else []), [D] * (4 if nxt else 3), tr,
                    comm=comm)


def _loss_head(name, x, target, g, tr, nxt):
    D = x.shape[1]

    def fn(r, v):
        n, rstd = _rms_stats(r[0])
        err = n * v[0] - r[1]
        loss = 0.5 * jnp.sum(jnp.mean(err * err, axis=1, keepdims=True), axis=0, keepdims=True)
        dy = err * (1.0 / D)
        dn = dy * v[0]
        dx = rstd * (dn - n * jnp.mean(dn * n, axis=1, keepdims=True))
        dog, dgate = _gated(dx, r[2], v[1])
        return [dx, dog], [jnp.broadcast_to(loss, (1, LANE)), _colsum(dy * n), dgate]
    return _rowwise(name, fn, [x, target, nxt[0]], [g, nxt[1]], [(D, F32), (D, BF16)], [LANE, D, D], tr)


_INV_SQRT2 = 1.0 / math.sqrt(2.0)
_INV_SQRT2PI = 1.0 / math.sqrt(2.0 * math.pi)


def _gelu(a):
    return 0.5 * a * (1.0 + lax.erf(a * _INV_SQRT2))


def _gelu_grad(a):
    return 0.5 * (1.0 + lax.erf(a * _INV_SQRT2)) + a * jnp.exp(-0.5 * a * a) * _INV_SQRT2PI


def _ln_stats(v):
    mu = jnp.mean(v, axis=1, keepdims=True)
    vc = v - mu
    rstd = lax.rsqrt(jnp.mean(vc * vc, axis=1, keepdims=True) + EPS)
    return vc * rstd, rstd


def _chunk_mix(w_ref, vn, tr, G):
    rows = []
    for ch in range(tr // LANE):
        cols = []
        for g in range(G):
            blk = vn[ch * LANE:(ch + 1) * LANE, g * LANE:(g + 1) * LANE]
            cols.append(jnp.dot(w_ref[g], blk, preferred_element_type=F32))
        rows.append(jnp.concatenate(cols, axis=1))
    return jnp.concatenate(rows, axis=0)


def _sgu_mid_fwd(a, ln_g, ln_b, w_mask, bias_full, tr):
    S, D2 = a.shape
    D = D2 // 2
    G = D // LANE

    def body(a_ref, g_ref, b_ref, w_ref, bias_ref, y_ref):
        u = _gelu(a_ref[:, :D])
        v = _gelu(a_ref[:, D:])
        vhat, _ = _ln_stats(v)
        vn = (vhat * g_ref[...] + b_ref[...]).astype(BF16)
        sv = _chunk_mix(w_ref, vn, tr, G) + jnp.concatenate([bias_ref[...]] * (tr // LANE), axis=0)
        y_ref[...] = (u * sv).astype(BF16)

    return pl.pallas_call(
        body, name="sgu_mid_fwd", grid=(S // tr,),
        in_specs=[pl.BlockSpec((tr, D2), lambda i: (i, 0)), pl.BlockSpec((1, D), lambda i: (0, 0)),
                  pl.BlockSpec((1, D), lambda i: (0, 0)), pl.BlockSpec((G, LANE, LANE), lambda i: (0, 0, 0)),
                  pl.BlockSpec((LANE, D), lambda i: (0, 0))],
        out_specs=pl.BlockSpec((tr, D), lambda i: (i, 0)), out_shape=SDS((S, D), BF16),
        compiler_params=_params(("arbitrary",)))(a, ln_g, ln_b, w_mask, bias_full)


def _sgu_mid_bwd(a, dy, ln_g, ln_b, w_mask, bias_full, tr, comm=None):
    S, D2 = a.shape
    D = D2 // 2
    G = D // LANE
    nch = tr // LANE

    def body(a_ref, dy_ref, g_ref, b_ref, w_ref, bias_ref, da_ref, dw_ref, dbias_ref, dg_ref, db_ref, dbin_ref):
        au, av = a_ref[:, :D], a_ref[:, D:]
        u = _gelu(au)
        v = _gelu(av)
        vhat, rstd = _ln_stats(v)
        vn = (vhat * g_ref[...] + b_ref[...]).astype(BF16)
        sv = _chunk_mix(w_ref, vn, tr, G) + jnp.concatenate([bias_ref[...]] * nch, axis=0)
        dy_ = dy_ref[...].astype(F32)
        du = dy_ * sv
        dsv = dy_ * u
        dsv_b = dsv.astype(BF16)
        first = pl.program_id(0) == 0

        @pl.when(first)
        def _():
            dw_ref[...] = jnp.zeros_like(dw_ref)
            dbias_ref[...] = jnp.zeros_like(dbias_ref)
            dg_ref[...] = jnp.zeros_like(dg_ref)
            db_ref[...] = jnp.zeros_like(db_ref)
            dbin_ref[...] = jnp.zeros_like(dbin_ref)

        rows = []
        dbias = None
        for ch in range(nch):
            r0 = ch * LANE
            cols = []
            for g in range(G):
                c0 = g * LANE
                ds_blk = dsv_b[r0:r0 + LANE, c0:c0 + LANE]
                vn_blk = vn[r0:r0 + LANE, c0:c0 + LANE]
                cols.append(lax.dot_general(w_ref[g], ds_blk, (((0,), (0,)), ((), ())),
                                            preferred_element_type=F32))
                dw_ref[g] += lax.dot_general(ds_blk, vn_blk, (((1,), (1,)), ((), ())),
                                             preferred_element_type=F32)
            rows.append(jnp.concatenate(cols, axis=1))
            blk = dsv[r0:r0 + LANE, :]
            dbias = blk if dbias is None else dbias + blk
        dvn = jnp.concatenate(rows, axis=0)
        dbias_ref[...] += dbias
        dg_ref[...] += _colsum(dvn * vhat)
        db_ref[...] += _colsum(dvn)
        dvh = dvn * g_ref[...]
        dv = rstd * (dvh - jnp.mean(dvh, axis=1, keepdims=True)
                     - vhat * jnp.mean(dvh * vhat, axis=1, keepdims=True))
        da_u = du * _gelu_grad(au)
        da_v = dv * _gelu_grad(av)
        da_ref[:, :D] = da_u.astype(BF16)
        da_ref[:, D:] = da_v.astype(BF16)
        dbin_ref[:, :D] += _colsum(da_u)
        dbin_ref[:, D:] += _colsum(da_v)

    full = lambda shp: pl.BlockSpec(shp, lambda i, nd=len(shp): (0,) * nd)
    return _call(
        body, "sgu_mid_bwd", grid=(S // tr,),
        in_specs=[pl.BlockSpec((tr, D2), lambda i: (i, 0)), pl.BlockSpec((tr, D), lambda i: (i, 0)),
                  full((1, D)), full((1, D)), full((G, LANE, LANE)), full((LANE, D))],
        out_specs=[pl.BlockSpec((tr, D2), lambda i: (i, 0)), full((G, LANE, LANE)), full((LANE, D)),
                   full((1, D)), full((1, D)), full((1, D2))],
        out_shape=[SDS((S, D2), BF16), SDS((G, LANE, LANE), F32), SDS((LANE, D), F32),
                   SDS((1, D), F32), SDS((1, D), F32), SDS((1, D2), F32)],
        sem=("arbitrary",), comm=comm, args=[a, dy, ln_g, ln_b, w_mask, bias_full])


def _tri(lower):
    r = lax.broadcasted_iota(jnp.int32, (LANE, LANE), 0)
    c = lax.broadcasted_iota(jnp.int32, (LANE, LANE), 1)
    return jnp.where((c <= r) if lower else (c >= r), 1.0, 0.0).astype(F32)


def _fox_gate_fwd(fl, bf_pad):
    S = fl.shape[0]
    nblk = S // LANE

    def body(fl_ref, b_ref, f_ref):
        tri = _tri(True)

        def step(i, carry):
            r0 = pl.multiple_of(i * LANE, LANE)
            z = fl_ref[pl.ds(r0, LANE), :] + b_ref[...]
            logf = jnp.minimum(z, 0.0) - jnp.log1p(jnp.exp(-jnp.abs(z)))
            f = jnp.dot(tri, logf, preferred_element_type=F32, precision=lax.Precision.HIGHEST) + carry
            f_ref[pl.ds(r0, LANE), :] = f
            return f[LANE - 1:LANE, :]
        lax.fori_loop(0, nblk, step, jnp.zeros((1, LANE), F32))

    return pl.pallas_call(body, name="fox_gate_fwd", out_shape=SDS((S, LANE), F32),
                          compiler_params=_params())(fl, bf_pad)


def _fox_gate_bwd(fl, dF, bf_pad):
    S = fl.shape[0]
    nblk = S // LANE

    def body(fl_ref, df_ref, b_ref, dfl_ref, db_ref):
        tri = _tri(True)

        def prefix(i, carry):
            r0 = pl.multiple_of(i * LANE, LANE)
            p = jnp.dot(tri, df_ref[pl.ds(r0, LANE), :], preferred_element_type=F32,
                        precision=lax.Precision.HIGHEST) + carry
            dfl_ref[pl.ds(r0, LANE), :] = p
            return p[LANE - 1:LANE, :]
        total = lax.fori_loop(0, nblk, prefix, jnp.zeros((1, LANE), F32))
        db_ref[...] = jnp.zeros_like(db_ref)

        def finish(i, carry):
            r0 = pl.multiple_of(i * LANE, LANE)
            dlogf = total - dfl_ref[pl.ds(r0, LANE), :] + df_ref[pl.ds(r0, LANE), :]
            z = fl_ref[pl.ds(r0, LANE), :] + b_ref[...]
            dfl = dlogf / (1.0 + jnp.exp(z))
            dfl_ref[pl.ds(r0, LANE), :] = dfl
            db_ref[...] += _colsum(dfl)
            return carry
        lax.fori_loop(0, nblk, finish, 0)

    return pl.pallas_call(body, name="fox_gate_bwd", out_shape=[SDS((S, LANE), F32), SDS((1, LANE), F32)],
                          compiler_params=_params())(fl, dF, bf_pad)


def _lane_pick(blk, h):
    lane = lax.broadcasted_iota(jnp.int32, blk.shape, 1)
    return jnp.sum(jnp.where(lane == h, blk, 0.0), axis=1, keepdims=True)


_NEG = -1e30
_LOG2E = 1.0 / math.log(2.0)
_LN2 = math.log(2.0)
_SUB = 64


def _attn_fwd(qkv, F_sh, F_rows, H, tq, comm=None):
    S = qkv.shape[0]
    D = H * LANE
    nq = S // tq
    scale = 1.0 / math.sqrt(LANE)

    def body(q_ref, k_ref, v_ref, fsh_ref, frow_ref, o_ref, lse_ref):
        h, i = pl.program_id(0), pl.program_id(1)
        q = q_ref[...]
        fq = _lane_pick(fsh_ref[...], h) * _LOG2E

        def block(j, carry, diagonal):
            m, l, acc = carry
            r0 = pl.multiple_of(j * tq, tq)
            k = k_ref[pl.ds(r0, tq), :]
            v = v_ref[pl.ds(r0, tq), :]
            s = lax.dot_general(q, k, (((1,), (1,)), ((), ())), preferred_element_type=F32)
            s = s + (fq - frow_ref[j])
            if diagonal:
                row = lax.broadcasted_iota(jnp.int32, (tq, tq), 0)
                col = lax.broadcasted_iota(jnp.int32, (tq, tq), 1)
                s = jnp.where(col <= row, s, _NEG)
            m_new = jnp.maximum(m, jnp.max(s, axis=1, keepdims=True))
            alpha = jnp.exp2(m - m_new)
            p = jnp.exp2(s - m_new)
            l = alpha * l + jnp.sum(p, axis=1, keepdims=True)
            acc = alpha * acc + jnp.dot(p.astype(BF16), v, preferred_element_type=F32)
            return m_new, l, acc

        init = (jnp.full((tq, 1), _NEG, F32), jnp.zeros((tq, 1), F32), jnp.zeros((tq, LANE), F32))
        carry = lax.fori_loop(0, i, lambda j, cr: block(j, cr, False), init)
        m, l, acc = block(i, carry, True)
        o_ref[...] = (acc / l).astype(BF16)
        lse_ref[...] = (m + jnp.log2(l)) * _LN2

    return _call(
        body, "attn_fwd", grid=(H, nq),
        in_specs=[pl.BlockSpec((tq, LANE), lambda h, i: (i, h)),
                  pl.BlockSpec((S, LANE), lambda h, i: (0, H + h)),
                  pl.BlockSpec((S, LANE), lambda h, i: (0, 2 * H + h)),
                  pl.BlockSpec((tq, LANE), lambda h, i: (i, 0)),
                  pl.BlockSpec((None, nq, 1, tq), lambda h, i: (h, 0, 0, 0))],
        out_specs=[pl.BlockSpec((tq, LANE), lambda h, i: (i, h)),
                   pl.BlockSpec((None, tq, 1), lambda h, i: (h, i, 0))],
        out_shape=[SDS((S, D), BF16), SDS((H, S, 1), F32)],
        sem=("parallel", "arbitrary"), comm=comm, args=[qkv, qkv, qkv, F_sh, F_rows])


def _attn_bwd(qkv, do, F_sh, A_rows, delta_rows, H, tq, comm=None):
    S = qkv.shape[0]
    D = H * LANE
    nq = S // tq
    scale = 1.0 / math.sqrt(LANE)

    def body(q_ref, do_ref, k_ref, v_ref, fsh_ref, a_ref, dl_ref, dq_ref, dk_ref, dv_ref, df_ref, dfr_ref,
             dq_acc, dfr_acc, st_scr, dp_scr, pt_scr, ds_scr, fk_scr, df_scr, dk_scr, dv_scr):
        h, j = pl.program_id(0), pl.program_id(1)

        @pl.when(j == 0)
        def _():
            dq_acc[...] = jnp.zeros_like(dq_acc)
            dfr_acc[...] = jnp.zeros_like(dfr_acc)

        k = k_ref[...]
        v = v_ref[...]
        fk_scr[...] = _lane_pick(fsh_ref[...], h) * _LOG2E
        df_scr[...] = jnp.zeros((tq, 1), F32)
        dk_scr[...] = jnp.zeros((tq, LANE), F32)
        dv_scr[...] = jnp.zeros((tq, LANE), F32)

        def block(i, diagonal):
            r0 = pl.multiple_of(i * tq, tq)
            q = q_ref[pl.ds(r0, tq), :]
            do_ = do_ref[pl.ds(r0, tq), :]
            st_scr[...] = lax.dot_general(k, q, (((1,), (1,)), ((), ())), preferred_element_type=F32)
            dp_scr[...] = lax.dot_general(v, do_, (((1,), (1,)), ((), ())), preferred_element_type=F32)
            a_row, dl_row = a_ref[i], dl_ref[i]
            dfr = jnp.zeros((1, tq), F32)
            for r in range(tq // _SUB):
                rows = slice(r * _SUB, (r + 1) * _SUB)
                arg = st_scr[rows, :] + (a_row - fk_scr[rows, :])
                if diagonal:
                    row = lax.broadcasted_iota(jnp.int32, (_SUB, tq), 0) + r * _SUB
                    col = lax.broadcasted_iota(jnp.int32, (_SUB, tq), 1)
                    arg = jnp.where(row <= col, arg, _NEG)
                pt = jnp.exp2(arg)
                dst = pt * (dp_scr[rows, :] - dl_row)
                df_scr[rows, :] += jnp.sum(dst, axis=1, keepdims=True)
                dfr = dfr + jnp.sum(dst, axis=0, keepdims=True)
                pt_scr[rows, :] = pt.astype(BF16)
                ds_scr[rows, :] = dst.astype(BF16)
            dfr_acc[i] += dfr
            dsb = ds_scr[...]
            dv_scr[...] += jnp.dot(pt_scr[...], do_, preferred_element_type=F32)
            dk_scr[...] += jnp.dot(dsb, q, preferred_element_type=F32)
            dq_acc[pl.ds(r0, tq), :] += lax.dot_general(dsb, k, (((0,), (0,)), ((), ())),
                                                        preferred_element_type=F32)

        def full_block(i, carry):
            block(i, False)
            return carry

        block(j, True)
        lax.fori_loop(j + 1, nq, full_block, 0)
        dk_ref[...] = (dk_scr[...] * _LN2).astype(BF16)
        dv_ref[...] = dv_scr[...].astype(BF16)
        df_ref[...] = -df_scr[...]

        @pl.when(j == nq - 1)
        def _():
            dq_ref[...] = (dq_acc[...] * scale).astype(BF16)
            dfr_ref[...] = dfr_acc[...]

    return _call(
        body, "attn_bwd", grid=(H, nq),
        in_specs=[pl.BlockSpec((S, LANE), lambda h, j: (0, h)),
                  pl.BlockSpec((S, LANE), lambda h, j: (0, h)),
                  pl.BlockSpec((tq, LANE), lambda h, j: (j, H + h)),
                  pl.BlockSpec((tq, LANE), lambda h, j: (j, 2 * H + h)),
                  pl.BlockSpec((tq, LANE), lambda h, j: (j, 0)),
                  pl.BlockSpec((None, nq, 1, tq), lambda h, j: (h, 0, 0, 0)),
                  pl.BlockSpec((None, nq, 1, tq), lambda h, j: (h, 0, 0, 0))],
        out_specs=[pl.BlockSpec((S, LANE), lambda h, j: (0, h)),
                   pl.BlockSpec((tq, LANE), lambda h, j: (j, h)),
                   pl.BlockSpec((tq, LANE), lambda h, j: (j, h)),
                   pl.BlockSpec((None, tq, 1), lambda h, j: (h, j, 0)),
                   pl.BlockSpec((None, nq, 1, tq), lambda h, j: (h, 0, 0, 0))],
        out_shape=[SDS((S, D), BF16), SDS((S, D), BF16), SDS((S, D), BF16), SDS((H, S, 1), F32),
                   SDS((H, nq, 1, tq), F32)],
        scratch=[pltpu.VMEM((S, LANE), F32), pltpu.VMEM((nq, 1, tq), F32),
                 pltpu.VMEM((tq, tq), F32), pltpu.VMEM((tq, tq), F32), pltpu.VMEM((tq, tq), BF16),
                 pltpu.VMEM((tq, tq), BF16), pltpu.VMEM((tq, 1), F32), pltpu.VMEM((tq, 1), F32),
                 pltpu.VMEM((tq, LANE), F32), pltpu.VMEM((tq, LANE), F32)],
        sem=("parallel", "arbitrary"), comm=comm, args=[qkv, do, qkv, qkv, F_sh, A_rows, delta_rows])


def _head_dots(do, o, H, tr):
    def fn(r, v):
        prod = r[0].astype(F32) * r[1].astype(F32)
        lane = lax.broadcasted_iota(jnp.int32, (prod.shape[0], LANE), 1)
        out = jnp.zeros((prod.shape[0], LANE), F32)
        for h in range(H):
            s = jnp.sum(prod[:, h * LANE:(h + 1) * LANE], axis=1, keepdims=True)
            out = jnp.where(lane == h, s, out)
        return [out], []
    return _rowwise("attn_delta", fn, [do, o], [], [(LANE, F32)], [], tr)[0]


def _adamw(name, w, g, m, v, layer=None, prev=None):
    R, C = g.shape
    tr = _pick(R, max(8, (512 * 1024) // max(C, 1) // 8 * 8), 8)
    c1 = 1.0 - ADAM_B1 ** ADAM_STEP
    c2 = 1.0 - ADAM_B2 ** ADAM_STEP

    def body(w_ref, g_ref, m_ref, v_ref, go_ref, d_ref, mo_ref, vo_ref):
        g_ = g_ref[...]
        m_ = ADAM_B1 * m_ref[...] + (1.0 - ADAM_B1) * g_
        v_ = ADAM_B2 * v_ref[...] + (1.0 - ADAM_B2) * (g_ * g_)
        go_ref[...] = g_
        d_ref[...] = -ADAM_LR * ((m_ / c1) / (jnp.sqrt(v_ / c2) + ADAM_EPS) + ADAM_WD * w_ref[...])
        mo_ref[...] = m_
        vo_ref[...] = v_

    if w.ndim == 3 and w.shape[1] == 1 and layer is None:
        tc = _pick(C, 2 * LANE, LANE)
        gspec = pl.BlockSpec((R, tc), lambda i: (0, i))
        pspec = pl.BlockSpec((R, None, tc), lambda i: (0, 0, i))
        grid = (C // tc,)
    else:
        gspec = pl.BlockSpec((tr, C), lambda i: (i, 0))
        pspec = gspec if layer is None else pl.BlockSpec((None, tr, C), lambda i: (layer, i, 0))
        grid = (R // tr,)
    return _call(body, name, grid=grid, in_specs=[pspec, gspec, pspec, pspec], out_specs=[pspec] * 4,
                 out_shape=[SDS(w.shape, F32)] * 4, sem=("parallel",), args=[w, g, m, v],
                 keep=[(p, k) for k, p in enumerate(prev)] if prev else ())


def _sum_slots(name, a):
    n, R, C = a.shape
    tr = _pick(R, 256, 8)

    def body(a_ref, o_ref):
        acc = a_ref[0]
        for k in range(1, n):
            acc = acc + a_ref[k]
        o_ref[...] = acc

    return pl.pallas_call(body, name=name, grid=(R // tr,),
                          in_specs=[pl.BlockSpec((n, tr, C), lambda i: (0, i, 0))],
                          out_specs=pl.BlockSpec((tr, C), lambda i: (i, 0)), out_shape=SDS((R, C), F32),
                          compiler_params=_params(("parallel",)))(a)


def _silu_rows(name, a):
    def body(a_ref, o_ref):
        z = a_ref[...]
        o_ref[...] = (z / (1.0 + jnp.exp(-z))).astype(BF16)
    return pl.pallas_call(body, name=name, out_shape=SDS(a.shape, BF16), compiler_params=_params())(a)


def _place():
    return lax.axis_index("x"), lax.axis_index("y"), lax.axis_index("c")


def _rcopy(src, dst, ssem, rsem, dev):
    return pltpu.make_async_remote_copy(src_ref=src, dst_ref=dst, send_sem=ssem, recv_sem=rsem,
                                        device_id=dev, device_id_type=MESH)


def _other_chips(x, y):
    return [(1 - x, y), (x, 1 - y), (1 - x, 1 - y)]


def _job_gather_devices(buf8):
    def views(o):
        x, y, c = _place()
        slot = lambda px, py, pc: o.at[4 * px + 2 * py + pc]
        return c, (x, y, 1 - c), _other_chips(x, y), slot, slot(x, y, c)

    def first(o, ss, rs):
        c, sib, chips, slot, me = views(o)
        return [_rcopy(me, me, ss.at[0], rs.at[0], sib)] + [
            _rcopy(me, me, ss.at[1 + j], rs.at[1 + j], (cx, cy, c)) for j, (cx, cy) in enumerate(chips)]

    def start(i, o, ss, rs):
        for cp in first(o[0], ss, rs):
            cp.start()

    def wait(i, o, ss, rs):
        c, sib, chips, slot, me = views(o[0])
        x, y = sib[0], sib[1]
        passed = []
        for j, (cx, cy) in enumerate(chips):
            got = slot(cx, cy, c)
            _rcopy(me, got, ss.at[1 + j], rs.at[1 + j], (cx, cy, c)).wait_recv()
            passed.append(_rcopy(got, got, ss.at[4 + j], rs.at[4 + j], sib))
            passed[j].start()
        _rcopy(me, slot(x, y, 1 - c), ss.at[0], rs.at[0], sib).wait_recv()
        for j, (cx, cy) in enumerate(chips):
            _rcopy(me, slot(cx, cy, 1 - c), ss.at[4 + j], rs.at[4 + j], sib).wait_recv()
        for cp in first(o[0], ss, rs) + passed:
            cp.wait_send()

    return _Job([buf8], [SDS(buf8.shape, buf8.dtype)], {0: 0}, N_DEV - 1, start, wait)


def _gather_devices(name, buf, slot_idx, extra=None):
    R, C = buf.shape

    def place(s_ref, b_ref, o_ref):
        o_ref[...] = b_ref[...]

    buf8 = _call(place, name + "_place", grid=(1,), prefetch=slot_idx, args=[buf],
                 in_specs=[pl.BlockSpec((R, C), lambda i, s: (0, 0))],
                 out_specs=[pl.BlockSpec((None, R, C), lambda i, s: (s[0], 0, 0))],
                 out_shape=[SDS((N_DEV, R, C), buf.dtype)])[0]
    jobs, absorb_extra = extra if extra else ([], None)
    got = []

    def absorb(outs):
        got.append(outs[0])
        if absorb_extra:
            absorb_extra(outs[1:])

    _comm_call(name, ([_job_gather_devices(buf8)] + jobs, absorb))
    return got[0]


def _cast_slabs(name, ws, idx, comm=None):
    n = len(ws)
    steps = next(s for s in (8, 4, 2, 1)
                 if all((w.shape[2] % (LANE * s) if l is None else w.shape[1] % (16 * s)) == 0 for w, l in ws))

    def body(s_ref, *refs):
        for w_ref, o_ref in zip(refs[:n], refs[n:]):
            o_ref[...] = w_ref[...].astype(BF16)

    in_specs, out_specs, out_shape = [], [], []
    for w, l in ws:
        if l is None:
            R, _, C = w.shape
            in_specs.append(pl.BlockSpec((R, None, C // steps), lambda i, s: (0, 0, i)))
            out_specs.append(pl.BlockSpec((None, R, C // steps), lambda i, s: (s[1], 0, i)))
        else:
            _, R, C = w.shape
            in_specs.append(pl.BlockSpec((None, R // steps, C), lambda i, s, l=l: (l, i, 0)))
            out_specs.append(pl.BlockSpec((None, R // steps, C), lambda i, s: (s[1], i, 0)))
        out_shape.append(SDS((N_CHIPS, R, C), BF16))
    return _call(body, name, grid=(steps,), prefetch=idx, comm=comm, sem=("parallel",), args=[w for w, _ in ws],
                 in_specs=in_specs, out_specs=out_specs, out_shape=out_shape)


def _half(ref, lead, hf, cols):
    n = ref.shape[-1 if cols else -2] // 2
    assert 2 * n == ref.shape[-1 if cols else -2]
    cut = pl.ds(hf * n, n)
    return ref.at[(*lead, slice(None), cut) if cols else (*lead, cut, slice(None))]


def _job_gather_ici(buf, cols=False):
    def views(o):
        x, y, c = _place()
        return c, _other_chips(x, y), _half(o, (2 * x + y,), c, cols)

    def start(i, o, ss, rs):
        c, chips, mine = views(o[0])
        for j, (cx, cy) in enumerate(chips):
            _rcopy(mine, mine, ss.at[j], rs.at[j], (cx, cy, c)).start()

    def wait(i, o, ss, rs):
        c, chips, mine = views(o[0])
        for j, (cx, cy) in enumerate(chips):
            cp = _rcopy(mine, _half(o[0], (2 * cx + cy,), c, cols), ss.at[j], rs.at[j], (cx, cy, c))
            cp.wait_send()
            cp.wait_recv()

    return _Job([buf], [SDS(buf.shape, buf.dtype)], {0: 0}, 3, start, wait)


def _job_gather_pair(buf, cols=False):
    def views(o):
        x, y, c = _place()
        return c, (x, y, 1 - c), _other_chips(x, y)

    def start(i, o, ss, rs):
        c, sib, chips = views(o[0])
        for j, (cx, cy) in enumerate(chips):
            got = _half(o[0], (2 * cx + cy,), c, cols)
            _rcopy(got, got, ss.at[j], rs.at[j], sib).start()

    def wait(i, o, ss, rs):
        c, sib, chips = views(o[0])
        for j, (cx, cy) in enumerate(chips):
            cp = _rcopy(_half(o[0], (2 * cx + cy,), c, cols), _half(o[0], (2 * cx + cy,), 1 - c, cols),
                        ss.at[j], rs.at[j], sib)
            cp.wait_send()
            cp.wait_recv()

    return _Job([buf], [SDS(buf.shape, buf.dtype)], {0: 0}, 3, start, wait)


class _Gather:
    def __init__(self, bufs, by_cols=()):
        self.todo, self.half, self.done, self.by_cols = dict(bufs), {}, {}, set(by_cols)

    def comm(self, admit=()):
        second, first = list(self.half), list(admit)
        jobs = [_job_gather_pair(self.half[n], n in self.by_cols) for n in second]
        jobs += [_job_gather_ici(self.todo[n], n in self.by_cols) for n in first]

        def absorb(outs):
            for n, o in zip(second + first, outs):
                if n in self.half:
                    del self.half[n]
                    self.done[n] = o
                else:
                    del self.todo[n]
                    self.half[n] = o
        return (jobs, absorb) if jobs else None


def _half_shape(R, C, cols):
    return (R, C // 2) if cols else (R // 2, C)


def _job_rs_pair(g4, cols=False):
    _, R, C = g4.shape

    def desc(i, o, ss, rs):
        x, y, c = _place()
        return _rcopy(_half(i[0], (slice(None),), 1 - c, cols), o[0], ss.at[0], rs.at[0], (x, y, 1 - c))

    return _Job([g4], [SDS((N_CHIPS,) + _half_shape(R, C, cols), F32)], {}, 1,
                lambda i, o, ss, rs: desc(i, o, ss, rs).start(), lambda i, o, ss, rs: desc(i, o, ss, rs).wait())


def _job_rs_chips(p4):
    _, hR, C = p4.shape

    def descs(i, o, ss, rs):
        x, y, c = _place()
        return [_rcopy(i[0].at[2 * cx + cy], o[0].at[j], ss.at[j], rs.at[j], (cx, cy, c))
                for j, (cx, cy) in enumerate(_other_chips(x, y))]

    def start(i, o, ss, rs):
        for cp in descs(i, o, ss, rs):
            cp.start()

    def wait(i, o, ss, rs):
        for cp in descs(i, o, ss, rs):
            cp.wait_send()
            cp.wait_recv()

    return _Job([p4], [SDS((3, hR, C), p4.dtype)], {}, 3, start, wait)


def _job_rs_join(buf, cols=False):
    def desc(o, ss, rs, recv):
        x, y, c = _place()
        mine = _half(o[0], (), c, cols)
        return _rcopy(mine, _half(o[0], (), 1 - c, cols) if recv else mine, ss.at[0], rs.at[0], (x, y, 1 - c))

    def wait(i, o, ss, rs):
        cp = desc(o, ss, rs, True)
        cp.wait_send()
        cp.wait_recv()

    return _Job([buf], [SDS(buf.shape, buf.dtype)], {0: 0}, 1,
                lambda i, o, ss, rs: desc(o, ss, rs, False).start(), wait)


def _walk(R, C, cols):
    if cols:
        tc = _pick(C, 2 * LANE, LANE)
        return (R, tc), C // tc
    tr = _pick(R, max(16, (1 << 20) // C // 16 * 16), 16)
    return (tr, C), R // tr


def _rs_add_pair(name, g4, recv, idx, cols=False):
    _, R, C = g4.shape
    hshape = _half_shape(R, C, cols)
    blk, nb = _walk(*hshape, cols)
    at = (lambda k, i: (k, 0, i)) if cols else (lambda k, i: (k, i, 0))

    def body(s_ref, a_ref, b_ref, pb_ref, po_ref):
        s = a_ref[...] + b_ref[...]
        pb_ref[...] = s.astype(BF16)

        @pl.when(pl.program_id(1) == s_ref[1])
        def _():
            po_ref[...] = s

    gs = pltpu.PrefetchScalarGridSpec(
        num_scalar_prefetch=1, grid=(nb, N_CHIPS),
        in_specs=[pl.BlockSpec((None,) + blk, lambda i, k, s: at(k, s[0] * nb + i)),
                  pl.BlockSpec((None,) + blk, lambda i, k, s: at(k, i))],
        out_specs=[pl.BlockSpec((None,) + blk, lambda i, k, s: at(k, i)),
                   pl.BlockSpec(blk, lambda i, k, s: at(k, i)[1:])])
    return pl.pallas_call(body, name=name, grid_spec=gs,
                          out_shape=[SDS((N_CHIPS,) + hshape, BF16), SDS(hshape, F32)],
                          compiler_params=_params(("parallel", "arbitrary")))(idx, g4, recv)


def _rs_add_chips(name, own, recv3, idx, cols=False):
    hR, hC = own.shape
    blk, nb = _walk(hR, hC, cols)
    at = (lambda i: (0, i)) if cols else (lambda i: (i, 0))

    def body(s_ref, a_ref, b_ref, o_ref):
        o_ref[...] = ((a_ref[...] + b_ref[0].astype(F32)) + b_ref[1].astype(F32)) + b_ref[2].astype(F32)

    gs = pltpu.PrefetchScalarGridSpec(
        num_scalar_prefetch=1, grid=(nb,),
        in_specs=[pl.BlockSpec(blk, lambda i, s: at(i)), pl.BlockSpec((3,) + blk, lambda i, s: (0,) + at(i))],
        out_specs=pl.BlockSpec(blk, lambda i, s: at(s[0] * nb + i)))
    return pl.pallas_call(body, name=name, grid_spec=gs,
                          out_shape=SDS((hR, 2 * hC) if cols else (2 * hR, hC), F32),
                          compiler_params=_params(("parallel",)))(idx, own, recv3)


class _ReduceScatter:
    def __init__(self, idx):
        self.idx, self.items, self.done = idx, [], {}

    def push(self, tag, g4, cols=False):
        self.items.append([tag, 0, g4, cols])

    def comm(self, ici=1):
        cur, jobs = [], []
        for item in self.items:
            tag, stage, data, cols = item
            if stage == 1:
                if ici == 0:
                    continue
                ici -= 1
            cur.append(item)
            jobs.append(_job_rs_pair(data, cols) if stage == 0 else _job_rs_chips(data[0]) if stage == 1
                        else _job_rs_join(data, cols))

        def absorb(outs):
            for item, o in zip(cur, outs):
                tag, stage, data, cols = item
                if stage == 0:
                    item[1:3] = [1, _rs_add_pair(tag + "_rs_add2", data, o, self.idx, cols)]
                elif stage == 1:
                    item[1:3] = [2, _rs_add_chips(tag + "_rs_add4", data[1], o, self.idx, cols)]
                else:
                    self.items.remove(item)
                    self.done[tag] = o
        return (jobs, absorb) if jobs else None


def _resid_epilogue(accs, ex):
    return [ex[0] + ex[1] * accs[0], accs[0]]


def _sigmoid(x):
    return 0.5 * jnp.tanh(0.5 * x) + 0.5


def _swiglu_epilogue(accs, ex):
    gt, up = accs
    return [gt, up, gt * _sigmoid(gt) * up]


def _swiglu_bwd_epilogue(accs, ex):
    dact = accs[0]
    gt, up = ex[0].astype(F32), ex[1].astype(F32)
    sg = _sigmoid(gt)
    silu = gt * sg
    return [dact * up * (sg + silu * (1.0 - sg)), dact * silu]


def kernel(x, c, ada_w, ada_b, norm_mix_g, norm_ffn_g, a_w_in, a_b_in, a_ln_g, a_ln_b, a_w_s, a_b_s, a_w_out, b_w_in, b_b_f, b_w_out, ffn_w_gate, ffn_w_up, ffn_w_down, final_g, loss_target, m_ada_w, m_ada_b, m_norm_mix_g, m_norm_ffn_g, m_a_w_in, m_a_b_in, m_a_ln_g, m_a_ln_b, m_a_w_s, m_a_b_s, m_a_w_out, m_b_w_in, m_b_b_f, m_b_w_out, m_ffn_w_gate, m_ffn_w_up, m_ffn_w_down, m_final_g, v_ada_w, v_ada_b, v_norm_mix_g, v_norm_ffn_g, v_a_w_in, v_a_b_in, v_a_ln_g, v_a_ln_b, v_a_w_s, v_a_b_s, v_a_w_out, v_b_w_in, v_b_b_f, v_b_w_out, v_ffn_w_gate, v_ffn_w_up, v_ffn_w_down, v_final_g):
    S, D = x.shape[1], x.shape[2]
    H = D // LANE
    G = D // LANE
    FH = ffn_w_down.shape[1] * N_CHIPS
    depth = ada_w.shape[0]
    assert depth == 2 and a_w_in.shape[0] == 1 and b_w_in.shape[0] == 1
    mx, my, mc = _place()
    chip = 2 * mx + my
    dev = 4 * mx + 2 * my + mc
    x0 = x[0]
    target = loss_target[0]
    tr = _pick(S, 256, 8)
    tm = _pick(S, 512, 8)
    tq_f = _pick(S, 1024, LANE)
    tq = _pick(S, 512, LANE)
    nq = S // tq

    idx = jnp.stack([mc, chip]).astype(jnp.int32)
    t3 = lambda a: jnp.transpose(a, (2, 0, 1))
    shards = {"a_in": (a_w_in, 0), "a_out": (a_w_out, 0), "b_in": (t3(b_w_in), None), "b_out": (b_w_out, 0)}
    for l in range(depth):
        shards.update({f"wg{l}": (ffn_w_gate, l), f"wu{l}": (ffn_w_up, l), f"wd{l}": (ffn_w_down, l)})
    first = ["a_in", "a_out"]
    rest = [n for n in shards if n not in first]
    ag = _Gather(dict(zip(first, _cast_slabs("cast_first", [shards[n] for n in first], idx))), by_cols=["b_in"])
    ag.todo.update(zip(rest, _cast_slabs("cast_rest", [shards[n] for n in rest], idx, comm=ag.comm(first))))
    _comm_call("ag_first_pair", ag.comm(["wg0"]))
    wa_in4 = ag.done["a_in"]
    wa_out = ag.done["a_out"].reshape(D, D)

    dev_idx = dev.astype(jnp.int32).reshape(1)
    c_all = _gather_devices("ag_c", jnp.pad(c, ((0, 7), (0, 0))), dev_idx).reshape(N_DEV, 8, D)[:, 0, :]
    c_act = _silu_rows("c_silu", c_all)
    n_loc = ada_w.shape[2]
    mods = []
    for l in range(depth):
        b_loc = lax.dynamic_slice_in_dim(ada_b[l:l + 1], chip * n_loc, n_loc, axis=1)
        mods.append(_mm(f"ada_fwd{l}", "nn", [c_act], [ada_w], M=N_DEV, N=n_loc, K=D, tm=N_DEV, b_layer=l,
                        tn=_pick(n_loc, 1024, LANE), tk=D, extras=[("row", b_loc)],
                        epilogue=lambda accs, ex: [accs[0] + ex[0]])[0])
    mod_all = _gather_devices("ag_mod", jnp.concatenate(mods, axis=1), dev_idx)
    mod_mine = lax.dynamic_index_in_dim(mod_all[0::2], dev, axis=1, keepdims=False)
    mod_mine = mod_mine.reshape(N_CHIPS, depth, n_loc).transpose(1, 0, 2).reshape(depth, 1, N_MOD * D)
    mod = [[mod_mine[l, :, i * D:(i + 1) * D] for i in range(N_MOD)] for l in range(depth)]

    row = lambda a: a.reshape(1, -1)

    tril = jnp.tril(jnp.ones((LANE, LANE), dtype=bool))
    w_mask = jnp.where(tril[None], a_w_s[0], 0.0).astype(BF16)
    bias_full = jnp.repeat(a_b_s[0].T, LANE, axis=1)
    bf_pad = jnp.pad(b_b_f, ((0, 0), (0, LANE - H)))

    admit = {"sgu_in": ["wu0"], "mix_out0": ["wd0"], "ffn_in0": ["b_in", "b_out"], "ffn_out0": [],
             "fox_qkv": ["wg1"], "attn_fwd": ["wu1", "wd1"], "mix_out1": []}
    saved = []
    xs = x0
    for l in range(depth):
        sh1, sc1, g1, sh2, sc2, g2 = mod[l]
        st = {"x_in": xs}
        h1 = _normmod_fwd(f"normmod_mix{l}", xs, row(norm_mix_g[l]), sc1, sh1, tr)
        st["h1"] = h1
        if l == 0:
            a = _mm("sgu_in", "nn", [h1], [wa_in4], M=S, N=2 * D, K=D, tm=tm, tn=2 * D // N_CHIPS, tk=D,
                    b_stacked=True, extras=[("row", a_b_in)], comm=ag.comm(admit["sgu_in"]),
                    epilogue=lambda accs, ex: [accs[0] + ex[0]])[0]
            y = _sgu_mid_fwd(a, a_ln_g, a_ln_b, w_mask, bias_full, tr)
            st["a"], st["y"] = a, y
            w_o, mix_out = wa_out, y
        else:
            w_qkv = ag.done["b_in"].reshape(-1, D)
            w_f = jnp.pad(w_qkv[3 * D:], ((0, LANE - H), (0, 0)))
            q_scale = jnp.concatenate([jnp.full((1, D), _LOG2E / math.sqrt(LANE), F32), jnp.ones((1, 2 * D), F32)],
                                      axis=1)
            qkv = _mm("fox_qkv", "nt", [h1], [w_qkv], M=S, N=3 * D, K=D, tm=tm, tn=_pick(3 * D, 1024, LANE),
                      tk=D, out_dtypes=(BF16,), extras=[("row", q_scale)],
                      epilogue=lambda accs, ex: [accs[0] * ex[0]], comm=ag.comm(admit["fox_qkv"]))[0]
            fl = _mm("fox_f", "nt", [h1], [w_f], M=S, N=LANE, K=D, tm=tm, tn=LANE, tk=D)[0]
            F_sh = _fox_gate_fwd(fl, bf_pad)
            F_hs = F_sh[:, :H].T
            F_rows = F_hs.reshape(H, nq, 1, tq)
            o, lse = _attn_fwd(qkv, F_sh, F_hs.reshape(H, S // tq_f, 1, tq_f) * _LOG2E, H, tq_f,
                               comm=ag.comm(admit["attn_fwd"]))
            st.update(qkv=qkv, fl=fl, F_sh=F_sh, F_rows=F_rows, o=o, lse=lse, w_qkv=w_qkv, w_f=w_f)
            w_o, mix_out = ag.done["b_out"].reshape(D, D), o
        x1, out1 = _mm(f"mix_out{l}", "nn", [mix_out], [w_o], M=S, N=D, K=D, tm=tm, tn=_pick(D, 1024, LANE),
                       tk=D, extras=[("tile", xs), ("row", g1)], out_dtypes=(F32, BF16),
                       epilogue=_resid_epilogue, comm=ag.comm(admit[f"mix_out{l}"]))
        st["x_mid"], st["out1"] = x1, out1
        h2 = _normmod_fwd(f"normmod_ffn{l}", x1, row(norm_ffn_g[l]), sc2, sh2, tr)
        gt, up, act = _mm(f"ffn_in{l}", "nn", [h2, h2], [ag.done[f"wg{l}"], ag.done[f"wu{l}"]], M=S, N=FH, K=D,
                          tm=tm, tn=FH // N_CHIPS, tk=D, b_stacked=True, acc_of=[0, 1], n_acc=2,
                          out_dtypes=(BF16, BF16, BF16), epilogue=_swiglu_epilogue,
                          comm=ag.comm(admit.get(f"ffn_in{l}", ())))
        x2, out2 = _mm(f"ffn_out{l}", "nn", [act], [ag.done[f"wd{l}"].reshape(FH, D)], M=S, N=D, K=FH, tm=tm,
                       tn=_pick(D, 1024, LANE), tk=FH, extras=[("tile", x1), ("row", g2)],
                       out_dtypes=(F32, BF16), epilogue=_resid_epilogue,
                       comm=ag.comm(admit.get(f"ffn_out{l}", ())))
        st.update(h2=h2, gt=gt, up=up, act=act, out2=out2)
        saved.append(st)
        xs = x2
    assert not ag.todo and not ag.half
    wg4 = [ag.done[f"wg{l}"] for l in range(depth)]
    wu4 = [ag.done[f"wu{l}"] for l in range(depth)]
    wd = [ag.done[f"wd{l}"].reshape(FH, D) for l in range(depth)]
    wb_out = ag.done["b_out"].reshape(D, D)

    dx, dog, loss_vec, g_final, dg2 = _loss_head("loss_head", xs, target, row(final_g), tr,
                                                 nxt=(saved[-1]["out2"], mod[-1][5]))
    loss = lax.psum(loss_vec[0, 0], ("x", "y", "c"))

    rs = _ReduceScatter(idx)
    dmods = [None] * depth
    gmix = [None] * depth
    gffn = [None] * depth
    for l in reversed(range(depth)):
        sh1, sc1, g1, sh2, sc2, g2 = mod[l]
        st = saved[l]
        dgt, dup = _mm(f"ffn_dact{l}", "nt", [dog], [wd[l]], M=S, N=FH, K=D, tm=tm, tn=FH // N_CHIPS, tk=D,
                       extras=[("tile", st["gt"]), ("tile", st["up"])], out_dtypes=(BF16, BF16),
                       epilogue=_swiglu_bwd_epilogue, comm=rs.comm())
        rs.push(f"wd{l}", _mm(f"ffn_dwd{l}", "tn", [st["act"]], [dog], M=FH, N=D, K=S, tm=FH // N_CHIPS,
                              tn=_pick(D, 512, LANE), tk=S, outer="i",
                              comm=rs.comm())[0].reshape(N_CHIPS, FH // N_CHIPS, D))
        rs.push(f"wg{l}", _mm(f"ffn_dwg{l}", "tn", [st["h2"]], [dgt], M=D, N=FH, K=S, tm=_pick(D, 512, LANE),
                              tn=FH // N_CHIPS, tk=S, out_stacked=True, comm=rs.comm())[0])
        rs.push(f"wu{l}", _mm(f"ffn_dwu{l}", "tn", [st["h2"]], [dup], M=D, N=FH, K=S, tm=_pick(D, 512, LANE),
                              tn=FH // N_CHIPS, tk=S, out_stacked=True, comm=rs.comm())[0])
        dh2 = _mm(f"ffn_dh{l}", "nt", [dgt, dup], [wg4[l], wu4[l]], M=S, N=D, K=FH, tm=_pick(S, 512, 8),
                  tn=_pick(D, 1024, LANE), tk=FH, b_stacked=True, out_dtypes=(BF16,), comm=rs.comm())[0]
        dx, dog, dsh2, dsc2, gffn[l], dg1 = _normmod_bwd(f"normmod_ffn_bwd{l}", st["x_mid"], dh2, dx,
                                                         row(norm_ffn_g[l]), sc2, tr, nxt=(st["out1"], g1))
        if l == 0:
            dy = _mm("sgu_dy", "nt", [dog], [wa_out], M=S, N=D, K=D, tm=tm, tn=_pick(D, 1024, LANE), tk=D,
                     out_dtypes=(BF16,))[0]
            rs.push("a_out", _mm("sgu_dwout", "tn", [st["y"]], [dog], M=D, N=D, K=S, tm=_pick(D, 512, LANE),
                                 tn=_pick(D, 1024, LANE),
                                 tk=S)[0].reshape(N_CHIPS, D // N_CHIPS, D))
            da, dws, dbias, g_ln_g, g_ln_b, g_b_in = _sgu_mid_bwd(st["a"], dy, a_ln_g, a_ln_b, w_mask, bias_full, tr,
                                                                  comm=rs.comm())
            rs.push("a_in", _mm("sgu_dwin", "tn", [st["h1"]], [da], M=D, N=2 * D, K=S, tm=_pick(D, 512, LANE),
                                tn=2 * D // N_CHIPS, tk=S, out_stacked=True, comm=rs.comm())[0])
            dh1 = _mm("sgu_dh", "nt", [da], [wa_in4], M=S, N=D, K=2 * D, tm=_pick(S, 512, 8), tn=_pick(D, 1024, LANE),
                      tk=2 * D, b_stacked=True, out_dtypes=(BF16,), comm=rs.comm())[0]
            g_w_s = jnp.where(tril[None], dws, 0.0)
            g_b_s = jnp.sum(dbias.reshape(LANE, G, LANE), axis=2).T
        else:
            do = _mm("fox_do", "nt", [dog], [wb_out], M=S, N=D, K=D, tm=tm, tn=_pick(D, 1024, LANE), tk=D,
                     out_dtypes=(BF16,))[0]
            rs.push("b_out", _mm("fox_dwout", "tn", [st["o"]], [dog], M=D, N=D, K=S, tm=_pick(D, 512, LANE),
                                 tn=_pick(D, 1024, LANE),
                                 tk=S)[0].reshape(N_CHIPS, D // N_CHIPS, D))
            delta = _head_dots(do, st["o"], H, tr)
            delta_rows = delta[:, :H].T.reshape(H, nq, 1, tq)
            A_rows = (st["F_rows"] - st["lse"].reshape(H, nq, 1, tq)) * _LOG2E
            w_qkv, w_f = st["w_qkv"], st["w_f"]
            dq, dk, dv, dF_k, dF_q = _attn_bwd(st["qkv"], do, st["F_sh"], A_rows, delta_rows, H, tq,
                                               comm=rs.comm(ici=2))
            dF_sh = jnp.pad((dF_k.reshape(H, S) + dF_q.reshape(H, S)).T, ((0, 0), (0, LANE - H)))
            dfl, db_f = _fox_gate_bwd(st["fl"], dF_sh, bf_pad)
            dfl_b = dfl.astype(BF16)
            dh_f = _mm("fox_dh_f", "nn", [dfl_b], [w_f], M=S, N=D, K=LANE, tm=tm, tn=_pick(D, 1024, LANE),
                       tk=LANE)[0]
            dh1 = _mm("fox_dh", "nn", [dq, dk, dv], [w_qkv, w_qkv, w_qkv], M=S, N=D, K=D, tm=_pick(S, 256, 8),
                      tn=_pick(D, 1024, LANE), tk=D, b_koffs=[0, 1, 2],
                      extras=[("tile", dh_f)], epilogue=lambda accs, ex: [accs[0] + ex[0]], out_dtypes=(BF16,),
                      comm=rs.comm())[0]
            tmw = _pick(D, 512, LANE)
            g_bT = None
            for p, (nm, d_) in enumerate((("q", dq), ("k", dk), ("v", dv))):
                g_bT = _mm(f"fox_dw{nm}", "tn", [d_], [st["h1"]], M=D, N=D, K=S, tm=tmw, tn=_pick(D, 1024, LANE),
                           tk=S, out_into=(3 * D + H, p * (D // tmw), g_bT), comm=rs.comm(ici=0))[0]
            dwf = _mm("fox_dwf", "tn", [dfl_b], [st["h1"]], M=LANE, N=D, K=S, tm=LANE, tn=_pick(D, 1024, LANE),
                      tk=S)[0]
            g_bT = lax.dynamic_update_slice(g_bT, dwf[:H], (3 * D, 0))
            rs.push("b_in", g_bT.reshape(N_CHIPS, -1, D), cols=True)
            g_b_f = db_f[:, :H]
        below = (saved[l - 1]["out2"], mod[l - 1][5]) if l else None
        res = _normmod_bwd(f"normmod_mix_bwd{l}", st["x_in"], dh1, dx, row(norm_mix_g[l]), sc1, tr, nxt=below,
                           comm=rs.comm(ici=1 - l))
        if l:
            dx, dog_below, dsh1, dsc1, gmix[l], dg2_below = res
        else:
            dx, dsh1, dsc1, gmix[l] = res
        dmods[l] = jnp.concatenate([dsh1, dsc1, dg1, dsh2, dsc2, dg2], axis=1)
        if l:
            dog, dg2 = dog_below, dg2_below
    grad_x = dx[None]

    small = [jnp.concatenate(dmods, axis=0), jnp.concatenate(gmix, axis=0), jnp.concatenate(gffn, axis=0),
             g_b_in, g_ln_g, g_ln_b, g_w_s[None], g_b_s[None], g_b_f, g_final[0]]
    small_w = [ada_b, norm_mix_g, norm_ffn_g, a_b_in, a_ln_g, a_ln_b, a_w_s, a_b_s, b_b_f, final_g]
    small_m = [m_ada_b, m_norm_mix_g, m_norm_ffn_g, m_a_b_in, m_a_ln_g, m_a_ln_b, m_a_w_s, m_a_b_s, m_b_b_f, m_final_g]
    small_v = [v_ada_b, v_norm_mix_g, v_norm_ffn_g, v_a_b_in, v_a_ln_g, v_a_ln_b, v_a_w_s, v_a_b_s, v_b_b_f, v_final_g]
    sizes = [w.size for w in small_w]
    total = sum(sizes)
    padded = -(-total // (8 * LANE)) * (8 * LANE)

    def pack(parts):
        flat = jnp.concatenate([p.reshape(-1) for p in parts])
        return jnp.pad(flat, (0, padded - total)).reshape(padded // LANE, LANE)

    def unpack(buf):
        flat = buf.reshape(-1)
        outs, off = [], 0
        for w, n in zip(small_w, sizes):
            outs.append(flat[off:off + n].reshape(w.shape))
            off += n
        return outs

    g_all = _gather_devices("ag_small_grads", pack(small), dev_idx, extra=rs.comm())
    g_small = _sum_slots("sum_small_grads", g_all)
    _, d_small, m_small, v_small = _adamw("adamw_small", pack(small_w), g_small, pack(small_m), pack(small_v))
    sg, sd, sm, sv_ = unpack(g_small), unpack(d_small), unpack(m_small), unpack(v_small)

    n_dm = depth * N_MOD * D
    dmod_all = g_all.reshape(N_DEV, -1)[:, :n_dm].reshape(N_DEV, depth, N_MOD * D)
    u_ada = None
    for l in range(depth):
        dm_loc = lax.dynamic_slice_in_dim(dmod_all[:, l, :], chip * n_loc, n_loc, axis=1).astype(BF16)
        g_ada = _mm(f"ada_dw{l}", "tn", [c_act], [dm_loc], M=D, N=n_loc, K=N_DEV, tm=_pick(D, 512, LANE),
                    tn=_pick(n_loc, 1024, LANE), tk=N_DEV)[0]
        u_ada = _adamw(f"adamw_ada_w{l}", ada_w, g_ada, m_ada_w, v_ada_w, layer=l, prev=u_ada)

    flushes = 0
    while rs.items:
        _comm_call(f"rs_flush{flushes}", rs.comm())
        flushes += 1
    groups = {"a_in": (a_w_in, m_a_w_in, v_a_w_in), "a_out": (a_w_out, m_a_w_out, v_a_w_out),
              "b_out": (b_w_out, m_b_w_out, v_b_w_out),
              "wg": (ffn_w_gate, m_ffn_w_gate, v_ffn_w_gate), "wu": (ffn_w_up, m_ffn_w_up, v_ffn_w_up),
              "wd": (ffn_w_down, m_ffn_w_down, v_ffn_w_down)}
    upd = {}
    for n, (w_, m_, v_) in groups.items():
        for l in range(w_.shape[0]):
            tag = n if n in rs.done else f"{n}{l}"
            upd[n] = _adamw("adamw_" + tag, w_, rs.done[tag], m_, v_, layer=l, prev=upd.get(n))
    u_a_in, u_a_out, u_b_out, u_wg, u_wu, u_wd = (upd[n] for n in groups)
    u_b_in = [jnp.transpose(t, (1, 2, 0)) for t in
              _adamw("adamw_b_in", t3(b_w_in), rs.done["b_in"], t3(m_b_w_in), t3(v_b_w_in))]

    def leaves(i, small_list):
        s = small_list
        return [u_ada[i], s[0], s[1], s[2], u_a_in[i], s[3], s[4], s[5], s[6], s[7], u_a_out[i],
                u_b_in[i], s[8], u_b_out[i], u_wg[i], u_wu[i], u_wd[i], s[9]]

    return (loss, grad_x, *leaves(0, sg), *leaves(1, sd), *leaves(2, sm), *leaves(3, sv_))
```
